```python
import jax, jax.numpy as jnp
from jax import lax
import numpy as np

D_MODEL = 2048
BATCH = 8
SEQ = 8192
DEPTH = 4

N_Q_HEADS = 16
N_KV_HEADS = 4
HEAD_DIM = 64
Q_PER_KV = N_Q_HEADS // N_KV_HEADS
ATTN_WIDTH = N_Q_HEADS * HEAD_DIM
KV_WIDTH = N_KV_HEADS * HEAD_DIM
WINDOW = 128
ROPE_THETA = 500000.0
ROPE_DIM = HEAD_DIM // 4
SGU_WIDTH = D_MODEL // 2
SGU_GROUPS = 8
SGU_GROUP_DIM = SGU_WIDTH // SGU_GROUPS
CHUNK = 128
D_FF = -(-8 * D_MODEL // (3 * 256)) * 256
OFF_Q = 0
OFF_K = OFF_Q + ATTN_WIDTH
OFF_V = OFF_K + KV_WIDTH
OFF_Z = OFF_V + KV_WIDTH
OFF_G = OFF_Z + 2 * SGU_WIDTH
IN_WIDTH = OFF_G + 2 * D_MODEL
EPS = 1e-5
NEG = -1e30

kernel_name = "hybrid_swa_sink_gmlp_gated_block"


def rmsnorm(x, g):
    xf = x.astype(jnp.float32)
    y = xf * lax.rsqrt(jnp.mean(xf * xf, axis=-1, keepdims=True) + EPS) * g.astype(jnp.float32)
    return y.astype(x.dtype)


def partial_rope(t, cos, sin):
    half = ROPE_DIM // 2
    t1 = t[..., :half].astype(jnp.float32)
    t2 = t[..., half:ROPE_DIM].astype(jnp.float32)
    rot = jnp.concatenate([t1 * cos - t2 * sin, t2 * cos + t1 * sin], axis=-1).astype(t.dtype)
    return jnp.concatenate([rot, t[..., ROPE_DIM:]], axis=-1)


def sliding_window_attention(q, k, v, sinks):
    B, S = q.shape[0], q.shape[1]
    nb = S // WINDOW
    qb = q.reshape(B, nb, WINDOW, N_KV_HEADS, Q_PER_KV, HEAD_DIM)

    def band(t):
        tb = t.reshape(B, nb, WINDOW, N_KV_HEADS, HEAD_DIM)
        prev = jnp.pad(tb, ((0, 0), (1, 0), (0, 0), (0, 0), (0, 0)))[:, :-1]
        return jnp.concatenate([prev, tb], axis=2)

    kb, vb = band(k), band(v)
    scores = jnp.einsum('bnqhgd,bnkhd->bnhgqk', qb, kb).astype(jnp.float32) * (HEAD_DIM ** -0.5)
    qi = jnp.arange(WINDOW)[:, None]
    kj = jnp.arange(2 * WINDOW)[None, :]
    rel = qi + WINDOW - kj
    band_ok = (rel >= 0) & (rel < WINDOW)
    blk_ok = (jnp.arange(nb)[:, None, None] > 0) | (kj >= WINDOW)[None]
    mask = band_ok[None] & blk_ok
    scores = jnp.where(mask[None, :, None, None], scores, NEG)
    sink = sinks.astype(jnp.float32).reshape(N_KV_HEADS, Q_PER_KV)[None, None, :, :, None, None]
    sink = jnp.broadcast_to(sink, scores.shape[:-1] + (1,))
    probs = jax.nn.softmax(jnp.concatenate([scores, sink], axis=-1), axis=-1)[..., :-1]
    out = jnp.einsum('bnhgqk,bnkhd->bnqhgd', probs.astype(v.dtype), vb)
    return out.reshape(B, S, ATTN_WIDTH)


def spatial_gating(z, ln_g, ln_b, w_s, b_s):
    B, S = z.shape[0], z.shape[1]
    u, v = jnp.split(z, 2, axis=-1)
    vf = v.astype(jnp.float32)
    mu = jnp.mean(vf, axis=-1, keepdims=True)
    var = jnp.mean(jnp.square(vf - mu), axis=-1, keepdims=True)
    vn = ((vf - mu) * lax.rsqrt(var + EPS) * ln_g.astype(jnp.float32) + ln_b.astype(jnp.float32)).astype(z.dtype)
    vc = vn.reshape(B, S // CHUNK, CHUNK, SGU_GROUPS, SGU_GROUP_DIM)
    causal = jnp.tril(jnp.ones((CHUNK, CHUNK), dtype=bool))
    w = jnp.where(causal[None], w_s, jnp.zeros_like(w_s))
    sv = jnp.einsum('gts,bnsgc->bntgc', w, vc) + b_s.T[None, None, :, :, None]
    return u * sv.reshape(B, S, SGU_WIDTH)


def _fwd_setup_inputs(seed: int = 0) -> dict:
    key = jax.random.key(seed)
    ks = jax.random.split(key, 20)
    f32 = jnp.float32
    nrm = lambda k, shape, scale: jax.random.normal(k, shape, f32) * scale
    x = jax.random.normal(ks[0], (BATCH, SEQ, D_MODEL), f32)
    offset = jax.random.randint(ks[1], (BATCH, 1), 0, 4096, dtype=jnp.int32)
    positions = (jnp.arange(SEQ, dtype=jnp.int32)[None, :] + offset).astype(jnp.int32)
    return {
        "x": x,
        "positions": positions,
        "norm1_g": 1.0 + nrm(ks[2], (DEPTH, D_MODEL), 0.02),
        "w_in": nrm(ks[3], (DEPTH, D_MODEL, IN_WIDTH), D_MODEL ** -0.5),
        "b_in": nrm(ks[4], (DEPTH, IN_WIDTH), 0.02),
        "sinks": nrm(ks[5], (DEPTH, N_Q_HEADS), 1.0),
        "sgu_ln_g": 1.0 + nrm(ks[6], (DEPTH, SGU_WIDTH), 0.02),
        "sgu_ln_b": nrm(ks[7], (DEPTH, SGU_WIDTH), 0.02),
        "sgu_w": nrm(ks[8], (DEPTH, SGU_GROUPS, CHUNK, CHUNK), CHUNK ** -0.5),
        "sgu_b": 1.0 + nrm(ks[9], (DEPTH, SGU_GROUPS, CHUNK), 0.02),
        "w_attn_branch": nrm(ks[10], (DEPTH, ATTN_WIDTH, D_MODEL), ATTN_WIDTH ** -0.5),
        "w_sgu_branch": nrm(ks[11], (DEPTH, SGU_WIDTH, D_MODEL), SGU_WIDTH ** -0.5),
        "w_out": nrm(ks[12], (DEPTH, D_MODEL, D_MODEL), D_MODEL ** -0.5),
        "norm2_g": 1.0 + nrm(ks[13], (DEPTH, D_MODEL), 0.02),
        "w_gate_up": nrm(ks[14], (DEPTH, D_MODEL, 2 * D_FF), D_MODEL ** -0.5),
        "w_down": nrm(ks[15], (DEPTH, D_FF, D_MODEL), D_FF ** -0.5),
        "final_g": 1.0 + nrm(ks[16], (D_MODEL,), 0.02),
    }


def _fwd_reference(x, positions, norm1_g, w_in, b_in, sinks, sgu_ln_g, sgu_ln_b, sgu_w, sgu_b,
              w_attn_branch, w_sgu_branch, w_out, norm2_g, w_gate_up, w_down, final_g):
    B, S = x.shape[0], x.shape[1]
    inv_freq = ROPE_THETA ** (-jnp.arange(0, ROPE_DIM, 2, dtype=jnp.float32) / ROPE_DIM)
    ang = positions.astype(jnp.float32)[..., None] * inv_freq
    cos = jnp.cos(ang)[:, :, None, :]
    sin = jnp.sin(ang)[:, :, None, :]
    h = x
    for l in range(DEPTH):
        xn = rmsnorm(h, norm1_g[l])
        proj = jnp.einsum('bsd,de->bse', xn, w_in[l]) + b_in[l]
        q = proj[..., OFF_Q:OFF_K].reshape(B, S, N_Q_HEADS, HEAD_DIM)
        k = proj[..., OFF_K:OFF_V].reshape(B, S, N_KV_HEADS, HEAD_DIM)
        v = proj[..., OFF_V:OFF_Z].reshape(B, S, N_KV_HEADS, HEAD_DIM)
        z = jax.nn.gelu(proj[..., OFF_Z:OFF_G], approximate=False)
        gates = jax.nn.sigmoid(proj[..., OFF_G:].astype(jnp.float32)).astype(h.dtype)
        g_attn, g_sgu = jnp.split(gates, 2, axis=-1)
        q = partial_rope(q, cos, sin)
        k = partial_rope(k, cos, sin)
        y_attn = sliding_window_attention(q, k, v, sinks[l])
        y_sgu = spatial_gating(z, sgu_ln_g[l], sgu_ln_b[l], sgu_w[l], sgu_b[l])
        merged = (g_attn * jnp.einsum('bse,ed->bsd', y_attn, w_attn_branch[l])
                  + g_sgu * jnp.einsum('bse,ed->bsd', y_sgu, w_sgu_branch[l]))
        h = h + jnp.einsum('bsd,de->bse', merged, w_out[l])
        hn = rmsnorm(h, norm2_g[l])
        gu = jnp.einsum('bsd,df->bsf', hn, w_gate_up[l])
        gate, up = jnp.split(gu, 2, axis=-1)
        h = h + jnp.einsum('bsf,fd->bsd', jax.nn.silu(gate) * up, w_down[l])
    return rmsnorm(h, final_g)


import jax as _jax
import jax.numpy as _jnp

TWIN_FORMAT = 'train_step'
FWD_PARAMS = ['x', 'positions', 'norm1_g', 'w_in', 'b_in', 'sinks', 'sgu_ln_g', 'sgu_ln_b', 'sgu_w', 'sgu_b', 'w_attn_branch', 'w_sgu_branch', 'w_out', 'norm2_g', 'w_gate_up', 'w_down', 'final_g']
TWIN_WEIGHTS = ['norm1_g', 'w_in', 'b_in', 'sinks', 'sgu_ln_g', 'sgu_ln_b', 'sgu_w', 'sgu_b', 'w_attn_branch', 'w_sgu_branch', 'w_out', 'norm2_g', 'w_gate_up', 'w_down', 'final_g']
TWIN_DIFF_INPUT = 'x'
TWIN_INPUTS = ['x', 'positions', 'norm1_g', 'w_in', 'b_in', 'sinks', 'sgu_ln_g', 'sgu_ln_b', 'sgu_w', 'sgu_b', 'w_attn_branch', 'w_sgu_branch', 'w_out', 'norm2_g', 'w_gate_up', 'w_down', 'final_g', 'loss_target', 'm_norm1_g', 'm_w_in', 'm_b_in', 'm_sinks', 'm_sgu_ln_g', 'm_sgu_ln_b', 'm_sgu_w', 'm_sgu_b', 'm_w_attn_branch', 'm_w_sgu_branch', 'm_w_out', 'm_norm2_g', 'm_w_gate_up', 'm_w_down', 'm_final_g', 'v_norm1_g', 'v_w_in', 'v_b_in', 'v_sinks', 'v_sgu_ln_g', 'v_sgu_ln_b', 'v_sgu_w', 'v_sgu_b', 'v_w_attn_branch', 'v_w_sgu_branch', 'v_w_out', 'v_norm2_g', 'v_w_gate_up', 'v_w_down', 'v_final_g']
TWIN_OUTPUTS = ['loss', 'grad_x', 'grad_norm1_g', 'grad_w_in', 'grad_b_in', 'grad_sinks', 'grad_sgu_ln_g', 'grad_sgu_ln_b', 'grad_sgu_w', 'grad_sgu_b', 'grad_w_attn_branch', 'grad_w_sgu_branch', 'grad_w_out', 'grad_norm2_g', 'grad_w_gate_up', 'grad_w_down', 'grad_final_g', 'delta_norm1_g', 'delta_w_in', 'delta_b_in', 'delta_sinks', 'delta_sgu_ln_g', 'delta_sgu_ln_b', 'delta_sgu_w', 'delta_sgu_b', 'delta_w_attn_branch', 'delta_w_sgu_branch', 'delta_w_out', 'delta_norm2_g', 'delta_w_gate_up', 'delta_w_down', 'delta_final_g', 'new_m_norm1_g', 'new_m_w_in', 'new_m_b_in', 'new_m_sinks', 'new_m_sgu_ln_g', 'new_m_sgu_ln_b', 'new_m_sgu_w', 'new_m_sgu_b', 'new_m_w_attn_branch', 'new_m_w_sgu_branch', 'new_m_w_out', 'new_m_norm2_g', 'new_m_w_gate_up', 'new_m_w_down', 'new_m_final_g', 'new_v_norm1_g', 'new_v_w_in', 'new_v_b_in', 'new_v_sinks', 'new_v_sgu_ln_g', 'new_v_sgu_ln_b', 'new_v_sgu_w', 'new_v_sgu_b', 'new_v_w_attn_branch', 'new_v_w_sgu_branch', 'new_v_w_out', 'new_v_norm2_g', 'new_v_w_gate_up', 'new_v_w_down', 'new_v_final_g']
TWIN_LEAF_KINDS = {'loss': 'loss', 'grad_x': 'grad_x', 'grad_norm1_g': 'grad_w', 'grad_w_in': 'grad_w', 'grad_b_in': 'grad_w', 'grad_sinks': 'grad_w', 'grad_sgu_ln_g': 'grad_w', 'grad_sgu_ln_b': 'grad_w', 'grad_sgu_w': 'grad_w', 'grad_sgu_b': 'grad_w', 'grad_w_attn_branch': 'grad_w', 'grad_w_sgu_branch': 'grad_w', 'grad_w_out': 'grad_w', 'grad_norm2_g': 'grad_w', 'grad_w_gate_up': 'grad_w', 'grad_w_down': 'grad_w', 'grad_final_g': 'grad_w', 'delta_norm1_g': 'delta_w', 'delta_w_in': 'delta_w', 'delta_b_in': 'delta_w', 'delta_sinks': 'delta_w', 'delta_sgu_ln_g': 'delta_w', 'delta_sgu_ln_b': 'delta_w', 'delta_sgu_w': 'delta_w', 'delta_sgu_b': 'delta_w', 'delta_w_attn_branch': 'delta_w', 'delta_w_sgu_branch': 'delta_w', 'delta_w_out': 'delta_w', 'delta_norm2_g': 'delta_w', 'delta_w_gate_up': 'delta_w', 'delta_w_down': 'delta_w', 'delta_final_g': 'delta_w', 'new_m_norm1_g': 'new_m', 'new_m_w_in': 'new_m', 'new_m_b_in': 'new_m', 'new_m_sinks': 'new_m', 'new_m_sgu_ln_g': 'new_m', 'new_m_sgu_ln_b': 'new_m', 'new_m_sgu_w': 'new_m', 'new_m_sgu_b': 'new_m', 'new_m_w_attn_branch': 'new_m', 'new_m_w_sgu_branch': 'new_m', 'new_m_w_out': 'new_m', 'new_m_norm2_g': 'new_m', 'new_m_w_gate_up': 'new_m', 'new_m_w_down': 'new_m', 'new_m_final_g': 'new_m', 'new_v_norm1_g': 'new_v', 'new_v_w_in': 'new_v', 'new_v_b_in': 'new_v', 'new_v_sinks': 'new_v', 'new_v_sgu_ln_g': 'new_v', 'new_v_sgu_ln_b': 'new_v', 'new_v_sgu_w': 'new_v', 'new_v_sgu_b': 'new_v', 'new_v_w_attn_branch': 'new_v', 'new_v_w_sgu_branch': 'new_v', 'new_v_w_out': 'new_v', 'new_v_norm2_g': 'new_v', 'new_v_w_gate_up': 'new_v', 'new_v_w_down': 'new_v', 'new_v_final_g': 'new_v'}


def _forward(args):
    return _fwd_reference(*[args[k] for k in FWD_PARAMS])


def _output_shape():
    def fwd():
        inp = _fwd_setup_inputs(0)
        return _fwd_reference(*[inp[k] for k in FWD_PARAMS])
    out = _jax.eval_shape(fwd)
    return out.shape, out.dtype

N_MICROBATCH = 1
ADAM_LR = 0.001
ADAM_B1 = 0.9
ADAM_B2 = 0.999
ADAM_EPS = 1e-08
ADAM_WD = 0.01
ADAM_STEP = 10
PER_EXAMPLE_BATCH_AXIS = {'x': 0, 'positions': 0, 'loss_target': 0}
SHARED_INPUTS = []
_WEIGHT_DTYPES = {'norm1_g': _jnp.float32, 'w_in': _jnp.float32, 'b_in': _jnp.float32, 'sinks': _jnp.float32, 'sgu_ln_g': _jnp.float32, 'sgu_ln_b': _jnp.float32, 'sgu_w': _jnp.float32, 'sgu_b': _jnp.float32, 'w_attn_branch': _jnp.float32, 'w_sgu_branch': _jnp.float32, 'w_out': _jnp.float32, 'norm2_g': _jnp.float32, 'w_gate_up': _jnp.float32, 'w_down': _jnp.float32, 'final_g': _jnp.float32}
MOMENT_SCALE = {'norm1_g': 7.214312e-02, 'w_in': 3.674288e-02, 'b_in': 5.215568e-02, 'sinks': 1.783950e-02, 'sgu_ln_g': 4.246676e-02, 'sgu_ln_b': 4.207402e-02, 'sgu_w': 4.230956e-02, 'sgu_b': 5.988736e-02, 'w_attn_branch': 1.563233e-02, 'w_sgu_branch': 5.408107e-02, 'w_out': 5.509517e-02, 'norm2_g': 8.568138e-02, 'w_gate_up': 3.686665e-02, 'w_down': 6.019322e-02, 'final_g': 3.198748e+01}


def _to_microbatches(a, axis):
    t = _jnp.moveaxis(a, axis, 0)
    t = t.reshape((N_MICROBATCH, t.shape[0] // N_MICROBATCH) + t.shape[1:])
    return _jnp.moveaxis(t, 1, axis + 1)


def setup_inputs(seed: int = 0) -> dict:
    inp = _fwd_setup_inputs(seed)
    key = _jax.random.fold_in(_jax.random.key(seed), 7919)
    shape, _ = _output_shape()
    out = dict(inp)
    out["loss_target"] = _jax.random.normal(_jax.random.fold_in(key, 0), shape, _jnp.float32)
    for i, name in enumerate(TWIN_WEIGHTS):
        w = inp[name].astype(_jnp.float32)
        if MOMENT_SCALE is None:
            s = _jnp.sqrt(_jnp.mean(_jnp.square(w)) + 1e-30)
        else:
            s = MOMENT_SCALE[name]
        km, kv = _jax.random.split(_jax.random.fold_in(key, i + 1))
        out[name] = w
        out["m_" + name] = s * _jax.random.normal(km, w.shape, _jnp.float32)
        out["v_" + name] = (s * s) * _jax.random.uniform(kv, w.shape, _jnp.float32, 0.5, 1.5)
    if N_MICROBATCH > 1:
        for name, axis in PER_EXAMPLE_BATCH_AXIS.items():
            out[name] = _to_microbatches(out[name], axis)
    return {'x': out['x'], 'positions': out['positions'], 'norm1_g': out['norm1_g'], 'w_in': out['w_in'], 'b_in': out['b_in'], 'sinks': out['sinks'], 'sgu_ln_g': out['sgu_ln_g'], 'sgu_ln_b': out['sgu_ln_b'], 'sgu_w': out['sgu_w'], 'sgu_b': out['sgu_b'], 'w_attn_branch': out['w_attn_branch'], 'w_sgu_branch': out['w_sgu_branch'], 'w_out': out['w_out'], 'norm2_g': out['norm2_g'], 'w_gate_up': out['w_gate_up'], 'w_down': out['w_down'], 'final_g': out['final_g'], 'loss_target': out['loss_target'], 'm_norm1_g': out['m_norm1_g'], 'm_w_in': out['m_w_in'], 'm_b_in': out['m_b_in'], 'm_sinks': out['m_sinks'], 'm_sgu_ln_g': out['m_sgu_ln_g'], 'm_sgu_ln_b': out['m_sgu_ln_b'], 'm_sgu_w': out['m_sgu_w'], 'm_sgu_b': out['m_sgu_b'], 'm_w_attn_branch': out['m_w_attn_branch'], 'm_w_sgu_branch': out['m_w_sgu_branch'], 'm_w_out': out['m_w_out'], 'm_norm2_g': out['m_norm2_g'], 'm_w_gate_up': out['m_w_gate_up'], 'm_w_down': out['m_w_down'], 'm_final_g': out['m_final_g'], 'v_norm1_g': out['v_norm1_g'], 'v_w_in': out['v_w_in'], 'v_b_in': out['v_b_in'], 'v_sinks': out['v_sinks'], 'v_sgu_ln_g': out['v_sgu_ln_g'], 'v_sgu_ln_b': out['v_sgu_ln_b'], 'v_sgu_w': out['v_sgu_w'], 'v_sgu_b': out['v_sgu_b'], 'v_w_attn_branch': out['v_w_attn_branch'], 'v_w_sgu_branch': out['v_w_sgu_branch'], 'v_w_out': out['v_w_out'], 'v_norm2_g': out['v_norm2_g'], 'v_w_gate_up': out['v_w_gate_up'], 'v_w_down': out['v_w_down'], 'v_final_g': out['v_final_g']}


def _loss(weights, diff, rest, loss_target):
    with _jax.named_scope("forward"):
        args = {**rest, TWIN_DIFF_INPUT: diff, **{k: w.astype(_WEIGHT_DTYPES[k]) for k, w in weights.items()}}
        y = _forward(args)
    with _jax.named_scope("loss_head"):
        err = _jnp.square(y.astype(_jnp.float32) - loss_target)
        return 0.5 * _jnp.sum(_jnp.mean(err, axis=-1)) if err.ndim else 0.5 * err


def _adamw(w, g, m, v):
    m = ADAM_B1 * m + (1.0 - ADAM_B1) * g
    v = ADAM_B2 * v + (1.0 - ADAM_B2) * _jnp.square(g)
    m_hat = m / (1.0 - ADAM_B1 ** ADAM_STEP)
    v_hat = v / (1.0 - ADAM_B2 ** ADAM_STEP)
    delta = -ADAM_LR * (m_hat / (_jnp.sqrt(v_hat) + ADAM_EPS) + ADAM_WD * w)
    return delta, m, v


def reference(x, positions, norm1_g, w_in, b_in, sinks, sgu_ln_g, sgu_ln_b, sgu_w, sgu_b, w_attn_branch, w_sgu_branch, w_out, norm2_g, w_gate_up, w_down, final_g, loss_target, m_norm1_g, m_w_in, m_b_in, m_sinks, m_sgu_ln_g, m_sgu_ln_b, m_sgu_w, m_sgu_b, m_w_attn_branch, m_w_sgu_branch, m_w_out, m_norm2_g, m_w_gate_up, m_w_down, m_final_g, v_norm1_g, v_w_in, v_b_in, v_sinks, v_sgu_ln_g, v_sgu_ln_b, v_sgu_w, v_sgu_b, v_w_attn_branch, v_w_sgu_branch, v_w_out, v_norm2_g, v_w_gate_up, v_w_down, v_final_g):
    given = dict(x=x, positions=positions, norm1_g=norm1_g, w_in=w_in, b_in=b_in, sinks=sinks, sgu_ln_g=sgu_ln_g, sgu_ln_b=sgu_ln_b, sgu_w=sgu_w, sgu_b=sgu_b, w_attn_branch=w_attn_branch, w_sgu_branch=w_sgu_branch, w_out=w_out, norm2_g=norm2_g, w_gate_up=w_gate_up, w_down=w_down, final_g=final_g, loss_target=loss_target, m_norm1_g=m_norm1_g, m_w_in=m_w_in, m_b_in=m_b_in, m_sinks=m_sinks, m_sgu_ln_g=m_sgu_ln_g, m_sgu_ln_b=m_sgu_ln_b, m_sgu_w=m_sgu_w, m_sgu_b=m_sgu_b, m_w_attn_branch=m_w_attn_branch, m_w_sgu_branch=m_w_sgu_branch, m_w_out=m_w_out, m_norm2_g=m_norm2_g, m_w_gate_up=m_w_gate_up, m_w_down=m_w_down, m_final_g=m_final_g, v_norm1_g=v_norm1_g, v_w_in=v_w_in, v_b_in=v_b_in, v_sinks=v_sinks, v_sgu_ln_g=v_sgu_ln_g, v_sgu_ln_b=v_sgu_ln_b, v_sgu_w=v_sgu_w, v_sgu_b=v_sgu_b, v_w_attn_branch=v_w_attn_branch, v_w_sgu_branch=v_w_sgu_branch, v_w_out=v_w_out, v_norm2_g=v_norm2_g, v_w_gate_up=v_w_gate_up, v_w_down=v_w_down, v_final_g=v_final_g)
    weights = {n: given[n] for n in TWIN_WEIGHTS}
    shared = {n: given[n] for n in SHARED_INPUTS}
    per_example = {n: given[n] for n in ['x', 'positions']}
    grad_fn = _jax.value_and_grad(_loss, argnums=(0, 1))

    def one_microbatch(ex, loss_target):
        ex = dict(ex)
        diff = ex.pop(TWIN_DIFF_INPUT)
        return grad_fn(weights, diff, {**shared, **ex}, loss_target)

    if N_MICROBATCH == 1:
        loss, (grad_w, grad_x) = one_microbatch(per_example, given["loss_target"])
    else:
        def body(carry, xs):
            loss_sum, grad_sum = carry
            l_k, (gw_k, gx_k) = one_microbatch(xs[0], xs[1])
            with _jax.named_scope("update"):
                return (loss_sum + l_k, _jax.tree.map(_jnp.add, grad_sum, gw_k)), gx_k

        init = (_jnp.zeros((), _jnp.float32), _jax.tree.map(_jnp.zeros_like, weights))
        (loss, grad_w), grad_x = _jax.lax.scan(body, init, (per_example, given["loss_target"]))
    with _jax.named_scope("update"):
        delta_w, new_m, new_v = {}, {}, {}
        for n in TWIN_WEIGHTS:
            delta_w[n], new_m[n], new_v[n] = _adamw(weights[n], grad_w[n], given["m_" + n], given["v_" + n])
    return (loss, grad_x, *[grad_w[n] for n in TWIN_WEIGHTS], *[delta_w[n] for n in TWIN_WEIGHTS],
            *[new_m[n] for n in TWIN_WEIGHTS], *[new_v[n] for n in TWIN_WEIGHTS])
```

```python
import functools
import math

import jax
import jax.numpy as jnp
from jax import lax
from jax.experimental import pallas as pl
from jax.experimental.pallas import tpu as pltpu

F32 = jnp.float32
BF16 = jnp.bfloat16

N_DEV = 8
N_LAYERS = 4
HEAD_DIM = 64
WINDOW = 128
CHUNK = 128
GROUP_DIM = 128
ROPE_DIM = HEAD_DIM // 4
ROPE_HALF = ROPE_DIM // 2
ROPE_THETA = 500000.0
EPS = 1e-5
NEG = -1e30
ATTN_SCALE = HEAD_DIM ** -0.5
ADAM_LR = 0.001
ADAM_B1 = 0.9
ADAM_B2 = 0.999
ADAM_EPS = 1e-08
ADAM_WD = 0.01
ADAM_STEP = 10
LANES = 128
SUBLANES = 8
BF16_SUBLANES = 16
PACK_UNIT = SUBLANES * LANES
ADAM_BLOCK_ELEMS = 256 * 1024
V7X_VMEM_LIMIT_BYTES = 56 * 1024 * 1024
MATMUL_TILE_PREFS = (1024, 1408, 768, 512, 384, 256, 128)
ROW_TILE_PREFS = (512, 256, 128)
MESH_TYPE = pl.DeviceIdType.MESH
ANY = pl.BlockSpec(memory_space=pl.ANY)


def _pick(n, prefs):
    for p in prefs:
        if n % p == 0:
            return p
    return n


def _params(*sem):
    return pltpu.CompilerParams(dimension_semantics=sem, vmem_limit_bytes=V7X_VMEM_LIMIT_BYTES)


_DIMS = {"nn": (((1,), (0,)), ((), ())), "nt": (((1,), (1,)), ((), ())), "tn": (((0,), (0,)), ((), ()))}


def _matmul(a, b, mode, out_dtype, name, bias=None, res=None):
    if mode == "nn":
        (m, k), n = a.shape, b.shape[1]
    elif mode == "nt":
        (m, k), n = a.shape, b.shape[0]
    else:
        (k, m), n = a.shape, b.shape[1]
    tm, tn, tk = _pick(m, MATMUL_TILE_PREFS), _pick(n, MATMUL_TILE_PREFS), _pick(k, MATMUL_TILE_PREFS)
    nk = k // tk
    dims = _DIMS[mode]
    a_spec = pl.BlockSpec((tk, tm), lambda i, j, kk: (kk, i)) if mode == "tn" else pl.BlockSpec((tm, tk), lambda i, j, kk: (i, kk))
    b_spec = pl.BlockSpec((tn, tk), lambda i, j, kk: (j, kk)) if mode == "nt" else pl.BlockSpec((tk, tn), lambda i, j, kk: (kk, j))
    in_specs, args = [a_spec, b_spec], [a, b]
    if bias is not None:
        in_specs.append(pl.BlockSpec((1, tn), lambda i, j, kk: (0, j)))
        args.append(bias)
    if res is not None:
        in_specs.append(pl.BlockSpec((tm, tn), lambda i, j, kk: (i, j)))
        args.append(res)

    def body(*refs):
        a_ref, b_ref = refs[0], refs[1]
        pos = 2
        bias_ref = res_ref = None
        if bias is not None:
            bias_ref = refs[pos]
            pos += 1
        if res is not None:
            res_ref = refs[pos]
            pos += 1
        o_ref = refs[pos]

        def finish(r):
            if bias_ref is not None:
                r = r + bias_ref[...]
            if res_ref is not None:
                r = r + res_ref[...]
            o_ref[...] = r.astype(out_dtype)

        part = lax.dot_general(a_ref[...], b_ref[...], dims, preferred_element_type=F32)
        if nk == 1:
            finish(part)
        else:
            acc_ref = refs[pos + 1]
            kk = pl.program_id(2)

            @pl.when(kk == 0)
            def _():
                acc_ref[...] = part

            @pl.when(kk > 0)
            def _():
                acc_ref[...] += part

            @pl.when(kk == nk - 1)
            def _():
                finish(acc_ref[...])

    return pl.pallas_call(
        body,
        name=name,
        grid=(m // tm, n // tn, nk),
        in_specs=in_specs,
        out_specs=pl.BlockSpec((tm, tn), lambda i, j, kk: (i, j)),
        out_shape=jax.ShapeDtypeStruct((m, n), out_dtype),
        scratch_shapes=[] if nk == 1 else [pltpu.VMEM((tm, tn), F32)],
        compiler_params=_params("parallel", "parallel", "arbitrary"),
    )(*args)


def _rowwise(body, name, rows, tr, ins, consts, outs, accs=()):
    n_in, n_c, n_o = len(ins), len(consts), len(outs)

    def wrapped(*refs):
        body(pl.program_id(0), refs[:n_in], refs[n_in:n_in + n_c], refs[n_in + n_c:n_in + n_c + n_o],
             refs[n_in + n_c + n_o:])

    def whole(shape):
        zeros = (0,) * len(shape)
        return pl.BlockSpec(tuple(shape), lambda i: zeros)

    in_specs = [pl.BlockSpec((tr, a.shape[1]), lambda i: (i, 0)) for a in ins] + [whole(c.shape) for c in consts]
    out_specs = [pl.BlockSpec((tr, o.shape[1]), lambda i: (i, 0)) for o in outs] + [whole(a.shape) for a in accs]
    return pl.pallas_call(
        wrapped,
        name=name,
        grid=(rows // tr,),
        in_specs=in_specs,
        out_specs=out_specs,
        out_shape=list(outs) + list(accs),
        compiler_params=_params("arbitrary" if accs else "parallel"),
    )(*ins, *consts)


def _sds(shape, dtype):
    return jax.ShapeDtypeStruct(tuple(shape), dtype)


def _rms_fwd(h, g, name):
    s, d = h.shape
    tr = _pick(s, ROW_TILE_PREFS)

    def body(i, ins, consts, outs, accs):
        x = ins[0][...]
        r = lax.rsqrt(jnp.mean(x * x, axis=-1, keepdims=True) + EPS)
        outs[0][...] = (x * r * consts[0][...]).astype(BF16)

    return _rowwise(body, name, s, tr, [h], [g], [_sds((s, d), BF16)])[0]


def _rms_bwd(h, g, dy, dh_up, name):
    s, d = h.shape
    tr = _pick(s, ROW_TILE_PREFS)

    def body(i, ins, consts, outs, accs):
        x, dyv, up = ins[0][...], ins[1][...].astype(F32), ins[2][...]
        r = lax.rsqrt(jnp.mean(x * x, axis=-1, keepdims=True) + EPS)
        xr = x * r
        gy = dyv * consts[0][...]
        dx = r * (gy - xr * jnp.mean(gy * xr, axis=-1, keepdims=True))
        outs[0][...] = up + dx

        @pl.when(i == 0)
        def _():
            accs[0][...] = jnp.zeros_like(accs[0])

        accs[0][...] += jnp.sum(dyv * xr, axis=0, keepdims=True)

    return _rowwise(body, name, s, tr, [h, dy, dh_up], [g], [_sds((s, d), F32)], [_sds((1, d), F32)])


def _loss_head(h, g, target, name):
    s, d = h.shape
    tr = _pick(s, ROW_TILE_PREFS)

    def body(i, ins, consts, outs, accs):
        x, t = ins[0][...], ins[1][...]
        gv = consts[0][...]
        r = lax.rsqrt(jnp.mean(x * x, axis=-1, keepdims=True) + EPS)
        xr = x * r
        diff = xr * gv - t
        dyv = diff * (1.0 / d)
        gy = dyv * gv
        outs[0][...] = r * (gy - xr * jnp.mean(gy * xr, axis=-1, keepdims=True))

        @pl.when(i == 0)
        def _():
            accs[0][...] = jnp.zeros_like(accs[0])
            accs[1][...] = jnp.zeros_like(accs[1])

        accs[0][...] += jnp.sum(dyv * xr, axis=0, keepdims=True)
        part = 0.5 * jnp.sum(jnp.mean(diff * diff, axis=-1, keepdims=True), axis=0, keepdims=True)
        accs[1][...] += jnp.broadcast_to(part, accs[1].shape)

    return _rowwise(body, name, s, tr, [h, target], [g], [_sds((s, d), F32)],
                    [_sds((1, d), F32), _sds((SUBLANES, LANES), F32)])


def _colsum(a, name):
    s, w = a.shape
    tr = _pick(s, ROW_TILE_PREFS)

    def body(i, ins, consts, outs, accs):
        @pl.when(i == 0)
        def _():
            accs[0][...] = jnp.zeros_like(accs[0])

        accs[0][...] += jnp.sum(ins[0][...].astype(F32), axis=0, keepdims=True)

    return _rowwise(body, name, s, tr, [a], [], [], [_sds((1, w), F32)])[0]


def _sigmoid(x):
    return 1.0 / (1.0 + jnp.exp(-x))


def _merge_fwd(pg, a, bm, name):
    s, d = a.shape
    tr = _pick(s, ROW_TILE_PREFS)

    def body(i, ins, consts, outs, accs):
        p = ins[0][...].astype(F32)
        ga, gs = _sigmoid(p[:, :d]), _sigmoid(p[:, d:])
        outs[0][...] = (ga * ins[1][...].astype(F32) + gs * ins[2][...].astype(F32)).astype(BF16)

    return _rowwise(body, name, s, tr, [pg, a, bm], [], [_sds((s, d), BF16)])[0]


def _merge_bwd(pg, a, bm, dm, name):
    s, d = a.shape
    tr = _pick(s, ROW_TILE_PREFS)

    def body(i, ins, consts, outs, accs):
        p = ins[0][...].astype(F32)
        av, bv, dmv = ins[1][...].astype(F32), ins[2][...].astype(F32), ins[3][...].astype(F32)
        ga, gs = _sigmoid(p[:, :d]), _sigmoid(p[:, d:])
        outs[0][...] = (dmv * ga).astype(BF16)
        outs[1][...] = (dmv * gs).astype(BF16)
        outs[2][:, :d] = (dmv * av * ga * (1.0 - ga)).astype(BF16)
        outs[2][:, d:] = (dmv * bv * gs * (1.0 - gs)).astype(BF16)

    return _rowwise(body, name, s, tr, [pg, a, bm, dm], [],
                    [_sds((s, d), BF16), _sds((s, d), BF16), _sds((s, 2 * d), BF16)])


def _swiglu_fwd(gu, name):
    s, w = gu.shape
    f = w // 2
    tr = _pick(s, (256, 128))

    def body(i, ins, consts, outs, accs):
        gate, up = ins[0][:, :f].astype(F32), ins[0][:, f:].astype(F32)
        outs[0][...] = (gate * _sigmoid(gate) * up).astype(BF16)

    return _rowwise(body, name, s, tr, [gu], [], [_sds((s, f), BF16)])[0]


def _swiglu_bwd(gu, dact, name):
    s, w = gu.shape
    f = w // 2
    tr = _pick(s, (256, 128))

    def body(i, ins, consts, outs, accs):
        gate, up = ins[0][:, :f].astype(F32), ins[0][:, f:].astype(F32)
        da = ins[1][...].astype(F32)
        sg = _sigmoid(gate)
        outs[0][:, :f] = (da * up * sg * (1.0 + gate * (1.0 - sg))).astype(BF16)
        outs[0][:, f:] = (da * gate * sg).astype(BF16)

    return _rowwise(body, name, s, tr, [gu, dact], [], [_sds((s, w), BF16)])[0]


def _rope_tables(pos_col, name):
    s = pos_col.shape[0]
    tr = _pick(s, (1024, 512, 256, 128))
    inv = ROPE_THETA ** (-jnp.arange(0, ROPE_DIM, 2, dtype=F32) / ROPE_DIM)
    lane = jnp.arange(LANES)
    inv_lanes = inv[lane % ROPE_HALF].reshape(1, LANES)

    def body(i, ins, consts, outs, accs):
        ang = ins[0][...].astype(F32) * consts[0][...]
        c, sn = jnp.cos(ang), jnp.sin(ang)
        in_head = lax.broadcasted_iota(jnp.int32, ang.shape, 1) % HEAD_DIM
        outs[0][:, 0:LANES] = jnp.where(in_head < ROPE_DIM, c, 1.0)
        outs[0][:, LANES:2 * LANES] = jnp.where(in_head < ROPE_HALF, -sn, 0.0)
        outs[0][:, 2 * LANES:] = jnp.where((in_head >= ROPE_HALF) & (in_head < ROPE_DIM), sn, 0.0)

    return _rowwise(body, name, s, tr, [pos_col], [inv_lanes], [_sds((s, 3 * LANES), F32)])[0]


def _rope(x, tab, inverse=False):
    width = x.shape[1]
    reps = width // LANES
    c = jnp.tile(tab[:, 0:LANES], (1, reps))
    lo = jnp.tile(tab[:, LANES:2 * LANES], (1, reps))
    hi = jnp.tile(tab[:, 2 * LANES:], (1, reps))
    if inverse:
        lo, hi = -lo, -hi
    return x * c + pltpu.roll(x, width - ROPE_HALF, 1) * lo + pltpu.roll(x, ROPE_HALF, 1) * hi


def _attn_specs(aw, kw):
    kb = aw // kw
    prev = lambda i: jnp.maximum(i - 1, 0)
    return [
        pl.BlockSpec(memory_space=pltpu.SMEM),
        pl.BlockSpec((WINDOW, aw), lambda i: (i, 0)),
        pl.BlockSpec((WINDOW, kw), lambda i: (i, kb)),
        pl.BlockSpec((WINDOW, kw), lambda i: (prev(i), kb)),
        pl.BlockSpec((WINDOW, kw), lambda i: (i, kb + 1)),
        pl.BlockSpec((WINDOW, kw), lambda i: (prev(i), kb + 1)),
        pl.BlockSpec((WINDOW, 3 * LANES), lambda i: (i, 0)),
        pl.BlockSpec((WINDOW, 3 * LANES), lambda i: (prev(i), 0)),
    ]


def _attn_common(i, q_ref, kc_ref, kp_ref, vc_ref, vp_ref, tq_ref, tp_ref):
    tq, tp = tq_ref[...], tp_ref[...]
    q = _rope(q_ref[...].astype(F32), tq).astype(BF16)
    kc = _rope(kc_ref[...].astype(F32), tq)
    kp = _rope(kp_ref[...].astype(F32), tp)
    k2 = jnp.concatenate([kp, kc], axis=0).astype(BF16)
    v2 = jnp.concatenate([vp_ref[...], vc_ref[...]], axis=0)
    qi = lax.broadcasted_iota(jnp.int32, (WINDOW, 2 * WINDOW), 0)
    kj = lax.broadcasted_iota(jnp.int32, (WINDOW, 2 * WINDOW), 1)
    rel = qi + WINDOW - kj
    ok = (rel >= 0) & (rel < WINDOW) & ((kj >= WINDOW) | (i > 0))
    return q, k2, v2, ok, tq, tp


def _head_probs(qh, kg, ok, sink):
    s = lax.dot_general(qh, kg, _DIMS["nt"], preferred_element_type=F32) * ATTN_SCALE
    s = jnp.where(ok, s, NEG)
    m = jnp.maximum(jnp.max(s, axis=1, keepdims=True), sink)
    p = jnp.exp(s - m)
    es = jnp.exp(sink - m)
    inv = 1.0 / (jnp.sum(p, axis=1, keepdims=True) + es)
    return p * inv, es * inv


def _attn_fwd(qkv, tabs, sinks, aw, kw, name):
    s = qkv.shape[0]
    nq, nkv = aw // HEAD_DIM, kw // HEAD_DIM
    qpk = nq // nkv

    def body(s_ref, q_ref, kc_ref, kp_ref, vc_ref, vp_ref, tq_ref, tp_ref, o_ref):
        i = pl.program_id(0)
        q, k2, v2, ok, _, _ = _attn_common(i, q_ref, kc_ref, kp_ref, vc_ref, vp_ref, tq_ref, tp_ref)
        for h in range(nq):
            g = h // qpk
            hs, gs = slice(h * HEAD_DIM, (h + 1) * HEAD_DIM), slice(g * HEAD_DIM, (g + 1) * HEAD_DIM)
            pn, _ = _head_probs(q[:, hs], k2[:, gs], ok, s_ref[h])
            o = jnp.dot(pn.astype(BF16), v2[:, gs], preferred_element_type=F32)
            o_ref[:, hs] = o.astype(BF16)

    return pl.pallas_call(
        body,
        name=name,
        grid=(s // WINDOW,),
        in_specs=_attn_specs(aw, kw),
        out_specs=pl.BlockSpec((WINDOW, aw), lambda i: (i, 0)),
        out_shape=_sds((s, aw), BF16),
        compiler_params=_params("parallel"),
    )(sinks, qkv, qkv, qkv, qkv, qkv, tabs, tabs)


def _attn_bwd(qkv, tabs, sinks, o, do, aw, kw, name):
    s = qkv.shape[0]
    nb = s // WINDOW
    nq, nkv = aw // HEAD_DIM, kw // HEAD_DIM
    qpk = nq // nkv

    def body(s_ref, q_ref, kc_ref, kp_ref, vc_ref, vp_ref, tq_ref, tp_ref, o_ref, do_ref,
             dq_ref, dkv_ref, ds_ref, ck_ref, cv_ref):
        i = pl.program_id(0)

        @pl.when(i == 0)
        def _():
            ck_ref[...] = jnp.zeros_like(ck_ref)
            cv_ref[...] = jnp.zeros_like(cv_ref)
            ds_ref[...] = jnp.zeros_like(ds_ref)

        q, k2, v2, ok, tq, tp = _attn_common(i, q_ref, kc_ref, kp_ref, vc_ref, vp_ref, tq_ref, tp_ref)
        dov, ov = do_ref[...], o_ref[...]
        row0 = lax.broadcasted_iota(jnp.int32, (SUBLANES, LANES), 0) == 0
        lane = lax.broadcasted_iota(jnp.int32, (SUBLANES, LANES), 1)
        dsink = jnp.zeros((SUBLANES, LANES), F32)
        dq_parts, dk_parts, dv_parts = [], [], []
        for g in range(nkv):
            gs = slice(g * HEAD_DIM, (g + 1) * HEAD_DIM)
            kg, vg = k2[:, gs], v2[:, gs]
            dk_g = jnp.zeros((2 * WINDOW, HEAD_DIM), F32)
            dv_g = jnp.zeros((2 * WINDOW, HEAD_DIM), F32)
            for j in range(qpk):
                h = g * qpk + j
                hs = slice(h * HEAD_DIM, (h + 1) * HEAD_DIM)
                qh, doh = q[:, hs], dov[:, hs]
                pn, psink = _head_probs(qh, kg, ok, s_ref[h])
                delta = jnp.sum(doh.astype(F32) * ov[:, hs].astype(F32), axis=1, keepdims=True)
                dp = lax.dot_general(doh, vg, _DIMS["nt"], preferred_element_type=F32)
                dsb = (pn * (dp - delta)).astype(BF16)
                dsink = dsink + jnp.where(row0 & (lane == h), -jnp.sum(psink * delta, axis=0, keepdims=True), 0.0)
                dq_parts.append(jnp.dot(dsb, kg, preferred_element_type=F32) * ATTN_SCALE)
                dk_g = dk_g + lax.dot_general(dsb, qh, _DIMS["tn"], preferred_element_type=F32) * ATTN_SCALE
                dv_g = dv_g + lax.dot_general(pn.astype(BF16), doh, _DIMS["tn"], preferred_element_type=F32)
            dk_parts.append(dk_g)
            dv_parts.append(dv_g)
        ds_ref[...] += dsink
        dq_ref[...] = _rope(jnp.concatenate(dq_parts, axis=1), tq, inverse=True).astype(BF16)
        dk2 = jnp.concatenate(dk_parts, axis=1)
        dv2 = jnp.concatenate(dv_parts, axis=1)
        dk_prev = _rope(ck_ref[...] + dk2[:WINDOW], tp, inverse=True)
        dv_prev = cv_ref[...] + dv2[:WINDOW]

        @pl.when(i > 0)
        def _():
            dkv_ref[pl.ds(pl.multiple_of((i - 1) * WINDOW, WINDOW), WINDOW), :] = jnp.concatenate(
                [dk_prev, dv_prev], axis=1).astype(BF16)

        ck_ref[...] = dk2[WINDOW:]
        cv_ref[...] = dv2[WINDOW:]

        @pl.when(i == nb - 1)
        def _():
            dkv_ref[pl.ds(pl.multiple_of(i * WINDOW, WINDOW), WINDOW), :] = jnp.concatenate(
                [_rope(dk2[WINDOW:], tq, inverse=True), dv2[WINDOW:]], axis=1).astype(BF16)

    blk = pl.BlockSpec((WINDOW, aw), lambda i: (i, 0))
    return pl.pallas_call(
        body,
        name=name,
        grid=(nb,),
        in_specs=_attn_specs(aw, kw) + [blk, blk],
        out_specs=[blk, pl.BlockSpec((s, 2 * kw), lambda i: (0, 0)), pl.BlockSpec((SUBLANES, LANES), lambda i: (0, 0))],
        out_shape=[_sds((s, aw), BF16), _sds((s, 2 * kw), BF16), _sds((SUBLANES, LANES), F32)],
        scratch_shapes=[pltpu.VMEM((WINDOW, kw), F32), pltpu.VMEM((WINDOW, kw), F32)],
        compiler_params=_params("arbitrary"),
    )(sinks, qkv, qkv, qkv, qkv, qkv, tabs, tabs, o, do)


_INV_SQRT2 = 1.0 / math.sqrt(2.0)
_INV_SQRT2PI = 1.0 / math.sqrt(2.0 * math.pi)


def _gelu(x):
    return x * (lax.erf(x * _INV_SQRT2) + 1.0) * 0.5


def _gelu_grad(x):
    return 0.5 * (lax.erf(x * _INV_SQRT2) + 1.0) + x * jnp.exp(-0.5 * x * x) * _INV_SQRT2PI


def _sgu_norm(pv, lg, lb):
    zv = _gelu(pv)
    mu = jnp.mean(zv, axis=-1, keepdims=True)
    cen = zv - mu
    rs = lax.rsqrt(jnp.mean(cen * cen, axis=-1, keepdims=True) + EPS)
    xhat = cen * rs
    return xhat, rs, (xhat * lg + lb).astype(BF16)


def _causal(w, upper=False):
    t = lax.broadcasted_iota(jnp.int32, (CHUNK, CHUNK), 0)
    u = lax.broadcasted_iota(jnp.int32, (CHUNK, CHUNK), 1)
    return jnp.where((u >= t) if upper else (t >= u), w, 0.0).astype(BF16)


def _sgu_fwd(pz, lg, lb, w, bt, name):
    s, sw = pz.shape[0], pz.shape[1] // 2
    groups = sw // GROUP_DIM

    def body(i, ins, consts, outs, accs):
        lgv, lbv, w_ref, btv = consts[0][...], consts[1][...], consts[2], consts[3][...]
        zu = _gelu(ins[0][:, :sw].astype(F32))
        _, _, vn = _sgu_norm(ins[0][:, sw:].astype(F32), lgv, lbv)
        for g in range(groups):
            gs = slice(g * GROUP_DIM, (g + 1) * GROUP_DIM)
            sv = jnp.dot(_causal(w_ref[g]), vn[:, gs], preferred_element_type=F32) + btv[:, g:g + 1]
            outs[0][:, gs] = (zu[:, gs] * sv).astype(BF16)

    return _rowwise(body, name, s, CHUNK, [pz], [lg, lb, w, bt], [_sds((s, sw), BF16)])[0]


def _sgu_bwd(pz, dy, lg, lb, w, wt, bt, name):
    s, sw = pz.shape[0], pz.shape[1] // 2
    groups = sw // GROUP_DIM

    def body(i, ins, consts, outs, accs):
        lgv, lbv, w_ref, wt_ref, btv = consts[0][...], consts[1][...], consts[2], consts[3], consts[4][...]

        @pl.when(i == 0)
        def _():
            for a in accs:
                a[...] = jnp.zeros_like(a)

        pu, pv = ins[0][:, :sw].astype(F32), ins[0][:, sw:].astype(F32)
        dyv = ins[1][...].astype(F32)
        zu = _gelu(pu)
        xhat, rs, vn = _sgu_norm(pv, lgv, lbv)
        dvn_parts, db_parts = [], []
        lower = lax.broadcasted_iota(jnp.int32, (CHUNK, CHUNK), 0) >= lax.broadcasted_iota(jnp.int32, (CHUNK, CHUNK), 1)
        for g in range(groups):
            gs = slice(g * GROUP_DIM, (g + 1) * GROUP_DIM)
            sv = jnp.dot(_causal(w_ref[g]), vn[:, gs], preferred_element_type=F32) + btv[:, g:g + 1]
            outs[0][:, gs] = (dyv[:, gs] * sv * _gelu_grad(pu[:, gs])).astype(BF16)
            dsv = dyv[:, gs] * zu[:, gs]
            dsvb = dsv.astype(BF16)
            db_parts.append(jnp.sum(dsv, axis=1, keepdims=True))
            accs[2][g] += jnp.where(lower, lax.dot_general(dsvb, vn[:, gs], _DIMS["nt"], preferred_element_type=F32), 0.0)
            dvn_parts.append(jnp.dot(_causal(wt_ref[g], upper=True), dsvb, preferred_element_type=F32))
        dvn = jnp.concatenate(dvn_parts, axis=1)
        accs[3][...] += jnp.concatenate(db_parts, axis=1)
        accs[0][...] += jnp.sum(dvn * xhat, axis=0, keepdims=True)
        accs[1][...] += jnp.sum(dvn, axis=0, keepdims=True)
        dxh = dvn * lgv
        dz = rs * (dxh - jnp.mean(dxh, axis=-1, keepdims=True) - xhat * jnp.mean(dxh * xhat, axis=-1, keepdims=True))
        outs[0][:, sw:] = (dz * _gelu_grad(pv)).astype(BF16)

    return _rowwise(body, name, s, CHUNK, [pz, dy], [lg, lb, w, wt, bt], [_sds((s, 2 * sw), BF16)],
                    [_sds((1, sw), F32), _sds((1, sw), F32), _sds((groups, CHUNK, CHUNK), F32), _sds((CHUNK, groups), F32)])


def _mesh_place():
    x, y, c = lax.axis_index("x"), lax.axis_index("y"), lax.axis_index("c")
    return x, y, c, 4 * x + 2 * y + c


def _peer(x, y, c, k):
    px, py, pc = x ^ ((k >> 2) & 1), y ^ ((k >> 1) & 1), c ^ (k & 1)
    return (px, py, pc), 4 * px + 2 * py + pc


def _all_gather(shards, name):
    n = len(shards)

    def body(*refs):
        ins, outs = refs[:n], refs[n:2 * n]
        send_sems, recv_sems, local_sems = refs[2 * n:]
        x, y, c, me = _mesh_place()
        copies = []
        for t in range(n):
            own = pltpu.make_async_copy(ins[t], outs[t].at[me], local_sems.at[t])
            own.start()
            copies.append(own)
            for k in range(1, N_DEV):
                peer, pidx = _peer(x, y, c, k)
                cp = pltpu.make_async_remote_copy(
                    src_ref=ins[t], dst_ref=outs[t].at[me], send_sem=send_sems.at[t, k - 1],
                    recv_sem=recv_sems.at[t, k - 1], device_id=peer, device_id_type=MESH_TYPE)
                cp.start()
                copies.append((cp, t, k, pidx))
        for item in copies:
            if isinstance(item, tuple):
                cp, t, k, pidx = item
                cp.wait_send()
                pltpu.make_async_remote_copy(
                    src_ref=ins[t], dst_ref=outs[t].at[pidx], send_sem=send_sems.at[t, k - 1],
                    recv_sem=recv_sems.at[t, k - 1], device_id=(x, y, c), device_id_type=MESH_TYPE).wait_recv()
            else:
                item.wait()

    return pl.pallas_call(
        body,
        name=name,
        in_specs=[ANY] * n,
        out_specs=[ANY] * n,
        out_shape=[_sds((N_DEV,) + a.shape, a.dtype) for a in shards],
        scratch_shapes=[pltpu.SemaphoreType.DMA((n, N_DEV - 1)), pltpu.SemaphoreType.DMA((n, N_DEV - 1)),
                        pltpu.SemaphoreType.DMA((n,))],
    )(*shards)


def _scatter_partials(sends, recvs, layer, name):
    n = len(sends)

    def body(*refs):
        ins, outs = refs[:n], refs[2 * n:3 * n]
        send_sems, recv_sems, local_sems = refs[3 * n:]
        x, y, c, me = _mesh_place()
        copies = []
        for t in range(n):
            own = pltpu.make_async_copy(ins[t].at[me], outs[t].at[layer, me], local_sems.at[t])
            own.start()
            copies.append(own)
            for k in range(1, N_DEV):
                peer, pidx = _peer(x, y, c, k)
                cp = pltpu.make_async_remote_copy(
                    src_ref=ins[t].at[pidx], dst_ref=outs[t].at[layer, me], send_sem=send_sems.at[t, k - 1],
                    recv_sem=recv_sems.at[t, k - 1], device_id=peer, device_id_type=MESH_TYPE)
                cp.start()
                copies.append((cp, t, k, pidx))
        for item in copies:
            if isinstance(item, tuple):
                cp, t, k, pidx = item
                cp.wait_send()
                pltpu.make_async_remote_copy(
                    src_ref=ins[t].at[pidx], dst_ref=outs[t].at[layer, pidx], send_sem=send_sems.at[t, k - 1],
                    recv_sem=recv_sems.at[t, k - 1], device_id=(x, y, c), device_id_type=MESH_TYPE).wait_recv()
            else:
                item.wait()

    return pl.pallas_call(
        body,
        name=name,
        in_specs=[ANY] * (2 * n),
        out_specs=[ANY] * n,
        out_shape=[_sds(r.shape, r.dtype) for r in recvs],
        input_output_aliases={n + t: t for t in range(n)},
        scratch_shapes=[pltpu.SemaphoreType.DMA((n, N_DEV - 1)), pltpu.SemaphoreType.DMA((n, N_DEV - 1)),
                        pltpu.SemaphoreType.DMA((n,))],
    )(*sends, *recvs)


def _adamw(w, g, m, v):
    m = ADAM_B1 * m + (1.0 - ADAM_B1) * g
    v = ADAM_B2 * v + (1.0 - ADAM_B2) * (g * g)
    m_hat = m / (1.0 - ADAM_B1 ** ADAM_STEP)
    v_hat = v / (1.0 - ADAM_B2 ** ADAM_STEP)
    delta = -ADAM_LR * (m_hat / (jnp.sqrt(v_hat) + ADAM_EPS) + ADAM_WD * w)
    return delta, m, v


def _reduce_adam(parts, w, m, v, name):
    nl, _, r, c = parts.shape
    fits = [t for t in range(BF16_SUBLANES, r + 1, BF16_SUBLANES) if r % t == 0 and t * c <= ADAM_BLOCK_ELEMS]
    tr = max(fits) if fits else r

    def body(p_ref, w_ref, m_ref, v_ref, g_out, d_out, m_out, v_out):
        g = p_ref[0].astype(F32)
        for d in range(1, N_DEV):
            g = g + p_ref[d].astype(F32)
        delta, mn, vn = _adamw(w_ref[...], g, m_ref[...], v_ref[...])
        g_out[...] = g
        d_out[...] = delta
        m_out[...] = mn
        v_out[...] = vn

    blk = pl.BlockSpec((None, tr, c), lambda l, i: (l, i, 0))
    out = _sds((nl, r, c), F32)
    return pl.pallas_call(
        body,
        name=name,
        grid=(nl, r // tr),
        in_specs=[pl.BlockSpec((None, N_DEV, tr, c), lambda l, i: (l, 0, i, 0)), blk, blk, blk],
        out_specs=[blk, blk, blk, blk],
        out_shape=[out, out, out, out],
        compiler_params=_params("parallel", "parallel"),
    )(parts, w, m, v)


def _pack(arrays):
    flat = []
    for a in arrays:
        a = a.reshape(-1).astype(F32)
        flat.append(jnp.pad(a, (0, (-a.shape[0]) % PACK_UNIT)))
    out = jnp.concatenate(flat)
    rows = out.shape[0] // LANES
    pad_rows = (-rows) % 512
    return jnp.pad(out, (0, pad_rows * LANES)).reshape(rows + pad_rows, LANES)


def _unpack(packed, shapes):
    flat = packed.reshape(-1)
    out, off = [], 0
    for shp in shapes:
        size = math.prod(shp)
        out.append(flat[off:off + size].reshape(shp))
        off += size + (-size) % PACK_UNIT
    return out


def _to_full_cols(g):
    d, k, n = g.shape
    return jnp.transpose(g, (1, 0, 2)).reshape(k, d * n)


def _to_col_shards(a):
    k, n = a.shape
    return jnp.transpose(a.reshape(k, N_DEV, n // N_DEV), (1, 0, 2))


def kernel(x, positions, norm1_g, w_in, b_in, sinks, sgu_ln_g, sgu_ln_b, sgu_w, sgu_b, w_attn_branch, w_sgu_branch, w_out, norm2_g, w_gate_up, w_down, final_g, loss_target, m_norm1_g, m_w_in, m_b_in, m_sinks, m_sgu_ln_g, m_sgu_ln_b, m_sgu_w, m_sgu_b, m_w_attn_branch, m_w_sgu_branch, m_w_out, m_norm2_g, m_w_gate_up, m_w_down, m_final_g, v_norm1_g, v_w_in, v_b_in, v_sinks, v_sgu_ln_g, v_sgu_ln_b, v_sgu_w, v_sgu_b, v_w_attn_branch, v_w_sgu_branch, v_w_out, v_norm2_g, v_w_gate_up, v_w_down, v_final_g):
    nl = w_in.shape[0]
    s, d = x.shape[1], x.shape[2]
    aw = w_attn_branch.shape[1]
    sw = w_sgu_branch.shape[1]
    in_w = w_in.shape[2] * N_DEV
    kw = (in_w - aw - 2 * sw - 2 * d) // 2
    qkv_w = aw + 2 * kw
    groups = sw // GROUP_DIM
    ff = w_down.shape[1] * N_DEV

    h = x.reshape(s, d)
    target = loss_target.reshape(s, d)
    tabs = _rope_tables(positions.reshape(s, 1), "rope_tables")

    big = [w_in, w_attn_branch, w_sgu_branch, w_out, w_gate_up, w_down]
    big_m = [m_w_in, m_w_attn_branch, m_w_sgu_branch, m_w_out, m_w_gate_up, m_w_down]
    big_v = [v_w_in, v_w_attn_branch, v_w_sgu_branch, v_w_out, v_w_gate_up, v_w_down]

    saved = []
    for l in range(nl):
        gathered = _all_gather([w[l].astype(BF16) for w in big], f"gather_weights_l{l}")
        w_in_f = _to_full_cols(gathered[0])
        wts = dict(
            qkv=w_in_f[:, :qkv_w], z=w_in_f[:, qkv_w:qkv_w + 2 * sw], g=w_in_f[:, qkv_w + 2 * sw:],
            ab=_to_full_cols(gathered[1]), sb=_to_full_cols(gathered[2]), out=gathered[3].reshape(d, d),
            gu=_to_full_cols(gathered[4]), down=gathered[5].reshape(ff, d))
        bias = b_in[l].reshape(1, in_w)
        g1, g2 = norm1_g[l].reshape(1, d), norm2_g[l].reshape(1, d)
        lg, lb = sgu_ln_g[l].reshape(1, sw), sgu_ln_b[l].reshape(1, sw)
        bt = sgu_b[l].T

        xn = _rms_fwd(h, g1, "rms1_fwd")
        qkv = _matmul(xn, wts["qkv"], "nn", BF16, "proj_qkv", bias=bias[:, :qkv_w])
        pz = _matmul(xn, wts["z"], "nn", BF16, "proj_z", bias=bias[:, qkv_w:qkv_w + 2 * sw])
        pg = _matmul(xn, wts["g"], "nn", BF16, "proj_g", bias=bias[:, qkv_w + 2 * sw:])
        y_attn = _attn_fwd(qkv, tabs, sinks[l], aw, kw, "attn_fwd")
        y_sgu = _sgu_fwd(pz, lg, lb, sgu_w[l], bt, "sgu_fwd")
        a_br = _matmul(y_attn, wts["ab"], "nn", BF16, "attn_branch")
        s_br = _matmul(y_sgu, wts["sb"], "nn", BF16, "sgu_branch")
        merged = _merge_fwd(pg, a_br, s_br, "merge_fwd")
        h_mid = _matmul(merged, wts["out"], "nn", F32, "out_proj", res=h)
        hn = _rms_fwd(h_mid, g2, "rms2_fwd")
        gu = _matmul(hn, wts["gu"], "nn", BF16, "gate_up")
        act = _swiglu_fwd(gu, "swiglu_fwd")
        h_out = _matmul(act, wts["down"], "nn", F32, "down_proj", res=h_mid)
        saved.append(dict(wts=wts, h=h, xn=xn, qkv=qkv, pz=pz, pg=pg, y_attn=y_attn, y_sgu=y_sgu, a_br=a_br,
                          s_br=s_br, merged=merged, h_mid=h_mid, hn=hn, gu=gu, act=act,
                          g1=g1, g2=g2, lg=lg, lb=lb, bt=bt))
        h = h_out

    dh, d_final_g, loss_blk = _loss_head(h, final_g.reshape(1, d), target, "loss_head")

    recvs = [lax.empty((nl, N_DEV) + w.shape[1:], BF16) for w in big]
    small = {n: [None] * nl for n in ("norm1_g", "b_in", "sinks", "sgu_ln_g", "sgu_ln_b", "sgu_w", "sgu_b", "norm2_g")}
    for l in reversed(range(nl)):
        sv = saved[l]
        wts = sv["wts"]
        dhb = dh.astype(BF16)
        d_act = _matmul(dhb, wts["down"], "nt", BF16, "d_act")
        dw_down = _matmul(sv["act"], dhb, "tn", BF16, "dw_down")
        d_gu = _swiglu_bwd(sv["gu"], d_act, "swiglu_bwd")
        dw_gu = _matmul(sv["hn"], d_gu, "tn", BF16, "dw_gate_up")
        d_hn = _matmul(d_gu, wts["gu"], "nt", BF16, "d_hn")
        dh_mid, dg2 = _rms_bwd(sv["h_mid"], sv["g2"], d_hn, dh, "rms2_bwd")
        dmb = dh_mid.astype(BF16)
        d_merged = _matmul(dmb, wts["out"], "nt", BF16, "d_merged")
        dw_out = _matmul(sv["merged"], dmb, "tn", BF16, "dw_out")
        d_a, d_s, d_pg = _merge_bwd(sv["pg"], sv["a_br"], sv["s_br"], d_merged, "merge_bwd")
        d_y_attn = _matmul(d_a, wts["ab"], "nt", BF16, "d_y_attn")
        dw_ab = _matmul(sv["y_attn"], d_a, "tn", BF16, "dw_attn_branch")
        d_y_sgu = _matmul(d_s, wts["sb"], "nt", BF16, "d_y_sgu")
        dw_sb = _matmul(sv["y_sgu"], d_s, "tn", BF16, "dw_sgu_branch")
        d_pz, d_lg, d_lb, d_sw, d_sbt = _sgu_bwd(sv["pz"], d_y_sgu, sv["lg"], sv["lb"], sgu_w[l],
                                                 jnp.transpose(sgu_w[l], (0, 2, 1)), sv["bt"], "sgu_bwd")
        d_q, d_kv, d_sinks = _attn_bwd(sv["qkv"], tabs, sinks[l], sv["y_attn"], d_y_attn, aw, kw, "attn_bwd")
        d_qkv = jnp.concatenate([d_q, d_kv], axis=1)
        dw_qkv = _matmul(sv["xn"], d_qkv, "tn", BF16, "dw_qkv")
        dw_z = _matmul(sv["xn"], d_pz, "tn", BF16, "dw_z")
        dw_g = _matmul(sv["xn"], d_pg, "tn", BF16, "dw_g")
        d_xn = _matmul(d_qkv, wts["qkv"], "nt", F32, "d_xn_qkv")
        d_xn = _matmul(d_pz, wts["z"], "nt", F32, "d_xn_z", res=d_xn)
        d_xn = _matmul(d_pg, wts["g"], "nt", F32, "d_xn_g", res=d_xn)
        dh, dg1 = _rms_bwd(sv["h"], sv["g1"], d_xn, dh_mid, "rms1_bwd")

        small["norm1_g"][l], small["norm2_g"][l] = dg1, dg2
        small["b_in"][l] = jnp.concatenate([_colsum(d_qkv, "db_qkv"), _colsum(d_pz, "db_z"), _colsum(d_pg, "db_g")], axis=1)
        small["sinks"][l] = d_sinks[0, :aw // HEAD_DIM]
        small["sgu_ln_g"][l], small["sgu_ln_b"][l] = d_lg, d_lb
        small["sgu_w"][l] = d_sw
        small["sgu_b"][l] = d_sbt.T

        sends = [_to_col_shards(jnp.concatenate([dw_qkv, dw_z, dw_g], axis=1)), _to_col_shards(dw_ab),
                 _to_col_shards(dw_sb), dw_out.reshape(N_DEV, d // N_DEV, d), _to_col_shards(dw_gu),
                 dw_down.reshape(N_DEV, ff // N_DEV, d)]
        recvs = _scatter_partials(sends, recvs, l, f"scatter_grads_l{l}")

    grad_x = dh.reshape(x.shape)

    big_out = [_reduce_adam(recvs[t], big[t], big_m[t], big_v[t], f"adam_{n}")
               for t, n in enumerate(("w_in", "w_attn_branch", "w_sgu_branch", "w_out", "w_gate_up", "w_down"))]

    names = ["norm1_g", "b_in", "sinks", "sgu_ln_g", "sgu_ln_b", "sgu_w", "sgu_b", "norm2_g"]
    small_w = [norm1_g, b_in, sinks, sgu_ln_g, sgu_ln_b, sgu_w, sgu_b, norm2_g, final_g]
    small_m = [m_norm1_g, m_b_in, m_sinks, m_sgu_ln_g, m_sgu_ln_b, m_sgu_w, m_sgu_b, m_norm2_g, m_final_g]
    small_v = [v_norm1_g, v_b_in, v_sinks, v_sgu_ln_g, v_sgu_ln_b, v_sgu_w, v_sgu_b, v_norm2_g, v_final_g]
    shapes = [w.shape for w in small_w] + [(1,)]
    partial = [jnp.stack([p.reshape(w.shape[1:]) for p in small[n]]) for n, w in zip(names, small_w)]
    partial += [d_final_g.reshape(final_g.shape), loss_blk[0, :1]]
    zero = jnp.zeros((1,), F32)
    pk_g = _pack(partial)
    (all_g,) = _all_gather([pk_g], "gather_small_grads")
    sm = _reduce_adam(all_g[None], _pack(small_w + [zero])[None], _pack(small_m + [zero])[None],
                      _pack(small_v + [zero])[None], "adam_small")
    sm_g, sm_d, sm_m, sm_v = [_unpack(a[0], shapes) for a in sm]
    loss = sm_g[-1].reshape(())

    def ordered(kind_small, kind_big):
        by_name = dict(zip(["norm1_g", "b_in", "sinks", "sgu_ln_g", "sgu_ln_b", "sgu_w", "sgu_b", "norm2_g", "final_g"], kind_small))
        by_name.update(zip(["w_in", "w_attn_branch", "w_sgu_branch", "w_out", "w_gate_up", "w_down"], kind_big))
        order = ["norm1_g", "w_in", "b_in", "sinks", "sgu_ln_g", "sgu_ln_b", "sgu_w", "sgu_b", "w_attn_branch",
                 "w_sgu_branch", "w_out", "norm2_g", "w_gate_up", "w_down", "final_g"]
        return [by_name[n] for n in order]

    outs = [loss, grad_x]
    for idx, sm_kind in enumerate((sm_g, sm_d, sm_m, sm_v)):
        outs += ordered(sm_kind[:-1], [o[idx] for o in big_out])
    return tuple(outs)
```

```python
import math

import jax
import jax.numpy as jnp
from jax import lax
from jax.experimental import pallas as pl
from jax.experimental.pallas import tpu as pltpu

F32 = jnp.float32
BF16 = jnp.bfloat16

N_DEV = 8
HEAD_DIM = 64
WINDOW = 128
CHUNK = 128
GROUP_DIM = 128
ROPE_DIM = HEAD_DIM // 4
ROPE_HALF = ROPE_DIM // 2
ROPE_THETA = 500000.0
EPS = 1e-5
NEG = -1e30
ATTN_SCALE = HEAD_DIM ** -0.5
ADAM_LR = 0.001
ADAM_B1 = 0.9
ADAM_B2 = 0.999
ADAM_EPS = 1e-08
ADAM_WD = 0.01
ADAM_STEP = 10
LANES = 128
SUBLANES = 8
BF16_SUBLANES = 16
PACK_UNIT = SUBLANES * LANES
ADAM_BLOCK_ELEMS = 256 * 1024
V7X_VMEM_LIMIT_BYTES = 56 * 1024 * 1024
MATMUL_TILE_PREFS = (1024, 1408, 768, 512, 384, 256, 128)
ROW_TILE_PREFS = (512, 256, 128)
MESH_TYPE = pl.DeviceIdType.MESH
ANY = pl.BlockSpec(memory_space=pl.ANY)
HBM = pl.BlockSpec(memory_space=pltpu.HBM)
SEM = pl.BlockSpec(memory_space=pltpu.SEMAPHORE)
DATAFLOW_EFFECT = pltpu.SideEffectType.DATAFLOW_SIDE_EFFECTING


def _pick(n, prefs):
    for p in prefs:
        if n % p == 0:
            return p
    return n


def _params(*sem):
    return pltpu.CompilerParams(dimension_semantics=sem, vmem_limit_bytes=V7X_VMEM_LIMIT_BYTES)


_DIMS = {"nn": (((1,), (0,)), ((), ())), "nt": (((1,), (1,)), ((), ())), "tn": (((0,), (0,)), ((), ()))}


def _matmul(a, b, mode, out_dtype, name, bias=None, res=None, after=None):
    if mode == "nn":
        (m, k), n = a.shape, b.shape[1]
    elif mode == "nt":
        (m, k), n = a.shape, b.shape[0]
    else:
        (k, m), n = a.shape, b.shape[1]
    tm, tn, tk = _pick(m, MATMUL_TILE_PREFS), _pick(n, MATMUL_TILE_PREFS), _pick(k, MATMUL_TILE_PREFS)
    nk = k // tk
    dims = _DIMS[mode]
    a_spec = pl.BlockSpec((tk, tm), lambda i, j, kk: (kk, i)) if mode == "tn" else pl.BlockSpec((tm, tk), lambda i, j, kk: (i, kk))
    b_spec = pl.BlockSpec((tn, tk), lambda i, j, kk: (j, kk)) if mode == "nt" else pl.BlockSpec((tk, tn), lambda i, j, kk: (kk, j))
    in_specs, args = [a_spec, b_spec], [a, b]
    if bias is not None:
        in_specs.append(pl.BlockSpec((1, tn), lambda i, j, kk: (0, j)))
        args.append(bias)
    if res is not None:
        in_specs.append(pl.BlockSpec((tm, tn), lambda i, j, kk: (i, j)))
        args.append(res)
    if after is not None:
        in_specs.append(ANY)
        args.append(after)

    def body(*refs):
        a_ref, b_ref = refs[0], refs[1]
        pos = 2
        bias_ref = res_ref = None
        if bias is not None:
            bias_ref = refs[pos]
            pos += 1
        if res is not None:
            res_ref = refs[pos]
            pos += 1
        if after is not None:
            pos += 1
        o_ref = refs[pos]

        def finish(r):
            if bias_ref is not None:
                r = r + bias_ref[...]
            if res_ref is not None:
                r = r + res_ref[...]
            o_ref[...] = r.astype(out_dtype)

        part = lax.dot_general(a_ref[...], b_ref[...], dims, preferred_element_type=F32)
        if nk == 1:
            finish(part)
        else:
            acc_ref = refs[pos + 1]
            kk = pl.program_id(2)

            @pl.when(kk == 0)
            def _():
                acc_ref[...] = part

            @pl.when(kk > 0)
            def _():
                acc_ref[...] += part

            @pl.when(kk == nk - 1)
            def _():
                finish(acc_ref[...])

    return pl.pallas_call(
        body,
        name=name,
        grid=(m // tm, n // tn, nk),
        in_specs=in_specs,
        out_specs=pl.BlockSpec((tm, tn), lambda i, j, kk: (i, j)),
        out_shape=jax.ShapeDtypeStruct((m, n), out_dtype),
        scratch_shapes=[] if nk == 1 else [pltpu.VMEM((tm, tn), F32)],
        compiler_params=_params("parallel", "parallel", "arbitrary"),
    )(*args)


def _rowwise(body, name, rows, tr, ins, consts, outs, accs=(), after=()):
    n_in, n_c, n_o, n_a = len(ins), len(consts), len(outs), len(after)

    def wrapped(*refs):
        body(pl.program_id(0), refs[:n_in], refs[n_in:n_in + n_c], refs[n_in + n_c + n_a:n_in + n_c + n_a + n_o],
             refs[n_in + n_c + n_a + n_o:])

    def whole(shape):
        zeros = (0,) * len(shape)
        return pl.BlockSpec(tuple(shape), lambda i: zeros)

    in_specs = ([pl.BlockSpec((tr, a.shape[1]), lambda i: (i, 0)) for a in ins] + [whole(c.shape) for c in consts]
                + [ANY] * n_a)
    out_specs = [pl.BlockSpec((tr, o.shape[1]), lambda i: (i, 0)) for o in outs] + [whole(a.shape) for a in accs]
    return pl.pallas_call(
        wrapped,
        name=name,
        grid=(rows // tr,),
        in_specs=in_specs,
        out_specs=out_specs,
        out_shape=list(outs) + list(accs),
        compiler_params=_params("arbitrary" if accs else "parallel"),
    )(*ins, *consts, *after)


def _sds(shape, dtype):
    return jax.ShapeDtypeStruct(tuple(shape), dtype)


def _rms_fwd(h, g, name, after=()):
    s, d = h.shape
    tr = _pick(s, ROW_TILE_PREFS)

    def body(i, ins, consts, outs, accs):
        x = ins[0][...]
        r = lax.rsqrt(jnp.mean(x * x, axis=-1, keepdims=True) + EPS)
        outs[0][...] = (x * r * consts[0][...]).astype(BF16)

    return _rowwise(body, name, s, tr, [h], [g], [_sds((s, d), BF16)], after=after)[0]


def _rms_bwd(h, g, dy, dh_up, name):
    s, d = h.shape
    tr = _pick(s, ROW_TILE_PREFS)

    def body(i, ins, consts, outs, accs):
        x, dyv, up = ins[0][...], ins[1][...].astype(F32), ins[2][...]
        r = lax.rsqrt(jnp.mean(x * x, axis=-1, keepdims=True) + EPS)
        xr = x * r
        gy = dyv * consts[0][...]
        dx = r * (gy - xr * jnp.mean(gy * xr, axis=-1, keepdims=True))
        outs[0][...] = up + dx
        outs[1][...] = (up + dx).astype(BF16)

        @pl.when(i == 0)
        def _():
            accs[0][...] = jnp.zeros_like(accs[0])

        accs[0][...] += jnp.sum(dyv * xr, axis=0, keepdims=True)

    return _rowwise(body, name, s, tr, [h, dy, dh_up], [g], [_sds((s, d), F32), _sds((s, d), BF16)], [_sds((1, d), F32)])


def _loss_head(h, g, target, name):
    s, d = h.shape
    tr = _pick(s, ROW_TILE_PREFS)

    def body(i, ins, consts, outs, accs):
        x, t = ins[0][...], ins[1][...]
        gv = consts[0][...]
        r = lax.rsqrt(jnp.mean(x * x, axis=-1, keepdims=True) + EPS)
        xr = x * r
        diff = xr * gv - t
        dyv = diff * (1.0 / d)
        gy = dyv * gv
        dx = r * (gy - xr * jnp.mean(gy * xr, axis=-1, keepdims=True))
        outs[0][...] = dx
        outs[1][...] = dx.astype(BF16)

        @pl.when(i == 0)
        def _():
            accs[0][...] = jnp.zeros_like(accs[0])
            accs[1][...] = jnp.zeros_like(accs[1])

        accs[0][...] += jnp.sum(dyv * xr, axis=0, keepdims=True)
        part = 0.5 * jnp.sum(jnp.mean(diff * diff, axis=-1, keepdims=True), axis=0, keepdims=True)
        accs[1][...] += jnp.broadcast_to(part, accs[1].shape)

    return _rowwise(body, name, s, tr, [h, target], [g], [_sds((s, d), F32), _sds((s, d), BF16)],
                    [_sds((1, d), F32), _sds((SUBLANES, LANES), F32)])


def _colsum(a, name):
    s, w = a.shape
    tr = _pick(s, ROW_TILE_PREFS)

    def body(i, ins, consts, outs, accs):
        @pl.when(i == 0)
        def _():
            accs[0][...] = jnp.zeros_like(accs[0])

        accs[0][...] += jnp.sum(ins[0][...].astype(F32), axis=0, keepdims=True)

    return _rowwise(body, name, s, tr, [a], [], [], [_sds((1, w), F32)])[0]


def _sigmoid(x):
    return 1.0 / (1.0 + jnp.exp(-x))


def _merge_fwd(pg, a, bm, name):
    s, d = a.shape
    tr = _pick(s, ROW_TILE_PREFS)

    def body(i, ins, consts, outs, accs):
        p = ins[0][...].astype(F32)
        ga, gs = _sigmoid(p[:, :d]), _sigmoid(p[:, d:])
        outs[0][...] = (ga * ins[1][...].astype(F32) + gs * ins[2][...].astype(F32)).astype(BF16)

    return _rowwise(body, name, s, tr, [pg, a, bm], [], [_sds((s, d), BF16)])[0]


def _merge_bwd(pg, a, bm, dm, name):
    s, d = a.shape
    tr = _pick(s, ROW_TILE_PREFS)

    def body(i, ins, consts, outs, accs):
        p = ins[0][...].astype(F32)
        av, bv, dmv = ins[1][...].astype(F32), ins[2][...].astype(F32), ins[3][...].astype(F32)
        ga, gs = _sigmoid(p[:, :d]), _sigmoid(p[:, d:])
        outs[0][...] = (dmv * ga).astype(BF16)
        outs[1][...] = (dmv * gs).astype(BF16)
        outs[2][:, :d] = (dmv * av * ga * (1.0 - ga)).astype(BF16)
        outs[2][:, d:] = (dmv * bv * gs * (1.0 - gs)).astype(BF16)

    return _rowwise(body, name, s, tr, [pg, a, bm, dm], [],
                    [_sds((s, d), BF16), _sds((s, d), BF16), _sds((s, 2 * d), BF16)])


def _swiglu_fwd(gu, name):
    s, w = gu.shape
    f = w // 2
    tr = _pick(s, (256, 128))

    def body(i, ins, consts, outs, accs):
        gate, up = ins[0][:, :f].astype(F32), ins[0][:, f:].astype(F32)
        outs[0][...] = (gate * _sigmoid(gate) * up).astype(BF16)

    return _rowwise(body, name, s, tr, [gu], [], [_sds((s, f), BF16)])[0]


def _swiglu_bwd(gu, dact, name):
    s, w = gu.shape
    f = w // 2
    tr = _pick(s, (256, 128))

    def body(i, ins, consts, outs, accs):
        gate, up = ins[0][:, :f].astype(F32), ins[0][:, f:].astype(F32)
        da = ins[1][...].astype(F32)
        sg = _sigmoid(gate)
        outs[0][:, :f] = (da * up * sg * (1.0 + gate * (1.0 - sg))).astype(BF16)
        outs[0][:, f:] = (da * gate * sg).astype(BF16)

    return _rowwise(body, name, s, tr, [gu, dact], [], [_sds((s, w), BF16)])[0]


def _rope_tables(pos_col, name):
    s = pos_col.shape[0]
    tr = _pick(s, (1024, 512, 256, 128))
    inv = ROPE_THETA ** (-jnp.arange(0, ROPE_DIM, 2, dtype=F32) / ROPE_DIM)
    lane = jnp.arange(LANES)
    inv_lanes = inv[lane % ROPE_HALF].reshape(1, LANES)

    def body(i, ins, consts, outs, accs):
        ang = ins[0][...].astype(F32) * consts[0][...]
        c, sn = jnp.cos(ang), jnp.sin(ang)
        in_head = lax.broadcasted_iota(jnp.int32, ang.shape, 1) % HEAD_DIM
        outs[0][:, 0:LANES] = jnp.where(in_head < ROPE_DIM, c, 1.0)
        outs[0][:, LANES:2 * LANES] = jnp.where(in_head < ROPE_HALF, -sn, 0.0)
        outs[0][:, 2 * LANES:] = jnp.where((in_head >= ROPE_HALF) & (in_head < ROPE_DIM), sn, 0.0)

    return _rowwise(body, name, s, tr, [pos_col], [inv_lanes], [_sds((s, 3 * LANES), F32)])[0]


def _rope(x, tab, inverse=False):
    width = x.shape[1]
    reps = width // LANES
    c = jnp.tile(tab[:, 0:LANES], (1, reps))
    lo = jnp.tile(tab[:, LANES:2 * LANES], (1, reps))
    hi = jnp.tile(tab[:, 2 * LANES:], (1, reps))
    if inverse:
        lo, hi = -lo, -hi
    return x * c + pltpu.roll(x, width - ROPE_HALF, 1) * lo + pltpu.roll(x, ROPE_HALF, 1) * hi


def _attn_specs(aw, kw):
    kb = aw // kw
    prev = lambda i: jnp.maximum(i - 1, 0)
    return [
        pl.BlockSpec(memory_space=pltpu.SMEM),
        pl.BlockSpec((WINDOW, aw), lambda i: (i, 0)),
        pl.BlockSpec((WINDOW, kw), lambda i: (i, kb)),
        pl.BlockSpec((WINDOW, kw), lambda i: (prev(i), kb)),
        pl.BlockSpec((WINDOW, kw), lambda i: (i, kb + 1)),
        pl.BlockSpec((WINDOW, kw), lambda i: (prev(i), kb + 1)),
        pl.BlockSpec((WINDOW, 3 * LANES), lambda i: (i, 0)),
        pl.BlockSpec((WINDOW, 3 * LANES), lambda i: (prev(i), 0)),
    ]


def _attn_common(i, q_ref, kc_ref, kp_ref, vc_ref, vp_ref, tq_ref, tp_ref):
    tq, tp = tq_ref[...], tp_ref[...]
    q = _rope(q_ref[...].astype(F32), tq).astype(BF16)
    kc = _rope(kc_ref[...].astype(F32), tq)
    kp = _rope(kp_ref[...].astype(F32), tp)
    k2 = jnp.concatenate([kp, kc], axis=0).astype(BF16)
    v2 = jnp.concatenate([vp_ref[...], vc_ref[...]], axis=0)
    qi = lax.broadcasted_iota(jnp.int32, (WINDOW, 2 * WINDOW), 0)
    kj = lax.broadcasted_iota(jnp.int32, (WINDOW, 2 * WINDOW), 1)
    rel = qi + WINDOW - kj
    ok = (rel >= 0) & (rel < WINDOW) & ((kj >= WINDOW) | (i > 0))
    return q, k2, v2, ok, tq, tp


def _head_probs(qh, kg, ok, sink):
    s = lax.dot_general(qh, kg, _DIMS["nt"], preferred_element_type=F32) * ATTN_SCALE
    s = jnp.where(ok, s, NEG)
    m = jnp.maximum(jnp.max(s, axis=1, keepdims=True), sink)
    p = jnp.exp(s - m)
    es = jnp.exp(sink - m)
    inv = 1.0 / (jnp.sum(p, axis=1, keepdims=True) + es)
    return p * inv, es * inv


def _attn_fwd(qkv, tabs, sinks, aw, kw, name):
    s = qkv.shape[0]
    nq, nkv = aw // HEAD_DIM, kw // HEAD_DIM
    qpk = nq // nkv

    def body(s_ref, q_ref, kc_ref, kp_ref, vc_ref, vp_ref, tq_ref, tp_ref, o_ref):
        i = pl.program_id(0)
        q, k2, v2, ok, _, _ = _attn_common(i, q_ref, kc_ref, kp_ref, vc_ref, vp_ref, tq_ref, tp_ref)
        for h in range(nq):
            g = h // qpk
            hs, gs = slice(h * HEAD_DIM, (h + 1) * HEAD_DIM), slice(g * HEAD_DIM, (g + 1) * HEAD_DIM)
            pn, _ = _head_probs(q[:, hs], k2[:, gs], ok, s_ref[h])
            o = jnp.dot(pn.astype(BF16), v2[:, gs], preferred_element_type=F32)
            o_ref[:, hs] = o.astype(BF16)

    return pl.pallas_call(
        body,
        name=name,
        grid=(s // WINDOW,),
        in_specs=_attn_specs(aw, kw),
        out_specs=pl.BlockSpec((WINDOW, aw), lambda i: (i, 0)),
        out_shape=_sds((s, aw), BF16),
        compiler_params=_params("parallel"),
    )(sinks, qkv, qkv, qkv, qkv, qkv, tabs, tabs)


def _attn_bwd(qkv, tabs, sinks, o, do, aw, kw, name):
    s = qkv.shape[0]
    nb = s // WINDOW
    nq, nkv = aw // HEAD_DIM, kw // HEAD_DIM
    qpk = nq // nkv

    def body(s_ref, q_ref, kc_ref, kp_ref, vc_ref, vp_ref, tq_ref, tp_ref, o_ref, do_ref,
             dq_ref, dkv_ref, ds_ref, ck_ref, cv_ref):
        i = pl.program_id(0)

        @pl.when(i == 0)
        def _():
            ck_ref[...] = jnp.zeros_like(ck_ref)
            cv_ref[...] = jnp.zeros_like(cv_ref)
            ds_ref[...] = jnp.zeros_like(ds_ref)

        q, k2, v2, ok, tq, tp = _attn_common(i, q_ref, kc_ref, kp_ref, vc_ref, vp_ref, tq_ref, tp_ref)
        dov, ov = do_ref[...], o_ref[...]
        row0 = lax.broadcasted_iota(jnp.int32, (SUBLANES, LANES), 0) == 0
        lane = lax.broadcasted_iota(jnp.int32, (SUBLANES, LANES), 1)
        dsink = jnp.zeros((SUBLANES, LANES), F32)
        dq_parts, dk_parts, dv_parts = [], [], []
        for g in range(nkv):
            gs = slice(g * HEAD_DIM, (g + 1) * HEAD_DIM)
            kg, vg = k2[:, gs], v2[:, gs]
            dk_g = jnp.zeros((2 * WINDOW, HEAD_DIM), F32)
            dv_g = jnp.zeros((2 * WINDOW, HEAD_DIM), F32)
            for j in range(qpk):
                h = g * qpk + j
                hs = slice(h * HEAD_DIM, (h + 1) * HEAD_DIM)
                qh, doh = q[:, hs], dov[:, hs]
                pn, psink = _head_probs(qh, kg, ok, s_ref[h])
                delta = jnp.sum(doh.astype(F32) * ov[:, hs].astype(F32), axis=1, keepdims=True)
                dp = lax.dot_general(doh, vg, _DIMS["nt"], preferred_element_type=F32)
                dsb = (pn * (dp - delta)).astype(BF16)
                dsink = dsink + jnp.where(row0 & (lane == h), -jnp.sum(psink * delta, axis=0, keepdims=True), 0.0)
                dq_parts.append(jnp.dot(dsb, kg, preferred_element_type=F32) * ATTN_SCALE)
                dk_g = dk_g + lax.dot_general(dsb, qh, _DIMS["tn"], preferred_element_type=F32) * ATTN_SCALE
                dv_g = dv_g + lax.dot_general(pn.astype(BF16), doh, _DIMS["tn"], preferred_element_type=F32)
            dk_parts.append(dk_g)
            dv_parts.append(dv_g)
        ds_ref[...] += dsink
        dq_ref[...] = _rope(jnp.concatenate(dq_parts, axis=1), tq, inverse=True).astype(BF16)
        dk2 = jnp.concatenate(dk_parts, axis=1)
        dv2 = jnp.concatenate(dv_parts, axis=1)
        dk_prev = _rope(ck_ref[...] + dk2[:WINDOW], tp, inverse=True)
        dv_prev = cv_ref[...] + dv2[:WINDOW]

        @pl.when(i > 0)
        def _():
            dkv_ref[pl.ds(pl.multiple_of((i - 1) * WINDOW, WINDOW), WINDOW), :] = jnp.concatenate(
                [dk_prev, dv_prev], axis=1).astype(BF16)

        ck_ref[...] = dk2[WINDOW:]
        cv_ref[...] = dv2[WINDOW:]

        @pl.when(i == nb - 1)
        def _():
            dkv_ref[pl.ds(pl.multiple_of(i * WINDOW, WINDOW), WINDOW), :] = jnp.concatenate(
                [_rope(dk2[WINDOW:], tq, inverse=True), dv2[WINDOW:]], axis=1).astype(BF16)

    blk = pl.BlockSpec((WINDOW, aw), lambda i: (i, 0))
    return pl.pallas_call(
        body,
        name=name,
        grid=(nb,),
        in_specs=_attn_specs(aw, kw) + [blk, blk],
        out_specs=[blk, pl.BlockSpec((s, 2 * kw), lambda i: (0, 0)), pl.BlockSpec((SUBLANES, LANES), lambda i: (0, 0))],
        out_shape=[_sds((s, aw), BF16), _sds((s, 2 * kw), BF16), _sds((SUBLANES, LANES), F32)],
        scratch_shapes=[pltpu.VMEM((WINDOW, kw), F32), pltpu.VMEM((WINDOW, kw), F32)],
        compiler_params=_params("arbitrary"),
    )(sinks, qkv, qkv, qkv, qkv, qkv, tabs, tabs, o, do)


_INV_SQRT2 = 1.0 / math.sqrt(2.0)
_INV_SQRT2PI = 1.0 / math.sqrt(2.0 * math.pi)


def _gelu(x):
    return x * (lax.erf(x * _INV_SQRT2) + 1.0) * 0.5


def _gelu_grad(x):
    return 0.5 * (lax.erf(x * _INV_SQRT2) + 1.0) + x * jnp.exp(-0.5 * x * x) * _INV_SQRT2PI


def _sgu_norm(pv, lg, lb):
    zv = _gelu(pv)
    mu = jnp.mean(zv, axis=-1, keepdims=True)
    cen = zv - mu
    rs = lax.rsqrt(jnp.mean(cen * cen, axis=-1, keepdims=True) + EPS)
    xhat = cen * rs
    return xhat, rs, (xhat * lg + lb).astype(BF16)


def _causal(w, upper=False):
    t = lax.broadcasted_iota(jnp.int32, (CHUNK, CHUNK), 0)
    u = lax.broadcasted_iota(jnp.int32, (CHUNK, CHUNK), 1)
    return jnp.where((u >= t) if upper else (t >= u), w, 0.0).astype(BF16)


def _sgu_fwd(pz, lg, lb, w, bt, name):
    s, sw = pz.shape[0], pz.shape[1] // 2
    groups = sw // GROUP_DIM

    def body(i, ins, consts, outs, accs):
        lgv, lbv, w_ref, btv = consts[0][...], consts[1][...], consts[2], consts[3][...]
        zu = _gelu(ins[0][:, :sw].astype(F32))
        _, _, vn = _sgu_norm(ins[0][:, sw:].astype(F32), lgv, lbv)
        for g in range(groups):
            gs = slice(g * GROUP_DIM, (g + 1) * GROUP_DIM)
            sv = jnp.dot(_causal(w_ref[g]), vn[:, gs], preferred_element_type=F32) + btv[:, g:g + 1]
            outs[0][:, gs] = (zu[:, gs] * sv).astype(BF16)

    return _rowwise(body, name, s, CHUNK, [pz], [lg, lb, w, bt], [_sds((s, sw), BF16)])[0]


def _sgu_bwd(pz, dy, lg, lb, w, wt, bt, name, after=()):
    s, sw = pz.shape[0], pz.shape[1] // 2
    groups = sw // GROUP_DIM

    def body(i, ins, consts, outs, accs):
        lgv, lbv, w_ref, wt_ref, btv = consts[0][...], consts[1][...], consts[2], consts[3], consts[4][...]

        @pl.when(i == 0)
        def _():
            for a in accs:
                a[...] = jnp.zeros_like(a)

        pu, pv = ins[0][:, :sw].astype(F32), ins[0][:, sw:].astype(F32)
        dyv = ins[1][...].astype(F32)
        zu = _gelu(pu)
        xhat, rs, vn = _sgu_norm(pv, lgv, lbv)
        dvn_parts, db_parts = [], []
        lower = lax.broadcasted_iota(jnp.int32, (CHUNK, CHUNK), 0) >= lax.broadcasted_iota(jnp.int32, (CHUNK, CHUNK), 1)
        for g in range(groups):
            gs = slice(g * GROUP_DIM, (g + 1) * GROUP_DIM)
            sv = jnp.dot(_causal(w_ref[g]), vn[:, gs], preferred_element_type=F32) + btv[:, g:g + 1]
            outs[0][:, gs] = (dyv[:, gs] * sv * _gelu_grad(pu[:, gs])).astype(BF16)
            dsv = dyv[:, gs] * zu[:, gs]
            dsvb = dsv.astype(BF16)
            db_parts.append(jnp.sum(dsv, axis=1, keepdims=True))
            accs[2][g] += jnp.where(lower, lax.dot_general(dsvb, vn[:, gs], _DIMS["nt"], preferred_element_type=F32), 0.0)
            dvn_parts.append(jnp.dot(_causal(wt_ref[g], upper=True), dsvb, preferred_element_type=F32))
        dvn = jnp.concatenate(dvn_parts, axis=1)
        accs[3][...] += jnp.concatenate(db_parts, axis=1)
        accs[0][...] += jnp.sum(dvn * xhat, axis=0, keepdims=True)
        accs[1][...] += jnp.sum(dvn, axis=0, keepdims=True)
        dxh = dvn * lgv
        dz = rs * (dxh - jnp.mean(dxh, axis=-1, keepdims=True) - xhat * jnp.mean(dxh * xhat, axis=-1, keepdims=True))
        outs[0][:, sw:] = (dz * _gelu_grad(pv)).astype(BF16)

    return _rowwise(body, name, s, CHUNK, [pz, dy], [lg, lb, w, wt, bt], [_sds((s, 2 * sw), BF16)],
                    [_sds((1, sw), F32), _sds((1, sw), F32), _sds((groups, CHUNK, CHUNK), F32), _sds((CHUNK, groups), F32)],
                    after=after)


def _mesh_place():
    x, y, c = lax.axis_index("x"), lax.axis_index("y"), lax.axis_index("c")
    return x, y, c, 4 * x + 2 * y + c


def _peer(x, y, c, k):
    px, py, pc = x ^ ((k >> 2) & 1), y ^ ((k >> 1) & 1), c ^ (k & 1)
    return (px, py, pc), 4 * px + 2 * py + pc


def _exchange_copy(src_ref, land_ref, send_sems, recv_sems, t, k, place, scatter, arriving):
    x, y, c, me = place
    peer, pidx = _peer(x, y, c, k)
    return pltpu.make_async_remote_copy(
        src_ref=src_ref.at[pidx] if scatter else src_ref,
        dst_ref=land_ref.at[pidx if arriving else me],
        send_sem=send_sems[k - 1], recv_sem=recv_sems[k - 1], device_id=peer, device_id_type=MESH_TYPE)


N_PEERS = N_DEV - 1


def _exchange_start(srcs, scatter, name, after):
    n = len(srcs)
    land_shapes = [a.shape if scatter else (N_DEV,) + a.shape for a in srcs]

    def body(*refs):
        src, land = refs[:n], refs[n:2 * n]
        send_sems = refs[2 * n + 1:2 * n + 1 + N_PEERS]
        recv_sems = refs[2 * n + 1 + N_PEERS:2 * n + 1 + 2 * N_PEERS]
        token = refs[-1]
        place = _mesh_place()
        for t in range(n):
            for k in range(1, N_DEV):
                _exchange_copy(src[t], land[t], send_sems, recv_sems, t, k, place, scatter, False).start()
        token[...] = jnp.zeros_like(token)

    return pl.pallas_call(
        body,
        name=name,
        out_shape=(*[pltpu.SemaphoreType.DMA(())] * (2 * N_PEERS), *[pltpu.HBM(a.shape, a.dtype) for a in srcs],
                   *[pltpu.HBM(shp, a.dtype) for shp, a in zip(land_shapes, srcs)], _sds((SUBLANES, LANES), F32)),
        in_specs=[HBM] * (2 * n) + [ANY],
        out_specs=(*[SEM] * (2 * N_PEERS), *[HBM] * (2 * n), pl.BlockSpec(memory_space=pltpu.VMEM)),
        input_output_aliases={i: 2 * N_PEERS + i for i in range(2 * n)},
        compiler_params=pltpu.CompilerParams(has_side_effects=DATAFLOW_EFFECT),
    )(*[pltpu.with_memory_space_constraint(a, pltpu.HBM) for a in srcs],
      *[pltpu.with_memory_space_constraint(lax.empty(shp, a.dtype), pltpu.HBM) for shp, a in zip(land_shapes, srcs)],
      after)


def _exchange_wait(started, after, scatter, name):
    sems = started[:2 * N_PEERS]
    thru = started[2 * N_PEERS:-1]
    n = len(thru) // 2

    def body(*refs):
        src, land = refs[:n], refs[n:2 * n]
        send_sems = refs[2 * n:2 * n + N_PEERS]
        recv_sems = refs[2 * n + N_PEERS:2 * n + 2 * N_PEERS]
        place = _mesh_place()
        for t in range(n):
            for k in range(1, N_DEV):
                cp = _exchange_copy(src[t], land[t], send_sems, recv_sems, t, k, place, scatter, True)
                cp.wait_send()
                cp.wait_recv()

    out = pl.pallas_call(
        body,
        name=name,
        out_shape=tuple(pltpu.HBM(a.shape, a.dtype) for a in thru),
        in_specs=[HBM] * (2 * n) + [SEM] * (2 * N_PEERS) + [ANY],
        out_specs=tuple([HBM] * (2 * n)),
        input_output_aliases={i: i for i in range(2 * n)},
        compiler_params=pltpu.CompilerParams(has_side_effects=DATAFLOW_EFFECT),
    )(*thru, *sems, after)
    return out[:n], out[n:]


def _adamw(w, g, m, v):
    m = ADAM_B1 * m + (1.0 - ADAM_B1) * g
    v = ADAM_B2 * v + (1.0 - ADAM_B2) * (g * g)
    m_hat = m / (1.0 - ADAM_B1 ** ADAM_STEP)
    v_hat = v / (1.0 - ADAM_B2 ** ADAM_STEP)
    delta = -ADAM_LR * (m_hat / (jnp.sqrt(v_hat) + ADAM_EPS) + ADAM_WD * w)
    return delta, m, v


def _adam_rows(r, c):
    fits = [t for t in range(BF16_SUBLANES, r + 1, BF16_SUBLANES) if r % t == 0 and t * c <= ADAM_BLOCK_ELEMS]
    return max(fits) if fits else r


def _adam_body(p_ref, w_ref, m_ref, v_ref, g_out, d_out, m_out, v_out):
    g = p_ref[0].astype(F32)
    for d in range(1, N_DEV):
        g = g + p_ref[d].astype(F32)
    delta, mn, vn = _adamw(w_ref[...], g, m_ref[...], v_ref[...])
    g_out[...] = g
    d_out[...] = delta
    m_out[...] = mn
    v_out[...] = vn


def _reduce_adam_layer(parts, w, m, v, prev, layer, name):
    nl, r, c = w.shape
    tr = _adam_rows(r, c)
    if prev is None:
        prev = [lax.empty((nl, r, c), F32) for _ in range(4)]

    def body(p_ref, w_ref, m_ref, v_ref, *rest):
        _adam_body(p_ref, w_ref, m_ref, v_ref, *rest[4:])

    blk = pl.BlockSpec((None, tr, c), lambda i: (layer, i, 0))
    out = _sds((nl, r, c), F32)
    return pl.pallas_call(
        body,
        name=name,
        grid=(r // tr,),
        in_specs=[pl.BlockSpec((N_DEV, tr, c), lambda i: (0, i, 0)), blk, blk, blk, ANY, ANY, ANY, ANY],
        out_specs=[blk, blk, blk, blk],
        out_shape=[out, out, out, out],
        input_output_aliases={4: 0, 5: 1, 6: 2, 7: 3},
        compiler_params=_params("parallel"),
    )(parts, w, m, v, *prev)


def _reduce_adam(parts, w, m, v, name):
    nl, _, r, c = parts.shape
    tr = _adam_rows(r, c)

    def body(*refs):
        _adam_body(*refs)

    blk = pl.BlockSpec((None, tr, c), lambda l, i: (l, i, 0))
    out = _sds((nl, r, c), F32)
    return pl.pallas_call(
        body,
        name=name,
        grid=(nl, r // tr),
        in_specs=[pl.BlockSpec((None, N_DEV, tr, c), lambda l, i: (l, 0, i, 0)), blk, blk, blk],
        out_specs=[blk, blk, blk, blk],
        out_shape=[out, out, out, out],
        compiler_params=_params("parallel", "parallel"),
    )(parts, w, m, v)


def _pack(arrays):
    flat = []
    for a in arrays:
        a = a.reshape(-1).astype(F32)
        flat.append(jnp.pad(a, (0, (-a.shape[0]) % PACK_UNIT)))
    out = jnp.concatenate(flat)
    rows = out.shape[0] // LANES
    pad_rows = (-rows) % 512
    return jnp.pad(out, (0, pad_rows * LANES)).reshape(rows + pad_rows, LANES)


def _unpack(packed, shapes):
    flat = packed.reshape(-1)
    out, off = [], 0
    for shp in shapes:
        size = math.prod(shp)
        out.append(flat[off:off + size].reshape(shp))
        off += size + (-size) % PACK_UNIT
    return out


def _to_full_cols(g):
    d, k, n = g.shape
    return jnp.transpose(g, (1, 0, 2)).reshape(k, d * n)


def _to_col_shards(a):
    k, n = a.shape
    return jnp.transpose(a.reshape(k, N_DEV, n // N_DEV), (1, 0, 2))


def kernel(x, positions, norm1_g, w_in, b_in, sinks, sgu_ln_g, sgu_ln_b, sgu_w, sgu_b, w_attn_branch, w_sgu_branch, w_out, norm2_g, w_gate_up, w_down, final_g, loss_target, m_norm1_g, m_w_in, m_b_in, m_sinks, m_sgu_ln_g, m_sgu_ln_b, m_sgu_w, m_sgu_b, m_w_attn_branch, m_w_sgu_branch, m_w_out, m_norm2_g, m_w_gate_up, m_w_down, m_final_g, v_norm1_g, v_w_in, v_b_in, v_sinks, v_sgu_ln_g, v_sgu_ln_b, v_sgu_w, v_sgu_b, v_w_attn_branch, v_w_sgu_branch, v_w_out, v_norm2_g, v_w_gate_up, v_w_down, v_final_g):
    nl = w_in.shape[0]
    s, d = x.shape[1], x.shape[2]
    aw = w_attn_branch.shape[1]
    sw = w_sgu_branch.shape[1]
    in_w = w_in.shape[2] * N_DEV
    kw = (in_w - aw - 2 * sw - 2 * d) // 2
    qkv_w = aw + 2 * kw
    groups = sw // GROUP_DIM
    ff = w_down.shape[1] * N_DEV

    h = x.reshape(s, d)
    target = loss_target.reshape(s, d)
    tabs = _rope_tables(positions.reshape(s, 1), "rope_tables")
    me = 4 * lax.axis_index("x") + 2 * lax.axis_index("y") + lax.axis_index("c")

    big = [w_in, w_attn_branch, w_sgu_branch, w_out, w_gate_up, w_down]
    big_m = [m_w_in, m_w_attn_branch, m_w_sgu_branch, m_w_out, m_w_gate_up, m_w_down]
    big_v = [v_w_in, v_w_attn_branch, v_w_sgu_branch, v_w_out, v_w_gate_up, v_w_down]
    big_names = ("w_in", "w_attn_branch", "w_sgu_branch", "w_out", "w_gate_up", "w_down")
    W_IN, W_AB, W_SB, W_OUT, W_GU, W_DOWN = range(6)
    first_layer_groups = ((W_IN,), (W_AB, W_SB, W_OUT), (W_GU, W_DOWN))
    grad_groups = ((W_DOWN, W_GU), (W_OUT, W_AB, W_SB), (W_IN,))

    def fill_own(land, own):
        return lax.dynamic_update_index_in_dim(land, own, me, 0)

    def start_gather(l, group, after):
        return _exchange_start([big[t][l].astype(BF16) for t in group], False, f"gather_start_l{l}_{big_names[group[0]]}", after)

    saved = []
    started = {}
    token = h
    for group in first_layer_groups:
        started[(0, group)] = start_gather(0, group, token)
        token = started[(0, group)][-1]
    for l in range(nl):
        if l + 1 < nl:
            started[(l + 1, tuple(range(6)))] = start_gather(l + 1, tuple(range(6)), token)
            token = started[(l + 1, tuple(range(6)))][-1]
        gathered = {}

        def weight(t, after, l=l, gathered=gathered):
            if t not in gathered:
                group = next(g for (ll, g) in started if ll == l and t in g)
                srcs, lands = _exchange_wait(started.pop((l, group)), after, False, f"gather_wait_l{l}_{big_names[group[0]]}")
                for tt, sr, ld in zip(group, srcs, lands):
                    gathered[tt] = fill_own(ld, sr)
            return gathered[t]

        bias = b_in[l].reshape(1, in_w)
        g1, g2 = norm1_g[l].reshape(1, d), norm2_g[l].reshape(1, d)
        lg, lb = sgu_ln_g[l].reshape(1, sw), sgu_ln_b[l].reshape(1, sw)
        bt = sgu_b[l].T

        xn = _rms_fwd(h, g1, "rms1_fwd", after=(token,))
        w_in_f = _to_full_cols(weight(W_IN, xn))
        wts = dict(qkv=w_in_f[:, :qkv_w], z=w_in_f[:, qkv_w:qkv_w + 2 * sw], g=w_in_f[:, qkv_w + 2 * sw:])
        qkv = _matmul(xn, wts["qkv"], "nn", BF16, "proj_qkv", bias=bias[:, :qkv_w])
        pz = _matmul(xn, wts["z"], "nn", BF16, "proj_z", bias=bias[:, qkv_w:qkv_w + 2 * sw])
        pg = _matmul(xn, wts["g"], "nn", BF16, "proj_g", bias=bias[:, qkv_w + 2 * sw:])
        y_attn = _attn_fwd(qkv, tabs, sinks[l], aw, kw, "attn_fwd")
        y_sgu = _sgu_fwd(pz, lg, lb, sgu_w[l], bt, "sgu_fwd")
        wts.update(ab=_to_full_cols(weight(W_AB, y_sgu)), sb=_to_full_cols(weight(W_SB, y_sgu)),
                   out=weight(W_OUT, y_sgu).reshape(d, d))
        a_br = _matmul(y_attn, wts["ab"], "nn", BF16, "attn_branch")
        s_br = _matmul(y_sgu, wts["sb"], "nn", BF16, "sgu_branch")
        merged = _merge_fwd(pg, a_br, s_br, "merge_fwd")
        h_mid = _matmul(merged, wts["out"], "nn", F32, "out_proj", res=h)
        hn = _rms_fwd(h_mid, g2, "rms2_fwd")
        wts.update(gu=_to_full_cols(weight(W_GU, hn)), down=weight(W_DOWN, hn).reshape(ff, d))
        gu = _matmul(hn, wts["gu"], "nn", BF16, "gate_up")
        act = _swiglu_fwd(gu, "swiglu_fwd")
        h_out = _matmul(act, wts["down"], "nn", F32, "down_proj", res=h_mid)
        saved.append(dict(wts=wts, h=h, xn=xn, qkv=qkv, pz=pz, pg=pg, y_attn=y_attn, y_sgu=y_sgu, a_br=a_br,
                          s_br=s_br, merged=merged, h_mid=h_mid, hn=hn, gu=gu, act=act,
                          g1=g1, g2=g2, lg=lg, lb=lb, bt=bt))
        h = h_out

    dh, dhb, d_final_g, loss_blk = _loss_head(h, final_g.reshape(1, d), target, "loss_head")

    small = {n: [None] * nl for n in ("norm1_g", "b_in", "sinks", "sgu_ln_g", "sgu_ln_b", "sgu_w", "sgu_b", "norm2_g")}
    scattering = {}

    def start_scatter(l, group, sends, after):
        scattering[(l, group)] = _exchange_start(sends, True, f"scatter_start_l{l}_{big_names[group[0]]}", after)
        return scattering[(l, group)][-1]

    for l in reversed(range(nl)):
        sv = saved[l]
        wts = sv["wts"]
        d_act = _matmul(dhb, wts["down"], "nt", BF16, "d_act")
        dw_down = _matmul(sv["act"], dhb, "tn", BF16, "dw_down")
        d_gu = _swiglu_bwd(sv["gu"], d_act, "swiglu_bwd")
        dw_gu = _matmul(sv["hn"], d_gu, "tn", BF16, "dw_gate_up")
        token = start_scatter(l, grad_groups[0], [dw_down.reshape(N_DEV, ff // N_DEV, d), _to_col_shards(dw_gu)], token)
        d_hn = _matmul(d_gu, wts["gu"], "nt", BF16, "d_hn", after=token)
        dh_mid, dmb, dg2 = _rms_bwd(sv["h_mid"], sv["g2"], d_hn, dh, "rms2_bwd")
        d_merged = _matmul(dmb, wts["out"], "nt", BF16, "d_merged")
        dw_out = _matmul(sv["merged"], dmb, "tn", BF16, "dw_out")
        d_a, d_s, d_pg = _merge_bwd(sv["pg"], sv["a_br"], sv["s_br"], d_merged, "merge_bwd")
        d_y_attn = _matmul(d_a, wts["ab"], "nt", BF16, "d_y_attn")
        dw_ab = _matmul(sv["y_attn"], d_a, "tn", BF16, "dw_attn_branch")
        d_y_sgu = _matmul(d_s, wts["sb"], "nt", BF16, "d_y_sgu")
        dw_sb = _matmul(sv["y_sgu"], d_s, "tn", BF16, "dw_sgu_branch")
        token = start_scatter(l, grad_groups[1], [dw_out.reshape(N_DEV, d // N_DEV, d), _to_col_shards(dw_ab),
                                                  _to_col_shards(dw_sb)], token)
        d_pz, d_lg, d_lb, d_sw, d_sbt = _sgu_bwd(sv["pz"], d_y_sgu, sv["lg"], sv["lb"], sgu_w[l],
                                                 jnp.transpose(sgu_w[l], (0, 2, 1)), sv["bt"], "sgu_bwd", after=(token,))
        d_q, d_kv, d_sinks = _attn_bwd(sv["qkv"], tabs, sinks[l], sv["y_attn"], d_y_attn, aw, kw, "attn_bwd")
        d_qkv = jnp.concatenate([d_q, d_kv], axis=1)
        dw_qkv = _matmul(sv["xn"], d_qkv, "tn", BF16, "dw_qkv")
        dw_z = _matmul(sv["xn"], d_pz, "tn", BF16, "dw_z")
        dw_g = _matmul(sv["xn"], d_pg, "tn", BF16, "dw_g")
        token = start_scatter(l, grad_groups[2], [_to_col_shards(jnp.concatenate([dw_qkv, dw_z, dw_g], axis=1))], token)
        d_xn = _matmul(d_qkv, wts["qkv"], "nt", F32, "d_xn_qkv", after=token)
        d_xn = _matmul(d_pz, wts["z"], "nt", F32, "d_xn_z", res=d_xn)
        d_xn = _matmul(d_pg, wts["g"], "nt", F32, "d_xn_g", res=d_xn)
        dh, dhb, dg1 = _rms_bwd(sv["h"], sv["g1"], d_xn, dh_mid, "rms1_bwd")

        small["norm1_g"][l], small["norm2_g"][l] = dg1, dg2
        small["b_in"][l] = jnp.concatenate([_colsum(d_qkv, "db_qkv"), _colsum(d_pz, "db_z"), _colsum(d_pg, "db_g")], axis=1)
        small["sinks"][l] = d_sinks[0, :aw // HEAD_DIM]
        small["sgu_ln_g"][l], small["sgu_ln_b"][l] = d_lg, d_lb
        small["sgu_w"][l] = d_sw
        small["sgu_b"][l] = d_sbt.T

    grad_x = dh.reshape(x.shape)

    names = ["norm1_g", "b_in", "sinks", "sgu_ln_g", "sgu_ln_b", "sgu_w", "sgu_b", "norm2_g"]
    small_w = [norm1_g, b_in, sinks, sgu_ln_g, sgu_ln_b, sgu_w, sgu_b, norm2_g, final_g]
    small_m = [m_norm1_g, m_b_in, m_sinks, m_sgu_ln_g, m_sgu_ln_b, m_sgu_w, m_sgu_b, m_norm2_g, m_final_g]
    small_v = [v_norm1_g, v_b_in, v_sinks, v_sgu_ln_g, v_sgu_ln_b, v_sgu_w, v_sgu_b, v_norm2_g, v_final_g]
    shapes = [w.shape for w in small_w] + [(1,)]
    partial = [jnp.stack([p.reshape(w.shape[1:]) for p in small[n]]) for n, w in zip(names, small_w)]
    partial += [d_final_g.reshape(final_g.shape), loss_blk[0, :1]]
    zero = jnp.zeros((1,), F32)
    small_started = _exchange_start([_pack(partial)], False, "gather_start_small_grads", dhb)

    big_out = [None] * len(big)
    after = small_started[-1]
    for l in reversed(range(nl)):
        for group in grad_groups:
            srcs, lands = _exchange_wait(scattering.pop((l, group)), after, True, f"scatter_wait_l{l}_{big_names[group[0]]}")
            for t, sr, ld in zip(group, srcs, lands):
                parts = fill_own(ld, lax.dynamic_index_in_dim(sr, me, 0, keepdims=False))
                big_out[t] = _reduce_adam_layer(parts, big[t], big_m[t], big_v[t], big_out[t], l, f"adam_{big_names[t]}")
                after = big_out[t][0]

    srcs, lands = _exchange_wait(small_started, after, False, "gather_wait_small_grads")
    all_g = fill_own(lands[0], srcs[0])
    sm = _reduce_adam(all_g[None], _pack(small_w + [zero])[None], _pack(small_m + [zero])[None],
                      _pack(small_v + [zero])[None], "adam_small")
    sm_g, sm_d, sm_m, sm_v = [_unpack(a[0], shapes) for a in sm]
    loss = sm_g[-1].reshape(())

    def ordered(kind_small, kind_big):
        by_name = dict(zip(["norm1_g", "b_in", "sinks", "sgu_ln_g", "sgu_ln_b", "sgu_w", "sgu_b", "norm2_g", "final_g"], kind_small))
        by_name.update(zip(["w_in", "w_attn_branch", "w_sgu_branch", "w_out", "w_gate_up", "w_down"], kind_big))
        order = ["norm1_g", "w_in", "b_in", "sinks", "sgu_ln_g", "sgu_ln_b", "sgu_w", "sgu_b", "w_attn_branch",
                 "w_sgu_branch", "w_out", "norm2_g", "w_gate_up", "w_down", "final_g"]
        return [by_name[n] for n in order]

    outs = [loss, grad_x]
    for idx, sm_kind in enumerate((sm_g, sm_d, sm_m, sm_v)):
        outs += ordered(sm_kind[:-1], [o[idx] for o in big_out])
    return tuple(outs)
```

```python
import math

import jax
import jax.numpy as jnp
from jax import lax
from jax.experimental import pallas as pl
from jax.experimental.pallas import tpu as pltpu

F32 = jnp.float32
BF16 = jnp.bfloat16

N_DEV = 8
HEAD_DIM = 64
WINDOW = 128
CHUNK = 128
GROUP_DIM = 128
ROPE_DIM = HEAD_DIM // 4
ROPE_HALF = ROPE_DIM // 2
ROPE_THETA = 500000.0
EPS = 1e-5
NEG = -1e30
ATTN_SCALE = HEAD_DIM ** -0.5
ADAM_LR = 0.001
ADAM_B1 = 0.9
ADAM_B2 = 0.999
ADAM_EPS = 1e-08
ADAM_WD = 0.01
ADAM_STEP = 10
LANES = 128
SUBLANES = 8
BF16_SUBLANES = 16
PACK_UNIT = SUBLANES * LANES
ADAM_BLOCK_ELEMS = 256 * 1024
V7X_VMEM_LIMIT_BYTES = 56 * 1024 * 1024
MATMUL_TILE_PREFS = (1024, 1408, 768, 512, 384, 256, 128)
MATMUL_WHOLE_K = 2048
MATMUL_K_PREFS = (2816, 2048, 1536, 1408, 1024, 768, 512, 384, 256, 128)
ROW_TILE_PREFS = (512, 256, 128)
MESH_TYPE = pl.DeviceIdType.MESH
ANY = pl.BlockSpec(memory_space=pl.ANY)
HBM = pl.BlockSpec(memory_space=pltpu.HBM)
SEM = pl.BlockSpec(memory_space=pltpu.SEMAPHORE)
DATAFLOW_EFFECT = pltpu.SideEffectType.DATAFLOW_SIDE_EFFECTING


def _pick(n, prefs):
    for p in prefs:
        if n % p == 0:
            return p
    return n


def _params(*sem):
    return pltpu.CompilerParams(dimension_semantics=sem, vmem_limit_bytes=V7X_VMEM_LIMIT_BYTES)


_DIMS = {"nn": (((1,), (0,)), ((), ())), "nt": (((1,), (1,)), ((), ())), "tn": (((0,), (0,)), ((), ()))}


def _matmul(a, b, mode, out_dtype, name, bias=None, res=None, after=None):
    if mode == "nn":
        (m, k), n = a.shape, b.shape[1]
    elif mode == "nt":
        (m, k), n = a.shape, b.shape[0]
    else:
        (k, m), n = a.shape, b.shape[1]
    tm, tn = _pick(m, MATMUL_TILE_PREFS), _pick(n, MATMUL_TILE_PREFS)
    tk = k if k <= MATMUL_WHOLE_K else _pick(k, MATMUL_K_PREFS)
    nk = k // tk
    dims = _DIMS[mode]
    a_spec = pl.BlockSpec((tk, tm), lambda i, j, kk: (kk, i)) if mode == "tn" else pl.BlockSpec((tm, tk), lambda i, j, kk: (i, kk))
    b_spec = pl.BlockSpec((tn, tk), lambda i, j, kk: (j, kk)) if mode == "nt" else pl.BlockSpec((tk, tn), lambda i, j, kk: (kk, j))
    in_specs, args = [a_spec, b_spec], [a, b]
    if bias is not None:
        in_specs.append(pl.BlockSpec((1, tn), lambda i, j, kk: (0, j)))
        args.append(bias)
    if res is not None:
        in_specs.append(pl.BlockSpec((tm, tn), lambda i, j, kk: (i, j)))
        args.append(res)
    if after is not None:
        in_specs.append(ANY)
        args.append(after)

    def body(*refs):
        a_ref, b_ref = refs[0], refs[1]
        pos = 2
        bias_ref = res_ref = None
        if bias is not None:
            bias_ref = refs[pos]
            pos += 1
        if res is not None:
            res_ref = refs[pos]
            pos += 1
        if after is not None:
            pos += 1
        o_ref = refs[pos]

        def finish(r):
            if bias_ref is not None:
                r = r + bias_ref[...]
            if res_ref is not None:
                r = r + res_ref[...]
            o_ref[...] = r.astype(out_dtype)

        part = lax.dot_general(a_ref[...], b_ref[...], dims, preferred_element_type=F32)
        if nk == 1:
            finish(part)
        else:
            acc_ref = refs[pos + 1]
            kk = pl.program_id(2)

            @pl.when(kk == 0)
            def _():
                acc_ref[...] = part

            @pl.when((kk > 0) & (kk < nk - 1))
            def _():
                acc_ref[...] += part

            @pl.when(kk == nk - 1)
            def _():
                finish(acc_ref[...] + part)

    return pl.pallas_call(
        body,
        name=name,
        grid=(m // tm, n // tn, nk),
        in_specs=in_specs,
        out_specs=pl.BlockSpec((tm, tn), lambda i, j, kk: (i, j)),
        out_shape=jax.ShapeDtypeStruct((m, n), out_dtype),
        scratch_shapes=[] if nk == 1 else [pltpu.VMEM((tm, tn), F32)],
        compiler_params=_params("parallel", "parallel", "arbitrary"),
    )(*args)


def _rowwise(body, name, rows, tr, ins, consts, outs, accs=(), after=()):
    n_in, n_c, n_o, n_a = len(ins), len(consts), len(outs), len(after)

    def wrapped(*refs):
        body(pl.program_id(0), refs[:n_in], refs[n_in:n_in + n_c], refs[n_in + n_c + n_a:n_in + n_c + n_a + n_o],
             refs[n_in + n_c + n_a + n_o:])

    def whole(shape):
        zeros = (0,) * len(shape)
        return pl.BlockSpec(tuple(shape), lambda i: zeros)

    in_specs = ([pl.BlockSpec((tr, a.shape[1]), lambda i: (i, 0)) for a in ins] + [whole(c.shape) for c in consts]
                + [ANY] * n_a)
    out_specs = [pl.BlockSpec((tr, o.shape[1]), lambda i: (i, 0)) for o in outs] + [whole(a.shape) for a in accs]
    return pl.pallas_call(
        wrapped,
        name=name,
        grid=(rows // tr,),
        in_specs=in_specs,
        out_specs=out_specs,
        out_shape=list(outs) + list(accs),
        compiler_params=_params("arbitrary" if accs else "parallel"),
    )(*ins, *consts, *after)


def _sds(shape, dtype):
    return jax.ShapeDtypeStruct(tuple(shape), dtype)


def _rms_fwd(h, g, name, after=()):
    s, d = h.shape
    tr = _pick(s, ROW_TILE_PREFS)

    def body(i, ins, consts, outs, accs):
        x = ins[0][...]
        r = lax.rsqrt(jnp.mean(x * x, axis=-1, keepdims=True) + EPS)
        outs[0][...] = (x * r * consts[0][...]).astype(BF16)

    return _rowwise(body, name, s, tr, [h], [g], [_sds((s, d), BF16)], after=after)[0]


def _rms_bwd(h, g, dy, dh_up, name):
    s, d = h.shape
    tr = _pick(s, ROW_TILE_PREFS)

    def body(i, ins, consts, outs, accs):
        x, dyv, up = ins[0][...], ins[1][...].astype(F32), ins[2][...]
        r = lax.rsqrt(jnp.mean(x * x, axis=-1, keepdims=True) + EPS)
        xr = x * r
        gy = dyv * consts[0][...]
        dx = r * (gy - xr * jnp.mean(gy * xr, axis=-1, keepdims=True))
        outs[0][...] = up + dx
        outs[1][...] = (up + dx).astype(BF16)

        @pl.when(i == 0)
        def _():
            accs[0][...] = jnp.zeros_like(accs[0])

        accs[0][...] += jnp.sum(dyv * xr, axis=0, keepdims=True)

    return _rowwise(body, name, s, tr, [h, dy, dh_up], [g], [_sds((s, d), F32), _sds((s, d), BF16)], [_sds((1, d), F32)])


def _loss_head(h, g, target, name):
    s, d = h.shape
    tr = _pick(s, ROW_TILE_PREFS)

    def body(i, ins, consts, outs, accs):
        x, t = ins[0][...], ins[1][...]
        gv = consts[0][...]
        r = lax.rsqrt(jnp.mean(x * x, axis=-1, keepdims=True) + EPS)
        xr = x * r
        diff = xr * gv - t
        dyv = diff * (1.0 / d)
        gy = dyv * gv
        dx = r * (gy - xr * jnp.mean(gy * xr, axis=-1, keepdims=True))
        outs[0][...] = dx
        outs[1][...] = dx.astype(BF16)

        @pl.when(i == 0)
        def _():
            accs[0][...] = jnp.zeros_like(accs[0])
            accs[1][...] = jnp.zeros_like(accs[1])

        accs[0][...] += jnp.sum(dyv * xr, axis=0, keepdims=True)
        part = 0.5 * jnp.sum(jnp.mean(diff * diff, axis=-1, keepdims=True), axis=0, keepdims=True)
        accs[1][...] += jnp.broadcast_to(part, accs[1].shape)

    return _rowwise(body, name, s, tr, [h, target], [g], [_sds((s, d), F32), _sds((s, d), BF16)],
                    [_sds((1, d), F32), _sds((SUBLANES, LANES), F32)])


def _colsum(a, name):
    s, w = a.shape
    tr = _pick(s, ROW_TILE_PREFS)

    def body(i, ins, consts, outs, accs):
        @pl.when(i == 0)
        def _():
            accs[0][...] = jnp.zeros_like(accs[0])

        accs[0][...] += jnp.sum(ins[0][...].astype(F32), axis=0, keepdims=True)

    return _rowwise(body, name, s, tr, [a], [], [], [_sds((1, w), F32)])[0]


def _sigmoid(x):
    return 1.0 / (1.0 + jnp.exp(-x))


def _merge_fwd(pg, a, bm, name):
    s, d = a.shape
    tr = _pick(s, ROW_TILE_PREFS)

    def body(i, ins, consts, outs, accs):
        p = ins[0][...].astype(F32)
        ga, gs = _sigmoid(p[:, :d]), _sigmoid(p[:, d:])
        outs[0][...] = (ga * ins[1][...].astype(F32) + gs * ins[2][...].astype(F32)).astype(BF16)

    return _rowwise(body, name, s, tr, [pg, a, bm], [], [_sds((s, d), BF16)])[0]


def _merge_bwd(pg, a, bm, dm, name):
    s, d = a.shape
    tr = _pick(s, ROW_TILE_PREFS)

    def body(i, ins, consts, outs, accs):
        p = ins[0][...].astype(F32)
        av, bv, dmv = ins[1][...].astype(F32), ins[2][...].astype(F32), ins[3][...].astype(F32)
        ga, gs = _sigmoid(p[:, :d]), _sigmoid(p[:, d:])
        outs[0][...] = (dmv * ga).astype(BF16)
        outs[1][...] = (dmv * gs).astype(BF16)
        outs[2][:, :d] = (dmv * av * ga * (1.0 - ga)).astype(BF16)
        outs[2][:, d:] = (dmv * bv * gs * (1.0 - gs)).astype(BF16)

    return _rowwise(body, name, s, tr, [pg, a, bm, dm], [],
                    [_sds((s, d), BF16), _sds((s, d), BF16), _sds((s, 2 * d), BF16)])


def _swiglu_fwd(gu, name):
    s, w = gu.shape
    f = w // 2
    tr = _pick(s, (256, 128))

    def body(i, ins, consts, outs, accs):
        gate, up = ins[0][:, :f].astype(F32), ins[0][:, f:].astype(F32)
        outs[0][...] = (gate * _sigmoid(gate) * up).astype(BF16)

    return _rowwise(body, name, s, tr, [gu], [], [_sds((s, f), BF16)])[0]


def _swiglu_bwd(gu, dact, name):
    s, w = gu.shape
    f = w // 2
    tr = _pick(s, (256, 128))

    def body(i, ins, consts, outs, accs):
        gate, up = ins[0][:, :f].astype(F32), ins[0][:, f:].astype(F32)
        da = ins[1][...].astype(F32)
        sg = _sigmoid(gate)
        outs[0][:, :f] = (da * up * sg * (1.0 + gate * (1.0 - sg))).astype(BF16)
        outs[0][:, f:] = (da * gate * sg).astype(BF16)

    return _rowwise(body, name, s, tr, [gu, dact], [], [_sds((s, w), BF16)])[0]


def _rope_tables(pos_col, name):
    s = pos_col.shape[0]
    tr = _pick(s, (1024, 512, 256, 128))
    inv = ROPE_THETA ** (-jnp.arange(0, ROPE_DIM, 2, dtype=F32) / ROPE_DIM)
    lane = jnp.arange(LANES)
    inv_lanes = inv[lane % ROPE_HALF].reshape(1, LANES)

    def body(i, ins, consts, outs, accs):
        ang = ins[0][...].astype(F32) * consts[0][...]
        c, sn = jnp.cos(ang), jnp.sin(ang)
        in_head = lax.broadcasted_iota(jnp.int32, ang.shape, 1) % HEAD_DIM
        outs[0][:, 0:LANES] = jnp.where(in_head < ROPE_DIM, c, 1.0)
        outs[0][:, LANES:2 * LANES] = jnp.where(in_head < ROPE_HALF, -sn, 0.0)
        outs[0][:, 2 * LANES:] = jnp.where((in_head >= ROPE_HALF) & (in_head < ROPE_DIM), sn, 0.0)

    return _rowwise(body, name, s, tr, [pos_col], [inv_lanes], [_sds((s, 3 * LANES), F32)])[0]


def _rope(x, tab, inverse=False):
    width = x.shape[1]
    reps = width // LANES
    c = jnp.tile(tab[:, 0:LANES], (1, reps))
    lo = jnp.tile(tab[:, LANES:2 * LANES], (1, reps))
    hi = jnp.tile(tab[:, 2 * LANES:], (1, reps))
    if inverse:
        lo, hi = -lo, -hi
    return x * c + pltpu.roll(x, width - ROPE_HALF, 1) * lo + pltpu.roll(x, ROPE_HALF, 1) * hi


def _attn_specs(aw, kw):
    kb = aw // kw
    prev = lambda i: jnp.maximum(i - 1, 0)
    return [
        pl.BlockSpec(memory_space=pltpu.SMEM),
        pl.BlockSpec((WINDOW, aw), lambda i: (i, 0)),
        pl.BlockSpec((WINDOW, kw), lambda i: (i, kb)),
        pl.BlockSpec((WINDOW, kw), lambda i: (prev(i), kb)),
        pl.BlockSpec((WINDOW, kw), lambda i: (i, kb + 1)),
        pl.BlockSpec((WINDOW, kw), lambda i: (prev(i), kb + 1)),
        pl.BlockSpec((WINDOW, 3 * LANES), lambda i: (i, 0)),
        pl.BlockSpec((WINDOW, 3 * LANES), lambda i: (prev(i), 0)),
    ]


def _attn_common(i, q_ref, kc_ref, kp_ref, vc_ref, vp_ref, tq_ref, tp_ref):
    tq, tp = tq_ref[...], tp_ref[...]
    q = _rope(q_ref[...].astype(F32), tq).astype(BF16)
    kc = _rope(kc_ref[...].astype(F32), tq)
    kp = _rope(kp_ref[...].astype(F32), tp)
    k2 = jnp.concatenate([kp, kc], axis=0).astype(BF16)
    v2 = jnp.concatenate([vp_ref[...], vc_ref[...]], axis=0)
    qi = lax.broadcasted_iota(jnp.int32, (WINDOW, 2 * WINDOW), 0)
    kj = lax.broadcasted_iota(jnp.int32, (WINDOW, 2 * WINDOW), 1)
    rel = qi + WINDOW - kj
    ok = (rel >= 0) & (rel < WINDOW) & ((kj >= WINDOW) | (i > 0))
    return q, k2, v2, ok, tq, tp


def _head_probs(qh, kg, ok, sink):
    s = lax.dot_general(qh, kg, _DIMS["nt"], preferred_element_type=F32) * ATTN_SCALE
    s = jnp.where(ok, s, NEG)
    m = jnp.maximum(jnp.max(s, axis=1, keepdims=True), sink)
    p = jnp.exp(s - m)
    es = jnp.exp(sink - m)
    inv = 1.0 / (jnp.sum(p, axis=1, keepdims=True) + es)
    return p * inv, es * inv


def _attn_fwd(qkv, tabs, sinks, aw, kw, name):
    s = qkv.shape[0]
    nq, nkv = aw // HEAD_DIM, kw // HEAD_DIM
    qpk = nq // nkv

    def body(s_ref, q_ref, kc_ref, kp_ref, vc_ref, vp_ref, tq_ref, tp_ref, o_ref):
        i = pl.program_id(0)
        q, k2, v2, ok, _, _ = _attn_common(i, q_ref, kc_ref, kp_ref, vc_ref, vp_ref, tq_ref, tp_ref)
        for h in range(nq):
            g = h // qpk
            hs, gs = slice(h * HEAD_DIM, (h + 1) * HEAD_DIM), slice(g * HEAD_DIM, (g + 1) * HEAD_DIM)
            pn, _ = _head_probs(q[:, hs], k2[:, gs], ok, s_ref[h])
            o = jnp.dot(pn.astype(BF16), v2[:, gs], preferred_element_type=F32)
            o_ref[:, hs] = o.astype(BF16)

    return pl.pallas_call(
        body,
        name=name,
        grid=(s // WINDOW,),
        in_specs=_attn_specs(aw, kw),
        out_specs=pl.BlockSpec((WINDOW, aw), lambda i: (i, 0)),
        out_shape=_sds((s, aw), BF16),
        compiler_params=_params("parallel"),
    )(sinks, qkv, qkv, qkv, qkv, qkv, tabs, tabs)


def _attn_bwd(qkv, tabs, sinks, o, do, aw, kw, name):
    s = qkv.shape[0]
    nb = s // WINDOW
    nq, nkv = aw // HEAD_DIM, kw // HEAD_DIM
    qpk = nq // nkv

    def body(s_ref, q_ref, kc_ref, kp_ref, vc_ref, vp_ref, tq_ref, tp_ref, o_ref, do_ref,
             dq_ref, dkv_ref, ds_ref, ck_ref, cv_ref):
        i = pl.program_id(0)

        @pl.when(i == 0)
        def _():
            ck_ref[...] = jnp.zeros_like(ck_ref)
            cv_ref[...] = jnp.zeros_like(cv_ref)
            ds_ref[...] = jnp.zeros_like(ds_ref)

        q, k2, v2, ok, tq, tp = _attn_common(i, q_ref, kc_ref, kp_ref, vc_ref, vp_ref, tq_ref, tp_ref)
        dov, ov = do_ref[...], o_ref[...]
        row0 = lax.broadcasted_iota(jnp.int32, (SUBLANES, LANES), 0) == 0
        lane = lax.broadcasted_iota(jnp.int32, (SUBLANES, LANES), 1)
        dsink = jnp.zeros((SUBLANES, LANES), F32)
        dq_parts, dk_parts, dv_parts = [], [], []
        for g in range(nkv):
            gs = slice(g * HEAD_DIM, (g + 1) * HEAD_DIM)
            kg, vg = k2[:, gs], v2[:, gs]
            dk_g = jnp.zeros((2 * WINDOW, HEAD_DIM), F32)
            dv_g = jnp.zeros((2 * WINDOW, HEAD_DIM), F32)
            for j in range(qpk):
                h = g * qpk + j
                hs = slice(h * HEAD_DIM, (h + 1) * HEAD_DIM)
                qh, doh = q[:, hs], dov[:, hs]
                pn, psink = _head_probs(qh, kg, ok, s_ref[h])
                delta = jnp.sum(doh.astype(F32) * ov[:, hs].astype(F32), axis=1, keepdims=True)
                dp = lax.dot_general(doh, vg, _DIMS["nt"], preferred_element_type=F32)
                dsb = (pn * (dp - delta)).astype(BF16)
                dsink = dsink + jnp.where(row0 & (lane == h), -jnp.sum(psink * delta, axis=0, keepdims=True), 0.0)
                dq_parts.append(jnp.dot(dsb, kg, preferred_element_type=F32) * ATTN_SCALE)
                dk_g = dk_g + lax.dot_general(dsb, qh, _DIMS["tn"], preferred_element_type=F32) * ATTN_SCALE
                dv_g = dv_g + lax.dot_general(pn.astype(BF16), doh, _DIMS["tn"], preferred_element_type=F32)
            dk_parts.append(dk_g)
            dv_parts.append(dv_g)
        ds_ref[...] += dsink
        dq_ref[...] = _rope(jnp.concatenate(dq_parts, axis=1), tq, inverse=True).astype(BF16)
        dk2 = jnp.concatenate(dk_parts, axis=1)
        dv2 = jnp.concatenate(dv_parts, axis=1)
        dk_prev = _rope(ck_ref[...] + dk2[:WINDOW], tp, inverse=True)
        dv_prev = cv_ref[...] + dv2[:WINDOW]

        @pl.when(i > 0)
        def _():
            dkv_ref[pl.ds(pl.multiple_of((i - 1) * WINDOW, WINDOW), WINDOW), :] = jnp.concatenate(
                [dk_prev, dv_prev], axis=1).astype(BF16)

        ck_ref[...] = dk2[WINDOW:]
        cv_ref[...] = dv2[WINDOW:]

        @pl.when(i == nb - 1)
        def _():
            dkv_ref[pl.ds(pl.multiple_of(i * WINDOW, WINDOW), WINDOW), :] = jnp.concatenate(
                [_rope(dk2[WINDOW:], tq, inverse=True), dv2[WINDOW:]], axis=1).astype(BF16)

    blk = pl.BlockSpec((WINDOW, aw), lambda i: (i, 0))
    return pl.pallas_call(
        body,
        name=name,
        grid=(nb,),
        in_specs=_attn_specs(aw, kw) + [blk, blk],
        out_specs=[blk, pl.BlockSpec((s, 2 * kw), lambda i: (0, 0)), pl.BlockSpec((SUBLANES, LANES), lambda i: (0, 0))],
        out_shape=[_sds((s, aw), BF16), _sds((s, 2 * kw), BF16), _sds((SUBLANES, LANES), F32)],
        scratch_shapes=[pltpu.VMEM((WINDOW, kw), F32), pltpu.VMEM((WINDOW, kw), F32)],
        compiler_params=_params("arbitrary"),
    )(sinks, qkv, qkv, qkv, qkv, qkv, tabs, tabs, o, do)


_INV_SQRT2 = 1.0 / math.sqrt(2.0)
_INV_SQRT2PI = 1.0 / math.sqrt(2.0 * math.pi)


def _gelu(x):
    return x * (lax.erf(x * _INV_SQRT2) + 1.0) * 0.5


def _gelu_grad(x):
    return 0.5 * (lax.erf(x * _INV_SQRT2) + 1.0) + x * jnp.exp(-0.5 * x * x) * _INV_SQRT2PI


def _sgu_norm(pv, lg, lb):
    zv = _gelu(pv)
    mu = jnp.mean(zv, axis=-1, keepdims=True)
    cen = zv - mu
    rs = lax.rsqrt(jnp.mean(cen * cen, axis=-1, keepdims=True) + EPS)
    xhat = cen * rs
    return xhat, rs, (xhat * lg + lb).astype(BF16)


def _causal(w, upper=False):
    t = lax.broadcasted_iota(jnp.int32, (CHUNK, CHUNK), 0)
    u = lax.broadcasted_iota(jnp.int32, (CHUNK, CHUNK), 1)
    return jnp.where((u >= t) if upper else (t >= u), w, 0.0).astype(BF16)


def _sgu_fwd(pz, lg, lb, w, bt, name):
    s, sw = pz.shape[0], pz.shape[1] // 2
    groups = sw // GROUP_DIM

    def body(i, ins, consts, outs, accs):
        lgv, lbv, w_ref, btv = consts[0][...], consts[1][...], consts[2], consts[3][...]
        zu = _gelu(ins[0][:, :sw].astype(F32))
        _, _, vn = _sgu_norm(ins[0][:, sw:].astype(F32), lgv, lbv)
        for g in range(groups):
            gs = slice(g * GROUP_DIM, (g + 1) * GROUP_DIM)
            sv = jnp.dot(_causal(w_ref[g]), vn[:, gs], preferred_element_type=F32) + btv[:, g:g + 1]
            outs[0][:, gs] = (zu[:, gs] * sv).astype(BF16)

    return _rowwise(body, name, s, CHUNK, [pz], [lg, lb, w, bt], [_sds((s, sw), BF16)])[0]


def _sgu_bwd(pz, dy, lg, lb, w, wt, bt, name, after=()):
    s, sw = pz.shape[0], pz.shape[1] // 2
    groups = sw // GROUP_DIM

    def body(i, ins, consts, outs, accs):
        lgv, lbv, w_ref, wt_ref, btv = consts[0][...], consts[1][...], consts[2], consts[3], consts[4][...]

        @pl.when(i == 0)
        def _():
            for a in accs:
                a[...] = jnp.zeros_like(a)

        pu, pv = ins[0][:, :sw].astype(F32), ins[0][:, sw:].astype(F32)
        dyv = ins[1][...].astype(F32)
        zu = _gelu(pu)
        xhat, rs, vn = _sgu_norm(pv, lgv, lbv)
        dvn_parts, db_parts = [], []
        lower = lax.broadcasted_iota(jnp.int32, (CHUNK, CHUNK), 0) >= lax.broadcasted_iota(jnp.int32, (CHUNK, CHUNK), 1)
        for g in range(groups):
            gs = slice(g * GROUP_DIM, (g + 1) * GROUP_DIM)
            sv = jnp.dot(_causal(w_ref[g]), vn[:, gs], preferred_element_type=F32) + btv[:, g:g + 1]
            outs[0][:, gs] = (dyv[:, gs] * sv * _gelu_grad(pu[:, gs])).astype(BF16)
            dsv = dyv[:, gs] * zu[:, gs]
            dsvb = dsv.astype(BF16)
            db_parts.append(jnp.sum(dsv, axis=1, keepdims=True))
            accs[2][g] += jnp.where(lower, lax.dot_general(dsvb, vn[:, gs], _DIMS["nt"], preferred_element_type=F32), 0.0)
            dvn_parts.append(jnp.dot(_causal(wt_ref[g], upper=True), dsvb, preferred_element_type=F32))
        dvn = jnp.concatenate(dvn_parts, axis=1)
        accs[3][...] += jnp.concatenate(db_parts, axis=1)
        accs[0][...] += jnp.sum(dvn * xhat, axis=0, keepdims=True)
        accs[1][...] += jnp.sum(dvn, axis=0, keepdims=True)
        dxh = dvn * lgv
        dz = rs * (dxh - jnp.mean(dxh, axis=-1, keepdims=True) - xhat * jnp.mean(dxh * xhat, axis=-1, keepdims=True))
        outs[0][:, sw:] = (dz * _gelu_grad(pv)).astype(BF16)

    return _rowwise(body, name, s, CHUNK, [pz, dy], [lg, lb, w, wt, bt], [_sds((s, 2 * sw), BF16)],
                    [_sds((1, sw), F32), _sds((1, sw), F32), _sds((groups, CHUNK, CHUNK), F32), _sds((CHUNK, groups), F32)],
                    after=after)


def _mesh_place():
    x, y, c = lax.axis_index("x"), lax.axis_index("y"), lax.axis_index("c")
    return x, y, c, 4 * x + 2 * y + c


def _peer(x, y, c, k):
    px, py, pc = x ^ ((k >> 2) & 1), y ^ ((k >> 1) & 1), c ^ (k & 1)
    return (px, py, pc), 4 * px + 2 * py + pc


def _col_block(ref, idx, width):
    return ref.at[:, pl.ds(pl.multiple_of(idx * width, LANES), width)]


def _exchange_copy(src_ref, land_ref, send_sems, recv_sems, k, place, scatter, arriving, cols):
    x, y, c, me = place
    peer, pidx = _peer(x, y, c, k)
    slot = pidx if arriving else me
    if scatter:
        src = _col_block(src_ref, pidx, land_ref.shape[-1]) if cols else src_ref.at[pidx]
        dst = land_ref.at[slot]
    else:
        src = src_ref
        dst = _col_block(land_ref, slot, src_ref.shape[-1]) if cols else land_ref.at[slot]
    return pltpu.make_async_remote_copy(
        src_ref=src, dst_ref=dst, send_sem=send_sems[k - 1], recv_sem=recv_sems[k - 1], device_id=peer,
        device_id_type=MESH_TYPE)


N_PEERS = N_DEV - 1


def _land_shape(a, scatter, cols):
    if scatter:
        return (N_DEV, a.shape[0], a.shape[1] // N_DEV) if cols else a.shape
    return (a.shape[0], N_DEV * a.shape[1]) if cols else (N_DEV,) + a.shape


def _exchange_start(srcs, scatter, cols, name, after):
    n = len(srcs)
    land_shapes = [_land_shape(a, scatter, cl) for a, cl in zip(srcs, cols)]

    def body(*refs):
        src, land = refs[:n], refs[n:2 * n]
        send_sems = refs[2 * n + 1:2 * n + 1 + N_PEERS]
        recv_sems = refs[2 * n + 1 + N_PEERS:2 * n + 1 + 2 * N_PEERS]
        token = refs[-1]
        place = _mesh_place()
        for t in range(n):
            for k in range(1, N_DEV):
                _exchange_copy(src[t], land[t], send_sems, recv_sems, k, place, scatter, False, cols[t]).start()
        token[...] = jnp.zeros_like(token)

    return pl.pallas_call(
        body,
        name=name,
        out_shape=(*[pltpu.SemaphoreType.DMA(())] * (2 * N_PEERS), *[pltpu.HBM(a.shape, a.dtype) for a in srcs],
                   *[pltpu.HBM(shp, a.dtype) for shp, a in zip(land_shapes, srcs)], _sds((SUBLANES, LANES), F32)),
        in_specs=[HBM] * (2 * n) + [ANY],
        out_specs=(*[SEM] * (2 * N_PEERS), *[HBM] * (2 * n), pl.BlockSpec(memory_space=pltpu.VMEM)),
        input_output_aliases={i: 2 * N_PEERS + i for i in range(2 * n)},
        compiler_params=pltpu.CompilerParams(has_side_effects=DATAFLOW_EFFECT),
    )(*[pltpu.with_memory_space_constraint(a, pltpu.HBM) for a in srcs],
      *[pltpu.with_memory_space_constraint(lax.empty(shp, a.dtype), pltpu.HBM) for shp, a in zip(land_shapes, srcs)],
      after)


def _exchange_wait(started, after, scatter, cols, name):
    sems = started[:2 * N_PEERS]
    thru = started[2 * N_PEERS:-1]
    n = len(thru) // 2

    def body(*refs):
        src, land = refs[:n], refs[n:2 * n]
        send_sems = refs[2 * n:2 * n + N_PEERS]
        recv_sems = refs[2 * n + N_PEERS:2 * n + 2 * N_PEERS]
        place = _mesh_place()
        for t in range(n):
            for k in range(1, N_DEV):
                cp = _exchange_copy(src[t], land[t], send_sems, recv_sems, k, place, scatter, True, cols[t])
                cp.wait_send()
                cp.wait_recv()

    out = pl.pallas_call(
        body,
        name=name,
        out_shape=tuple(pltpu.HBM(a.shape, a.dtype) for a in thru),
        in_specs=[HBM] * (2 * n) + [SEM] * (2 * N_PEERS) + [ANY],
        out_specs=tuple([HBM] * (2 * n)),
        input_output_aliases={i: i for i in range(2 * n)},
        compiler_params=pltpu.CompilerParams(has_side_effects=DATAFLOW_EFFECT),
    )(*thru, *sems, after)
    return out[:n], out[n:]


def _adamw(w, g, m, v):
    m = ADAM_B1 * m + (1.0 - ADAM_B1) * g
    v = ADAM_B2 * v + (1.0 - ADAM_B2) * (g * g)
    m_hat = m / (1.0 - ADAM_B1 ** ADAM_STEP)
    v_hat = v / (1.0 - ADAM_B2 ** ADAM_STEP)
    delta = -ADAM_LR * (m_hat / (jnp.sqrt(v_hat) + ADAM_EPS) + ADAM_WD * w)
    return delta, m, v


def _adam_rows(r, c):
    fits = [t for t in range(BF16_SUBLANES, r + 1, BF16_SUBLANES) if r % t == 0 and t * c <= ADAM_BLOCK_ELEMS]
    return max(fits) if fits else r


def _adam_body(p_ref, w_ref, m_ref, v_ref, g_out, d_out, m_out, v_out):
    g = p_ref[0].astype(F32)
    for d in range(1, N_DEV):
        g = g + p_ref[d].astype(F32)
    delta, mn, vn = _adamw(w_ref[...], g, m_ref[...], v_ref[...])
    g_out[...] = g
    d_out[...] = delta
    m_out[...] = mn
    v_out[...] = vn


def _reduce_adam_layer(parts, w, m, v, prev, layer, name):
    nl, r, c = w.shape
    tr = _adam_rows(r, c)
    if prev is None:
        prev = [lax.empty((nl, r, c), F32) for _ in range(4)]

    def body(p_ref, w_ref, m_ref, v_ref, *rest):
        _adam_body(p_ref, w_ref, m_ref, v_ref, *rest[4:])

    blk = pl.BlockSpec((None, tr, c), lambda i: (layer, i, 0))
    out = _sds((nl, r, c), F32)
    return pl.pallas_call(
        body,
        name=name,
        grid=(r // tr,),
        in_specs=[pl.BlockSpec((N_DEV, tr, c), lambda i: (0, i, 0)), blk, blk, blk, ANY, ANY, ANY, ANY],
        out_specs=[blk, blk, blk, blk],
        out_shape=[out, out, out, out],
        input_output_aliases={4: 0, 5: 1, 6: 2, 7: 3},
        compiler_params=_params("parallel"),
    )(parts, w, m, v, *prev)


def _reduce_adam(parts, w, m, v, name):
    nl, _, r, c = parts.shape
    tr = _adam_rows(r, c)

    def body(*refs):
        _adam_body(*refs)

    blk = pl.BlockSpec((None, tr, c), lambda l, i: (l, i, 0))
    out = _sds((nl, r, c), F32)
    return pl.pallas_call(
        body,
        name=name,
        grid=(nl, r // tr),
        in_specs=[pl.BlockSpec((None, N_DEV, tr, c), lambda l, i: (l, 0, i, 0)), blk, blk, blk],
        out_specs=[blk, blk, blk, blk],
        out_shape=[out, out, out, out],
        compiler_params=_params("parallel", "parallel"),
    )(parts, w, m, v)


def _pack(arrays):
    flat = []
    for a in arrays:
        a = a.reshape(-1).astype(F32)
        flat.append(jnp.pad(a, (0, (-a.shape[0]) % PACK_UNIT)))
    out = jnp.concatenate(flat)
    rows = out.shape[0] // LANES
    pad_rows = (-rows) % 512
    return jnp.pad(out, (0, pad_rows * LANES)).reshape(rows + pad_rows, LANES)


def _unpack(packed, shapes):
    flat = packed.reshape(-1)
    out, off = [], 0
    for shp in shapes:
        size = math.prod(shp)
        out.append(flat[off:off + size].reshape(shp))
        off += size + (-size) % PACK_UNIT
    return out


def _to_full_cols(g):
    d, k, n = g.shape
    return jnp.transpose(g, (1, 0, 2)).reshape(k, d * n)


def _to_col_shards(a):
    k, n = a.shape
    return jnp.transpose(a.reshape(k, N_DEV, n // N_DEV), (1, 0, 2))


def kernel(x, positions, norm1_g, w_in, b_in, sinks, sgu_ln_g, sgu_ln_b, sgu_w, sgu_b, w_attn_branch, w_sgu_branch, w_out, norm2_g, w_gate_up, w_down, final_g, loss_target, m_norm1_g, m_w_in, m_b_in, m_sinks, m_sgu_ln_g, m_sgu_ln_b, m_sgu_w, m_sgu_b, m_w_attn_branch, m_w_sgu_branch, m_w_out, m_norm2_g, m_w_gate_up, m_w_down, m_final_g, v_norm1_g, v_w_in, v_b_in, v_sinks, v_sgu_ln_g, v_sgu_ln_b, v_sgu_w, v_sgu_b, v_w_attn_branch, v_w_sgu_branch, v_w_out, v_norm2_g, v_w_gate_up, v_w_down, v_final_g):
    nl = w_in.shape[0]
    s, d = x.shape[1], x.shape[2]
    aw = w_attn_branch.shape[1]
    sw = w_sgu_branch.shape[1]
    in_w = w_in.shape[2] * N_DEV
    kw = (in_w - aw - 2 * sw - 2 * d) // 2
    qkv_w = aw + 2 * kw
    groups = sw // GROUP_DIM
    ff = w_down.shape[1] * N_DEV

    h = x.reshape(s, d)
    target = loss_target.reshape(s, d)
    tabs = _rope_tables(positions.reshape(s, 1), "rope_tables")
    me = 4 * lax.axis_index("x") + 2 * lax.axis_index("y") + lax.axis_index("c")

    big = [w_in, w_attn_branch, w_sgu_branch, w_out, w_gate_up, w_down]
    big_m = [m_w_in, m_w_attn_branch, m_w_sgu_branch, m_w_out, m_w_gate_up, m_w_down]
    big_v = [v_w_in, v_w_attn_branch, v_w_sgu_branch, v_w_out, v_w_gate_up, v_w_down]
    big_names = ("w_in", "w_attn_branch", "w_sgu_branch", "w_out", "w_gate_up", "w_down")
    W_IN, W_AB, W_SB, W_OUT, W_GU, W_DOWN = range(6)
    first_layer_groups = ((W_IN,), (W_AB, W_SB, W_OUT), (W_GU, W_DOWN))
    grad_groups = ((W_DOWN, W_GU), (W_OUT, W_AB, W_SB), (W_IN,))

    col_sharded = (W_IN, W_AB, W_SB, W_GU)
    by_cols = [t in col_sharded and big[t].shape[2] % LANES == 0 for t in range(6)]

    def fill_own(land, own):
        return lax.dynamic_update_index_in_dim(land, own, me, 0)

    def start_gather(l, group, after):
        return _exchange_start([big[t][l].astype(BF16) for t in group], False, tuple(by_cols[t] for t in group),
                               f"gather_start_l{l}_{big_names[group[0]]}", after)

    def full_weight(t, land, own):
        if by_cols[t]:
            return lax.dynamic_update_slice_in_dim(land, own, me * own.shape[1], axis=1)
        blocks = fill_own(land, own)
        return _to_full_cols(blocks) if t in col_sharded else blocks.reshape(N_DEV * own.shape[0], own.shape[1])

    saved = []
    started = {}
    token = h
    for group in first_layer_groups:
        started[(0, group)] = start_gather(0, group, token)
        token = started[(0, group)][-1]
    for l in range(nl):
        if l + 1 < nl:
            started[(l + 1, tuple(range(6)))] = start_gather(l + 1, tuple(range(6)), token)
            token = started[(l + 1, tuple(range(6)))][-1]
        gathered = {}

        def weight(t, after, l=l, gathered=gathered):
            if t not in gathered:
                group = next(g for (ll, g) in started if ll == l and t in g)
                srcs, lands = _exchange_wait(started.pop((l, group)), after, False, tuple(by_cols[tt] for tt in group),
                                             f"gather_wait_l{l}_{big_names[group[0]]}")
                for tt, sr, ld in zip(group, srcs, lands):
                    gathered[tt] = full_weight(tt, ld, sr)
            return gathered[t]

        bias = b_in[l].reshape(1, in_w)
        g1, g2 = norm1_g[l].reshape(1, d), norm2_g[l].reshape(1, d)
        lg, lb = sgu_ln_g[l].reshape(1, sw), sgu_ln_b[l].reshape(1, sw)
        bt = sgu_b[l].T

        xn = _rms_fwd(h, g1, "rms1_fwd", after=(token,))
        w_in_f = weight(W_IN, xn)
        wts = dict(qkv=w_in_f[:, :qkv_w], z=w_in_f[:, qkv_w:qkv_w + 2 * sw], g=w_in_f[:, qkv_w + 2 * sw:])
        qkv = _matmul(xn, wts["qkv"], "nn", BF16, "proj_qkv", bias=bias[:, :qkv_w])
        pz = _matmul(xn, wts["z"], "nn", BF16, "proj_z", bias=bias[:, qkv_w:qkv_w + 2 * sw])
        pg = _matmul(xn, wts["g"], "nn", BF16, "proj_g", bias=bias[:, qkv_w + 2 * sw:])
        y_attn = _attn_fwd(qkv, tabs, sinks[l], aw, kw, "attn_fwd")
        y_sgu = _sgu_fwd(pz, lg, lb, sgu_w[l], bt, "sgu_fwd")
        wts.update(ab=weight(W_AB, y_sgu), sb=weight(W_SB, y_sgu), out=weight(W_OUT, y_sgu))
        a_br = _matmul(y_attn, wts["ab"], "nn", BF16, "attn_branch")
        s_br = _matmul(y_sgu, wts["sb"], "nn", BF16, "sgu_branch")
        merged = _merge_fwd(pg, a_br, s_br, "merge_fwd")
        h_mid = _matmul(merged, wts["out"], "nn", F32, "out_proj", res=h)
        hn = _rms_fwd(h_mid, g2, "rms2_fwd")
        wts.update(gu=weight(W_GU, hn), down=weight(W_DOWN, hn))
        gu = _matmul(hn, wts["gu"], "nn", BF16, "gate_up")
        act = _swiglu_fwd(gu, "swiglu_fwd")
        h_out = _matmul(act, wts["down"], "nn", F32, "down_proj", res=h_mid)
        saved.append(dict(wts=wts, h=h, xn=xn, qkv=qkv, pz=pz, pg=pg, y_attn=y_attn, y_sgu=y_sgu, a_br=a_br,
                          s_br=s_br, merged=merged, h_mid=h_mid, hn=hn, gu=gu, act=act,
                          g1=g1, g2=g2, lg=lg, lb=lb, bt=bt))
        h = h_out

    dh, dhb, d_final_g, loss_blk = _loss_head(h, final_g.reshape(1, d), target, "loss_head")

    small = {n: [None] * nl for n in ("norm1_g", "b_in", "sinks", "sgu_ln_g", "sgu_ln_b", "sgu_w", "sgu_b", "norm2_g")}
    scattering = {}

    def start_scatter(l, group, grads, after):
        sends = []
        for t, dw in zip(group, grads):
            if by_cols[t]:
                sends.append(dw)
            elif t in col_sharded:
                sends.append(_to_col_shards(dw))
            else:
                sends.append(dw.reshape(N_DEV, dw.shape[0] // N_DEV, dw.shape[1]))
        scattering[(l, group)] = _exchange_start(sends, True, tuple(by_cols[t] for t in group),
                                                 f"scatter_start_l{l}_{big_names[group[0]]}", after)
        return scattering[(l, group)][-1]

    for l in reversed(range(nl)):
        sv = saved[l]
        wts = sv["wts"]
        d_act = _matmul(dhb, wts["down"], "nt", BF16, "d_act")
        dw_down = _matmul(sv["act"], dhb, "tn", BF16, "dw_down")
        d_gu = _swiglu_bwd(sv["gu"], d_act, "swiglu_bwd")
        dw_gu = _matmul(sv["hn"], d_gu, "tn", BF16, "dw_gate_up")
        token = start_scatter(l, grad_groups[0], [dw_down, dw_gu], token)
        d_hn = _matmul(d_gu, wts["gu"], "nt", BF16, "d_hn", after=token)
        dh_mid, dmb, dg2 = _rms_bwd(sv["h_mid"], sv["g2"], d_hn, dh, "rms2_bwd")
        d_merged = _matmul(dmb, wts["out"], "nt", BF16, "d_merged")
        dw_out = _matmul(sv["merged"], dmb, "tn", BF16, "dw_out")
        d_a, d_s, d_pg = _merge_bwd(sv["pg"], sv["a_br"], sv["s_br"], d_merged, "merge_bwd")
        d_y_attn = _matmul(d_a, wts["ab"], "nt", BF16, "d_y_attn")
        dw_ab = _matmul(sv["y_attn"], d_a, "tn", BF16, "dw_attn_branch")
        d_y_sgu = _matmul(d_s, wts["sb"], "nt", BF16, "d_y_sgu")
        dw_sb = _matmul(sv["y_sgu"], d_s, "tn", BF16, "dw_sgu_branch")
        token = start_scatter(l, grad_groups[1], [dw_out, dw_ab, dw_sb], token)
        d_pz, d_lg, d_lb, d_sw, d_sbt = _sgu_bwd(sv["pz"], d_y_sgu, sv["lg"], sv["lb"], sgu_w[l],
                                                 jnp.transpose(sgu_w[l], (0, 2, 1)), sv["bt"], "sgu_bwd", after=(token,))
        d_q, d_kv, d_sinks = _attn_bwd(sv["qkv"], tabs, sinks[l], sv["y_attn"], d_y_attn, aw, kw, "attn_bwd")
        d_qkv = jnp.concatenate([d_q, d_kv], axis=1)
        dw_qkv = _matmul(sv["xn"], d_qkv, "tn", BF16, "dw_qkv")
        dw_z = _matmul(sv["xn"], d_pz, "tn", BF16, "dw_z")
        dw_g = _matmul(sv["xn"], d_pg, "tn", BF16, "dw_g")
        token = start_scatter(l, grad_groups[2], [jnp.concatenate([dw_qkv, dw_z, dw_g], axis=1)], token)
        d_xn = _matmul(d_qkv, wts["qkv"], "nt", F32, "d_xn_qkv", after=token)
        d_xn = _matmul(d_pz, wts["z"], "nt", F32, "d_xn_z", res=d_xn)
        d_xn = _matmul(d_pg, wts["g"], "nt", F32, "d_xn_g", res=d_xn)
        dh, dhb, dg1 = _rms_bwd(sv["h"], sv["g1"], d_xn, dh_mid, "rms1_bwd")

        small["norm1_g"][l], small["norm2_g"][l] = dg1, dg2
        small["b_in"][l] = jnp.concatenate([_colsum(d_qkv, "db_qkv"), _colsum(d_pz, "db_z"), _colsum(d_pg, "db_g")], axis=1)
        small["sinks"][l] = d_sinks[0, :aw // HEAD_DIM]
        small["sgu_ln_g"][l], small["sgu_ln_b"][l] = d_lg, d_lb
        small["sgu_w"][l] = d_sw
        small["sgu_b"][l] = d_sbt.T

    grad_x = dh.reshape(x.shape)

    names = ["norm1_g", "b_in", "sinks", "sgu_ln_g", "sgu_ln_b", "sgu_w", "sgu_b", "norm2_g"]
    small_w = [norm1_g, b_in, sinks, sgu_ln_g, sgu_ln_b, sgu_w, sgu_b, norm2_g, final_g]
    small_m = [m_norm1_g, m_b_in, m_sinks, m_sgu_ln_g, m_sgu_ln_b, m_sgu_w, m_sgu_b, m_norm2_g, m_final_g]
    small_v = [v_norm1_g, v_b_in, v_sinks, v_sgu_ln_g, v_sgu_ln_b, v_sgu_w, v_sgu_b, v_norm2_g, v_final_g]
    shapes = [w.shape for w in small_w] + [(1,)]
    partial = [jnp.stack([p.reshape(w.shape[1:]) for p in small[n]]) for n, w in zip(names, small_w)]
    partial += [d_final_g.reshape(final_g.shape), loss_blk[0, :1]]
    zero = jnp.zeros((1,), F32)
    small_started = _exchange_start([_pack(partial)], False, (False,), "gather_start_small_grads", dhb)

    big_out = [None] * len(big)
    after = small_started[-1]
    for l in reversed(range(nl)):
        for group in grad_groups:
            srcs, lands = _exchange_wait(scattering.pop((l, group)), after, True, tuple(by_cols[t] for t in group),
                                         f"scatter_wait_l{l}_{big_names[group[0]]}")
            for t, sr, ld in zip(group, srcs, lands):
                if by_cols[t]:
                    own = lax.dynamic_slice_in_dim(sr, me * ld.shape[2], ld.shape[2], axis=1)
                else:
                    own = lax.dynamic_index_in_dim(sr, me, 0, keepdims=False)
                parts = fill_own(ld, own)
                big_out[t] = _reduce_adam_layer(parts, big[t], big_m[t], big_v[t], big_out[t], l, f"adam_{big_names[t]}")
                after = big_out[t][0]

    srcs, lands = _exchange_wait(small_started, after, False, (False,), "gather_wait_small_grads")
    all_g = fill_own(lands[0], srcs[0])
    sm = _reduce_adam(all_g[None], _pack(small_w + [zero])[None], _pack(small_m + [zero])[None],
                      _pack(small_v + [zero])[None], "adam_small")
    sm_g, sm_d, sm_m, sm_v = [_unpack(a[0], shapes) for a in sm]
    loss = sm_g[-1].reshape(())

    def ordered(kind_small, kind_big):
        by_name = dict(zip(["norm1_g", "b_in", "sinks", "sgu_ln_g", "sgu_ln_b", "sgu_w", "sgu_b", "norm2_g", "final_g"], kind_small))
        by_name.update(zip(["w_in", "w_attn_branch", "w_sgu_branch", "w_out", "w_gate_up", "w_down"], kind_big))
        order = ["norm1_g", "w_in", "b_in", "sinks", "sgu_ln_g", "sgu_ln_b", "sgu_w", "sgu_b", "w_attn_branch",
                 "w_sgu_branch", "w_out", "norm2_g", "w_gate_up", "w_down", "final_g"]
        return [by_name[n] for n in order]

    outs = [loss, grad_x]
    for idx, sm_kind in enumerate((sm_g, sm_d, sm_m, sm_v)):
        outs += ordered(sm_kind[:-1], [o[idx] for o in big_out])
    return tuple(outs)
```

```python
import math

import jax
import jax.numpy as jnp
from jax import lax
from jax.experimental import pallas as pl
from jax.experimental.pallas import tpu as pltpu

F32 = jnp.float32
BF16 = jnp.bfloat16

N_DEV = 8
HEAD_DIM = 64
WINDOW = 128
CHUNK = 128
GROUP_DIM = 128
ROPE_DIM = HEAD_DIM // 4
ROPE_HALF = ROPE_DIM // 2
ROPE_THETA = 500000.0
EPS = 1e-5
NEG = -1e30
ATTN_SCALE = HEAD_DIM ** -0.5
ADAM_LR = 0.001
ADAM_B1 = 0.9
ADAM_B2 = 0.999
ADAM_EPS = 1e-08
ADAM_WD = 0.01
ADAM_STEP = 10
LANES = 128
SUBLANES = 8
BF16_SUBLANES = 16
PACK_UNIT = SUBLANES * LANES
ADAM_BLOCK_ELEMS = 256 * 1024
V7X_VMEM_LIMIT_BYTES = 56 * 1024 * 1024
MATMUL_TILE_PREFS = (1024, 1408, 768, 512, 384, 256, 128)
MATMUL_WHOLE_K = 2048
MATMUL_K_PREFS = (2816, 2048, 1536, 1408, 1024, 768, 512, 384, 256, 128)
ROW_TILE_PREFS = (512, 256, 128)
SWIGLU_ROW_PREFS = (512, 256, 128)
MESH_TYPE = pl.DeviceIdType.MESH
ANY = pl.BlockSpec(memory_space=pl.ANY)
HBM = pl.BlockSpec(memory_space=pltpu.HBM)
SEM = pl.BlockSpec(memory_space=pltpu.SEMAPHORE)
DATAFLOW_EFFECT = pltpu.SideEffectType.DATAFLOW_SIDE_EFFECTING


def _pick(n, prefs):
    for p in prefs:
        if n % p == 0:
            return p
    return n


def _params(*sem):
    return pltpu.CompilerParams(dimension_semantics=sem, vmem_limit_bytes=V7X_VMEM_LIMIT_BYTES)


_DIMS = {"nn": (((1,), (0,)), ((), ())), "nt": (((1,), (1,)), ((), ())), "tn": (((0,), (0,)), ((), ()))}


def _matmul(a, b, mode, out_dtype, name, bias=None, res=None, after=None):
    if mode == "nn":
        (m, k), n = a.shape, b.shape[1]
    elif mode == "nt":
        (m, k), n = a.shape, b.shape[0]
    else:
        (k, m), n = a.shape, b.shape[1]
    tm, tn = _pick(m, MATMUL_TILE_PREFS), _pick(n, MATMUL_TILE_PREFS)
    tk = k if k <= MATMUL_WHOLE_K else _pick(k, MATMUL_K_PREFS)
    nk = k // tk
    dims = _DIMS[mode]
    a_spec = pl.BlockSpec((tk, tm), lambda i, j, kk: (kk, i)) if mode == "tn" else pl.BlockSpec((tm, tk), lambda i, j, kk: (i, kk))
    b_spec = pl.BlockSpec((tn, tk), lambda i, j, kk: (j, kk)) if mode == "nt" else pl.BlockSpec((tk, tn), lambda i, j, kk: (kk, j))
    in_specs, args = [a_spec, b_spec], [a, b]
    if bias is not None:
        in_specs.append(pl.BlockSpec((1, tn), lambda i, j, kk: (0, j)))
        args.append(bias)
    if res is not None:
        in_specs.append(pl.BlockSpec((tm, tn), lambda i, j, kk: (i, j)))
        args.append(res)
    if after is not None:
        in_specs.append(ANY)
        args.append(after)

    def body(*refs):
        a_ref, b_ref = refs[0], refs[1]
        pos = 2
        bias_ref = res_ref = None
        if bias is not None:
            bias_ref = refs[pos]
            pos += 1
        if res is not None:
            res_ref = refs[pos]
            pos += 1
        if after is not None:
            pos += 1
        o_ref = refs[pos]

        def finish(r):
            if bias_ref is not None:
                r = r + bias_ref[...]
            if res_ref is not None:
                r = r + res_ref[...]
            o_ref[...] = r.astype(out_dtype)

        part = lax.dot_general(a_ref[...], b_ref[...], dims, preferred_element_type=F32)
        if nk == 1:
            finish(part)
        else:
            acc_ref = refs[pos + 1]
            kk = pl.program_id(2)

            @pl.when(kk == 0)
            def _():
                acc_ref[...] = part

            @pl.when((kk > 0) & (kk < nk - 1))
            def _():
                acc_ref[...] += part

            @pl.when(kk == nk - 1)
            def _():
                finish(acc_ref[...] + part)

    return pl.pallas_call(
        body,
        name=name,
        grid=(m // tm, n // tn, nk),
        in_specs=in_specs,
        out_specs=pl.BlockSpec((tm, tn), lambda i, j, kk: (i, j)),
        out_shape=jax.ShapeDtypeStruct((m, n), out_dtype),
        scratch_shapes=[] if nk == 1 else [pltpu.VMEM((tm, tn), F32)],
        compiler_params=_params("parallel", "parallel", "arbitrary"),
    )(*args)


def _rowwise(body, name, rows, tr, ins, consts, outs, accs=(), after=()):
    n_in, n_c, n_o, n_a = len(ins), len(consts), len(outs), len(after)

    def wrapped(*refs):
        body(pl.program_id(0), refs[:n_in], refs[n_in:n_in + n_c], refs[n_in + n_c + n_a:n_in + n_c + n_a + n_o],
             refs[n_in + n_c + n_a + n_o:])

    def whole(shape):
        zeros = (0,) * len(shape)
        return pl.BlockSpec(tuple(shape), lambda i: zeros)

    in_specs = ([pl.BlockSpec((tr, a.shape[1]), lambda i: (i, 0)) for a in ins] + [whole(c.shape) for c in consts]
                + [ANY] * n_a)
    out_specs = [pl.BlockSpec((tr, o.shape[1]), lambda i: (i, 0)) for o in outs] + [whole(a.shape) for a in accs]
    return pl.pallas_call(
        wrapped,
        name=name,
        grid=(rows // tr,),
        in_specs=in_specs,
        out_specs=out_specs,
        out_shape=list(outs) + list(accs),
        compiler_params=_params("arbitrary" if accs else "parallel"),
    )(*ins, *consts, *after)


def _sds(shape, dtype):
    return jax.ShapeDtypeStruct(tuple(shape), dtype)


def _rms_fwd(h, g, name, after=()):
    s, d = h.shape
    tr = _pick(s, ROW_TILE_PREFS)

    def body(i, ins, consts, outs, accs):
        x = ins[0][...]
        r = lax.rsqrt(jnp.mean(x * x, axis=-1, keepdims=True) + EPS)
        outs[0][...] = (x * r * consts[0][...]).astype(BF16)

    return _rowwise(body, name, s, tr, [h], [g], [_sds((s, d), BF16)], after=after)[0]


def _rms_bwd(h, g, dy, dh_up, name):
    s, d = h.shape
    tr = _pick(s, ROW_TILE_PREFS)

    def body(i, ins, consts, outs, accs):
        x, dyv, up = ins[0][...], ins[1][...].astype(F32), ins[2][...]
        r = lax.rsqrt(jnp.mean(x * x, axis=-1, keepdims=True) + EPS)
        xr = x * r
        gy = dyv * consts[0][...]
        dx = r * (gy - xr * jnp.mean(gy * xr, axis=-1, keepdims=True))
        outs[0][...] = up + dx
        outs[1][...] = (up + dx).astype(BF16)

        @pl.when(i == 0)
        def _():
            accs[0][...] = jnp.zeros_like(accs[0])

        accs[0][...] += jnp.sum(dyv * xr, axis=0, keepdims=True)

    return _rowwise(body, name, s, tr, [h, dy, dh_up], [g], [_sds((s, d), F32), _sds((s, d), BF16)], [_sds((1, d), F32)])


def _loss_head(h, g, target, name):
    s, d = h.shape
    tr = _pick(s, ROW_TILE_PREFS)

    def body(i, ins, consts, outs, accs):
        x, t = ins[0][...], ins[1][...]
        gv = consts[0][...]
        r = lax.rsqrt(jnp.mean(x * x, axis=-1, keepdims=True) + EPS)
        xr = x * r
        diff = xr * gv - t
        dyv = diff * (1.0 / d)
        gy = dyv * gv
        dx = r * (gy - xr * jnp.mean(gy * xr, axis=-1, keepdims=True))
        outs[0][...] = dx
        outs[1][...] = dx.astype(BF16)

        @pl.when(i == 0)
        def _():
            accs[0][...] = jnp.zeros_like(accs[0])
            accs[1][...] = jnp.zeros_like(accs[1])

        accs[0][...] += jnp.sum(dyv * xr, axis=0, keepdims=True)
        part = 0.5 * jnp.sum(jnp.mean(diff * diff, axis=-1, keepdims=True), axis=0, keepdims=True)
        accs[1][...] += jnp.broadcast_to(part, accs[1].shape)

    return _rowwise(body, name, s, tr, [h, target], [g], [_sds((s, d), F32), _sds((s, d), BF16)],
                    [_sds((1, d), F32), _sds((SUBLANES, LANES), F32)])


def _colsum(a, name):
    s, w = a.shape
    tr = _pick(s, ROW_TILE_PREFS)

    def body(i, ins, consts, outs, accs):
        @pl.when(i == 0)
        def _():
            accs[0][...] = jnp.zeros_like(accs[0])

        accs[0][...] += jnp.sum(ins[0][...].astype(F32), axis=0, keepdims=True)

    return _rowwise(body, name, s, tr, [a], [], [], [_sds((1, w), F32)])[0]


def _sigmoid(x):
    return 1.0 / (1.0 + jnp.exp(-x))


def _merge_fwd(pg, a, bm, name):
    s, d = a.shape
    tr = _pick(s, ROW_TILE_PREFS)

    def body(i, ins, consts, outs, accs):
        p = ins[0][...].astype(F32)
        ga, gs = _sigmoid(p[:, :d]), _sigmoid(p[:, d:])
        outs[0][...] = (ga * ins[1][...].astype(F32) + gs * ins[2][...].astype(F32)).astype(BF16)

    return _rowwise(body, name, s, tr, [pg, a, bm], [], [_sds((s, d), BF16)])[0]


def _merge_bwd(pg, a, bm, dm, name):
    s, d = a.shape
    tr = _pick(s, ROW_TILE_PREFS)

    def body(i, ins, consts, outs, accs):
        p = ins[0][...].astype(F32)
        av, bv, dmv = ins[1][...].astype(F32), ins[2][...].astype(F32), ins[3][...].astype(F32)
        ga, gs = _sigmoid(p[:, :d]), _sigmoid(p[:, d:])
        outs[0][...] = (dmv * ga).astype(BF16)
        outs[1][...] = (dmv * gs).astype(BF16)
        outs[2][:, :d] = (dmv * av * ga * (1.0 - ga)).astype(BF16)
        outs[2][:, d:] = (dmv * bv * gs * (1.0 - gs)).astype(BF16)

    return _rowwise(body, name, s, tr, [pg, a, bm, dm], [],
                    [_sds((s, d), BF16), _sds((s, d), BF16), _sds((s, 2 * d), BF16)])


def _gate_up_swiglu(hn, w_gu, name):
    s, d = hn.shape
    n2 = w_gu.shape[1]
    fb = n2 // N_DEV
    tm = _pick(s, SWIGLU_ROW_PREFS)

    def body(a_ref, b_ref, gu_ref, act_ref):
        r = jnp.dot(a_ref[...], b_ref[...], preferred_element_type=F32)
        gu_ref[...] = r.astype(BF16)
        gate, up = r[:, :fb], r[:, fb:]
        act_ref[...] = (gate * _sigmoid(gate) * up).astype(BF16)

    return pl.pallas_call(
        body,
        name=name,
        grid=(s // tm, N_DEV // 2),
        in_specs=[pl.BlockSpec((tm, d), lambda i, j: (i, 0)), pl.BlockSpec((d, 2 * fb), lambda i, j: (0, j))],
        out_specs=[pl.BlockSpec((tm, 2 * fb), lambda i, j: (i, j)), pl.BlockSpec((tm, fb), lambda i, j: (i, j))],
        out_shape=[_sds((s, n2), BF16), _sds((s, n2 // 2), BF16)],
        compiler_params=_params("parallel", "parallel"),
    )(hn, w_gu)


def _d_act_swiglu(dhb, w_down, gu, name):
    s, d = dhb.shape
    n2 = gu.shape[1]
    fb = n2 // N_DEV
    tm = _pick(s, SWIGLU_ROW_PREFS)

    def body(a_ref, b_ref, gu_ref, o_ref):
        da = lax.dot_general(a_ref[...], b_ref[...], _DIMS["nt"], preferred_element_type=F32)
        gate, up = gu_ref[:, :fb].astype(F32), gu_ref[:, fb:].astype(F32)
        sg = _sigmoid(gate)
        o_ref[:, :fb] = (da * up * sg * (1.0 + gate * (1.0 - sg))).astype(BF16)
        o_ref[:, fb:] = (da * gate * sg).astype(BF16)

    pair = pl.BlockSpec((tm, 2 * fb), lambda i, j: (i, j))
    return pl.pallas_call(
        body,
        name=name,
        grid=(s // tm, N_DEV // 2),
        in_specs=[pl.BlockSpec((tm, d), lambda i, j: (i, 0)), pl.BlockSpec((fb, d), lambda i, j: (j, 0)), pair],
        out_specs=pair,
        out_shape=_sds((s, n2), BF16),
        compiler_params=_params("parallel", "parallel"),
    )(dhb, w_down, gu)


def _rope_tables(pos_col, name):
    s = pos_col.shape[0]
    tr = _pick(s, (1024, 512, 256, 128))
    inv = ROPE_THETA ** (-jnp.arange(0, ROPE_DIM, 2, dtype=F32) / ROPE_DIM)
    lane = jnp.arange(LANES)
    inv_lanes = inv[lane % ROPE_HALF].reshape(1, LANES)

    def body(i, ins, consts, outs, accs):
        ang = ins[0][...].astype(F32) * consts[0][...]
        c, sn = jnp.cos(ang), jnp.sin(ang)
        in_head = lax.broadcasted_iota(jnp.int32, ang.shape, 1) % HEAD_DIM
        outs[0][:, 0:LANES] = jnp.where(in_head < ROPE_DIM, c, 1.0)
        outs[0][:, LANES:2 * LANES] = jnp.where(in_head < ROPE_HALF, -sn, 0.0)
        outs[0][:, 2 * LANES:] = jnp.where((in_head >= ROPE_HALF) & (in_head < ROPE_DIM), sn, 0.0)

    return _rowwise(body, name, s, tr, [pos_col], [inv_lanes], [_sds((s, 3 * LANES), F32)])[0]


def _rope(x, tab, inverse=False):
    width = x.shape[1]
    reps = width // LANES
    c = jnp.tile(tab[:, 0:LANES], (1, reps))
    lo = jnp.tile(tab[:, LANES:2 * LANES], (1, reps))
    hi = jnp.tile(tab[:, 2 * LANES:], (1, reps))
    if inverse:
        lo, hi = -lo, -hi
    return x * c + pltpu.roll(x, width - ROPE_HALF, 1) * lo + pltpu.roll(x, ROPE_HALF, 1) * hi


def _attn_specs(aw, kw):
    kb = aw // kw
    prev = lambda i: jnp.maximum(i - 1, 0)
    return [
        pl.BlockSpec(memory_space=pltpu.SMEM),
        pl.BlockSpec((WINDOW, aw), lambda i: (i, 0)),
        pl.BlockSpec((WINDOW, kw), lambda i: (i, kb)),
        pl.BlockSpec((WINDOW, kw), lambda i: (prev(i), kb)),
        pl.BlockSpec((WINDOW, kw), lambda i: (i, kb + 1)),
        pl.BlockSpec((WINDOW, kw), lambda i: (prev(i), kb + 1)),
        pl.BlockSpec((WINDOW, 3 * LANES), lambda i: (i, 0)),
        pl.BlockSpec((WINDOW, 3 * LANES), lambda i: (prev(i), 0)),
    ]


def _attn_common(i, q_ref, kc_ref, kp_ref, vc_ref, vp_ref, tq_ref, tp_ref):
    tq, tp = tq_ref[...], tp_ref[...]
    q = _rope(q_ref[...].astype(F32), tq).astype(BF16)
    kc = _rope(kc_ref[...].astype(F32), tq)
    kp = _rope(kp_ref[...].astype(F32), tp)
    k2 = jnp.concatenate([kp, kc], axis=0).astype(BF16)
    v2 = jnp.concatenate([vp_ref[...], vc_ref[...]], axis=0)
    qi = lax.broadcasted_iota(jnp.int32, (WINDOW, 2 * WINDOW), 0)
    kj = lax.broadcasted_iota(jnp.int32, (WINDOW, 2 * WINDOW), 1)
    rel = qi + WINDOW - kj
    ok = (rel >= 0) & (rel < WINDOW) & ((kj >= WINDOW) | (i > 0))
    return q, k2, v2, ok, tq, tp


def _head_probs(qh, kg, ok, sink):
    s = lax.dot_general(qh, kg, _DIMS["nt"], preferred_element_type=F32) * ATTN_SCALE
    s = jnp.where(ok, s, NEG)
    m = jnp.maximum(jnp.max(s, axis=1, keepdims=True), sink)
    p = jnp.exp(s - m)
    es = jnp.exp(sink - m)
    inv = 1.0 / (jnp.sum(p, axis=1, keepdims=True) + es)
    return p * inv, es * inv


def _attn_fwd(qkv, tabs, sinks, aw, kw, name):
    s = qkv.shape[0]
    nq, nkv = aw // HEAD_DIM, kw // HEAD_DIM
    qpk = nq // nkv

    def body(s_ref, q_ref, kc_ref, kp_ref, vc_ref, vp_ref, tq_ref, tp_ref, o_ref):
        i = pl.program_id(0)
        q, k2, v2, ok, _, _ = _attn_common(i, q_ref, kc_ref, kp_ref, vc_ref, vp_ref, tq_ref, tp_ref)
        for h in range(nq):
            g = h // qpk
            hs, gs = slice(h * HEAD_DIM, (h + 1) * HEAD_DIM), slice(g * HEAD_DIM, (g + 1) * HEAD_DIM)
            pn, _ = _head_probs(q[:, hs], k2[:, gs], ok, s_ref[h])
            o = jnp.dot(pn.astype(BF16), v2[:, gs], preferred_element_type=F32)
            o_ref[:, hs] = o.astype(BF16)

    return pl.pallas_call(
        body,
        name=name,
        grid=(s // WINDOW,),
        in_specs=_attn_specs(aw, kw),
        out_specs=pl.BlockSpec((WINDOW, aw), lambda i: (i, 0)),
        out_shape=_sds((s, aw), BF16),
        compiler_params=_params("parallel"),
    )(sinks, qkv, qkv, qkv, qkv, qkv, tabs, tabs)


def _attn_bwd(qkv, tabs, sinks, o, do, aw, kw, name):
    s = qkv.shape[0]
    nb = s // WINDOW
    nq, nkv = aw // HEAD_DIM, kw // HEAD_DIM
    qpk = nq // nkv

    def body(s_ref, q_ref, kc_ref, kp_ref, vc_ref, vp_ref, tq_ref, tp_ref, o_ref, do_ref,
             dq_ref, dkv_ref, ds_ref, ck_ref, cv_ref):
        i = pl.program_id(0)

        @pl.when(i == 0)
        def _():
            ck_ref[...] = jnp.zeros_like(ck_ref)
            cv_ref[...] = jnp.zeros_like(cv_ref)
            ds_ref[...] = jnp.zeros_like(ds_ref)

        q, k2, v2, ok, tq, tp = _attn_common(i, q_ref, kc_ref, kp_ref, vc_ref, vp_ref, tq_ref, tp_ref)
        dov, ov = do_ref[...], o_ref[...]
        row0 = lax.broadcasted_iota(jnp.int32, (SUBLANES, LANES), 0) == 0
        lane = lax.broadcasted_iota(jnp.int32, (SUBLANES, LANES), 1)
        dsink = jnp.zeros((SUBLANES, LANES), F32)
        dq_parts, dk_parts, dv_parts = [], [], []
        for g in range(nkv):
            gs = slice(g * HEAD_DIM, (g + 1) * HEAD_DIM)
            kg, vg = k2[:, gs], v2[:, gs]
            dk_g = jnp.zeros((2 * WINDOW, HEAD_DIM), F32)
            dv_g = jnp.zeros((2 * WINDOW, HEAD_DIM), F32)
            for j in range(qpk):
                h = g * qpk + j
                hs = slice(h * HEAD_DIM, (h + 1) * HEAD_DIM)
                qh, doh = q[:, hs], dov[:, hs]
                pn, psink = _head_probs(qh, kg, ok, s_ref[h])
                delta = jnp.sum(doh.astype(F32) * ov[:, hs].astype(F32), axis=1, keepdims=True)
                dp = lax.dot_general(doh, vg, _DIMS["nt"], preferred_element_type=F32)
                dsb = (pn * (dp - delta)).astype(BF16)
                dsink = dsink + jnp.where(row0 & (lane == h), -jnp.sum(psink * delta, axis=0, keepdims=True), 0.0)
                dq_parts.append(jnp.dot(dsb, kg, preferred_element_type=F32) * ATTN_SCALE)
                dk_g = dk_g + lax.dot_general(dsb, qh, _DIMS["tn"], preferred_element_type=F32) * ATTN_SCALE
                dv_g = dv_g + lax.dot_general(pn.astype(BF16), doh, _DIMS["tn"], preferred_element_type=F32)
            dk_parts.append(dk_g)
            dv_parts.append(dv_g)
        ds_ref[...] += dsink
        dq_ref[...] = _rope(jnp.concatenate(dq_parts, axis=1), tq, inverse=True).astype(BF16)
        dk2 = jnp.concatenate(dk_parts, axis=1)
        dv2 = jnp.concatenate(dv_parts, axis=1)
        dk_prev = _rope(ck_ref[...] + dk2[:WINDOW], tp, inverse=True)
        dv_prev = cv_ref[...] + dv2[:WINDOW]

        @pl.when(i > 0)
        def _():
            dkv_ref[pl.ds(pl.multiple_of((i - 1) * WINDOW, WINDOW), WINDOW), :] = jnp.concatenate(
                [dk_prev, dv_prev], axis=1).astype(BF16)

        ck_ref[...] = dk2[WINDOW:]
        cv_ref[...] = dv2[WINDOW:]

        @pl.when(i == nb - 1)
        def _():
            dkv_ref[pl.ds(pl.multiple_of(i * WINDOW, WINDOW), WINDOW), :] = jnp.concatenate(
                [_rope(dk2[WINDOW:], tq, inverse=True), dv2[WINDOW:]], axis=1).astype(BF16)

    blk = pl.BlockSpec((WINDOW, aw), lambda i: (i, 0))
    return pl.pallas_call(
        body,
        name=name,
        grid=(nb,),
        in_specs=_attn_specs(aw, kw) + [blk, blk],
        out_specs=[blk, pl.BlockSpec((s, 2 * kw), lambda i: (0, 0)), pl.BlockSpec((SUBLANES, LANES), lambda i: (0, 0))],
        out_shape=[_sds((s, aw), BF16), _sds((s, 2 * kw), BF16), _sds((SUBLANES, LANES), F32)],
        scratch_shapes=[pltpu.VMEM((WINDOW, kw), F32), pltpu.VMEM((WINDOW, kw), F32)],
        compiler_params=_params("arbitrary"),
    )(sinks, qkv, qkv, qkv, qkv, qkv, tabs, tabs, o, do)


_INV_SQRT2 = 1.0 / math.sqrt(2.0)
_INV_SQRT2PI = 1.0 / math.sqrt(2.0 * math.pi)


def _gelu(x):
    return x * (lax.erf(x * _INV_SQRT2) + 1.0) * 0.5


def _gelu_grad(x):
    return 0.5 * (lax.erf(x * _INV_SQRT2) + 1.0) + x * jnp.exp(-0.5 * x * x) * _INV_SQRT2PI


def _sgu_norm(pv, lg, lb):
    zv = _gelu(pv)
    mu = jnp.mean(zv, axis=-1, keepdims=True)
    cen = zv - mu
    rs = lax.rsqrt(jnp.mean(cen * cen, axis=-1, keepdims=True) + EPS)
    xhat = cen * rs
    return xhat, rs, (xhat * lg + lb).astype(BF16)


def _causal(w, upper=False):
    t = lax.broadcasted_iota(jnp.int32, (CHUNK, CHUNK), 0)
    u = lax.broadcasted_iota(jnp.int32, (CHUNK, CHUNK), 1)
    return jnp.where((u >= t) if upper else (t >= u), w, 0.0).astype(BF16)


def _sgu_fwd(pz, lg, lb, w, bt, name):
    s, sw = pz.shape[0], pz.shape[1] // 2
    groups = sw // GROUP_DIM

    def body(i, ins, consts, outs, accs):
        lgv, lbv, w_ref, btv = consts[0][...], consts[1][...], consts[2], consts[3][...]
        zu = _gelu(ins[0][:, :sw].astype(F32))
        _, _, vn = _sgu_norm(ins[0][:, sw:].astype(F32), lgv, lbv)
        for g in range(groups):
            gs = slice(g * GROUP_DIM, (g + 1) * GROUP_DIM)
            sv = jnp.dot(_causal(w_ref[g]), vn[:, gs], preferred_element_type=F32) + btv[:, g:g + 1]
            outs[0][:, gs] = (zu[:, gs] * sv).astype(BF16)

    return _rowwise(body, name, s, CHUNK, [pz], [lg, lb, w, bt], [_sds((s, sw), BF16)])[0]


def _sgu_bwd(pz, dy, lg, lb, w, wt, bt, name, after=()):
    s, sw = pz.shape[0], pz.shape[1] // 2
    groups = sw // GROUP_DIM

    def body(i, ins, consts, outs, accs):
        lgv, lbv, w_ref, wt_ref, btv = consts[0][...], consts[1][...], consts[2], consts[3], consts[4][...]

        @pl.when(i == 0)
        def _():
            for a in accs:
                a[...] = jnp.zeros_like(a)

        pu, pv = ins[0][:, :sw].astype(F32), ins[0][:, sw:].astype(F32)
        dyv = ins[1][...].astype(F32)
        zu = _gelu(pu)
        xhat, rs, vn = _sgu_norm(pv, lgv, lbv)
        dvn_parts, db_parts = [], []
        lower = lax.broadcasted_iota(jnp.int32, (CHUNK, CHUNK), 0) >= lax.broadcasted_iota(jnp.int32, (CHUNK, CHUNK), 1)
        for g in range(groups):
            gs = slice(g * GROUP_DIM, (g + 1) * GROUP_DIM)
            sv = jnp.dot(_causal(w_ref[g]), vn[:, gs], preferred_element_type=F32) + btv[:, g:g + 1]
            outs[0][:, gs] = (dyv[:, gs] * sv * _gelu_grad(pu[:, gs])).astype(BF16)
            dsv = dyv[:, gs] * zu[:, gs]
            dsvb = dsv.astype(BF16)
            db_parts.append(jnp.sum(dsv, axis=1, keepdims=True))
            accs[2][g] += jnp.where(lower, lax.dot_general(dsvb, vn[:, gs], _DIMS["nt"], preferred_element_type=F32), 0.0)
            dvn_parts.append(jnp.dot(_causal(wt_ref[g], upper=True), dsvb, preferred_element_type=F32))
        dvn = jnp.concatenate(dvn_parts, axis=1)
        accs[3][...] += jnp.concatenate(db_parts, axis=1)
        accs[0][...] += jnp.sum(dvn * xhat, axis=0, keepdims=True)
        accs[1][...] += jnp.sum(dvn, axis=0, keepdims=True)
        dxh = dvn * lgv
        dz = rs * (dxh - jnp.mean(dxh, axis=-1, keepdims=True) - xhat * jnp.mean(dxh * xhat, axis=-1, keepdims=True))
        outs[0][:, sw:] = (dz * _gelu_grad(pv)).astype(BF16)

    return _rowwise(body, name, s, CHUNK, [pz, dy], [lg, lb, w, wt, bt], [_sds((s, 2 * sw), BF16)],
                    [_sds((1, sw), F32), _sds((1, sw), F32), _sds((groups, CHUNK, CHUNK), F32), _sds((CHUNK, groups), F32)],
                    after=after)


def _mesh_place():
    x, y, c = lax.axis_index("x"), lax.axis_index("y"), lax.axis_index("c")
    return x, y, c, 4 * x + 2 * y + c


def _peer(x, y, c, k):
    px, py, pc = x ^ ((k >> 2) & 1), y ^ ((k >> 1) & 1), c ^ (k & 1)
    return (px, py, pc), 4 * px + 2 * py + pc


BY_SLOTS, BY_COLS, BY_PAIRED_COLS = 0, 1, 2


def _col_block(ref, idx, width, cols):
    if cols == BY_PAIRED_COLS:
        idx = (idx % (N_DEV // 2)) * 2 + idx // (N_DEV // 2)
    return ref.at[:, pl.ds(pl.multiple_of(idx * width, LANES), width)]


def _exchange_copy(src_ref, land_ref, send_sems, recv_sems, k, place, scatter, arriving, cols):
    x, y, c, me = place
    peer, pidx = _peer(x, y, c, k)
    slot = pidx if arriving else me
    if scatter:
        src = _col_block(src_ref, pidx, land_ref.shape[-1], cols) if cols else src_ref.at[pidx]
        dst = land_ref.at[slot]
    else:
        src = src_ref
        dst = _col_block(land_ref, slot, src_ref.shape[-1], cols) if cols else land_ref.at[slot]
    return pltpu.make_async_remote_copy(
        src_ref=src, dst_ref=dst, send_sem=send_sems[k - 1], recv_sem=recv_sems[k - 1], device_id=peer,
        device_id_type=MESH_TYPE)


def _own_copy(src_ref, land_ref, sem, place, scatter, cols):
    me = place[3]
    if scatter:
        src = _col_block(src_ref, me, land_ref.shape[-1], cols) if cols else src_ref.at[me]
        dst = land_ref.at[me]
    else:
        src = src_ref
        dst = _col_block(land_ref, me, src_ref.shape[-1], cols) if cols else land_ref.at[me]
    return pltpu.make_async_copy(src, dst, sem)


N_PEERS = N_DEV - 1
N_EXCHANGE_SEMS = 2 * N_PEERS + 1


def _land_shape(a, scatter, cols):
    if scatter:
        return (N_DEV, a.shape[0], a.shape[1] // N_DEV) if cols else a.shape
    return (a.shape[0], N_DEV * a.shape[1]) if cols else (N_DEV,) + a.shape


def _exchange_start(srcs, scatter, cols, name, after):
    n = len(srcs)
    land_shapes = [_land_shape(a, scatter, cl) for a, cl in zip(srcs, cols)]

    def body(*refs):
        src, land = refs[:n], refs[n:2 * n]
        send_sems = refs[2 * n + 1:2 * n + 1 + N_PEERS]
        recv_sems = refs[2 * n + 1 + N_PEERS:2 * n + 1 + 2 * N_PEERS]
        own_sem = refs[2 * n + 1 + 2 * N_PEERS]
        token = refs[-1]
        place = _mesh_place()
        for t in range(n):
            for k in range(1, N_DEV):
                _exchange_copy(src[t], land[t], send_sems, recv_sems, k, place, scatter, False, cols[t]).start()
            _own_copy(src[t], land[t], own_sem, place, scatter, cols[t]).start()
        token[...] = jnp.zeros_like(token)

    return pl.pallas_call(
        body,
        name=name,
        out_shape=(*[pltpu.SemaphoreType.DMA(())] * N_EXCHANGE_SEMS, *[pltpu.HBM(a.shape, a.dtype) for a in srcs],
                   *[pltpu.HBM(shp, a.dtype) for shp, a in zip(land_shapes, srcs)], _sds((SUBLANES, LANES), F32)),
        in_specs=[HBM] * (2 * n) + [ANY],
        out_specs=(*[SEM] * N_EXCHANGE_SEMS, *[HBM] * (2 * n), pl.BlockSpec(memory_space=pltpu.VMEM)),
        input_output_aliases={i: N_EXCHANGE_SEMS + i for i in range(2 * n)},
        compiler_params=pltpu.CompilerParams(has_side_effects=DATAFLOW_EFFECT),
    )(*[pltpu.with_memory_space_constraint(a, pltpu.HBM) for a in srcs],
      *[pltpu.with_memory_space_constraint(lax.empty(shp, a.dtype), pltpu.HBM) for shp, a in zip(land_shapes, srcs)],
      after)


def _exchange_wait(started, after, scatter, cols, name):
    sems = started[:N_EXCHANGE_SEMS]
    thru = started[N_EXCHANGE_SEMS:-1]
    n = len(thru) // 2

    def body(*refs):
        src, land = refs[:n], refs[n:2 * n]
        send_sems = refs[2 * n:2 * n + N_PEERS]
        recv_sems = refs[2 * n + N_PEERS:2 * n + 2 * N_PEERS]
        own_sem = refs[2 * n + 2 * N_PEERS]
        place = _mesh_place()
        for t in range(n):
            for k in range(1, N_DEV):
                cp = _exchange_copy(src[t], land[t], send_sems, recv_sems, k, place, scatter, True, cols[t])
                cp.wait_send()
                cp.wait_recv()
            _own_copy(src[t], land[t], own_sem, place, scatter, cols[t]).wait()

    out = pl.pallas_call(
        body,
        name=name,
        out_shape=tuple(pltpu.HBM(a.shape, a.dtype) for a in thru),
        in_specs=[HBM] * (2 * n) + [SEM] * N_EXCHANGE_SEMS + [ANY],
        out_specs=tuple([HBM] * (2 * n)),
        input_output_aliases={i: i for i in range(2 * n)},
        compiler_params=pltpu.CompilerParams(has_side_effects=DATAFLOW_EFFECT),
    )(*thru, *sems, after)
    return out[:n], out[n:]


def _adamw(w, g, m, v):
    m = ADAM_B1 * m + (1.0 - ADAM_B1) * g
    v = ADAM_B2 * v + (1.0 - ADAM_B2) * (g * g)
    m_hat = m / (1.0 - ADAM_B1 ** ADAM_STEP)
    v_hat = v / (1.0 - ADAM_B2 ** ADAM_STEP)
    delta = -ADAM_LR * (m_hat / (jnp.sqrt(v_hat) + ADAM_EPS) + ADAM_WD * w)
    return delta, m, v


def _adam_rows(r, c):
    fits = [t for t in range(BF16_SUBLANES, r + 1, BF16_SUBLANES) if r % t == 0 and t * c <= ADAM_BLOCK_ELEMS]
    return max(fits) if fits else r


def _adam_body(p_ref, w_ref, m_ref, v_ref, g_out, d_out, m_out, v_out):
    g = p_ref[0].astype(F32)
    for d in range(1, N_DEV):
        g = g + p_ref[d].astype(F32)
    delta, mn, vn = _adamw(w_ref[...], g, m_ref[...], v_ref[...])
    g_out[...] = g
    d_out[...] = delta
    m_out[...] = mn
    v_out[...] = vn


def _reduce_adam_layer(parts, w, m, v, prev, layer, name):
    nl, r, c = w.shape
    tr = _adam_rows(r, c)
    if prev is None:
        prev = [lax.empty((nl, r, c), F32) for _ in range(4)]

    def body(p_ref, w_ref, m_ref, v_ref, *rest):
        _adam_body(p_ref, w_ref, m_ref, v_ref, *rest[4:])

    blk = pl.BlockSpec((None, tr, c), lambda i: (layer, i, 0))
    out = _sds((nl, r, c), F32)
    return pl.pallas_call(
        body,
        name=name,
        grid=(r // tr,),
        in_specs=[pl.BlockSpec((N_DEV, tr, c), lambda i: (0, i, 0)), blk, blk, blk, ANY, ANY, ANY, ANY],
        out_specs=[blk, blk, blk, blk],
        out_shape=[out, out, out, out],
        input_output_aliases={4: 0, 5: 1, 6: 2, 7: 3},
        compiler_params=_params("parallel"),
    )(parts, w, m, v, *prev)


def _reduce_adam(parts, w, m, v, name):
    nl, _, r, c = parts.shape
    tr = _adam_rows(r, c)

    def body(*refs):
        _adam_body(*refs)

    blk = pl.BlockSpec((None, tr, c), lambda l, i: (l, i, 0))
    out = _sds((nl, r, c), F32)
    return pl.pallas_call(
        body,
        name=name,
        grid=(nl, r // tr),
        in_specs=[pl.BlockSpec((None, N_DEV, tr, c), lambda l, i: (l, 0, i, 0)), blk, blk, blk],
        out_specs=[blk, blk, blk, blk],
        out_shape=[out, out, out, out],
        compiler_params=_params("parallel", "parallel"),
    )(parts, w, m, v)


def _pack(arrays):
    flat = []
    for a in arrays:
        a = a.reshape(-1).astype(F32)
        flat.append(jnp.pad(a, (0, (-a.shape[0]) % PACK_UNIT)))
    out = jnp.concatenate(flat)
    rows = out.shape[0] // LANES
    pad_rows = (-rows) % 512
    return jnp.pad(out, (0, pad_rows * LANES)).reshape(rows + pad_rows, LANES)


def _unpack(packed, shapes):
    flat = packed.reshape(-1)
    out, off = [], 0
    for shp in shapes:
        size = math.prod(shp)
        out.append(flat[off:off + size].reshape(shp))
        off += size + (-size) % PACK_UNIT
    return out


def _to_full_cols(g):
    d, k, n = g.shape
    return jnp.transpose(g, (1, 0, 2)).reshape(k, d * n)


def _to_col_shards(a):
    k, n = a.shape
    return jnp.transpose(a.reshape(k, N_DEV, n // N_DEV), (1, 0, 2))


def kernel(x, positions, norm1_g, w_in, b_in, sinks, sgu_ln_g, sgu_ln_b, sgu_w, sgu_b, w_attn_branch, w_sgu_branch, w_out, norm2_g, w_gate_up, w_down, final_g, loss_target, m_norm1_g, m_w_in, m_b_in, m_sinks, m_sgu_ln_g, m_sgu_ln_b, m_sgu_w, m_sgu_b, m_w_attn_branch, m_w_sgu_branch, m_w_out, m_norm2_g, m_w_gate_up, m_w_down, m_final_g, v_norm1_g, v_w_in, v_b_in, v_sinks, v_sgu_ln_g, v_sgu_ln_b, v_sgu_w, v_sgu_b, v_w_attn_branch, v_w_sgu_branch, v_w_out, v_norm2_g, v_w_gate_up, v_w_down, v_final_g):
    nl = w_in.shape[0]
    s, d = x.shape[1], x.shape[2]
    aw = w_attn_branch.shape[1]
    sw = w_sgu_branch.shape[1]
    in_w = w_in.shape[2] * N_DEV
    kw = (in_w - aw - 2 * sw - 2 * d) // 2
    qkv_w = aw + 2 * kw
    groups = sw // GROUP_DIM
    ff = w_down.shape[1] * N_DEV

    h = x.reshape(s, d)
    target = loss_target.reshape(s, d)
    tabs = _rope_tables(positions.reshape(s, 1), "rope_tables")

    big = [w_in, w_attn_branch, w_sgu_branch, w_out, w_gate_up, w_down]
    big_m = [m_w_in, m_w_attn_branch, m_w_sgu_branch, m_w_out, m_w_gate_up, m_w_down]
    big_v = [v_w_in, v_w_attn_branch, v_w_sgu_branch, v_w_out, v_w_gate_up, v_w_down]
    big_names = ("w_in", "w_attn_branch", "w_sgu_branch", "w_out", "w_gate_up", "w_down")
    W_IN, W_AB, W_SB, W_OUT, W_GU, W_DOWN = range(6)
    weight_groups = ((W_IN,), (W_AB, W_SB, W_OUT), (W_GU, W_DOWN))
    grad_groups = ((W_DOWN, W_GU), (W_OUT, W_AB, W_SB), (W_IN,))

    col_sharded = (W_IN, W_AB, W_SB, W_GU)
    by_cols = [BY_COLS if t in col_sharded and big[t].shape[2] % LANES == 0 else BY_SLOTS for t in range(6)]
    assert by_cols[W_GU] == BY_COLS, "the fused swiglu kernels need gate/up column blocks of whole lane tiles"
    by_cols[W_GU] = BY_PAIRED_COLS

    def start_gather(l, group, after):
        return _exchange_start([big[t][l].astype(BF16) for t in group], False, tuple(by_cols[t] for t in group),
                               f"gather_start_l{l}_{big_names[group[0]]}", after)

    def full_weight(t, land):
        if by_cols[t]:
            return land
        return _to_full_cols(land) if t in col_sharded else land.reshape(N_DEV * land.shape[1], land.shape[2])

    saved = []
    started = {}
    token = h
    for l in range(nl):
        for ll in ((0, 1) if l == 0 else (l + 1,)):
            if ll < nl:
                for group in weight_groups:
                    started[(ll, group)] = start_gather(ll, group, token)
                    token = started[(ll, group)][-1]
        gathered = {}

        def weight(t, after, l=l, gathered=gathered):
            if t not in gathered:
                group = next(g for (ll, g) in started if ll == l and t in g)
                srcs, lands = _exchange_wait(started.pop((l, group)), after, False, tuple(by_cols[tt] for tt in group),
                                             f"gather_wait_l{l}_{big_names[group[0]]}")
                for tt, ld in zip(group, lands):
                    gathered[tt] = full_weight(tt, ld)
            return gathered[t]

        bias = b_in[l].reshape(1, in_w)
        g1, g2 = norm1_g[l].reshape(1, d), norm2_g[l].reshape(1, d)
        lg, lb = sgu_ln_g[l].reshape(1, sw), sgu_ln_b[l].reshape(1, sw)
        bt = sgu_b[l].T

        xn = _rms_fwd(h, g1, "rms1_fwd", after=(token,))
        w_in_f = weight(W_IN, xn)
        wts = dict(qkv=w_in_f[:, :qkv_w], z=w_in_f[:, qkv_w:qkv_w + 2 * sw], g=w_in_f[:, qkv_w + 2 * sw:])
        qkv = _matmul(xn, wts["qkv"], "nn", BF16, "proj_qkv", bias=bias[:, :qkv_w])
        pz = _matmul(xn, wts["z"], "nn", BF16, "proj_z", bias=bias[:, qkv_w:qkv_w + 2 * sw])
        pg = _matmul(xn, wts["g"], "nn", BF16, "proj_g", bias=bias[:, qkv_w + 2 * sw:])
        y_attn = _attn_fwd(qkv, tabs, sinks[l], aw, kw, "attn_fwd")
        y_sgu = _sgu_fwd(pz, lg, lb, sgu_w[l], bt, "sgu_fwd")
        wts.update(ab=weight(W_AB, y_sgu), sb=weight(W_SB, y_sgu), out=weight(W_OUT, y_sgu))
        a_br = _matmul(y_attn, wts["ab"], "nn", BF16, "attn_branch")
        s_br = _matmul(y_sgu, wts["sb"], "nn", BF16, "sgu_branch")
        merged = _merge_fwd(pg, a_br, s_br, "merge_fwd")
        h_mid = _matmul(merged, wts["out"], "nn", F32, "out_proj", res=h)
        hn = _rms_fwd(h_mid, g2, "rms2_fwd")
        wts.update(gu=weight(W_GU, hn), down=weight(W_DOWN, hn))
        gu, act = _gate_up_swiglu(hn, wts["gu"], "gate_up")
        h_out = _matmul(act, wts["down"], "nn", F32, "down_proj", res=h_mid)
        saved.append(dict(wts=wts, h=h, xn=xn, qkv=qkv, pz=pz, pg=pg, y_attn=y_attn, y_sgu=y_sgu, a_br=a_br,
                          s_br=s_br, merged=merged, h_mid=h_mid, hn=hn, gu=gu, act=act,
                          g1=g1, g2=g2, lg=lg, lb=lb, bt=bt))
        h = h_out

    dh, dhb, d_final_g, loss_blk = _loss_head(h, final_g.reshape(1, d), target, "loss_head")

    small = {n: [None] * nl for n in ("norm1_g", "b_in", "sinks", "sgu_ln_g", "sgu_ln_b", "sgu_w", "sgu_b", "norm2_g")}
    scattering = {}

    def start_scatter(l, group, grads, after):
        sends = []
        for t, dw in zip(group, grads):
            if by_cols[t]:
                sends.append(dw)
            elif t in col_sharded:
                sends.append(_to_col_shards(dw))
            else:
                sends.append(dw.reshape(N_DEV, dw.shape[0] // N_DEV, dw.shape[1]))
        scattering[(l, group)] = _exchange_start(sends, True, tuple(by_cols[t] for t in group),
                                                 f"scatter_start_l{l}_{big_names[group[0]]}", after)
        return scattering[(l, group)][-1]

    for l in reversed(range(nl)):
        sv = saved[l]
        wts = sv["wts"]
        d_gu = _d_act_swiglu(dhb, wts["down"], sv["gu"], "d_act")
        dw_down = _matmul(sv["act"], dhb, "tn", BF16, "dw_down")
        dw_gu = _matmul(sv["hn"], d_gu, "tn", BF16, "dw_gate_up")
        token = start_scatter(l, grad_groups[0], [dw_down, dw_gu], token)
        d_hn = _matmul(d_gu, wts["gu"], "nt", BF16, "d_hn", after=token)
        dh_mid, dmb, dg2 = _rms_bwd(sv["h_mid"], sv["g2"], d_hn, dh, "rms2_bwd")
        d_merged = _matmul(dmb, wts["out"], "nt", BF16, "d_merged")
        dw_out = _matmul(sv["merged"], dmb, "tn", BF16, "dw_out")
        d_a, d_s, d_pg = _merge_bwd(sv["pg"], sv["a_br"], sv["s_br"], d_merged, "merge_bwd")
        d_y_attn = _matmul(d_a, wts["ab"], "nt", BF16, "d_y_attn")
        dw_ab = _matmul(sv["y_attn"], d_a, "tn", BF16, "dw_attn_branch")
        d_y_sgu = _matmul(d_s, wts["sb"], "nt", BF16, "d_y_sgu")
        dw_sb = _matmul(sv["y_sgu"], d_s, "tn", BF16, "dw_sgu_branch")
        token = start_scatter(l, grad_groups[1], [dw_out, dw_ab, dw_sb], token)
        d_pz, d_lg, d_lb, d_sw, d_sbt = _sgu_bwd(sv["pz"], d_y_sgu, sv["lg"], sv["lb"], sgu_w[l],
                                                 jnp.transpose(sgu_w[l], (0, 2, 1)), sv["bt"], "sgu_bwd", after=(token,))
        d_q, d_kv, d_sinks = _attn_bwd(sv["qkv"], tabs, sinks[l], sv["y_attn"], d_y_attn, aw, kw, "attn_bwd")
        d_qkv = jnp.concatenate([d_q, d_kv], axis=1)
        dw_qkv = _matmul(sv["xn"], d_qkv, "tn", BF16, "dw_qkv")
        dw_z = _matmul(sv["xn"], d_pz, "tn", BF16, "dw_z")
        dw_g = _matmul(sv["xn"], d_pg, "tn", BF16, "dw_g")
        token = start_scatter(l, grad_groups[2], [jnp.concatenate([dw_qkv, dw_z, dw_g], axis=1)], token)
        d_xn = _matmul(d_qkv, wts["qkv"], "nt", F32, "d_xn_qkv", after=token)
        d_xn = _matmul(d_pz, wts["z"], "nt", F32, "d_xn_z", res=d_xn)
        d_xn = _matmul(d_pg, wts["g"], "nt", F32, "d_xn_g", res=d_xn)
        dh, dhb, dg1 = _rms_bwd(sv["h"], sv["g1"], d_xn, dh_mid, "rms1_bwd")

        small["norm1_g"][l], small["norm2_g"][l] = dg1, dg2
        small["b_in"][l] = jnp.concatenate([_colsum(d_qkv, "db_qkv"), _colsum(d_pz, "db_z"), _colsum(d_pg, "db_g")], axis=1)
        small["sinks"][l] = d_sinks[0, :aw // HEAD_DIM]
        small["sgu_ln_g"][l], small["sgu_ln_b"][l] = d_lg, d_lb
        small["sgu_w"][l] = d_sw
        small["sgu_b"][l] = d_sbt.T

    grad_x = dh.reshape(x.shape)

    names = ["norm1_g", "b_in", "sinks", "sgu_ln_g", "sgu_ln_b", "sgu_w", "sgu_b", "norm2_g"]
    small_w = [norm1_g, b_in, sinks, sgu_ln_g, sgu_ln_b, sgu_w, sgu_b, norm2_g, final_g]
    small_m = [m_norm1_g, m_b_in, m_sinks, m_sgu_ln_g, m_sgu_ln_b, m_sgu_w, m_sgu_b, m_norm2_g, m_final_g]
    small_v = [v_norm1_g, v_b_in, v_sinks, v_sgu_ln_g, v_sgu_ln_b, v_sgu_w, v_sgu_b, v_norm2_g, v_final_g]
    shapes = [w.shape for w in small_w] + [(1,)]
    partial = [jnp.stack([p.reshape(w.shape[1:]) for p in small[n]]) for n, w in zip(names, small_w)]
    partial += [d_final_g.reshape(final_g.shape), loss_blk[0, :1]]
    zero = jnp.zeros((1,), F32)
    small_started = _exchange_start([_pack(partial)], False, (False,), "gather_start_small_grads", dhb)

    big_out = [None] * len(big)
    after = small_started[-1]
    for l in reversed(range(nl)):
        for group in grad_groups:
            srcs, lands = _exchange_wait(scattering.pop((l, group)), after, True, tuple(by_cols[t] for t in group),
                                         f"scatter_wait_l{l}_{big_names[group[0]]}")
            for t, parts in zip(group, lands):
                big_out[t] = _reduce_adam_layer(parts, big[t], big_m[t], big_v[t], big_out[t], l, f"adam_{big_names[t]}")
                after = big_out[t][0]

    srcs, lands = _exchange_wait(small_started, after, False, (False,), "gather_wait_small_grads")
    sm = _reduce_adam(lands[0][None], _pack(small_w + [zero])[None], _pack(small_m + [zero])[None],
                      _pack(small_v + [zero])[None], "adam_small")
    sm_g, sm_d, sm_m, sm_v = [_unpack(a[0], shapes) for a in sm]
    loss = sm_g[-1].reshape(())

    def ordered(kind_small, kind_big):
        by_name = dict(zip(["norm1_g", "b_in", "sinks", "sgu_ln_g", "sgu_ln_b", "sgu_w", "sgu_b", "norm2_g", "final_g"], kind_small))
        by_name.update(zip(["w_in", "w_attn_branch", "w_sgu_branch", "w_out", "w_gate_up", "w_down"], kind_big))
        order = ["norm1_g", "w_in", "b_in", "sinks", "sgu_ln_g", "sgu_ln_b", "sgu_w", "sgu_b", "w_attn_branch",
                 "w_sgu_branch", "w_out", "norm2_g", "w_gate_up", "w_down", "final_g"]
        return [by_name[n] for n in order]

    outs = [loss, grad_x]
    for idx, sm_kind in enumerate((sm_g, sm_d, sm_m, sm_v)):
        outs += ordered(sm_kind[:-1], [o[idx] for o in big_out])
    return tuple(outs)
```

```python
import math

import jax
import jax.numpy as jnp
from jax import lax
from jax.experimental import pallas as pl
from jax.experimental.pallas import tpu as pltpu

F32 = jnp.float32
BF16 = jnp.bfloat16

N_DEV = 8
HEAD_DIM = 64
WINDOW = 128
CHUNK = 128
GROUP_DIM = 128
ROPE_DIM = HEAD_DIM // 4
ROPE_HALF = ROPE_DIM // 2
ROPE_THETA = 500000.0
EPS = 1e-5
NEG = -1e30
ATTN_SCALE = HEAD_DIM ** -0.5
ADAM_LR = 0.001
ADAM_B1 = 0.9
ADAM_B2 = 0.999
ADAM_EPS = 1e-08
ADAM_WD = 0.01
ADAM_STEP = 10
LANES = 128
SUBLANES = 8
BF16_SUBLANES = 16
PACK_UNIT = SUBLANES * LANES
ADAM_BLOCK_ELEMS = 256 * 1024
V7X_VMEM_LIMIT_BYTES = 56 * 1024 * 1024
MATMUL_TILE_PREFS = (1024, 1408, 768, 512, 384, 256, 128)
MATMUL_WHOLE_K = 2048
MATMUL_TN_K = 4096
MATMUL_VMEM_BUDGET_BYTES = 52 * 1024 * 1024
MATMUL_K_PREFS = (2816, 2048, 1536, 1408, 1024, 768, 512, 384, 256, 128)
ROW_TILE_PREFS = (512, 256, 128)
SWIGLU_ROW_PREFS = (512, 256, 128)
MERGE_ROW_PREFS = (512, 256, 128)
GATE_BLOCK_PREFS = (1024, 512, 256, 128)
MESH_TYPE = pl.DeviceIdType.MESH
ANY = pl.BlockSpec(memory_space=pl.ANY)
HBM = pl.BlockSpec(memory_space=pltpu.HBM)
SEM = pl.BlockSpec(memory_space=pltpu.SEMAPHORE)
DATAFLOW_EFFECT = pltpu.SideEffectType.DATAFLOW_SIDE_EFFECTING


def _pick(n, prefs):
    for p in prefs:
        if n % p == 0:
            return p
    return n


def _params(*sem):
    return pltpu.CompilerParams(dimension_semantics=sem, vmem_limit_bytes=V7X_VMEM_LIMIT_BYTES)


_DIMS = {"nn": (((1,), (0,)), ((), ())), "nt": (((1,), (1,)), ((), ())), "tn": (((0,), (0,)), ((), ()))}


def _matmul(a, b, mode, out_dtype, name, bias=None, res=None, after=None):
    if mode == "nn":
        (m, k), n = a.shape, b.shape[1]
    elif mode == "nt":
        (m, k), n = a.shape, b.shape[0]
    else:
        (k, m), n = a.shape, b.shape[1]
    tm, tn = _pick(m, MATMUL_TILE_PREFS), _pick(n, MATMUL_TILE_PREFS)
    if k <= MATMUL_WHOLE_K:
        tk = k
    else:
        fits = [t for t in ((MATMUL_TN_K,) if mode == "tn" else ()) + MATMUL_K_PREFS
                if k % t == 0 and 4 * t * (tm + tn) + 16 * tm * tn <= MATMUL_VMEM_BUDGET_BYTES]
        tk = fits[0]
    nk = k // tk
    dims = _DIMS[mode]
    a_spec = pl.BlockSpec((tk, tm), lambda i, j, kk: (kk, i)) if mode == "tn" else pl.BlockSpec((tm, tk), lambda i, j, kk: (i, kk))
    b_spec = pl.BlockSpec((tn, tk), lambda i, j, kk: (j, kk)) if mode == "nt" else pl.BlockSpec((tk, tn), lambda i, j, kk: (kk, j))
    in_specs, args = [a_spec, b_spec], [a, b]
    if bias is not None:
        in_specs.append(pl.BlockSpec((1, tn), lambda i, j, kk: (0, j)))
        args.append(bias)
    if res is not None:
        in_specs.append(pl.BlockSpec((tm, tn), lambda i, j, kk: (i, j)))
        args.append(res)
    if after is not None:
        in_specs.append(ANY)
        args.append(after)

    def body(*refs):
        a_ref, b_ref = refs[0], refs[1]
        pos = 2
        bias_ref = res_ref = None
        if bias is not None:
            bias_ref = refs[pos]
            pos += 1
        if res is not None:
            res_ref = refs[pos]
            pos += 1
        if after is not None:
            pos += 1
        o_ref = refs[pos]

        def finish(r):
            if bias_ref is not None:
                r = r + bias_ref[...]
            if res_ref is not None:
                r = r + res_ref[...]
            o_ref[...] = r.astype(out_dtype)

        part = lax.dot_general(a_ref[...], b_ref[...], dims, preferred_element_type=F32)
        if nk == 1:
            finish(part)
        else:
            acc_ref = refs[pos + 1]
            kk = pl.program_id(2)

            @pl.when(kk == 0)
            def _():
                acc_ref[...] = part

            @pl.when((kk > 0) & (kk < nk - 1))
            def _():
                acc_ref[...] += part

            @pl.when(kk == nk - 1)
            def _():
                finish(acc_ref[...] + part)

    return pl.pallas_call(
        body,
        name=name,
        grid=(m // tm, n // tn, nk),
        in_specs=in_specs,
        out_specs=pl.BlockSpec((tm, tn), lambda i, j, kk: (i, j)),
        out_shape=jax.ShapeDtypeStruct((m, n), out_dtype),
        scratch_shapes=[] if nk == 1 else [pltpu.VMEM((tm, tn), F32)],
        compiler_params=_params("parallel", "parallel", "arbitrary"),
    )(*args)


def _rowwise(body, name, rows, tr, ins, consts, outs, accs=(), after=()):
    n_in, n_c, n_o, n_a = len(ins), len(consts), len(outs), len(after)

    def wrapped(*refs):
        body(pl.program_id(0), refs[:n_in], refs[n_in:n_in + n_c], refs[n_in + n_c + n_a:n_in + n_c + n_a + n_o],
             refs[n_in + n_c + n_a + n_o:])

    def whole(shape):
        zeros = (0,) * len(shape)
        return pl.BlockSpec(tuple(shape), lambda i: zeros)

    in_specs = ([pl.BlockSpec((tr, a.shape[1]), lambda i: (i, 0)) for a in ins] + [whole(c.shape) for c in consts]
                + [ANY] * n_a)
    out_specs = [pl.BlockSpec((tr, o.shape[1]), lambda i: (i, 0)) for o in outs] + [whole(a.shape) for a in accs]
    return pl.pallas_call(
        wrapped,
        name=name,
        grid=(rows // tr,),
        in_specs=in_specs,
        out_specs=out_specs,
        out_shape=list(outs) + list(accs),
        compiler_params=_params("arbitrary" if accs else "parallel"),
    )(*ins, *consts, *after)


def _sds(shape, dtype):
    return jax.ShapeDtypeStruct(tuple(shape), dtype)


def _rms_fwd(h, g, name, after=()):
    s, d = h.shape
    tr = _pick(s, ROW_TILE_PREFS)

    def body(i, ins, consts, outs, accs):
        x = ins[0][...]
        r = lax.rsqrt(jnp.mean(x * x, axis=-1, keepdims=True) + EPS)
        outs[0][...] = (x * r * consts[0][...]).astype(BF16)

    return _rowwise(body, name, s, tr, [h], [g], [_sds((s, d), BF16)], after=after)[0]


def _rms_bwd(h, g, dy, dh_up, name):
    s, d = h.shape
    tr = _pick(s, ROW_TILE_PREFS)

    def body(i, ins, consts, outs, accs):
        x, dyv, up = ins[0][...], ins[1][...].astype(F32), ins[2][...]
        r = lax.rsqrt(jnp.mean(x * x, axis=-1, keepdims=True) + EPS)
        xr = x * r
        gy = dyv * consts[0][...]
        dx = r * (gy - xr * jnp.mean(gy * xr, axis=-1, keepdims=True))
        outs[0][...] = up + dx
        outs[1][...] = (up + dx).astype(BF16)

        @pl.when(i == 0)
        def _():
            accs[0][...] = jnp.zeros_like(accs[0])

        accs[0][...] += jnp.sum(dyv * xr, axis=0, keepdims=True)

    return _rowwise(body, name, s, tr, [h, dy, dh_up], [g], [_sds((s, d), F32), _sds((s, d), BF16)], [_sds((1, d), F32)])


def _loss_head(h, g, target, name):
    s, d = h.shape
    tr = _pick(s, ROW_TILE_PREFS)

    def body(i, ins, consts, outs, accs):
        x, t = ins[0][...], ins[1][...]
        gv = consts[0][...]
        r = lax.rsqrt(jnp.mean(x * x, axis=-1, keepdims=True) + EPS)
        xr = x * r
        diff = xr * gv - t
        dyv = diff * (1.0 / d)
        gy = dyv * gv
        dx = r * (gy - xr * jnp.mean(gy * xr, axis=-1, keepdims=True))
        outs[0][...] = dx
        outs[1][...] = dx.astype(BF16)

        @pl.when(i == 0)
        def _():
            accs[0][...] = jnp.zeros_like(accs[0])
            accs[1][...] = jnp.zeros_like(accs[1])

        accs[0][...] += jnp.sum(dyv * xr, axis=0, keepdims=True)
        part = 0.5 * jnp.sum(jnp.mean(diff * diff, axis=-1, keepdims=True), axis=0, keepdims=True)
        accs[1][...] += jnp.broadcast_to(part, accs[1].shape)

    return _rowwise(body, name, s, tr, [h, target], [g], [_sds((s, d), F32), _sds((s, d), BF16)],
                    [_sds((1, d), F32), _sds((SUBLANES, LANES), F32)])


def _colsum(a, name):
    s, w = a.shape
    tr = _pick(s, ROW_TILE_PREFS)

    def body(i, ins, consts, outs, accs):
        @pl.when(i == 0)
        def _():
            accs[0][...] = jnp.zeros_like(accs[0])

        accs[0][...] += jnp.sum(ins[0][...].astype(F32), axis=0, keepdims=True)

    return _rowwise(body, name, s, tr, [a], [], [], [_sds((1, w), F32)])[0]


def _sigmoid(x):
    return 1.0 / (1.0 + jnp.exp(-x))


def _branches_merge(y_attn, y_sgu, w_ab, w_sb, pg, gb, name):
    s, aw = y_attn.shape
    sw, d = w_sb.shape
    tm = _pick(s, MERGE_ROW_PREFS)

    def body(ya_ref, ys_ref, wa_ref, ws_ref, pg_ref, a_out, s_out, m_out):
        a = jnp.dot(ya_ref[...], wa_ref[...], preferred_element_type=F32)
        b = jnp.dot(ys_ref[...], ws_ref[...], preferred_element_type=F32)
        ga, gs = _sigmoid(pg_ref[:, :gb].astype(F32)), _sigmoid(pg_ref[:, gb:].astype(F32))
        a_out[...] = a.astype(BF16)
        s_out[...] = b.astype(BF16)
        m_out[...] = (ga * a + gs * b).astype(BF16)

    blk = pl.BlockSpec((tm, gb), lambda i, j: (i, j))
    out = _sds((s, d), BF16)
    return pl.pallas_call(
        body,
        name=name,
        grid=(s // tm, d // gb),
        in_specs=[pl.BlockSpec((tm, aw), lambda i, j: (i, 0)), pl.BlockSpec((tm, sw), lambda i, j: (i, 0)),
                  pl.BlockSpec((aw, gb), lambda i, j: (0, j)), pl.BlockSpec((sw, gb), lambda i, j: (0, j)),
                  pl.BlockSpec((tm, 2 * gb), lambda i, j: (i, j))],
        out_specs=[blk, blk, blk],
        out_shape=[out, out, out],
        compiler_params=_params("parallel", "parallel"),
    )(y_attn, y_sgu, w_ab, w_sb, pg)


def _d_merged_merge_bwd(dmb, w_out, pg, a_br, s_br, gb, name):
    s, d = dmb.shape
    tm = _pick(s, MERGE_ROW_PREFS)

    def body(a_ref, b_ref, pg_ref, ab_ref, sb_ref, da_out, ds_out, dpg_out, db_out):
        dm = lax.dot_general(a_ref[...], b_ref[...], _DIMS["nt"], preferred_element_type=F32)
        ga, gs = _sigmoid(pg_ref[:, :gb].astype(F32)), _sigmoid(pg_ref[:, gb:].astype(F32))
        da_out[...] = (dm * ga).astype(BF16)
        ds_out[...] = (dm * gs).astype(BF16)
        dpa = dm * ab_ref[...].astype(F32) * ga * (1.0 - ga)
        dps = dm * sb_ref[...].astype(F32) * gs * (1.0 - gs)
        dpg_out[:, :gb] = dpa.astype(BF16)
        dpg_out[:, gb:] = dps.astype(BF16)

        @pl.when(pl.program_id(1) == 0)
        def _():
            db_out[...] = jnp.zeros_like(db_out)

        db_out[:, :gb] += jnp.sum(dpa, axis=0, keepdims=True)
        db_out[:, gb:] += jnp.sum(dps, axis=0, keepdims=True)

    blk = pl.BlockSpec((tm, gb), lambda j, i: (i, j))
    pair = pl.BlockSpec((tm, 2 * gb), lambda j, i: (i, j))
    return pl.pallas_call(
        body,
        name=name,
        grid=(d // gb, s // tm),
        in_specs=[pl.BlockSpec((tm, d), lambda j, i: (i, 0)), pl.BlockSpec((gb, d), lambda j, i: (j, 0)), pair, blk, blk],
        out_specs=[blk, blk, pair, pl.BlockSpec((1, 2 * gb), lambda j, i: (0, j))],
        out_shape=[_sds((s, d), BF16), _sds((s, d), BF16), _sds((s, 2 * d), BF16), _sds((1, 2 * d), F32)],
        compiler_params=_params("parallel", "arbitrary"),
    )(dmb, w_out, pg, a_br, s_br)


def _gate_up_swiglu(hn, w_gu, name):
    s, d = hn.shape
    n2 = w_gu.shape[1]
    fb = n2 // N_DEV
    tm = _pick(s, SWIGLU_ROW_PREFS)

    def body(a_ref, b_ref, gu_ref, act_ref):
        r = jnp.dot(a_ref[...], b_ref[...], preferred_element_type=F32)
        gu_ref[...] = r.astype(BF16)
        gate, up = r[:, :fb], r[:, fb:]
        act_ref[...] = (gate * _sigmoid(gate) * up).astype(BF16)

    return pl.pallas_call(
        body,
        name=name,
        grid=(s // tm, N_DEV // 2),
        in_specs=[pl.BlockSpec((tm, d), lambda i, j: (i, 0)), pl.BlockSpec((d, 2 * fb), lambda i, j: (0, j))],
        out_specs=[pl.BlockSpec((tm, 2 * fb), lambda i, j: (i, j)), pl.BlockSpec((tm, fb), lambda i, j: (i, j))],
        out_shape=[_sds((s, n2), BF16), _sds((s, n2 // 2), BF16)],
        compiler_params=_params("parallel", "parallel"),
    )(hn, w_gu)


def _d_act_swiglu(dhb, w_down, gu, name):
    s, d = dhb.shape
    n2 = gu.shape[1]
    fb = n2 // N_DEV
    tm = _pick(s, SWIGLU_ROW_PREFS)

    def body(a_ref, b_ref, gu_ref, o_ref):
        da = lax.dot_general(a_ref[...], b_ref[...], _DIMS["nt"], preferred_element_type=F32)
        gate, up = gu_ref[:, :fb].astype(F32), gu_ref[:, fb:].astype(F32)
        sg = _sigmoid(gate)
        o_ref[:, :fb] = (da * up * sg * (1.0 + gate * (1.0 - sg))).astype(BF16)
        o_ref[:, fb:] = (da * gate * sg).astype(BF16)

    pair = pl.BlockSpec((tm, 2 * fb), lambda i, j: (i, j))
    return pl.pallas_call(
        body,
        name=name,
        grid=(s // tm, N_DEV // 2),
        in_specs=[pl.BlockSpec((tm, d), lambda i, j: (i, 0)), pl.BlockSpec((fb, d), lambda i, j: (j, 0)), pair],
        out_specs=pair,
        out_shape=_sds((s, n2), BF16),
        compiler_params=_params("parallel", "parallel"),
    )(dhb, w_down, gu)


def _rope_tables(pos_col, name):
    s = pos_col.shape[0]
    tr = _pick(s, (1024, 512, 256, 128))
    inv = ROPE_THETA ** (-jnp.arange(0, ROPE_DIM, 2, dtype=F32) / ROPE_DIM)
    lane = jnp.arange(LANES)
    inv_lanes = inv[lane % ROPE_HALF].reshape(1, LANES)

    def body(i, ins, consts, outs, accs):
        ang = ins[0][...].astype(F32) * consts[0][...]
        c, sn = jnp.cos(ang), jnp.sin(ang)
        in_head = lax.broadcasted_iota(jnp.int32, ang.shape, 1) % HEAD_DIM
        outs[0][:, 0:LANES] = jnp.where(in_head < ROPE_DIM, c, 1.0)
        outs[0][:, LANES:2 * LANES] = jnp.where(in_head < ROPE_HALF, -sn, 0.0)
        outs[0][:, 2 * LANES:] = jnp.where((in_head >= ROPE_HALF) & (in_head < ROPE_DIM), sn, 0.0)

    return _rowwise(body, name, s, tr, [pos_col], [inv_lanes], [_sds((s, 3 * LANES), F32)])[0]


def _rope(x, tab, inverse=False):
    width = x.shape[1]
    reps = width // LANES
    c = jnp.tile(tab[:, 0:LANES], (1, reps))
    lo = jnp.tile(tab[:, LANES:2 * LANES], (1, reps))
    hi = jnp.tile(tab[:, 2 * LANES:], (1, reps))
    if inverse:
        lo, hi = -lo, -hi
    return x * c + pltpu.roll(x, width - ROPE_HALF, 1) * lo + pltpu.roll(x, ROPE_HALF, 1) * hi


def _attn_specs(aw, kw):
    kb = aw // kw
    prev = lambda i: jnp.maximum(i - 1, 0)
    return [
        pl.BlockSpec(memory_space=pltpu.SMEM),
        pl.BlockSpec((WINDOW, aw), lambda i: (i, 0)),
        pl.BlockSpec((WINDOW, kw), lambda i: (i, kb)),
        pl.BlockSpec((WINDOW, kw), lambda i: (prev(i), kb)),
        pl.BlockSpec((WINDOW, kw), lambda i: (i, kb + 1)),
        pl.BlockSpec((WINDOW, kw), lambda i: (prev(i), kb + 1)),
        pl.BlockSpec((WINDOW, 3 * LANES), lambda i: (i, 0)),
        pl.BlockSpec((WINDOW, 3 * LANES), lambda i: (prev(i), 0)),
    ]


def _attn_common(i, q_ref, kc_ref, kp_ref, vc_ref, vp_ref, tq_ref, tp_ref):
    tq, tp = tq_ref[...], tp_ref[...]
    q = _rope(q_ref[...].astype(F32), tq).astype(BF16)
    kc = _rope(kc_ref[...].astype(F32), tq)
    kp = _rope(kp_ref[...].astype(F32), tp)
    k2 = jnp.concatenate([kp, kc], axis=0).astype(BF16)
    v2 = jnp.concatenate([vp_ref[...], vc_ref[...]], axis=0)
    qi = lax.broadcasted_iota(jnp.int32, (WINDOW, 2 * WINDOW), 0)
    kj = lax.broadcasted_iota(jnp.int32, (WINDOW, 2 * WINDOW), 1)
    rel = qi + WINDOW - kj
    ok = (rel >= 0) & (rel < WINDOW) & ((kj >= WINDOW) | (i > 0))
    return q, k2, v2, ok, tq, tp


def _head_probs(qh, kg, ok, sink):
    s = lax.dot_general(qh, kg, _DIMS["nt"], preferred_element_type=F32) * ATTN_SCALE
    s = jnp.where(ok, s, NEG)
    m = jnp.maximum(jnp.max(s, axis=1, keepdims=True), sink)
    p = jnp.exp(s - m)
    es = jnp.exp(sink - m)
    inv = 1.0 / (jnp.sum(p, axis=1, keepdims=True) + es)
    return p * inv, es * inv


def _attn_fwd(qkv, tabs, sinks, aw, kw, name):
    s = qkv.shape[0]
    nq, nkv = aw // HEAD_DIM, kw // HEAD_DIM
    qpk = nq // nkv

    def body(s_ref, q_ref, kc_ref, kp_ref, vc_ref, vp_ref, tq_ref, tp_ref, o_ref):
        i = pl.program_id(0)
        q, k2, v2, ok, _, _ = _attn_common(i, q_ref, kc_ref, kp_ref, vc_ref, vp_ref, tq_ref, tp_ref)
        for h in range(nq):
            g = h // qpk
            hs, gs = slice(h * HEAD_DIM, (h + 1) * HEAD_DIM), slice(g * HEAD_DIM, (g + 1) * HEAD_DIM)
            pn, _ = _head_probs(q[:, hs], k2[:, gs], ok, s_ref[h])
            o = jnp.dot(pn.astype(BF16), v2[:, gs], preferred_element_type=F32)
            o_ref[:, hs] = o.astype(BF16)

    return pl.pallas_call(
        body,
        name=name,
        grid=(s // WINDOW,),
        in_specs=_attn_specs(aw, kw),
        out_specs=pl.BlockSpec((WINDOW, aw), lambda i: (i, 0)),
        out_shape=_sds((s, aw), BF16),
        compiler_params=_params("parallel"),
    )(sinks, qkv, qkv, qkv, qkv, qkv, tabs, tabs)


def _attn_bwd(qkv, tabs, sinks, o, do, aw, kw, name):
    s = qkv.shape[0]
    nb = s // WINDOW
    nq, nkv = aw // HEAD_DIM, kw // HEAD_DIM
    qpk = nq // nkv

    def body(s_ref, q_ref, kc_ref, kp_ref, vc_ref, vp_ref, tq_ref, tp_ref, o_ref, do_ref,
             dq_ref, dkv_ref, ds_ref, ck_ref, cv_ref):
        i = pl.program_id(0)

        @pl.when(i == 0)
        def _():
            ck_ref[...] = jnp.zeros_like(ck_ref)
            cv_ref[...] = jnp.zeros_like(cv_ref)
            ds_ref[...] = jnp.zeros_like(ds_ref)

        q, k2, v2, ok, tq, tp = _attn_common(i, q_ref, kc_ref, kp_ref, vc_ref, vp_ref, tq_ref, tp_ref)
        dov, ov = do_ref[...], o_ref[...]
        row0 = lax.broadcasted_iota(jnp.int32, (SUBLANES, LANES), 0) == 0
        lane = lax.broadcasted_iota(jnp.int32, (SUBLANES, LANES), 1)
        dsink = jnp.zeros((SUBLANES, LANES), F32)
        dq_parts, dk_parts, dv_parts = [], [], []
        for g in range(nkv):
            gs = slice(g * HEAD_DIM, (g + 1) * HEAD_DIM)
            kg, vg = k2[:, gs], v2[:, gs]
            dk_g = jnp.zeros((2 * WINDOW, HEAD_DIM), F32)
            dv_g = jnp.zeros((2 * WINDOW, HEAD_DIM), F32)
            for j in range(qpk):
                h = g * qpk + j
                hs = slice(h * HEAD_DIM, (h + 1) * HEAD_DIM)
                qh, doh = q[:, hs], dov[:, hs]
                pn, psink = _head_probs(qh, kg, ok, s_ref[h])
                delta = jnp.sum(doh.astype(F32) * ov[:, hs].astype(F32), axis=1, keepdims=True)
                dp = lax.dot_general(doh, vg, _DIMS["nt"], preferred_element_type=F32)
                dsb = (pn * (dp - delta)).astype(BF16)
                dsink = dsink + jnp.where(row0 & (lane == h), -jnp.sum(psink * delta, axis=0, keepdims=True), 0.0)
                dq_parts.append(jnp.dot(dsb, kg, preferred_element_type=F32) * ATTN_SCALE)
                dk_g = dk_g + lax.dot_general(dsb, qh, _DIMS["tn"], preferred_element_type=F32) * ATTN_SCALE
                dv_g = dv_g + lax.dot_general(pn.astype(BF16), doh, _DIMS["tn"], preferred_element_type=F32)
            dk_parts.append(dk_g)
            dv_parts.append(dv_g)
        ds_ref[...] += dsink
        dq_ref[...] = _rope(jnp.concatenate(dq_parts, axis=1), tq, inverse=True).astype(BF16)
        dk2 = jnp.concatenate(dk_parts, axis=1)
        dv2 = jnp.concatenate(dv_parts, axis=1)
        dk_prev = _rope(ck_ref[...] + dk2[:WINDOW], tp, inverse=True)
        dv_prev = cv_ref[...] + dv2[:WINDOW]

        @pl.when(i > 0)
        def _():
            dkv_ref[pl.ds(pl.multiple_of((i - 1) * WINDOW, WINDOW), WINDOW), :] = jnp.concatenate(
                [dk_prev, dv_prev], axis=1).astype(BF16)

        ck_ref[...] = dk2[WINDOW:]
        cv_ref[...] = dv2[WINDOW:]

        @pl.when(i == nb - 1)
        def _():
            dkv_ref[pl.ds(pl.multiple_of(i * WINDOW, WINDOW), WINDOW), :] = jnp.concatenate(
                [_rope(dk2[WINDOW:], tq, inverse=True), dv2[WINDOW:]], axis=1).astype(BF16)

    blk = pl.BlockSpec((WINDOW, aw), lambda i: (i, 0))
    return pl.pallas_call(
        body,
        name=name,
        grid=(nb,),
        in_specs=_attn_specs(aw, kw) + [blk, blk],
        out_specs=[blk, pl.BlockSpec((s, 2 * kw), lambda i: (0, 0)), pl.BlockSpec((SUBLANES, LANES), lambda i: (0, 0))],
        out_shape=[_sds((s, aw), BF16), _sds((s, 2 * kw), BF16), _sds((SUBLANES, LANES), F32)],
        scratch_shapes=[pltpu.VMEM((WINDOW, kw), F32), pltpu.VMEM((WINDOW, kw), F32)],
        compiler_params=_params("arbitrary"),
    )(sinks, qkv, qkv, qkv, qkv, qkv, tabs, tabs, o, do)


_INV_SQRT2 = 1.0 / math.sqrt(2.0)
_INV_SQRT2PI = 1.0 / math.sqrt(2.0 * math.pi)


def _gelu(x):
    return x * (lax.erf(x * _INV_SQRT2) + 1.0) * 0.5


def _gelu_grad(x):
    return 0.5 * (lax.erf(x * _INV_SQRT2) + 1.0) + x * jnp.exp(-0.5 * x * x) * _INV_SQRT2PI


def _sgu_norm(pv, lg, lb):
    zv = _gelu(pv)
    mu = jnp.mean(zv, axis=-1, keepdims=True)
    cen = zv - mu
    rs = lax.rsqrt(jnp.mean(cen * cen, axis=-1, keepdims=True) + EPS)
    xhat = cen * rs
    return xhat, rs, (xhat * lg + lb).astype(BF16)


def _causal(w, upper=False):
    t = lax.broadcasted_iota(jnp.int32, (CHUNK, CHUNK), 0)
    u = lax.broadcasted_iota(jnp.int32, (CHUNK, CHUNK), 1)
    return jnp.where((u >= t) if upper else (t >= u), w, 0.0).astype(BF16)


def _sgu_fwd(pz, lg, lb, w, bt, name):
    s, sw = pz.shape[0], pz.shape[1] // 2
    groups = sw // GROUP_DIM

    def body(i, ins, consts, outs, accs):
        lgv, lbv, w_ref, btv = consts[0][...], consts[1][...], consts[2], consts[3][...]
        zu = _gelu(ins[0][:, :sw].astype(F32))
        _, _, vn = _sgu_norm(ins[0][:, sw:].astype(F32), lgv, lbv)
        for g in range(groups):
            gs = slice(g * GROUP_DIM, (g + 1) * GROUP_DIM)
            sv = jnp.dot(_causal(w_ref[g]), vn[:, gs], preferred_element_type=F32) + btv[:, g:g + 1]
            outs[0][:, gs] = (zu[:, gs] * sv).astype(BF16)

    return _rowwise(body, name, s, CHUNK, [pz], [lg, lb, w, bt], [_sds((s, sw), BF16)])[0]


def _sgu_bwd(pz, dy, lg, lb, w, wt, bt, name, after=()):
    s, sw = pz.shape[0], pz.shape[1] // 2
    groups = sw // GROUP_DIM

    def body(i, ins, consts, outs, accs):
        lgv, lbv, w_ref, wt_ref, btv = consts[0][...], consts[1][...], consts[2], consts[3], consts[4][...]

        @pl.when(i == 0)
        def _():
            for a in accs:
                a[...] = jnp.zeros_like(a)

        pu, pv = ins[0][:, :sw].astype(F32), ins[0][:, sw:].astype(F32)
        dyv = ins[1][...].astype(F32)
        zu = _gelu(pu)
        xhat, rs, vn = _sgu_norm(pv, lgv, lbv)
        dvn_parts, db_parts = [], []
        lower = lax.broadcasted_iota(jnp.int32, (CHUNK, CHUNK), 0) >= lax.broadcasted_iota(jnp.int32, (CHUNK, CHUNK), 1)
        for g in range(groups):
            gs = slice(g * GROUP_DIM, (g + 1) * GROUP_DIM)
            sv = jnp.dot(_causal(w_ref[g]), vn[:, gs], preferred_element_type=F32) + btv[:, g:g + 1]
            dpu = dyv[:, gs] * sv * _gelu_grad(pu[:, gs])
            outs[0][:, gs] = dpu.astype(BF16)
            accs[4][:, gs] += jnp.sum(dpu, axis=0, keepdims=True)
            dsv = dyv[:, gs] * zu[:, gs]
            dsvb = dsv.astype(BF16)
            db_parts.append(jnp.sum(dsv, axis=1, keepdims=True))
            accs[2][g] += jnp.where(lower, lax.dot_general(dsvb, vn[:, gs], _DIMS["nt"], preferred_element_type=F32), 0.0)
            dvn_parts.append(jnp.dot(_causal(wt_ref[g], upper=True), dsvb, preferred_element_type=F32))
        dvn = jnp.concatenate(dvn_parts, axis=1)
        accs[3][...] += jnp.concatenate(db_parts, axis=1)
        accs[0][...] += jnp.sum(dvn * xhat, axis=0, keepdims=True)
        accs[1][...] += jnp.sum(dvn, axis=0, keepdims=True)
        dxh = dvn * lgv
        dz = rs * (dxh - jnp.mean(dxh, axis=-1, keepdims=True) - xhat * jnp.mean(dxh * xhat, axis=-1, keepdims=True))
        dpv = dz * _gelu_grad(pv)
        outs[0][:, sw:] = dpv.astype(BF16)
        accs[4][:, sw:] += jnp.sum(dpv, axis=0, keepdims=True)

    return _rowwise(body, name, s, CHUNK, [pz, dy], [lg, lb, w, wt, bt], [_sds((s, 2 * sw), BF16)],
                    [_sds((1, sw), F32), _sds((1, sw), F32), _sds((groups, CHUNK, CHUNK), F32), _sds((CHUNK, groups), F32),
                     _sds((1, 2 * sw), F32)],
                    after=after)


def _mesh_place():
    x, y, c = lax.axis_index("x"), lax.axis_index("y"), lax.axis_index("c")
    return x, y, c, 4 * x + 2 * y + c


def _peer(x, y, c, k):
    px, py, pc = x ^ ((k >> 2) & 1), y ^ ((k >> 1) & 1), c ^ (k & 1)
    return (px, py, pc), 4 * px + 2 * py + pc


BY_SLOTS, BY_COLS, BY_PAIRED_COLS = 0, 1, 2


def _col_block(ref, idx, width, cols):
    if cols == BY_PAIRED_COLS:
        idx = (idx % (N_DEV // 2)) * 2 + idx // (N_DEV // 2)
    return ref.at[:, pl.ds(pl.multiple_of(idx * width, LANES), width)]


def _exchange_copy(src_ref, land_ref, send_sems, recv_sems, k, place, scatter, arriving, cols):
    x, y, c, me = place
    peer, pidx = _peer(x, y, c, k)
    slot = pidx if arriving else me
    if scatter:
        src = _col_block(src_ref, pidx, land_ref.shape[-1], cols) if cols else src_ref.at[pidx]
        dst = land_ref.at[slot]
    else:
        src = src_ref
        dst = _col_block(land_ref, slot, src_ref.shape[-1], cols) if cols else land_ref.at[slot]
    return pltpu.make_async_remote_copy(
        src_ref=src, dst_ref=dst, send_sem=send_sems[k - 1], recv_sem=recv_sems[k - 1], device_id=peer,
        device_id_type=MESH_TYPE)


def _own_copy(src_ref, land_ref, sem, place, scatter, cols):
    me = place[3]
    if scatter:
        src = _col_block(src_ref, me, land_ref.shape[-1], cols) if cols else src_ref.at[me]
        dst = land_ref.at[me]
    else:
        src = src_ref
        dst = _col_block(land_ref, me, src_ref.shape[-1], cols) if cols else land_ref.at[me]
    return pltpu.make_async_copy(src, dst, sem)


N_PEERS = N_DEV - 1
N_EXCHANGE_SEMS = 2 * N_PEERS + 1


def _land_shape(a, scatter, cols):
    if scatter:
        return (N_DEV, a.shape[0], a.shape[1] // N_DEV) if cols else a.shape
    return (a.shape[0], N_DEV * a.shape[1]) if cols else (N_DEV,) + a.shape


def _exchange_start(srcs, scatter, cols, name, after):
    n = len(srcs)
    land_shapes = [_land_shape(a, scatter, cl) for a, cl in zip(srcs, cols)]

    def body(*refs):
        src, land = refs[:n], refs[n:2 * n]
        send_sems = refs[2 * n + 1:2 * n + 1 + N_PEERS]
        recv_sems = refs[2 * n + 1 + N_PEERS:2 * n + 1 + 2 * N_PEERS]
        own_sem = refs[2 * n + 1 + 2 * N_PEERS]
        token = refs[-1]
        place = _mesh_place()
        for t in range(n):
            for k in range(1, N_DEV):
                _exchange_copy(src[t], land[t], send_sems, recv_sems, k, place, scatter, False, cols[t]).start()
            _own_copy(src[t], land[t], own_sem, place, scatter, cols[t]).start()
        token[...] = jnp.zeros_like(token)

    return pl.pallas_call(
        body,
        name=name,
        out_shape=(*[pltpu.SemaphoreType.DMA(())] * N_EXCHANGE_SEMS, *[pltpu.HBM(a.shape, a.dtype) for a in srcs],
                   *[pltpu.HBM(shp, a.dtype) for shp, a in zip(land_shapes, srcs)], _sds((SUBLANES, LANES), F32)),
        in_specs=[HBM] * (2 * n) + [ANY],
        out_specs=(*[SEM] * N_EXCHANGE_SEMS, *[HBM] * (2 * n), pl.BlockSpec(memory_space=pltpu.VMEM)),
        input_output_aliases={i: N_EXCHANGE_SEMS + i for i in range(2 * n)},
        compiler_params=pltpu.CompilerParams(has_side_effects=DATAFLOW_EFFECT),
    )(*[pltpu.with_memory_space_constraint(a, pltpu.HBM) for a in srcs],
      *[pltpu.with_memory_space_constraint(lax.empty(shp, a.dtype), pltpu.HBM) for shp, a in zip(land_shapes, srcs)],
      after)


def _exchange_wait(started, after, scatter, cols, name):
    sems = started[:N_EXCHANGE_SEMS]
    thru = started[N_EXCHANGE_SEMS:-1]
    n = len(thru) // 2

    def body(*refs):
        src, land = refs[:n], refs[n:2 * n]
        send_sems = refs[2 * n:2 * n + N_PEERS]
        recv_sems = refs[2 * n + N_PEERS:2 * n + 2 * N_PEERS]
        own_sem = refs[2 * n + 2 * N_PEERS]
        place = _mesh_place()
        for t in range(n):
            for k in range(1, N_DEV):
                cp = _exchange_copy(src[t], land[t], send_sems, recv_sems, k, place, scatter, True, cols[t])
                cp.wait_send()
                cp.wait_recv()
            _own_copy(src[t], land[t], own_sem, place, scatter, cols[t]).wait()

    out = pl.pallas_call(
        body,
        name=name,
        out_shape=tuple(pltpu.HBM(a.shape, a.dtype) for a in thru),
        in_specs=[HBM] * (2 * n) + [SEM] * N_EXCHANGE_SEMS + [ANY],
        out_specs=tuple([HBM] * (2 * n)),
        input_output_aliases={i: i for i in range(2 * n)},
        compiler_params=pltpu.CompilerParams(has_side_effects=DATAFLOW_EFFECT),
    )(*thru, *sems, after)
    return out[:n], out[n:]


def _adamw(w, g, m, v):
    m = ADAM_B1 * m + (1.0 - ADAM_B1) * g
    v = ADAM_B2 * v + (1.0 - ADAM_B2) * (g * g)
    m_hat = m / (1.0 - ADAM_B1 ** ADAM_STEP)
    v_hat = v / (1.0 - ADAM_B2 ** ADAM_STEP)
    delta = -ADAM_LR * (m_hat / (jnp.sqrt(v_hat) + ADAM_EPS) + ADAM_WD * w)
    return delta, m, v


def _adam_rows(r, c):
    fits = [t for t in range(BF16_SUBLANES, r + 1, BF16_SUBLANES) if r % t == 0 and t * c <= ADAM_BLOCK_ELEMS]
    return max(fits) if fits else r


def _adam_body(p_ref, w_ref, m_ref, v_ref, g_out, d_out, m_out, v_out):
    g = p_ref[0].astype(F32)
    for d in range(1, N_DEV):
        g = g + p_ref[d].astype(F32)
    delta, mn, vn = _adamw(w_ref[...], g, m_ref[...], v_ref[...])
    g_out[...] = g
    d_out[...] = delta
    m_out[...] = mn
    v_out[...] = vn


def _reduce_adam_layer(parts, w, m, v, prev, layer, name):
    nl, r, c = w.shape
    tr = _adam_rows(r, c)
    if prev is None:
        prev = [lax.empty((nl, r, c), F32) for _ in range(4)]

    def body(p_ref, w_ref, m_ref, v_ref, *rest):
        _adam_body(p_ref, w_ref, m_ref, v_ref, *rest[4:])

    blk = pl.BlockSpec((None, tr, c), lambda i: (layer, i, 0))
    out = _sds((nl, r, c), F32)
    return pl.pallas_call(
        body,
        name=name,
        grid=(r // tr,),
        in_specs=[pl.BlockSpec((N_DEV, tr, c), lambda i: (0, i, 0)), blk, blk, blk, ANY, ANY, ANY, ANY],
        out_specs=[blk, blk, blk, blk],
        out_shape=[out, out, out, out],
        input_output_aliases={4: 0, 5: 1, 6: 2, 7: 3},
        compiler_params=_params("parallel"),
    )(parts, w, m, v, *prev)


def _reduce_adam(parts, w, m, v, name):
    nl, _, r, c = parts.shape
    tr = _adam_rows(r, c)

    def body(*refs):
        _adam_body(*refs)

    blk = pl.BlockSpec((None, tr, c), lambda l, i: (l, i, 0))
    out = _sds((nl, r, c), F32)
    return pl.pallas_call(
        body,
        name=name,
        grid=(nl, r // tr),
        in_specs=[pl.BlockSpec((None, N_DEV, tr, c), lambda l, i: (l, 0, i, 0)), blk, blk, blk],
        out_specs=[blk, blk, blk, blk],
        out_shape=[out, out, out, out],
        compiler_params=_params("parallel", "parallel"),
    )(parts, w, m, v)


def _pack(arrays):
    flat = []
    for a in arrays:
        a = a.reshape(-1).astype(F32)
        flat.append(jnp.pad(a, (0, (-a.shape[0]) % PACK_UNIT)))
    out = jnp.concatenate(flat)
    rows = out.shape[0] // LANES
    pad_rows = (-rows) % 512
    return jnp.pad(out, (0, pad_rows * LANES)).reshape(rows + pad_rows, LANES)


def _unpack(packed, shapes):
    flat = packed.reshape(-1)
    out, off = [], 0
    for shp in shapes:
        size = math.prod(shp)
        out.append(flat[off:off + size].reshape(shp))
        off += size + (-size) % PACK_UNIT
    return out


def _in_runs(d, qkv_w, sw, gb):
    g0 = qkv_w + 2 * sw
    runs = [(0, qkv_w, "qkv", 0), (qkv_w, 2 * sw, "z", 0)]
    for j in range(d // gb):
        runs.append((g0 + j * gb, gb, "g", 2 * j * gb))
        runs.append((g0 + d + j * gb, gb, "g", (2 * j + 1) * gb))
    return runs


def _pieces_from_global(take, runs):
    out = {}
    for piece in ("qkv", "z", "g"):
        own = sorted((r for r in runs if r[2] == piece), key=lambda r: r[3])
        out[piece] = jnp.concatenate([take(g, g + w) for g, w, _, _ in own], axis=-1)
    return out


def _global_from_pieces(pieces, runs, a, b):
    segs = []
    for g, w, piece, start in sorted(runs):
        lo, hi = max(a, g), min(b, g + w)
        if lo < hi:
            segs.append(pieces[piece][..., start + lo - g:start + hi - g])
    return jnp.concatenate(segs, axis=-1)


def _take_from_shards(land):
    c = land.shape[2]

    def take(a, b):
        parts = []
        while a < b:
            dev = a // c
            lo, hi = a - dev * c, min(b - dev * c, c)
            parts.append(land[dev][:, lo:hi])
            a = dev * c + hi
        return jnp.concatenate(parts, axis=-1)

    return take


def _to_full_cols(g):
    d, k, n = g.shape
    return jnp.transpose(g, (1, 0, 2)).reshape(k, d * n)


def _to_col_shards(a):
    k, n = a.shape
    return jnp.transpose(a.reshape(k, N_DEV, n // N_DEV), (1, 0, 2))


def kernel(x, positions, norm1_g, w_in, b_in, sinks, sgu_ln_g, sgu_ln_b, sgu_w, sgu_b, w_attn_branch, w_sgu_branch, w_out, norm2_g, w_gate_up, w_down, final_g, loss_target, m_norm1_g, m_w_in, m_b_in, m_sinks, m_sgu_ln_g, m_sgu_ln_b, m_sgu_w, m_sgu_b, m_w_attn_branch, m_w_sgu_branch, m_w_out, m_norm2_g, m_w_gate_up, m_w_down, m_final_g, v_norm1_g, v_w_in, v_b_in, v_sinks, v_sgu_ln_g, v_sgu_ln_b, v_sgu_w, v_sgu_b, v_w_attn_branch, v_w_sgu_branch, v_w_out, v_norm2_g, v_w_gate_up, v_w_down, v_final_g):
    nl = w_in.shape[0]
    s, d = x.shape[1], x.shape[2]
    aw = w_attn_branch.shape[1]
    sw = w_sgu_branch.shape[1]
    in_w = w_in.shape[2] * N_DEV
    kw = (in_w - aw - 2 * sw - 2 * d) // 2
    qkv_w = aw + 2 * kw
    groups = sw // GROUP_DIM
    ff = w_down.shape[1] * N_DEV

    h = x.reshape(s, d)
    target = loss_target.reshape(s, d)
    tabs = _rope_tables(positions.reshape(s, 1), "rope_tables")

    big = [w_in, w_attn_branch, w_sgu_branch, w_out, w_gate_up, w_down]
    big_m = [m_w_in, m_w_attn_branch, m_w_sgu_branch, m_w_out, m_w_gate_up, m_w_down]
    big_v = [v_w_in, v_w_attn_branch, v_w_sgu_branch, v_w_out, v_w_gate_up, v_w_down]
    big_names = ("w_in", "w_attn_branch", "w_sgu_branch", "w_out", "w_gate_up", "w_down")
    W_IN, W_AB, W_SB, W_OUT, W_GU, W_DOWN = range(6)
    weight_groups = ((W_IN,), (W_AB, W_SB, W_OUT), (W_GU, W_DOWN))
    grad_groups = ((W_DOWN, W_GU), (W_OUT, W_AB, W_SB), (W_IN,))

    col_sharded = (W_IN, W_AB, W_SB, W_GU)
    by_cols = [BY_COLS if t in col_sharded and big[t].shape[2] % LANES == 0 else BY_SLOTS for t in range(6)]
    assert by_cols[W_GU] == BY_COLS, "the fused swiglu kernels need gate/up column blocks of whole lane tiles"
    by_cols[W_GU] = BY_PAIRED_COLS

    def start_gather(l, group, after):
        return _exchange_start([big[t][l].astype(BF16) for t in group], False, tuple(by_cols[t] for t in group),
                               f"gather_start_l{l}_{big_names[group[0]]}", after)

    def full_weight(t, land):
        if by_cols[t]:
            return land
        if t == W_IN:
            return _pieces_from_global(_take_from_shards(land), in_runs)
        return _to_full_cols(land) if t in col_sharded else land.reshape(N_DEV * land.shape[1], land.shape[2])

    gate_block = _pick(d, GATE_BLOCK_PREFS)
    in_runs = _in_runs(d, qkv_w, sw, gate_block)

    saved = []
    started = {}
    token = h
    for l in range(nl):
        for ll in ((0, 1) if l == 0 else (l + 1,)):
            if ll < nl:
                for group in weight_groups:
                    started[(ll, group)] = start_gather(ll, group, token)
                    token = started[(ll, group)][-1]
        gathered = {}

        def weight(t, after, l=l, gathered=gathered):
            if t not in gathered:
                group = next(g for (ll, g) in started if ll == l and t in g)
                srcs, lands = _exchange_wait(started.pop((l, group)), after, False, tuple(by_cols[tt] for tt in group),
                                             f"gather_wait_l{l}_{big_names[group[0]]}")
                for tt, ld in zip(group, lands):
                    gathered[tt] = full_weight(tt, ld)
            return gathered[t]

        bias = b_in[l].reshape(1, in_w)
        g1, g2 = norm1_g[l].reshape(1, d), norm2_g[l].reshape(1, d)
        lg, lb = sgu_ln_g[l].reshape(1, sw), sgu_ln_b[l].reshape(1, sw)
        bt = sgu_b[l].T

        xn = _rms_fwd(h, g1, "rms1_fwd", after=(token,))
        wts = dict(weight(W_IN, xn))
        biases = _pieces_from_global(lambda a, b: bias[:, a:b], in_runs)
        qkv = _matmul(xn, wts["qkv"], "nn", BF16, "proj_qkv", bias=biases["qkv"])
        pz = _matmul(xn, wts["z"], "nn", BF16, "proj_z", bias=biases["z"])
        pg = _matmul(xn, wts["g"], "nn", BF16, "proj_g", bias=biases["g"])
        y_attn = _attn_fwd(qkv, tabs, sinks[l], aw, kw, "attn_fwd")
        y_sgu = _sgu_fwd(pz, lg, lb, sgu_w[l], bt, "sgu_fwd")
        wts.update(ab=weight(W_AB, y_sgu), sb=weight(W_SB, y_sgu), out=weight(W_OUT, y_sgu))
        a_br, s_br, merged = _branches_merge(y_attn, y_sgu, wts["ab"], wts["sb"], pg, gate_block, "branches_merge")
        h_mid = _matmul(merged, wts["out"], "nn", F32, "out_proj", res=h)
        hn = _rms_fwd(h_mid, g2, "rms2_fwd")
        wts.update(gu=weight(W_GU, hn), down=weight(W_DOWN, hn))
        gu, act = _gate_up_swiglu(hn, wts["gu"], "gate_up")
        h_out = _matmul(act, wts["down"], "nn", F32, "down_proj", res=h_mid)
        saved.append(dict(wts=wts, h=h, xn=xn, qkv=qkv, pz=pz, pg=pg, y_attn=y_attn, y_sgu=y_sgu, a_br=a_br,
                          s_br=s_br, merged=merged, h_mid=h_mid, hn=hn, gu=gu, act=act,
                          g1=g1, g2=g2, lg=lg, lb=lb, bt=bt))
        h = h_out

    dh, dhb, d_final_g, loss_blk = _loss_head(h, final_g.reshape(1, d), target, "loss_head")

    small = {n: [None] * nl for n in ("norm1_g", "b_in", "sinks", "sgu_ln_g", "sgu_ln_b", "sgu_w", "sgu_b", "norm2_g")}
    scattering = {}

    def start_scatter(l, group, grads, after):
        sends = []
        for t, dw in zip(group, grads):
            if by_cols[t]:
                sends.append(dw)
            elif t == W_IN:
                c = big[t].shape[2]
                sends.append(jnp.stack([_global_from_pieces(dw, in_runs, dev * c, (dev + 1) * c) for dev in range(N_DEV)]))
            elif t in col_sharded:
                sends.append(_to_col_shards(dw))
            else:
                sends.append(dw.reshape(N_DEV, dw.shape[0] // N_DEV, dw.shape[1]))
        scattering[(l, group)] = _exchange_start(sends, True, tuple(by_cols[t] for t in group),
                                                 f"scatter_start_l{l}_{big_names[group[0]]}", after)
        return scattering[(l, group)][-1]

    for l in reversed(range(nl)):
        sv = saved[l]
        wts = sv["wts"]
        d_gu = _d_act_swiglu(dhb, wts["down"], sv["gu"], "d_act")
        dw_down = _matmul(sv["act"], dhb, "tn", BF16, "dw_down")
        dw_gu = _matmul(sv["hn"], d_gu, "tn", BF16, "dw_gate_up")
        token = start_scatter(l, grad_groups[0], [dw_down, dw_gu], token)
        d_hn = _matmul(d_gu, wts["gu"], "nt", BF16, "d_hn", after=token)
        dh_mid, dmb, dg2 = _rms_bwd(sv["h_mid"], sv["g2"], d_hn, dh, "rms2_bwd")
        d_a, d_s, d_pg, db_g = _d_merged_merge_bwd(dmb, wts["out"], sv["pg"], sv["a_br"], sv["s_br"], gate_block, "d_merged")
        dw_out = _matmul(sv["merged"], dmb, "tn", BF16, "dw_out")
        d_y_attn = _matmul(d_a, wts["ab"], "nt", BF16, "d_y_attn")
        dw_ab = _matmul(sv["y_attn"], d_a, "tn", BF16, "dw_attn_branch")
        d_y_sgu = _matmul(d_s, wts["sb"], "nt", BF16, "d_y_sgu")
        dw_sb = _matmul(sv["y_sgu"], d_s, "tn", BF16, "dw_sgu_branch")
        token = start_scatter(l, grad_groups[1], [dw_out, dw_ab, dw_sb], token)
        d_pz, d_lg, d_lb, d_sw, d_sbt, db_z = _sgu_bwd(sv["pz"], d_y_sgu, sv["lg"], sv["lb"], sgu_w[l],
                                                 jnp.transpose(sgu_w[l], (0, 2, 1)), sv["bt"], "sgu_bwd", after=(token,))
        d_q, d_kv, d_sinks = _attn_bwd(sv["qkv"], tabs, sinks[l], sv["y_attn"], d_y_attn, aw, kw, "attn_bwd")
        d_qkv = jnp.concatenate([d_q, d_kv], axis=1)
        dw_qkv = _matmul(sv["xn"], d_qkv, "tn", BF16, "dw_qkv")
        dw_z = _matmul(sv["xn"], d_pz, "tn", BF16, "dw_z")
        dw_g = _matmul(sv["xn"], d_pg, "tn", BF16, "dw_g")
        token = start_scatter(l, grad_groups[2], [dict(qkv=dw_qkv, z=dw_z, g=dw_g)], token)
        d_xn = _matmul(d_qkv, wts["qkv"], "nt", F32, "d_xn_qkv", after=token)
        d_xn = _matmul(d_pz, wts["z"], "nt", F32, "d_xn_z", res=d_xn)
        d_xn = _matmul(d_pg, wts["g"], "nt", F32, "d_xn_g", res=d_xn)
        dh, dhb, dg1 = _rms_bwd(sv["h"], sv["g1"], d_xn, dh_mid, "rms1_bwd")

        small["norm1_g"][l], small["norm2_g"][l] = dg1, dg2
        small["b_in"][l] = _global_from_pieces(dict(qkv=_colsum(d_qkv, "db_qkv"), z=db_z, g=db_g), in_runs, 0, in_w)
        small["sinks"][l] = d_sinks[0, :aw // HEAD_DIM]
        small["sgu_ln_g"][l], small["sgu_ln_b"][l] = d_lg, d_lb
        small["sgu_w"][l] = d_sw
        small["sgu_b"][l] = d_sbt.T

    grad_x = dh.reshape(x.shape)

    names = ["norm1_g", "b_in", "sinks", "sgu_ln_g", "sgu_ln_b", "sgu_w", "sgu_b", "norm2_g"]
    small_w = [norm1_g, b_in, sinks, sgu_ln_g, sgu_ln_b, sgu_w, sgu_b, norm2_g, final_g]
    small_m = [m_norm1_g, m_b_in, m_sinks, m_sgu_ln_g, m_sgu_ln_b, m_sgu_w, m_sgu_b, m_norm2_g, m_final_g]
    small_v = [v_norm1_g, v_b_in, v_sinks, v_sgu_ln_g, v_sgu_ln_b, v_sgu_w, v_sgu_b, v_norm2_g, v_final_g]
    shapes = [w.shape for w in small_w] + [(1,)]
    partial = [jnp.stack([p.reshape(w.shape[1:]) for p in small[n]]) for n, w in zip(names, small_w)]
    partial += [d_final_g.reshape(final_g.shape), loss_blk[0, :1]]
    zero = jnp.zeros((1,), F32)
    small_started = _exchange_start([_pack(partial)], False, (False,), "gather_start_small_grads", dhb)

    big_out = [None] * len(big)
    after = small_started[-1]
    for l in reversed(range(nl)):
        for group in grad_groups:
            srcs, lands = _exchange_wait(scattering.pop((l, group)), after, True, tuple(by_cols[t] for t in group),
                                         f"scatter_wait_l{l}_{big_names[group[0]]}")
            for t, parts in zip(group, lands):
                big_out[t] = _reduce_adam_layer(parts, big[t], big_m[t], big_v[t], big_out[t], l, f"adam_{big_names[t]}")
                after = big_out[t][0]

    srcs, lands = _exchange_wait(small_started, after, False, (False,), "gather_wait_small_grads")
    sm = _reduce_adam(lands[0][None], _pack(small_w + [zero])[None], _pack(small_m + [zero])[None],
                      _pack(small_v + [zero])[None], "adam_small")
    sm_g, sm_d, sm_m, sm_v = [_unpack(a[0], shapes) for a in sm]
    loss = sm_g[-1].reshape(())

    def ordered(kind_small, kind_big):
        by_name = dict(zip(["norm1_g", "b_in", "sinks", "sgu_ln_g", "sgu_ln_b", "sgu_w", "sgu_b", "norm2_g", "final_g"], kind_small))
        by_name.update(zip(["w_in", "w_attn_branch", "w_sgu_branch", "w_out", "w_gate_up", "w_down"], kind_big))
        order = ["norm1_g", "w_in", "b_in", "sinks", "sgu_ln_g", "sgu_ln_b", "sgu_w", "sgu_b", "w_attn_branch",
                 "w_sgu_branch", "w_out", "norm2_g", "w_gate_up", "w_down", "final_g"]
        return [by_name[n] for n in order]

    outs = [loss, grad_x]
    for idx, sm_kind in enumerate((sm_g, sm_d, sm_m, sm_v)):
        outs += ordered(sm_kind[:-1], [o[idx] for o in big_out])
    return tuple(outs)
```

```python
import math

import jax
import jax.numpy as jnp
from jax import lax
from jax.experimental import pallas as pl
from jax.experimental.pallas import tpu as pltpu

F32 = jnp.float32
BF16 = jnp.bfloat16

N_DEV = 8
HEAD_DIM = 64
WINDOW = 128
CHUNK = 128
GROUP_DIM = 128
ROPE_DIM = HEAD_DIM // 4
ROPE_HALF = ROPE_DIM // 2
ROPE_THETA = 500000.0
EPS = 1e-5
NEG = -1e30
ATTN_SCALE = HEAD_DIM ** -0.5
ADAM_LR = 0.001
ADAM_B1 = 0.9
ADAM_B2 = 0.999
ADAM_EPS = 1e-08
ADAM_WD = 0.01
ADAM_STEP = 10
LANES = 128
SUBLANES = 8
BF16_SUBLANES = 16
PACK_UNIT = SUBLANES * LANES
ADAM_BLOCK_ELEMS = 256 * 1024
V7X_VMEM_LIMIT_BYTES = 56 * 1024 * 1024
MATMUL_TILE_PREFS = (1024, 1408, 768, 512, 384, 256, 128)
MATMUL_WHOLE_K = 2048
MATMUL_TN_K = 4096
MATMUL_VMEM_BUDGET_BYTES = 52 * 1024 * 1024
MATMUL_K_PREFS = (2816, 2048, 1536, 1408, 1024, 768, 512, 384, 256, 128)
ROW_TILE_PREFS = (512, 256, 128)
SWIGLU_ROW_PREFS = (512, 256, 128)
MERGE_ROW_PREFS = (512, 256, 128)
GATE_BLOCK_PREFS = (1024, 512, 256, 128)
FUSED_ROW_CHUNK = 256
MESH_TYPE = pl.DeviceIdType.MESH
ANY = pl.BlockSpec(memory_space=pl.ANY)
HBM = pl.BlockSpec(memory_space=pltpu.HBM)
SEM = pl.BlockSpec(memory_space=pltpu.SEMAPHORE)
DATAFLOW_EFFECT = pltpu.SideEffectType.DATAFLOW_SIDE_EFFECTING


def _pick(n, prefs):
    for p in prefs:
        if n % p == 0:
            return p
    return n


def _params(*sem):
    return pltpu.CompilerParams(dimension_semantics=sem, vmem_limit_bytes=V7X_VMEM_LIMIT_BYTES)


_DIMS = {"nn": (((1,), (0,)), ((), ())), "nt": (((1,), (1,)), ((), ())), "tn": (((0,), (0,)), ((), ()))}


def _matmul(a, b, mode, out_dtype, name, bias=None, res=None, after=None):
    if mode == "nn":
        (m, k), n = a.shape, b.shape[1]
    elif mode == "nt":
        (m, k), n = a.shape, b.shape[0]
    else:
        (k, m), n = a.shape, b.shape[1]
    tm, tn = _pick(m, MATMUL_TILE_PREFS), _pick(n, MATMUL_TILE_PREFS)
    if k <= MATMUL_WHOLE_K:
        tk = k
    else:
        fits = [t for t in ((MATMUL_TN_K,) if mode == "tn" else ()) + MATMUL_K_PREFS
                if k % t == 0 and 4 * t * (tm + tn) + 16 * tm * tn <= MATMUL_VMEM_BUDGET_BYTES]
        tk = fits[0]
    nk = k // tk
    dims = _DIMS[mode]
    a_spec = pl.BlockSpec((tk, tm), lambda i, j, kk: (kk, i)) if mode == "tn" else pl.BlockSpec((tm, tk), lambda i, j, kk: (i, kk))
    b_spec = pl.BlockSpec((tn, tk), lambda i, j, kk: (j, kk)) if mode == "nt" else pl.BlockSpec((tk, tn), lambda i, j, kk: (kk, j))
    in_specs, args = [a_spec, b_spec], [a, b]
    if bias is not None:
        in_specs.append(pl.BlockSpec((1, tn), lambda i, j, kk: (0, j)))
        args.append(bias)
    if res is not None:
        in_specs.append(pl.BlockSpec((tm, tn), lambda i, j, kk: (i, j)))
        args.append(res)
    if after is not None:
        in_specs.append(ANY)
        args.append(after)

    def body(*refs):
        a_ref, b_ref = refs[0], refs[1]
        pos = 2
        bias_ref = res_ref = None
        if bias is not None:
            bias_ref = refs[pos]
            pos += 1
        if res is not None:
            res_ref = refs[pos]
            pos += 1
        if after is not None:
            pos += 1
        o_ref = refs[pos]

        def finish(r):
            if bias_ref is not None:
                r = r + bias_ref[...]
            if res_ref is not None:
                r = r + res_ref[...]
            o_ref[...] = r.astype(out_dtype)

        part = lax.dot_general(a_ref[...], b_ref[...], dims, preferred_element_type=F32)
        if nk == 1:
            finish(part)
        else:
            acc_ref = refs[pos + 1]
            kk = pl.program_id(2)

            @pl.when(kk == 0)
            def _():
                acc_ref[...] = part

            @pl.when((kk > 0) & (kk < nk - 1))
            def _():
                acc_ref[...] += part

            @pl.when(kk == nk - 1)
            def _():
                finish(acc_ref[...] + part)

    return pl.pallas_call(
        body,
        name=name,
        grid=(m // tm, n // tn, nk),
        in_specs=in_specs,
        out_specs=pl.BlockSpec((tm, tn), lambda i, j, kk: (i, j)),
        out_shape=jax.ShapeDtypeStruct((m, n), out_dtype),
        scratch_shapes=[] if nk == 1 else [pltpu.VMEM((tm, tn), F32)],
        compiler_params=_params("parallel", "parallel", "arbitrary"),
    )(*args)


def _rowwise(body, name, rows, tr, ins, consts, outs, accs=(), after=()):
    n_in, n_c, n_o, n_a = len(ins), len(consts), len(outs), len(after)

    def wrapped(*refs):
        body(pl.program_id(0), refs[:n_in], refs[n_in:n_in + n_c], refs[n_in + n_c + n_a:n_in + n_c + n_a + n_o],
             refs[n_in + n_c + n_a + n_o:])

    def whole(shape):
        zeros = (0,) * len(shape)
        return pl.BlockSpec(tuple(shape), lambda i: zeros)

    in_specs = ([pl.BlockSpec((tr, a.shape[1]), lambda i: (i, 0)) for a in ins] + [whole(c.shape) for c in consts]
                + [ANY] * n_a)
    out_specs = [pl.BlockSpec((tr, o.shape[1]), lambda i: (i, 0)) for o in outs] + [whole(a.shape) for a in accs]
    return pl.pallas_call(
        wrapped,
        name=name,
        grid=(rows // tr,),
        in_specs=in_specs,
        out_specs=out_specs,
        out_shape=list(outs) + list(accs),
        compiler_params=_params("arbitrary" if accs else "parallel"),
    )(*ins, *consts, *after)


def _sds(shape, dtype):
    return jax.ShapeDtypeStruct(tuple(shape), dtype)


def _rms_fwd(h, g, name, after=()):
    s, d = h.shape
    tr = _pick(s, ROW_TILE_PREFS)

    def body(i, ins, consts, outs, accs):
        x = ins[0][...]
        r = lax.rsqrt(jnp.mean(x * x, axis=-1, keepdims=True) + EPS)
        outs[0][...] = (x * r * consts[0][...]).astype(BF16)

    return _rowwise(body, name, s, tr, [h], [g], [_sds((s, d), BF16)], after=after)[0]


def _rms_bwd(h, g, dy, dh_up, name):
    s, d = h.shape
    tr = _pick(s, ROW_TILE_PREFS)

    def body(i, ins, consts, outs, accs):
        x, dyv, up = ins[0][...], ins[1][...].astype(F32), ins[2][...]
        r = lax.rsqrt(jnp.mean(x * x, axis=-1, keepdims=True) + EPS)
        xr = x * r
        gy = dyv * consts[0][...]
        dx = r * (gy - xr * jnp.mean(gy * xr, axis=-1, keepdims=True))
        outs[0][...] = up + dx
        outs[1][...] = (up + dx).astype(BF16)

        @pl.when(i == 0)
        def _():
            accs[0][...] = jnp.zeros_like(accs[0])

        accs[0][...] += jnp.sum(dyv * xr, axis=0, keepdims=True)

    return _rowwise(body, name, s, tr, [h, dy, dh_up], [g], [_sds((s, d), F32), _sds((s, d), BF16)], [_sds((1, d), F32)])


def _loss_head(h, g, target, name):
    s, d = h.shape
    tr = _pick(s, ROW_TILE_PREFS)

    def body(i, ins, consts, outs, accs):
        x, t = ins[0][...], ins[1][...]
        gv = consts[0][...]
        r = lax.rsqrt(jnp.mean(x * x, axis=-1, keepdims=True) + EPS)
        xr = x * r
        diff = xr * gv - t
        dyv = diff * (1.0 / d)
        gy = dyv * gv
        dx = r * (gy - xr * jnp.mean(gy * xr, axis=-1, keepdims=True))
        outs[0][...] = dx
        outs[1][...] = dx.astype(BF16)

        @pl.when(i == 0)
        def _():
            accs[0][...] = jnp.zeros_like(accs[0])
            accs[1][...] = jnp.zeros_like(accs[1])

        accs[0][...] += jnp.sum(dyv * xr, axis=0, keepdims=True)
        part = 0.5 * jnp.sum(jnp.mean(diff * diff, axis=-1, keepdims=True), axis=0, keepdims=True)
        accs[1][...] += jnp.broadcast_to(part, accs[1].shape)

    return _rowwise(body, name, s, tr, [h, target], [g], [_sds((s, d), F32), _sds((s, d), BF16)],
                    [_sds((1, d), F32), _sds((SUBLANES, LANES), F32)])


def _colsum(a, name):
    s, w = a.shape
    tr = _pick(s, ROW_TILE_PREFS)

    def body(i, ins, consts, outs, accs):
        @pl.when(i == 0)
        def _():
            accs[0][...] = jnp.zeros_like(accs[0])

        accs[0][...] += jnp.sum(ins[0][...].astype(F32), axis=0, keepdims=True)

    return _rowwise(body, name, s, tr, [a], [], [], [_sds((1, w), F32)])[0]


def _sigmoid(x):
    return 1.0 / (1.0 + jnp.exp(-x))


def _branches_merge(y_attn, y_sgu, w_ab, w_sb, pg, gb, name):
    s, aw = y_attn.shape
    sw, d = w_sb.shape
    tm = _pick(s, MERGE_ROW_PREFS)

    def body(ya_ref, ys_ref, wa_ref, ws_ref, pg_ref, a_out, s_out, m_out):
        for rows in _row_chunks(tm):
            a = jnp.dot(ya_ref[rows, :], wa_ref[...], preferred_element_type=F32)
            b = jnp.dot(ys_ref[rows, :], ws_ref[...], preferred_element_type=F32)
            ga, gs = _sigmoid(pg_ref[rows, :gb].astype(F32)), _sigmoid(pg_ref[rows, gb:].astype(F32))
            a_out[rows, :] = a.astype(BF16)
            s_out[rows, :] = b.astype(BF16)
            m_out[rows, :] = (ga * a + gs * b).astype(BF16)

    blk = pl.BlockSpec((tm, gb), lambda i, j: (i, j))
    out = _sds((s, d), BF16)
    return pl.pallas_call(
        body,
        name=name,
        grid=(s // tm, d // gb),
        in_specs=[pl.BlockSpec((tm, aw), lambda i, j: (i, 0)), pl.BlockSpec((tm, sw), lambda i, j: (i, 0)),
                  pl.BlockSpec((aw, gb), lambda i, j: (0, j)), pl.BlockSpec((sw, gb), lambda i, j: (0, j)),
                  pl.BlockSpec((tm, 2 * gb), lambda i, j: (i, j))],
        out_specs=[blk, blk, blk],
        out_shape=[out, out, out],
        compiler_params=_params("parallel", "parallel"),
    )(y_attn, y_sgu, w_ab, w_sb, pg)


def _d_merged_merge_bwd(dmb, w_out, pg, a_br, s_br, gb, name):
    s, d = dmb.shape
    tm = _pick(s, MERGE_ROW_PREFS)

    def body(a_ref, b_ref, pg_ref, ab_ref, sb_ref, da_out, ds_out, dpg_out, db_out):
        @pl.when(pl.program_id(1) == 0)
        def _():
            db_out[...] = jnp.zeros_like(db_out)

        for rows in _row_chunks(tm):
            dm = lax.dot_general(a_ref[rows, :], b_ref[...], _DIMS["nt"], preferred_element_type=F32)
            ga, gs = _sigmoid(pg_ref[rows, :gb].astype(F32)), _sigmoid(pg_ref[rows, gb:].astype(F32))
            da_out[rows, :] = (dm * ga).astype(BF16)
            ds_out[rows, :] = (dm * gs).astype(BF16)
            dpa = dm * ab_ref[rows, :].astype(F32) * ga * (1.0 - ga)
            dps = dm * sb_ref[rows, :].astype(F32) * gs * (1.0 - gs)
            dpg_out[rows, :gb] = dpa.astype(BF16)
            dpg_out[rows, gb:] = dps.astype(BF16)
            db_out[:, :gb] += jnp.sum(dpa, axis=0, keepdims=True)
            db_out[:, gb:] += jnp.sum(dps, axis=0, keepdims=True)

    blk = pl.BlockSpec((tm, gb), lambda j, i: (i, j))
    pair = pl.BlockSpec((tm, 2 * gb), lambda j, i: (i, j))
    return pl.pallas_call(
        body,
        name=name,
        grid=(d // gb, s // tm),
        in_specs=[pl.BlockSpec((tm, d), lambda j, i: (i, 0)), pl.BlockSpec((gb, d), lambda j, i: (j, 0)), pair, blk, blk],
        out_specs=[blk, blk, pair, pl.BlockSpec((1, 2 * gb), lambda j, i: (0, j))],
        out_shape=[_sds((s, d), BF16), _sds((s, d), BF16), _sds((s, 2 * d), BF16), _sds((1, 2 * d), F32)],
        compiler_params=_params("parallel", "arbitrary"),
    )(dmb, w_out, pg, a_br, s_br)


def _row_chunks(tm):
    rc = _pick(tm, (FUSED_ROW_CHUNK,))
    return [slice(r, r + rc) for r in range(0, tm, rc)]


def _gate_up_swiglu(hn, w_gu, name):
    s, d = hn.shape
    n2 = w_gu.shape[1]
    fb = n2 // N_DEV
    tm = _pick(s, SWIGLU_ROW_PREFS)

    def body(a_ref, b_ref, gu_ref, act_ref):
        r = jnp.dot(a_ref[...], b_ref[...], preferred_element_type=F32)
        gu_ref[...] = r.astype(BF16)
        gate, up = r[:, :fb], r[:, fb:]
        act_ref[...] = (gate * _sigmoid(gate) * up).astype(BF16)

    return pl.pallas_call(
        body,
        name=name,
        grid=(s // tm, N_DEV // 2),
        in_specs=[pl.BlockSpec((tm, d), lambda i, j: (i, 0)), pl.BlockSpec((d, 2 * fb), lambda i, j: (0, j))],
        out_specs=[pl.BlockSpec((tm, 2 * fb), lambda i, j: (i, j)), pl.BlockSpec((tm, fb), lambda i, j: (i, j))],
        out_shape=[_sds((s, n2), BF16), _sds((s, n2 // 2), BF16)],
        compiler_params=_params("parallel", "parallel"),
    )(hn, w_gu)


def _d_act_swiglu(dhb, w_down, gu, name):
    s, d = dhb.shape
    n2 = gu.shape[1]
    fb = n2 // N_DEV
    tm = _pick(s, SWIGLU_ROW_PREFS)

    def body(a_ref, b_ref, gu_ref, o_ref):
        for rows in _row_chunks(tm):
            da = lax.dot_general(a_ref[rows, :], b_ref[...], _DIMS["nt"], preferred_element_type=F32)
            gate, up = gu_ref[rows, :fb].astype(F32), gu_ref[rows, fb:].astype(F32)
            sg = _sigmoid(gate)
            o_ref[rows, :fb] = (da * up * sg * (1.0 + gate * (1.0 - sg))).astype(BF16)
            o_ref[rows, fb:] = (da * gate * sg).astype(BF16)

    pair = pl.BlockSpec((tm, 2 * fb), lambda i, j: (i, j))
    return pl.pallas_call(
        body,
        name=name,
        grid=(s // tm, N_DEV // 2),
        in_specs=[pl.BlockSpec((tm, d), lambda i, j: (i, 0)), pl.BlockSpec((fb, d), lambda i, j: (j, 0)), pair],
        out_specs=pair,
        out_shape=_sds((s, n2), BF16),
        compiler_params=_params("parallel", "parallel"),
    )(dhb, w_down, gu)


def _rope_tables(pos_col, name):
    s = pos_col.shape[0]
    tr = _pick(s, (1024, 512, 256, 128))
    inv = ROPE_THETA ** (-jnp.arange(0, ROPE_DIM, 2, dtype=F32) / ROPE_DIM)
    lane = jnp.arange(LANES)
    inv_lanes = inv[lane % ROPE_HALF].reshape(1, LANES)

    def body(i, ins, consts, outs, accs):
        ang = ins[0][...].astype(F32) * consts[0][...]
        c, sn = jnp.cos(ang), jnp.sin(ang)
        in_head = lax.broadcasted_iota(jnp.int32, ang.shape, 1) % HEAD_DIM
        outs[0][:, 0:LANES] = jnp.where(in_head < ROPE_DIM, c, 1.0)
        outs[0][:, LANES:2 * LANES] = jnp.where(in_head < ROPE_HALF, -sn, 0.0)
        outs[0][:, 2 * LANES:] = jnp.where((in_head >= ROPE_HALF) & (in_head < ROPE_DIM), sn, 0.0)

    return _rowwise(body, name, s, tr, [pos_col], [inv_lanes], [_sds((s, 3 * LANES), F32)])[0]


def _rope(x, tab, inverse=False):
    width = x.shape[1]
    reps = width // LANES
    c = jnp.tile(tab[:, 0:LANES], (1, reps))
    lo = jnp.tile(tab[:, LANES:2 * LANES], (1, reps))
    hi = jnp.tile(tab[:, 2 * LANES:], (1, reps))
    if inverse:
        lo, hi = -lo, -hi
    return x * c + pltpu.roll(x, width - ROPE_HALF, 1) * lo + pltpu.roll(x, ROPE_HALF, 1) * hi


def _attn_specs(aw, kw):
    kb = aw // kw
    prev = lambda i: jnp.maximum(i - 1, 0)
    return [
        pl.BlockSpec(memory_space=pltpu.SMEM),
        pl.BlockSpec((WINDOW, aw), lambda i: (i, 0)),
        pl.BlockSpec((WINDOW, kw), lambda i: (i, kb)),
        pl.BlockSpec((WINDOW, kw), lambda i: (prev(i), kb)),
        pl.BlockSpec((WINDOW, kw), lambda i: (i, kb + 1)),
        pl.BlockSpec((WINDOW, kw), lambda i: (prev(i), kb + 1)),
        pl.BlockSpec((WINDOW, 3 * LANES), lambda i: (i, 0)),
        pl.BlockSpec((WINDOW, 3 * LANES), lambda i: (prev(i), 0)),
    ]


def _attn_common(i, q_ref, kc_ref, kp_ref, vc_ref, vp_ref, tq_ref, tp_ref):
    tq, tp = tq_ref[...], tp_ref[...]
    qt = (_rope(q_ref[...].astype(F32), tq) * ATTN_SCALE).astype(BF16).T
    kc = _rope(kc_ref[...].astype(F32), tq)
    kp = _rope(kp_ref[...].astype(F32), tp)
    k2 = jnp.concatenate([kp, kc], axis=0).astype(BF16)
    v2 = jnp.concatenate([vp_ref[...], vc_ref[...]], axis=0)
    kj = lax.broadcasted_iota(jnp.int32, (2 * WINDOW, WINDOW), 0)
    qi = lax.broadcasted_iota(jnp.int32, (2 * WINDOW, WINDOW), 1)
    rel = qi + WINDOW - kj
    ok = (rel >= 0) & (rel < WINDOW) & ((kj >= WINDOW) | (i > 0))
    return qt, k2, v2, ok, tq, tp


def _head_probs(qt_h, kg, ok, sink):
    s = jnp.dot(kg, qt_h, preferred_element_type=F32)
    s = jnp.where(ok, s, NEG)
    m = jnp.maximum(jnp.max(s, axis=0, keepdims=True), sink)
    p = jnp.exp(s - m)
    es = jnp.exp(sink - m)
    inv = 1.0 / (jnp.sum(p, axis=0, keepdims=True) + es)
    return p * inv, es * inv


def _attn_fwd(qkv, tabs, sinks, aw, kw, name):
    s = qkv.shape[0]
    nq, nkv = aw // HEAD_DIM, kw // HEAD_DIM
    qpk = nq // nkv

    def body(s_ref, q_ref, kc_ref, kp_ref, vc_ref, vp_ref, tq_ref, tp_ref, o_ref):
        i = pl.program_id(0)
        qt, k2, v2, ok, _, _ = _attn_common(i, q_ref, kc_ref, kp_ref, vc_ref, vp_ref, tq_ref, tp_ref)
        v2t = v2.T
        ot_parts = []
        for h in range(nq):
            g = h // qpk
            hs, gs = slice(h * HEAD_DIM, (h + 1) * HEAD_DIM), slice(g * HEAD_DIM, (g + 1) * HEAD_DIM)
            pn, _ = _head_probs(qt[hs], k2[:, gs], ok, s_ref[h])
            ot_parts.append(jnp.dot(v2t[gs], pn.astype(BF16), preferred_element_type=F32))
        o_ref[...] = jnp.concatenate(ot_parts, axis=0).astype(BF16).T

    return pl.pallas_call(
        body,
        name=name,
        grid=(s // WINDOW,),
        in_specs=_attn_specs(aw, kw),
        out_specs=pl.BlockSpec((WINDOW, aw), lambda i: (i, 0)),
        out_shape=_sds((s, aw), BF16),
        compiler_params=_params("parallel"),
    )(sinks, qkv, qkv, qkv, qkv, qkv, tabs, tabs)


def _attn_bwd(qkv, tabs, sinks, o, do, aw, kw, name):
    s = qkv.shape[0]
    nb = s // WINDOW
    nq, nkv = aw // HEAD_DIM, kw // HEAD_DIM
    qpk = nq // nkv

    def body(s_ref, q_ref, kc_ref, kp_ref, vc_ref, vp_ref, tq_ref, tp_ref, o_ref, do_ref,
             dq_ref, dkv_ref, ds_ref, ck_ref, cv_ref):
        i = pl.program_id(0)

        @pl.when(i == 0)
        def _():
            ck_ref[...] = jnp.zeros_like(ck_ref)
            cv_ref[...] = jnp.zeros_like(cv_ref)
            ds_ref[...] = jnp.zeros_like(ds_ref)

        qt, k2, v2, ok, tq, tp = _attn_common(i, q_ref, kc_ref, kp_ref, vc_ref, vp_ref, tq_ref, tp_ref)
        dot_t, ot = do_ref[...].T, o_ref[...].T
        k2t = k2.T
        row0 = lax.broadcasted_iota(jnp.int32, (SUBLANES, LANES), 0) == 0
        lane = lax.broadcasted_iota(jnp.int32, (SUBLANES, LANES), 1)
        dsink = jnp.zeros((SUBLANES, LANES), F32)
        dqt_parts, dk_parts, dv_parts = [], [], []
        for g in range(nkv):
            gs = slice(g * HEAD_DIM, (g + 1) * HEAD_DIM)
            kg, vg = k2[:, gs], v2[:, gs]
            dk_g = jnp.zeros((2 * WINDOW, HEAD_DIM), F32)
            dv_g = jnp.zeros((2 * WINDOW, HEAD_DIM), F32)
            for j in range(qpk):
                h = g * qpk + j
                hs = slice(h * HEAD_DIM, (h + 1) * HEAD_DIM)
                pn, psink = _head_probs(qt[hs], kg, ok, s_ref[h])
                delta = jnp.sum(dot_t[hs].astype(F32) * ot[hs].astype(F32), axis=0, keepdims=True)
                dp = jnp.dot(vg, dot_t[hs], preferred_element_type=F32)
                dsb = (pn * (dp - delta)).astype(BF16)
                dsink = dsink + jnp.where(row0 & (lane == h), -jnp.sum(psink * delta, axis=1, keepdims=True), 0.0)
                dqt_parts.append(jnp.dot(k2t[gs], dsb, preferred_element_type=F32))
                dk_g = dk_g + lax.dot_general(dsb, qt[hs], _DIMS["nt"], preferred_element_type=F32)
                dv_g = dv_g + lax.dot_general(pn.astype(BF16), dot_t[hs], _DIMS["nt"], preferred_element_type=F32)
            dk_parts.append(dk_g)
            dv_parts.append(dv_g)
        ds_ref[...] += dsink
        dq_ref[...] = _rope(jnp.concatenate(dqt_parts, axis=0).T * ATTN_SCALE, tq, inverse=True).astype(BF16)
        dk2 = jnp.concatenate(dk_parts, axis=1)
        dv2 = jnp.concatenate(dv_parts, axis=1)
        dk_prev = _rope(ck_ref[...] + dk2[:WINDOW], tp, inverse=True)
        dv_prev = cv_ref[...] + dv2[:WINDOW]

        @pl.when(i > 0)
        def _():
            dkv_ref[pl.ds(pl.multiple_of((i - 1) * WINDOW, WINDOW), WINDOW), :] = jnp.concatenate(
                [dk_prev, dv_prev], axis=1).astype(BF16)

        ck_ref[...] = dk2[WINDOW:]
        cv_ref[...] = dv2[WINDOW:]

        @pl.when(i == nb - 1)
        def _():
            dkv_ref[pl.ds(pl.multiple_of(i * WINDOW, WINDOW), WINDOW), :] = jnp.concatenate(
                [_rope(dk2[WINDOW:], tq, inverse=True), dv2[WINDOW:]], axis=1).astype(BF16)

    blk = pl.BlockSpec((WINDOW, aw), lambda i: (i, 0))
    return pl.pallas_call(
        body,
        name=name,
        grid=(nb,),
        in_specs=_attn_specs(aw, kw) + [blk, blk],
        out_specs=[blk, pl.BlockSpec((s, 2 * kw), lambda i: (0, 0)), pl.BlockSpec((SUBLANES, LANES), lambda i: (0, 0))],
        out_shape=[_sds((s, aw), BF16), _sds((s, 2 * kw), BF16), _sds((SUBLANES, LANES), F32)],
        scratch_shapes=[pltpu.VMEM((WINDOW, kw), F32), pltpu.VMEM((WINDOW, kw), F32)],
        compiler_params=_params("arbitrary"),
    )(sinks, qkv, qkv, qkv, qkv, qkv, tabs, tabs, o, do)


_INV_SQRT2 = 1.0 / math.sqrt(2.0)
_INV_SQRT2PI = 1.0 / math.sqrt(2.0 * math.pi)


def _gelu(x):
    return x * (lax.erf(x * _INV_SQRT2) + 1.0) * 0.5


def _gelu_grad(x):
    return 0.5 * (lax.erf(x * _INV_SQRT2) + 1.0) + x * jnp.exp(-0.5 * x * x) * _INV_SQRT2PI


def _sgu_norm(pv, lg, lb):
    zv = _gelu(pv)
    mu = jnp.mean(zv, axis=-1, keepdims=True)
    cen = zv - mu
    rs = lax.rsqrt(jnp.mean(cen * cen, axis=-1, keepdims=True) + EPS)
    xhat = cen * rs
    return xhat, rs, (xhat * lg + lb).astype(BF16)


def _causal(w, upper=False):
    t = lax.broadcasted_iota(jnp.int32, (CHUNK, CHUNK), 0)
    u = lax.broadcasted_iota(jnp.int32, (CHUNK, CHUNK), 1)
    return jnp.where((u >= t) if upper else (t >= u), w, 0.0).astype(BF16)


def _sgu_fwd(pz, lg, lb, w, bt, name):
    s, sw = pz.shape[0], pz.shape[1] // 2
    groups = sw // GROUP_DIM

    def body(i, ins, consts, outs, accs):
        lgv, lbv, w_ref, btv = consts[0][...], consts[1][...], consts[2], consts[3][...]
        zu = _gelu(ins[0][:, :sw].astype(F32))
        _, _, vn = _sgu_norm(ins[0][:, sw:].astype(F32), lgv, lbv)
        for g in range(groups):
            gs = slice(g * GROUP_DIM, (g + 1) * GROUP_DIM)
            sv = jnp.dot(_causal(w_ref[g]), vn[:, gs], preferred_element_type=F32) + btv[:, g:g + 1]
            outs[0][:, gs] = (zu[:, gs] * sv).astype(BF16)

    return _rowwise(body, name, s, CHUNK, [pz], [lg, lb, w, bt], [_sds((s, sw), BF16)])[0]


def _sgu_bwd(pz, dy, lg, lb, w, wt, bt, name, after=()):
    s, sw = pz.shape[0], pz.shape[1] // 2
    groups = sw // GROUP_DIM

    def body(i, ins, consts, outs, accs):
        lgv, lbv, w_ref, wt_ref, btv = consts[0][...], consts[1][...], consts[2], consts[3], consts[4][...]

        @pl.when(i == 0)
        def _():
            for a in accs:
                a[...] = jnp.zeros_like(a)

        pu, pv = ins[0][:, :sw].astype(F32), ins[0][:, sw:].astype(F32)
        dyv = ins[1][...].astype(F32)
        zu = _gelu(pu)
        xhat, rs, vn = _sgu_norm(pv, lgv, lbv)
        dvn_parts, db_parts = [], []
        lower = lax.broadcasted_iota(jnp.int32, (CHUNK, CHUNK), 0) >= lax.broadcasted_iota(jnp.int32, (CHUNK, CHUNK), 1)
        for g in range(groups):
            gs = slice(g * GROUP_DIM, (g + 1) * GROUP_DIM)
            sv = jnp.dot(_causal(w_ref[g]), vn[:, gs], preferred_element_type=F32) + btv[:, g:g + 1]
            dpu = dyv[:, gs] * sv * _gelu_grad(pu[:, gs])
            outs[0][:, gs] = dpu.astype(BF16)
            accs[4][:, gs] += jnp.sum(dpu, axis=0, keepdims=True)
            dsv = dyv[:, gs] * zu[:, gs]
            dsvb = dsv.astype(BF16)
            db_parts.append(jnp.sum(dsv, axis=1, keepdims=True))
            accs[2][g] += jnp.where(lower, lax.dot_general(dsvb, vn[:, gs], _DIMS["nt"], preferred_element_type=F32), 0.0)
            dvn_parts.append(jnp.dot(_causal(wt_ref[g], upper=True), dsvb, preferred_element_type=F32))
        dvn = jnp.concatenate(dvn_parts, axis=1)
        accs[3][...] += jnp.concatenate(db_parts, axis=1)
        accs[0][...] += jnp.sum(dvn * xhat, axis=0, keepdims=True)
        accs[1][...] += jnp.sum(dvn, axis=0, keepdims=True)
        dxh = dvn * lgv
        dz = rs * (dxh - jnp.mean(dxh, axis=-1, keepdims=True) - xhat * jnp.mean(dxh * xhat, axis=-1, keepdims=True))
        dpv = dz * _gelu_grad(pv)
        outs[0][:, sw:] = dpv.astype(BF16)
        accs[4][:, sw:] += jnp.sum(dpv, axis=0, keepdims=True)

    return _rowwise(body, name, s, CHUNK, [pz, dy], [lg, lb, w, wt, bt], [_sds((s, 2 * sw), BF16)],
                    [_sds((1, sw), F32), _sds((1, sw), F32), _sds((groups, CHUNK, CHUNK), F32), _sds((CHUNK, groups), F32),
                     _sds((1, 2 * sw), F32)],
                    after=after)


def _mesh_place():
    x, y, c = lax.axis_index("x"), lax.axis_index("y"), lax.axis_index("c")
    return x, y, c, 4 * x + 2 * y + c


def _peer(x, y, c, k):
    px, py, pc = x ^ ((k >> 2) & 1), y ^ ((k >> 1) & 1), c ^ (k & 1)
    return (px, py, pc), 4 * px + 2 * py + pc


BY_SLOTS, BY_COLS, BY_PAIRED_COLS = 0, 1, 2


def _col_block(ref, idx, width, cols):
    if cols == BY_PAIRED_COLS:
        idx = (idx % (N_DEV // 2)) * 2 + idx // (N_DEV // 2)
    return ref.at[:, pl.ds(pl.multiple_of(idx * width, LANES), width)]


def _exchange_copy(src_ref, land_ref, send_sems, recv_sems, k, place, scatter, arriving, cols):
    x, y, c, me = place
    peer, pidx = _peer(x, y, c, k)
    slot = pidx if arriving else me
    if scatter:
        src = _col_block(src_ref, pidx, land_ref.shape[-1], cols) if cols else src_ref.at[pidx]
        dst = land_ref.at[slot]
    else:
        src = src_ref
        dst = _col_block(land_ref, slot, src_ref.shape[-1], cols) if cols else land_ref.at[slot]
    return pltpu.make_async_remote_copy(
        src_ref=src, dst_ref=dst, send_sem=send_sems[k - 1], recv_sem=recv_sems[k - 1], device_id=peer,
        device_id_type=MESH_TYPE)


def _own_copy(src_ref, land_ref, sem, place, scatter, cols):
    me = place[3]
    if scatter:
        src = _col_block(src_ref, me, land_ref.shape[-1], cols) if cols else src_ref.at[me]
        dst = land_ref.at[me]
    else:
        src = src_ref
        dst = _col_block(land_ref, me, src_ref.shape[-1], cols) if cols else land_ref.at[me]
    return pltpu.make_async_copy(src, dst, sem)


N_PEERS = N_DEV - 1
N_EXCHANGE_SEMS = 2 * N_PEERS + 1


def _land_shape(a, scatter, cols):
    if scatter:
        return (N_DEV, a.shape[0], a.shape[1] // N_DEV) if cols else a.shape
    return (a.shape[0], N_DEV * a.shape[1]) if cols else (N_DEV,) + a.shape


def _exchange_start(srcs, scatter, cols, name, after):
    n = len(srcs)
    land_shapes = [_land_shape(a, scatter, cl) for a, cl in zip(srcs, cols)]

    def body(*refs):
        src, land = refs[:n], refs[n:2 * n]
        send_sems = refs[2 * n + 1:2 * n + 1 + N_PEERS]
        recv_sems = refs[2 * n + 1 + N_PEERS:2 * n + 1 + 2 * N_PEERS]
        own_sem = refs[2 * n + 1 + 2 * N_PEERS]
        token = refs[-1]
        place = _mesh_place()
        for t in range(n):
            for k in range(1, N_DEV):
                _exchange_copy(src[t], land[t], send_sems, recv_sems, k, place, scatter, False, cols[t]).start()
            _own_copy(src[t], land[t], own_sem, place, scatter, cols[t]).start()
        token[...] = jnp.zeros_like(token)

    return pl.pallas_call(
        body,
        name=name,
        out_shape=(*[pltpu.SemaphoreType.DMA(())] * N_EXCHANGE_SEMS, *[pltpu.HBM(a.shape, a.dtype) for a in srcs],
                   *[pltpu.HBM(shp, a.dtype) for shp, a in zip(land_shapes, srcs)], _sds((SUBLANES, LANES), F32)),
        in_specs=[HBM] * (2 * n) + [ANY],
        out_specs=(*[SEM] * N_EXCHANGE_SEMS, *[HBM] * (2 * n), pl.BlockSpec(memory_space=pltpu.VMEM)),
        input_output_aliases={i: N_EXCHANGE_SEMS + i for i in range(2 * n)},
        compiler_params=pltpu.CompilerParams(has_side_effects=DATAFLOW_EFFECT),
    )(*[pltpu.with_memory_space_constraint(a, pltpu.HBM) for a in srcs],
      *[pltpu.with_memory_space_constraint(lax.empty(shp, a.dtype), pltpu.HBM) for shp, a in zip(land_shapes, srcs)],
      after)


def _exchange_wait(started, after, scatter, cols, name):
    sems = started[:N_EXCHANGE_SEMS]
    thru = started[N_EXCHANGE_SEMS:-1]
    n = len(thru) // 2

    def body(*refs):
        src, land = refs[:n], refs[n:2 * n]
        send_sems = refs[2 * n:2 * n + N_PEERS]
        recv_sems = refs[2 * n + N_PEERS:2 * n + 2 * N_PEERS]
        own_sem = refs[2 * n + 2 * N_PEERS]
        place = _mesh_place()
        for t in range(n):
            for k in range(1, N_DEV):
                cp = _exchange_copy(src[t], land[t], send_sems, recv_sems, k, place, scatter, True, cols[t])
                cp.wait_send()
                cp.wait_recv()
            _own_copy(src[t], land[t], own_sem, place, scatter, cols[t]).wait()

    out = pl.pallas_call(
        body,
        name=name,
        out_shape=tuple(pltpu.HBM(a.shape, a.dtype) for a in thru),
        in_specs=[HBM] * (2 * n) + [SEM] * N_EXCHANGE_SEMS + [ANY],
        out_specs=tuple([HBM] * (2 * n)),
        input_output_aliases={i: i for i in range(2 * n)},
        compiler_params=pltpu.CompilerParams(has_side_effects=DATAFLOW_EFFECT),
    )(*thru, *sems, after)
    return out[:n], out[n:]


def _adamw(w, g, m, v):
    m = ADAM_B1 * m + (1.0 - ADAM_B1) * g
    v = ADAM_B2 * v + (1.0 - ADAM_B2) * (g * g)
    m_hat = m / (1.0 - ADAM_B1 ** ADAM_STEP)
    v_hat = v / (1.0 - ADAM_B2 ** ADAM_STEP)
    delta = -ADAM_LR * (m_hat / (jnp.sqrt(v_hat) + ADAM_EPS) + ADAM_WD * w)
    return delta, m, v


def _adam_rows(r, c):
    fits = [t for t in range(BF16_SUBLANES, r + 1, BF16_SUBLANES) if r % t == 0 and t * c <= ADAM_BLOCK_ELEMS]
    return max(fits) if fits else r


def _adam_body(p_ref, w_ref, m_ref, v_ref, g_out, d_out, m_out, v_out):
    g = p_ref[0].astype(F32)
    for d in range(1, N_DEV):
        g = g + p_ref[d].astype(F32)
    delta, mn, vn = _adamw(w_ref[...], g, m_ref[...], v_ref[...])
    g_out[...] = g
    d_out[...] = delta
    m_out[...] = mn
    v_out[...] = vn


def _reduce_adam_layer(parts, w, m, v, prev, layer, name):
    nl, r, c = w.shape
    tr = _adam_rows(r, c)
    if prev is None:
        prev = [lax.empty((nl, r, c), F32) for _ in range(4)]

    def body(p_ref, w_ref, m_ref, v_ref, *rest):
        _adam_body(p_ref, w_ref, m_ref, v_ref, *rest[4:])

    blk = pl.BlockSpec((None, tr, c), lambda i: (layer, i, 0))
    out = _sds((nl, r, c), F32)
    return pl.pallas_call(
        body,
        name=name,
        grid=(r // tr,),
        in_specs=[pl.BlockSpec((N_DEV, tr, c), lambda i: (0, i, 0)), blk, blk, blk, ANY, ANY, ANY, ANY],
        out_specs=[blk, blk, blk, blk],
        out_shape=[out, out, out, out],
        input_output_aliases={4: 0, 5: 1, 6: 2, 7: 3},
        compiler_params=_params("parallel"),
    )(parts, w, m, v, *prev)


def _reduce_adam(parts, w, m, v, name):
    nl, _, r, c = parts.shape
    tr = _adam_rows(r, c)

    def body(*refs):
        _adam_body(*refs)

    blk = pl.BlockSpec((None, tr, c), lambda l, i: (l, i, 0))
    out = _sds((nl, r, c), F32)
    return pl.pallas_call(
        body,
        name=name,
        grid=(nl, r // tr),
        in_specs=[pl.BlockSpec((None, N_DEV, tr, c), lambda l, i: (l, 0, i, 0)), blk, blk, blk],
        out_specs=[blk, blk, blk, blk],
        out_shape=[out, out, out, out],
        compiler_params=_params("parallel", "parallel"),
    )(parts, w, m, v)


def _pack(arrays):
    flat = []
    for a in arrays:
        a = a.reshape(-1).astype(F32)
        flat.append(jnp.pad(a, (0, (-a.shape[0]) % PACK_UNIT)))
    out = jnp.concatenate(flat)
    rows = out.shape[0] // LANES
    pad_rows = (-rows) % 512
    return jnp.pad(out, (0, pad_rows * LANES)).reshape(rows + pad_rows, LANES)


def _unpack(packed, shapes):
    flat = packed.reshape(-1)
    out, off = [], 0
    for shp in shapes:
        size = math.prod(shp)
        out.append(flat[off:off + size].reshape(shp))
        off += size + (-size) % PACK_UNIT
    return out


def _in_runs(d, qkv_w, sw, gb):
    g0 = qkv_w + 2 * sw
    runs = [(0, qkv_w, "qkv", 0), (qkv_w, 2 * sw, "z", 0)]
    for j in range(d // gb):
        runs.append((g0 + j * gb, gb, "g", 2 * j * gb))
        runs.append((g0 + d + j * gb, gb, "g", (2 * j + 1) * gb))
    return runs


def _pieces_from_global(take, runs):
    out = {}
    for piece in ("qkv", "z", "g"):
        own = sorted((r for r in runs if r[2] == piece), key=lambda r: r[3])
        out[piece] = jnp.concatenate([take(g, g + w) for g, w, _, _ in own], axis=-1)
    return out


def _global_from_pieces(pieces, runs, a, b):
    segs = []
    for g, w, piece, start in sorted(runs):
        lo, hi = max(a, g), min(b, g + w)
        if lo < hi:
            segs.append(pieces[piece][..., start + lo - g:start + hi - g])
    return jnp.concatenate(segs, axis=-1)


def _take_from_shards(land):
    c = land.shape[2]

    def take(a, b):
        parts = []
        while a < b:
            dev = a // c
            lo, hi = a - dev * c, min(b - dev * c, c)
            parts.append(land[dev][:, lo:hi])
            a = dev * c + hi
        return jnp.concatenate(parts, axis=-1)

    return take


def _to_full_cols(g):
    d, k, n = g.shape
    return jnp.transpose(g, (1, 0, 2)).reshape(k, d * n)


def _to_col_shards(a):
    k, n = a.shape
    return jnp.transpose(a.reshape(k, N_DEV, n // N_DEV), (1, 0, 2))


def kernel(x, positions, norm1_g, w_in, b_in, sinks, sgu_ln_g, sgu_ln_b, sgu_w, sgu_b, w_attn_branch, w_sgu_branch, w_out, norm2_g, w_gate_up, w_down, final_g, loss_target, m_norm1_g, m_w_in, m_b_in, m_sinks, m_sgu_ln_g, m_sgu_ln_b, m_sgu_w, m_sgu_b, m_w_attn_branch, m_w_sgu_branch, m_w_out, m_norm2_g, m_w_gate_up, m_w_down, m_final_g, v_norm1_g, v_w_in, v_b_in, v_sinks, v_sgu_ln_g, v_sgu_ln_b, v_sgu_w, v_sgu_b, v_w_attn_branch, v_w_sgu_branch, v_w_out, v_norm2_g, v_w_gate_up, v_w_down, v_final_g):
    nl = w_in.shape[0]
    s, d = x.shape[1], x.shape[2]
    aw = w_attn_branch.shape[1]
    sw = w_sgu_branch.shape[1]
    in_w = w_in.shape[2] * N_DEV
    kw = (in_w - aw - 2 * sw - 2 * d) // 2
    qkv_w = aw + 2 * kw
    groups = sw // GROUP_DIM
    ff = w_down.shape[1] * N_DEV

    h = x.reshape(s, d)
    target = loss_target.reshape(s, d)
    tabs = _rope_tables(positions.reshape(s, 1), "rope_tables")

    big = [w_in, w_attn_branch, w_sgu_branch, w_out, w_gate_up, w_down]
    big_m = [m_w_in, m_w_attn_branch, m_w_sgu_branch, m_w_out, m_w_gate_up, m_w_down]
    big_v = [v_w_in, v_w_attn_branch, v_w_sgu_branch, v_w_out, v_w_gate_up, v_w_down]
    big_names = ("w_in", "w_attn_branch", "w_sgu_branch", "w_out", "w_gate_up", "w_down")
    W_IN, W_AB, W_SB, W_OUT, W_GU, W_DOWN = range(6)
    weight_groups = ((W_IN,), (W_AB, W_SB, W_OUT), (W_GU, W_DOWN))
    grad_groups = ((W_DOWN, W_GU), (W_OUT, W_AB, W_SB), (W_IN,))

    col_sharded = (W_IN, W_AB, W_SB, W_GU)
    by_cols = [BY_COLS if t in col_sharded and big[t].shape[2] % LANES == 0 else BY_SLOTS for t in range(6)]
    assert by_cols[W_GU] == BY_COLS, "the fused swiglu kernels need gate/up column blocks of whole lane tiles"
    by_cols[W_GU] = BY_PAIRED_COLS

    def start_gather(l, group, after):
        return _exchange_start([big[t][l].astype(BF16) for t in group], False, tuple(by_cols[t] for t in group),
                               f"gather_start_l{l}_{big_names[group[0]]}", after)

    def full_weight(t, land):
        if by_cols[t]:
            return land
        if t == W_IN:
            return _pieces_from_global(_take_from_shards(land), in_runs)
        return _to_full_cols(land) if t in col_sharded else land.reshape(N_DEV * land.shape[1], land.shape[2])

    gate_block = _pick(d, GATE_BLOCK_PREFS)
    in_runs = _in_runs(d, qkv_w, sw, gate_block)

    saved = []
    started = {}
    token = h
    for l in range(nl):
        for ll in ((0, 1) if l == 0 else (l + 1,)):
            if ll < nl:
                for group in weight_groups:
                    started[(ll, group)] = start_gather(ll, group, token)
                    token = started[(ll, group)][-1]
        gathered = {}

        def weight(t, after, l=l, gathered=gathered):
            if t not in gathered:
                group = next(g for (ll, g) in started if ll == l and t in g)
                srcs, lands = _exchange_wait(started.pop((l, group)), after, False, tuple(by_cols[tt] for tt in group),
                                             f"gather_wait_l{l}_{big_names[group[0]]}")
                for tt, ld in zip(group, lands):
                    gathered[tt] = full_weight(tt, ld)
            return gathered[t]

        bias = b_in[l].reshape(1, in_w)
        g1, g2 = norm1_g[l].reshape(1, d), norm2_g[l].reshape(1, d)
        lg, lb = sgu_ln_g[l].reshape(1, sw), sgu_ln_b[l].reshape(1, sw)
        bt = sgu_b[l].T

        xn = _rms_fwd(h, g1, "rms1_fwd", after=(token,))
        wts = dict(weight(W_IN, xn))
        biases = _pieces_from_global(lambda a, b: bias[:, a:b], in_runs)
        qkv = _matmul(xn, wts["qkv"], "nn", BF16, "proj_qkv", bias=biases["qkv"])
        pz = _matmul(xn, wts["z"], "nn", BF16, "proj_z", bias=biases["z"])
        pg = _matmul(xn, wts["g"], "nn", BF16, "proj_g", bias=biases["g"])
        y_attn = _attn_fwd(qkv, tabs, sinks[l], aw, kw, "attn_fwd")
        y_sgu = _sgu_fwd(pz, lg, lb, sgu_w[l], bt, "sgu_fwd")
        wts.update(ab=weight(W_AB, y_sgu), sb=weight(W_SB, y_sgu), out=weight(W_OUT, y_sgu))
        a_br, s_br, merged = _branches_merge(y_attn, y_sgu, wts["ab"], wts["sb"], pg, gate_block, "branches_merge")
        h_mid = _matmul(merged, wts["out"], "nn", F32, "out_proj", res=h)
        hn = _rms_fwd(h_mid, g2, "rms2_fwd")
        wts.update(gu=weight(W_GU, hn), down=weight(W_DOWN, hn))
        gu, act = _gate_up_swiglu(hn, wts["gu"], "gate_up")
        h_out = _matmul(act, wts["down"], "nn", F32, "down_proj", res=h_mid)
        saved.append(dict(wts=wts, h=h, xn=xn, qkv=qkv, pz=pz, pg=pg, y_attn=y_attn, y_sgu=y_sgu, a_br=a_br,
                          s_br=s_br, merged=merged, h_mid=h_mid, hn=hn, gu=gu, act=act,
                          g1=g1, g2=g2, lg=lg, lb=lb, bt=bt))
        h = h_out

    dh, dhb, d_final_g, loss_blk = _loss_head(h, final_g.reshape(1, d), target, "loss_head")

    small = {n: [None] * nl for n in ("norm1_g", "b_in", "sinks", "sgu_ln_g", "sgu_ln_b", "sgu_w", "sgu_b", "norm2_g")}
    scattering = {}

    def start_scatter(l, group, grads, after):
        sends = []
        for t, dw in zip(group, grads):
            if by_cols[t]:
                sends.append(dw)
            elif t == W_IN:
                c = big[t].shape[2]
                sends.append(jnp.stack([_global_from_pieces(dw, in_runs, dev * c, (dev + 1) * c) for dev in range(N_DEV)]))
            elif t in col_sharded:
                sends.append(_to_col_shards(dw))
            else:
                sends.append(dw.reshape(N_DEV, dw.shape[0] // N_DEV, dw.shape[1]))
        scattering[(l, group)] = _exchange_start(sends, True, tuple(by_cols[t] for t in group),
                                                 f"scatter_start_l{l}_{big_names[group[0]]}", after)
        return scattering[(l, group)][-1]

    for l in reversed(range(nl)):
        sv = saved[l]
        wts = sv["wts"]
        d_gu = _d_act_swiglu(dhb, wts["down"], sv["gu"], "d_act")
        dw_down = _matmul(sv["act"], dhb, "tn", BF16, "dw_down")
        dw_gu = _matmul(sv["hn"], d_gu, "tn", BF16, "dw_gate_up")
        token = start_scatter(l, grad_groups[0], [dw_down, dw_gu], token)
        d_hn = _matmul(d_gu, wts["gu"], "nt", BF16, "d_hn", after=token)
        dh_mid, dmb, dg2 = _rms_bwd(sv["h_mid"], sv["g2"], d_hn, dh, "rms2_bwd")
        d_a, d_s, d_pg, db_g = _d_merged_merge_bwd(dmb, wts["out"], sv["pg"], sv["a_br"], sv["s_br"], gate_block, "d_merged")
        dw_out = _matmul(sv["merged"], dmb, "tn", BF16, "dw_out")
        d_y_attn = _matmul(d_a, wts["ab"], "nt", BF16, "d_y_attn")
        dw_ab = _matmul(sv["y_attn"], d_a, "tn", BF16, "dw_attn_branch")
        d_y_sgu = _matmul(d_s, wts["sb"], "nt", BF16, "d_y_sgu")
        dw_sb = _matmul(sv["y_sgu"], d_s, "tn", BF16, "dw_sgu_branch")
        token = start_scatter(l, grad_groups[1], [dw_out, dw_ab, dw_sb], token)
        d_pz, d_lg, d_lb, d_sw, d_sbt, db_z = _sgu_bwd(sv["pz"], d_y_sgu, sv["lg"], sv["lb"], sgu_w[l],
                                                 jnp.transpose(sgu_w[l], (0, 2, 1)), sv["bt"], "sgu_bwd", after=(token,))
        d_q, d_kv, d_sinks = _attn_bwd(sv["qkv"], tabs, sinks[l], sv["y_attn"], d_y_attn, aw, kw, "attn_bwd")
        d_qkv = jnp.concatenate([d_q, d_kv], axis=1)
        dw_qkv = _matmul(sv["xn"], d_qkv, "tn", BF16, "dw_qkv")
        dw_z = _matmul(sv["xn"], d_pz, "tn", BF16, "dw_z")
        dw_g = _matmul(sv["xn"], d_pg, "tn", BF16, "dw_g")
        token = start_scatter(l, grad_groups[2], [dict(qkv=dw_qkv, z=dw_z, g=dw_g)], token)
        d_xn = _matmul(d_qkv, wts["qkv"], "nt", F32, "d_xn_qkv", after=token)
        d_xn = _matmul(d_pz, wts["z"], "nt", F32, "d_xn_z", res=d_xn)
        d_xn = _matmul(d_pg, wts["g"], "nt", F32, "d_xn_g", res=d_xn)
        dh, dhb, dg1 = _rms_bwd(sv["h"], sv["g1"], d_xn, dh_mid, "rms1_bwd")

        small["norm1_g"][l], small["norm2_g"][l] = dg1, dg2
        small["b_in"][l] = _global_from_pieces(dict(qkv=_colsum(d_qkv, "db_qkv"), z=db_z, g=db_g), in_runs, 0, in_w)
        small["sinks"][l] = d_sinks[0, :aw // HEAD_DIM]
        small["sgu_ln_g"][l], small["sgu_ln_b"][l] = d_lg, d_lb
        small["sgu_w"][l] = d_sw
        small["sgu_b"][l] = d_sbt.T

    grad_x = dh.reshape(x.shape)

    names = ["norm1_g", "b_in", "sinks", "sgu_ln_g", "sgu_ln_b", "sgu_w", "sgu_b", "norm2_g"]
    small_w = [norm1_g, b_in, sinks, sgu_ln_g, sgu_ln_b, sgu_w, sgu_b, norm2_g, final_g]
    small_m = [m_norm1_g, m_b_in, m_sinks, m_sgu_ln_g, m_sgu_ln_b, m_sgu_w, m_sgu_b, m_norm2_g, m_final_g]
    small_v = [v_norm1_g, v_b_in, v_sinks, v_sgu_ln_g, v_sgu_ln_b, v_sgu_w, v_sgu_b, v_norm2_g, v_final_g]
    shapes = [w.shape for w in small_w] + [(1,)]
    partial = [jnp.stack([p.reshape(w.shape[1:]) for p in small[n]]) for n, w in zip(names, small_w)]
    partial += [d_final_g.reshape(final_g.shape), loss_blk[0, :1]]
    zero = jnp.zeros((1,), F32)
    small_started = _exchange_start([_pack(partial)], False, (False,), "gather_start_small_grads", dhb)

    big_out = [None] * len(big)
    after = small_started[-1]
    for l in reversed(range(nl)):
        for group in grad_groups:
            srcs, lands = _exchange_wait(scattering.pop((l, group)), after, True, tuple(by_cols[t] for t in group),
                                         f"scatter_wait_l{l}_{big_names[group[0]]}")
            for t, parts in zip(group, lands):
                big_out[t] = _reduce_adam_layer(parts, big[t], big_m[t], big_v[t], big_out[t], l, f"adam_{big_names[t]}")
                after = big_out[t][0]

    srcs, lands = _exchange_wait(small_started, after, False, (False,), "gather_wait_small_grads")
    sm = _reduce_adam(lands[0][None], _pack(small_w + [zero])[None], _pack(small_m + [zero])[None],
                      _pack(small_v + [zero])[None], "adam_small")
    sm_g, sm_d, sm_m, sm_v = [_unpack(a[0], shapes) for a in sm]
    loss = sm_g[-1].reshape(())

    def ordered(kind_small, kind_big):
        by_name = dict(zip(["norm1_g", "b_in", "sinks", "sgu_ln_g", "sgu_ln_b", "sgu_w", "sgu_b", "norm2_g", "final_g"], kind_small))
        by_name.update(zip(["w_in", "w_attn_branch", "w_sgu_branch", "w_out", "w_gate_up", "w_down"], kind_big))
        order = ["norm1_g", "w_in", "b_in", "sinks", "sgu_ln_g", "sgu_ln_b", "sgu_w", "sgu_b", "w_attn_branch",
                 "w_sgu_branch", "w_out", "norm2_g", "w_gate_up", "w_down", "final_g"]
        return [by_name[n] for n in order]

    outs = [loss, grad_x]
    for idx, sm_kind in enumerate((sm_g, sm_d, sm_m, sm_v)):
        outs += ordered(sm_kind[:-1], [o[idx] for o in big_out])
    return tuple(outs)
```

```python
import math

import jax
import jax.numpy as jnp
from jax import lax
from jax.experimental import pallas as pl
from jax.experimental.pallas import tpu as pltpu

F32 = jnp.float32
BF16 = jnp.bfloat16

N_DEV = 8
HEAD_DIM = 64
WINDOW = 128
CHUNK = 128
GROUP_DIM = 128
ROPE_DIM = HEAD_DIM // 4
ROPE_HALF = ROPE_DIM // 2
ROPE_THETA = 500000.0
EPS = 1e-5
NEG = -1e30
ATTN_SCALE = HEAD_DIM ** -0.5
ADAM_LR = 0.001
ADAM_B1 = 0.9
ADAM_B2 = 0.999
ADAM_EPS = 1e-08
ADAM_WD = 0.01
ADAM_STEP = 10
LANES = 128
SUBLANES = 8
BF16_SUBLANES = 16
PACK_UNIT = SUBLANES * LANES
ADAM_BLOCK_ELEMS = 256 * 1024
V7X_VMEM_LIMIT_BYTES = 56 * 1024 * 1024
MATMUL_TILE_PREFS = (1024, 1408, 768, 512, 384, 256, 128)
MATMUL_WHOLE_K = 2048
MATMUL_TN_K = 4096
MATMUL_VMEM_BUDGET_BYTES = 52 * 1024 * 1024
MATMUL_K_PREFS = (2816, 2048, 1536, 1408, 1024, 768, 512, 384, 256, 128)
ROW_TILE_PREFS = (512, 256, 128)
SWIGLU_ROW_PREFS = (512, 256, 128)
D_ACT_ROW_PREFS = (1024, 512, 256, 128)
SUMMED_ROW_PREFS = (512, 256, 128)
MERGE_ROW_PREFS = (512, 256, 128)
GATE_BLOCK_PREFS = (1024, 512, 256, 128)
FUSED_ROW_CHUNK = 256
MESH_TYPE = pl.DeviceIdType.MESH
ANY = pl.BlockSpec(memory_space=pl.ANY)
HBM = pl.BlockSpec(memory_space=pltpu.HBM)
SEM = pl.BlockSpec(memory_space=pltpu.SEMAPHORE)
DATAFLOW_EFFECT = pltpu.SideEffectType.DATAFLOW_SIDE_EFFECTING


def _pick(n, prefs):
    for p in prefs:
        if n % p == 0:
            return p
    return n


def _params(*sem):
    return pltpu.CompilerParams(dimension_semantics=sem, vmem_limit_bytes=V7X_VMEM_LIMIT_BYTES)


_DIMS = {"nn": (((1,), (0,)), ((), ())), "nt": (((1,), (1,)), ((), ())), "tn": (((0,), (0,)), ((), ()))}


def _matmul(a, b, mode, out_dtype, name, bias=None, res=None, after=None):
    if mode == "nn":
        (m, k), n = a.shape, b.shape[1]
    elif mode == "nt":
        (m, k), n = a.shape, b.shape[0]
    else:
        (k, m), n = a.shape, b.shape[1]
    tm, tn = _pick(m, MATMUL_TILE_PREFS), _pick(n, MATMUL_TILE_PREFS)
    if k <= MATMUL_WHOLE_K:
        tk = k
    else:
        fits = [t for t in ((MATMUL_TN_K,) if mode == "tn" else ()) + MATMUL_K_PREFS
                if k % t == 0 and 4 * t * (tm + tn) + 16 * tm * tn <= MATMUL_VMEM_BUDGET_BYTES]
        tk = fits[0]
    nk = k // tk
    dims = _DIMS[mode]
    a_spec = pl.BlockSpec((tk, tm), lambda i, j, kk: (kk, i)) if mode == "tn" else pl.BlockSpec((tm, tk), lambda i, j, kk: (i, kk))
    b_spec = pl.BlockSpec((tn, tk), lambda i, j, kk: (j, kk)) if mode == "nt" else pl.BlockSpec((tk, tn), lambda i, j, kk: (kk, j))
    in_specs, args = [a_spec, b_spec], [a, b]
    if bias is not None:
        in_specs.append(pl.BlockSpec((1, tn), lambda i, j, kk: (0, j)))
        args.append(bias)
    if res is not None:
        in_specs.append(pl.BlockSpec((tm, tn), lambda i, j, kk: (i, j)))
        args.append(res)
    if after is not None:
        in_specs.append(ANY)
        args.append(after)

    def body(*refs):
        a_ref, b_ref = refs[0], refs[1]
        pos = 2
        bias_ref = res_ref = None
        if bias is not None:
            bias_ref = refs[pos]
            pos += 1
        if res is not None:
            res_ref = refs[pos]
            pos += 1
        if after is not None:
            pos += 1
        o_ref = refs[pos]

        def finish(r):
            if bias_ref is not None:
                r = r + bias_ref[...]
            if res_ref is not None:
                r = r + res_ref[...]
            o_ref[...] = r.astype(out_dtype)

        part = lax.dot_general(a_ref[...], b_ref[...], dims, preferred_element_type=F32)
        if nk == 1:
            finish(part)
        else:
            acc_ref = refs[pos + 1]
            kk = pl.program_id(2)

            @pl.when(kk == 0)
            def _():
                acc_ref[...] = part

            @pl.when((kk > 0) & (kk < nk - 1))
            def _():
                acc_ref[...] += part

            @pl.when(kk == nk - 1)
            def _():
                finish(acc_ref[...] + part)

    return pl.pallas_call(
        body,
        name=name,
        grid=(m // tm, n // tn, nk),
        in_specs=in_specs,
        out_specs=pl.BlockSpec((tm, tn), lambda i, j, kk: (i, j)),
        out_shape=jax.ShapeDtypeStruct((m, n), out_dtype),
        scratch_shapes=[] if nk == 1 else [pltpu.VMEM((tm, tn), F32)],
        compiler_params=_params("parallel", "parallel", "arbitrary"),
    )(*args)


def _out_proj_rms(merged, w_out, h, g, name):
    s, d = h.shape
    tm = _pick(s, SUMMED_ROW_PREFS)

    def body(a_ref, b_ref, h_ref, g_ref, o_ref, n_ref):
        for rows in _row_chunks(tm):
            r = h_ref[rows, :] + jnp.dot(a_ref[rows, :], b_ref[...], preferred_element_type=F32)
            o_ref[rows, :] = r
            rs = lax.rsqrt(jnp.mean(r * r, axis=-1, keepdims=True) + EPS)
            n_ref[rows, :] = (r * rs * g_ref[...]).astype(BF16)

    row = pl.BlockSpec((tm, d), lambda i: (i, 0))
    return pl.pallas_call(
        body,
        name=name,
        grid=(s // tm,),
        in_specs=[row, pl.BlockSpec((d, d), lambda i: (0, 0)), row, pl.BlockSpec((1, d), lambda i: (0, 0))],
        out_specs=[row, row],
        out_shape=[_sds((s, d), F32), _sds((s, d), BF16)],
        compiler_params=_params("parallel"),
    )(merged, w_out, h, g)


def _matmul_nt_sum(pairs, out_dtype, name, after):
    m, n = pairs[0][0].shape[0], pairs[0][1].shape[0]
    tm, tn = _pick(m, SUMMED_ROW_PREFS), _pick(n, MATMUL_TILE_PREFS)
    steps = []
    for a, _ in pairs:
        k = a.shape[1]
        tk = k if k <= MATMUL_WHOLE_K else _pick(k, (MATMUL_WHOLE_K,) + MATMUL_K_PREFS)
        steps.append((sum(s[1] for s in steps), k // tk, tk))
    nk = sum(s[1] for s in steps)
    in_specs, args = [], []
    for (a, b), (first, count, tk) in zip(pairs, steps):
        at = lambda i, j, kk, first=first, count=count: (i, jnp.clip(kk - first, 0, count - 1))
        bt = lambda i, j, kk, first=first, count=count: (j, jnp.clip(kk - first, 0, count - 1))
        in_specs += [pl.BlockSpec((tm, tk), at), pl.BlockSpec((tn, tk), bt)]
        args += [a, b]

    def body(*refs):
        o_ref, acc_ref = refs[-2], refs[-1]
        kk = pl.program_id(2)
        for p, (first, count, _) in enumerate(steps):
            @pl.when((kk >= first) & (kk < first + count))
            def _(p=p):
                part = lax.dot_general(refs[2 * p][...], refs[2 * p + 1][...], _DIMS["nt"], preferred_element_type=F32)

                @pl.when(kk == 0)
                def _():
                    acc_ref[...] = part

                @pl.when((kk > 0) & (kk < nk - 1))
                def _():
                    acc_ref[...] += part

                @pl.when(kk == nk - 1)
                def _():
                    o_ref[...] = (acc_ref[...] + part).astype(out_dtype)

    return pl.pallas_call(
        body,
        name=name,
        grid=(m // tm, n // tn, nk),
        in_specs=in_specs + [ANY],
        out_specs=pl.BlockSpec((tm, tn), lambda i, j, kk: (i, j)),
        out_shape=jax.ShapeDtypeStruct((m, n), out_dtype),
        scratch_shapes=[pltpu.VMEM((tm, tn), F32)],
        compiler_params=_params("parallel", "parallel", "arbitrary"),
    )(*args, after)


def _rowwise(body, name, rows, tr, ins, consts, outs, accs=(), after=()):
    n_in, n_c, n_o, n_a = len(ins), len(consts), len(outs), len(after)

    def wrapped(*refs):
        body(pl.program_id(0), refs[:n_in], refs[n_in:n_in + n_c], refs[n_in + n_c + n_a:n_in + n_c + n_a + n_o],
             refs[n_in + n_c + n_a + n_o:])

    def whole(shape):
        zeros = (0,) * len(shape)
        return pl.BlockSpec(tuple(shape), lambda i: zeros)

    in_specs = ([pl.BlockSpec((tr, a.shape[1]), lambda i: (i, 0)) for a in ins] + [whole(c.shape) for c in consts]
                + [ANY] * n_a)
    out_specs = [pl.BlockSpec((tr, o.shape[1]), lambda i: (i, 0)) for o in outs] + [whole(a.shape) for a in accs]
    return pl.pallas_call(
        wrapped,
        name=name,
        grid=(rows // tr,),
        in_specs=in_specs,
        out_specs=out_specs,
        out_shape=list(outs) + list(accs),
        compiler_params=_params("arbitrary" if accs else "parallel"),
    )(*ins, *consts, *after)


def _sds(shape, dtype):
    return jax.ShapeDtypeStruct(tuple(shape), dtype)


def _rms_fwd(h, g, name, after=()):
    s, d = h.shape
    tr = _pick(s, ROW_TILE_PREFS)

    def body(i, ins, consts, outs, accs):
        x = ins[0][...]
        r = lax.rsqrt(jnp.mean(x * x, axis=-1, keepdims=True) + EPS)
        outs[0][...] = (x * r * consts[0][...]).astype(BF16)

    return _rowwise(body, name, s, tr, [h], [g], [_sds((s, d), BF16)], after=after)[0]


def _rms_bwd(h, g, dy, dh_up, name):
    s, d = h.shape
    tr = _pick(s, ROW_TILE_PREFS)

    def body(i, ins, consts, outs, accs):
        x, dyv, up = ins[0][...], ins[1][...].astype(F32), ins[2][...]
        r = lax.rsqrt(jnp.mean(x * x, axis=-1, keepdims=True) + EPS)
        xr = x * r
        gy = dyv * consts[0][...]
        dx = r * (gy - xr * jnp.mean(gy * xr, axis=-1, keepdims=True))
        outs[0][...] = up + dx
        outs[1][...] = (up + dx).astype(BF16)

        @pl.when(i == 0)
        def _():
            accs[0][...] = jnp.zeros_like(accs[0])

        accs[0][...] += jnp.sum(dyv * xr, axis=0, keepdims=True)

    return _rowwise(body, name, s, tr, [h, dy, dh_up], [g], [_sds((s, d), F32), _sds((s, d), BF16)], [_sds((1, d), F32)])


def _loss_head(h, g, target, name):
    s, d = h.shape
    tr = _pick(s, ROW_TILE_PREFS)

    def body(i, ins, consts, outs, accs):
        x, t = ins[0][...], ins[1][...]
        gv = consts[0][...]
        r = lax.rsqrt(jnp.mean(x * x, axis=-1, keepdims=True) + EPS)
        xr = x * r
        diff = xr * gv - t
        dyv = diff * (1.0 / d)
        gy = dyv * gv
        dx = r * (gy - xr * jnp.mean(gy * xr, axis=-1, keepdims=True))
        outs[0][...] = dx
        outs[1][...] = dx.astype(BF16)

        @pl.when(i == 0)
        def _():
            accs[0][...] = jnp.zeros_like(accs[0])
            accs[1][...] = jnp.zeros_like(accs[1])

        accs[0][...] += jnp.sum(dyv * xr, axis=0, keepdims=True)
        part = 0.5 * jnp.sum(jnp.mean(diff * diff, axis=-1, keepdims=True), axis=0, keepdims=True)
        accs[1][...] += jnp.broadcast_to(part, accs[1].shape)

    return _rowwise(body, name, s, tr, [h, target], [g], [_sds((s, d), F32), _sds((s, d), BF16)],
                    [_sds((1, d), F32), _sds((SUBLANES, LANES), F32)])


def _colsum(a, name):
    s, w = a.shape
    tr = _pick(s, ROW_TILE_PREFS)

    def body(i, ins, consts, outs, accs):
        @pl.when(i == 0)
        def _():
            accs[0][...] = jnp.zeros_like(accs[0])

        accs[0][...] += jnp.sum(ins[0][...].astype(F32), axis=0, keepdims=True)

    return _rowwise(body, name, s, tr, [a], [], [], [_sds((1, w), F32)])[0]


def _sigmoid(x):
    return 1.0 / (1.0 + jnp.exp(-x))


def _branches_merge(y_attn, y_sgu, w_ab, w_sb, pg, gb, name):
    s, aw = y_attn.shape
    sw, d = w_sb.shape
    tm = _pick(s, MERGE_ROW_PREFS)

    def body(ya_ref, ys_ref, wa_ref, ws_ref, pg_ref, a_out, s_out, m_out):
        for rows in _row_chunks(tm):
            a = jnp.dot(ya_ref[rows, :], wa_ref[...], preferred_element_type=F32)
            b = jnp.dot(ys_ref[rows, :], ws_ref[...], preferred_element_type=F32)
            ga, gs = _sigmoid(pg_ref[rows, :gb].astype(F32)), _sigmoid(pg_ref[rows, gb:].astype(F32))
            a_out[rows, :] = a.astype(BF16)
            s_out[rows, :] = b.astype(BF16)
            m_out[rows, :] = (ga * a + gs * b).astype(BF16)

    blk = pl.BlockSpec((tm, gb), lambda i, j: (i, j))
    out = _sds((s, d), BF16)
    return pl.pallas_call(
        body,
        name=name,
        grid=(s // tm, d // gb),
        in_specs=[pl.BlockSpec((tm, aw), lambda i, j: (i, 0)), pl.BlockSpec((tm, sw), lambda i, j: (i, 0)),
                  pl.BlockSpec((aw, gb), lambda i, j: (0, j)), pl.BlockSpec((sw, gb), lambda i, j: (0, j)),
                  pl.BlockSpec((tm, 2 * gb), lambda i, j: (i, j))],
        out_specs=[blk, blk, blk],
        out_shape=[out, out, out],
        compiler_params=_params("parallel", "parallel"),
    )(y_attn, y_sgu, w_ab, w_sb, pg)


def _d_merged_merge_bwd(dmb, w_out, pg, a_br, s_br, gb, name):
    s, d = dmb.shape
    tm = _pick(s, MERGE_ROW_PREFS)

    def body(a_ref, b_ref, pg_ref, ab_ref, sb_ref, da_out, ds_out, dpg_out, db_out):
        @pl.when(pl.program_id(1) == 0)
        def _():
            db_out[...] = jnp.zeros_like(db_out)

        for rows in _row_chunks(tm):
            dm = lax.dot_general(a_ref[rows, :], b_ref[...], _DIMS["nt"], preferred_element_type=F32)
            ga, gs = _sigmoid(pg_ref[rows, :gb].astype(F32)), _sigmoid(pg_ref[rows, gb:].astype(F32))
            da_out[rows, :] = (dm * ga).astype(BF16)
            ds_out[rows, :] = (dm * gs).astype(BF16)
            dpa = dm * ab_ref[rows, :].astype(F32) * ga * (1.0 - ga)
            dps = dm * sb_ref[rows, :].astype(F32) * gs * (1.0 - gs)
            dpg_out[rows, :gb] = dpa.astype(BF16)
            dpg_out[rows, gb:] = dps.astype(BF16)
            db_out[:, :gb] += jnp.sum(dpa, axis=0, keepdims=True)
            db_out[:, gb:] += jnp.sum(dps, axis=0, keepdims=True)

    blk = pl.BlockSpec((tm, gb), lambda j, i: (i, j))
    pair = pl.BlockSpec((tm, 2 * gb), lambda j, i: (i, j))
    return pl.pallas_call(
        body,
        name=name,
        grid=(d // gb, s // tm),
        in_specs=[pl.BlockSpec((tm, d), lambda j, i: (i, 0)), pl.BlockSpec((gb, d), lambda j, i: (j, 0)), pair, blk, blk],
        out_specs=[blk, blk, pair, pl.BlockSpec((1, 2 * gb), lambda j, i: (0, j))],
        out_shape=[_sds((s, d), BF16), _sds((s, d), BF16), _sds((s, 2 * d), BF16), _sds((1, 2 * d), F32)],
        compiler_params=_params("parallel", "arbitrary"),
    )(dmb, w_out, pg, a_br, s_br)


def _row_chunks(tm):
    rc = _pick(tm, (FUSED_ROW_CHUNK,))
    return [slice(r, r + rc) for r in range(0, tm, rc)]


def _gate_up_swiglu(hn, w_gu, name):
    s, d = hn.shape
    n2 = w_gu.shape[1]
    fb = n2 // N_DEV
    tm = _pick(s, SWIGLU_ROW_PREFS)

    def body(a_ref, b_ref, gu_ref, act_ref):
        r = jnp.dot(a_ref[...], b_ref[...], preferred_element_type=F32)
        gu_ref[...] = r.astype(BF16)
        gate, up = r[:, :fb], r[:, fb:]
        act_ref[...] = (gate * _sigmoid(gate) * up).astype(BF16)

    return pl.pallas_call(
        body,
        name=name,
        grid=(s // tm, N_DEV // 2),
        in_specs=[pl.BlockSpec((tm, d), lambda i, j: (i, 0)), pl.BlockSpec((d, 2 * fb), lambda i, j: (0, j))],
        out_specs=[pl.BlockSpec((tm, 2 * fb), lambda i, j: (i, j)), pl.BlockSpec((tm, fb), lambda i, j: (i, j))],
        out_shape=[_sds((s, n2), BF16), _sds((s, n2 // 2), BF16)],
        compiler_params=_params("parallel", "parallel"),
    )(hn, w_gu)


def _d_act_swiglu(dhb, w_down, gu, name):
    s, d = dhb.shape
    n2 = gu.shape[1]
    fb = n2 // N_DEV
    tm = _pick(s, D_ACT_ROW_PREFS)

    def body(a_ref, b_ref, gu_ref, o_ref):
        for rows in _row_chunks(tm):
            da = lax.dot_general(a_ref[rows, :], b_ref[...], _DIMS["nt"], preferred_element_type=F32)
            gate, up = gu_ref[rows, :fb].astype(F32), gu_ref[rows, fb:].astype(F32)
            sg = _sigmoid(gate)
            o_ref[rows, :fb] = (da * up * sg * (1.0 + gate * (1.0 - sg))).astype(BF16)
            o_ref[rows, fb:] = (da * gate * sg).astype(BF16)

    pair = pl.BlockSpec((tm, 2 * fb), lambda i, j: (i, j))
    return pl.pallas_call(
        body,
        name=name,
        grid=(s // tm, N_DEV // 2),
        in_specs=[pl.BlockSpec((tm, d), lambda i, j: (i, 0)), pl.BlockSpec((fb, d), lambda i, j: (j, 0)), pair],
        out_specs=pair,
        out_shape=_sds((s, n2), BF16),
        compiler_params=_params("parallel", "parallel"),
    )(dhb, w_down, gu)


def _rope_tables(pos_col, name):
    s = pos_col.shape[0]
    tr = _pick(s, (1024, 512, 256, 128))
    inv = ROPE_THETA ** (-jnp.arange(0, ROPE_DIM, 2, dtype=F32) / ROPE_DIM)
    lane = jnp.arange(LANES)
    inv_lanes = inv[lane % ROPE_HALF].reshape(1, LANES)

    def body(i, ins, consts, outs, accs):
        ang = ins[0][...].astype(F32) * consts[0][...]
        c, sn = jnp.cos(ang), jnp.sin(ang)
        in_head = lax.broadcasted_iota(jnp.int32, ang.shape, 1) % HEAD_DIM
        outs[0][:, 0:LANES] = jnp.where(in_head < ROPE_DIM, c, 1.0)
        outs[0][:, LANES:2 * LANES] = jnp.where(in_head < ROPE_HALF, -sn, 0.0)
        outs[0][:, 2 * LANES:] = jnp.where((in_head >= ROPE_HALF) & (in_head < ROPE_DIM), sn, 0.0)

    return _rowwise(body, name, s, tr, [pos_col], [inv_lanes], [_sds((s, 3 * LANES), F32)])[0]


def _rope(x, tab, inverse=False):
    width = x.shape[1]
    reps = width // LANES
    c = jnp.tile(tab[:, 0:LANES], (1, reps))
    lo = jnp.tile(tab[:, LANES:2 * LANES], (1, reps))
    hi = jnp.tile(tab[:, 2 * LANES:], (1, reps))
    if inverse:
        lo, hi = -lo, -hi
    return x * c + pltpu.roll(x, width - ROPE_HALF, 1) * lo + pltpu.roll(x, ROPE_HALF, 1) * hi


def _attn_specs(aw, kw):
    kb = aw // kw
    prev = lambda i: jnp.maximum(i - 1, 0)
    return [
        pl.BlockSpec(memory_space=pltpu.SMEM),
        pl.BlockSpec((WINDOW, aw), lambda i: (i, 0)),
        pl.BlockSpec((WINDOW, kw), lambda i: (i, kb)),
        pl.BlockSpec((WINDOW, kw), lambda i: (prev(i), kb)),
        pl.BlockSpec((WINDOW, kw), lambda i: (i, kb + 1)),
        pl.BlockSpec((WINDOW, kw), lambda i: (prev(i), kb + 1)),
        pl.BlockSpec((WINDOW, 3 * LANES), lambda i: (i, 0)),
        pl.BlockSpec((WINDOW, 3 * LANES), lambda i: (prev(i), 0)),
    ]


def _attn_common(i, q_ref, kc_ref, kp_ref, vc_ref, vp_ref, tq_ref, tp_ref):
    tq, tp = tq_ref[...], tp_ref[...]
    qt = (_rope(q_ref[...].astype(F32), tq) * ATTN_SCALE).astype(BF16).T
    kc = _rope(kc_ref[...].astype(F32), tq)
    kp = _rope(kp_ref[...].astype(F32), tp)
    k2 = jnp.concatenate([kp, kc], axis=0).astype(BF16)
    v2 = jnp.concatenate([vp_ref[...], vc_ref[...]], axis=0)
    kj = lax.broadcasted_iota(jnp.int32, (2 * WINDOW, WINDOW), 0)
    qi = lax.broadcasted_iota(jnp.int32, (2 * WINDOW, WINDOW), 1)
    rel = qi + WINDOW - kj
    ok = (rel >= 0) & (rel < WINDOW) & ((kj >= WINDOW) | (i > 0))
    return qt, k2, v2, ok, tq, tp


def _head_probs(qt_h, kg, ok, sink):
    s = jnp.dot(kg, qt_h, preferred_element_type=F32)
    s = jnp.where(ok, s, NEG)
    m = jnp.maximum(jnp.max(s, axis=0, keepdims=True), sink)
    p = jnp.exp(s - m)
    es = jnp.exp(sink - m)
    inv = 1.0 / (jnp.sum(p, axis=0, keepdims=True) + es)
    return p * inv, es * inv


def _attn_fwd(qkv, tabs, sinks, aw, kw, name):
    s = qkv.shape[0]
    nq, nkv = aw // HEAD_DIM, kw // HEAD_DIM
    qpk = nq // nkv

    def body(s_ref, q_ref, kc_ref, kp_ref, vc_ref, vp_ref, tq_ref, tp_ref, o_ref):
        i = pl.program_id(0)
        tq, tp = tq_ref[...], tp_ref[...]
        q = (_rope(q_ref[...].astype(F32), tq) * ATTN_SCALE).astype(BF16)
        k2 = jnp.concatenate([_rope(kp_ref[...].astype(F32), tp), _rope(kc_ref[...].astype(F32), tq)], axis=0).astype(BF16)
        v2 = jnp.concatenate([vp_ref[...], vc_ref[...]], axis=0)
        qi = lax.broadcasted_iota(jnp.int32, (WINDOW, 2 * WINDOW), 0)
        kj = lax.broadcasted_iota(jnp.int32, (WINDOW, 2 * WINDOW), 1)
        rel = qi + WINDOW - kj
        ok = (rel >= 0) & (rel < WINDOW) & ((kj >= WINDOW) | (i > 0))
        for h in range(nq):
            g = h // qpk
            hs, gs = slice(h * HEAD_DIM, (h + 1) * HEAD_DIM), slice(g * HEAD_DIM, (g + 1) * HEAD_DIM)
            sc = lax.dot_general(q[:, hs], k2[:, gs], _DIMS["nt"], preferred_element_type=F32)
            sc = jnp.where(ok, sc, NEG)
            m = jnp.maximum(jnp.max(sc, axis=1, keepdims=True), s_ref[h])
            p = jnp.exp(sc - m)
            inv = 1.0 / (jnp.sum(p, axis=1, keepdims=True) + jnp.exp(s_ref[h] - m))
            o = jnp.dot((p * inv).astype(BF16), v2[:, gs], preferred_element_type=F32)
            o_ref[:, hs] = o.astype(BF16)

    return pl.pallas_call(
        body,
        name=name,
        grid=(s // WINDOW,),
        in_specs=_attn_specs(aw, kw),
        out_specs=pl.BlockSpec((WINDOW, aw), lambda i: (i, 0)),
        out_shape=_sds((s, aw), BF16),
        compiler_params=_params("parallel"),
    )(sinks, qkv, qkv, qkv, qkv, qkv, tabs, tabs)


def _attn_bwd(qkv, tabs, sinks, o, do, aw, kw, name):
    s = qkv.shape[0]
    nb = s // WINDOW
    nq, nkv = aw // HEAD_DIM, kw // HEAD_DIM
    qpk = nq // nkv

    def body(s_ref, q_ref, kc_ref, kp_ref, vc_ref, vp_ref, tq_ref, tp_ref, o_ref, do_ref,
             dq_ref, dkv_ref, ds_ref, ck_ref, cv_ref):
        i = pl.program_id(0)

        @pl.when(i == 0)
        def _():
            ck_ref[...] = jnp.zeros_like(ck_ref)
            cv_ref[...] = jnp.zeros_like(cv_ref)
            ds_ref[...] = jnp.zeros_like(ds_ref)

        qt, k2, v2, ok, tq, tp = _attn_common(i, q_ref, kc_ref, kp_ref, vc_ref, vp_ref, tq_ref, tp_ref)
        dot_t, ot = do_ref[...].T, o_ref[...].T
        k2t = k2.T
        row0 = lax.broadcasted_iota(jnp.int32, (SUBLANES, LANES), 0) == 0
        lane = lax.broadcasted_iota(jnp.int32, (SUBLANES, LANES), 1)
        dsink = jnp.zeros((SUBLANES, LANES), F32)
        dqt_parts, dk_parts, dv_parts = [], [], []
        for g in range(nkv):
            gs = slice(g * HEAD_DIM, (g + 1) * HEAD_DIM)
            kg, vg = k2[:, gs], v2[:, gs]
            dk_g = jnp.zeros((2 * WINDOW, HEAD_DIM), F32)
            dv_g = jnp.zeros((2 * WINDOW, HEAD_DIM), F32)
            for j in range(qpk):
                h = g * qpk + j
                hs = slice(h * HEAD_DIM, (h + 1) * HEAD_DIM)
                pn, psink = _head_probs(qt[hs], kg, ok, s_ref[h])
                delta = jnp.sum(dot_t[hs].astype(F32) * ot[hs].astype(F32), axis=0, keepdims=True)
                dp = jnp.dot(vg, dot_t[hs], preferred_element_type=F32)
                dsb = (pn * (dp - delta)).astype(BF16)
                dsink = dsink + jnp.where(row0 & (lane == h), -jnp.sum(psink * delta, axis=1, keepdims=True), 0.0)
                dqt_parts.append(jnp.dot(k2t[gs], dsb, preferred_element_type=F32))
                dk_g = dk_g + lax.dot_general(dsb, qt[hs], _DIMS["nt"], preferred_element_type=F32)
                dv_g = dv_g + lax.dot_general(pn.astype(BF16), dot_t[hs], _DIMS["nt"], preferred_element_type=F32)
            dk_parts.append(dk_g)
            dv_parts.append(dv_g)
        ds_ref[...] += dsink
        dq_ref[...] = _rope(jnp.concatenate(dqt_parts, axis=0).T * ATTN_SCALE, tq, inverse=True).astype(BF16)
        dk2 = jnp.concatenate(dk_parts, axis=1)
        dv2 = jnp.concatenate(dv_parts, axis=1)
        dk_prev = _rope(ck_ref[...] + dk2[:WINDOW], tp, inverse=True)
        dv_prev = cv_ref[...] + dv2[:WINDOW]

        @pl.when(i > 0)
        def _():
            dkv_ref[pl.ds(pl.multiple_of((i - 1) * WINDOW, WINDOW), WINDOW), :] = jnp.concatenate(
                [dk_prev, dv_prev], axis=1).astype(BF16)

        ck_ref[...] = dk2[WINDOW:]
        cv_ref[...] = dv2[WINDOW:]

        @pl.when(i == nb - 1)
        def _():
            dkv_ref[pl.ds(pl.multiple_of(i * WINDOW, WINDOW), WINDOW), :] = jnp.concatenate(
                [_rope(dk2[WINDOW:], tq, inverse=True), dv2[WINDOW:]], axis=1).astype(BF16)

    blk = pl.BlockSpec((WINDOW, aw), lambda i: (i, 0))
    return pl.pallas_call(
        body,
        name=name,
        grid=(nb,),
        in_specs=_attn_specs(aw, kw) + [blk, blk],
        out_specs=[blk, pl.BlockSpec((s, 2 * kw), lambda i: (0, 0)), pl.BlockSpec((SUBLANES, LANES), lambda i: (0, 0))],
        out_shape=[_sds((s, aw), BF16), _sds((s, 2 * kw), BF16), _sds((SUBLANES, LANES), F32)],
        scratch_shapes=[pltpu.VMEM((WINDOW, kw), F32), pltpu.VMEM((WINDOW, kw), F32)],
        compiler_params=_params("arbitrary"),
    )(sinks, qkv, qkv, qkv, qkv, qkv, tabs, tabs, o, do)


_INV_SQRT2 = 1.0 / math.sqrt(2.0)
_INV_SQRT2PI = 1.0 / math.sqrt(2.0 * math.pi)


def _gelu(x):
    return x * (lax.erf(x * _INV_SQRT2) + 1.0) * 0.5


def _gelu_grad(x):
    return 0.5 * (lax.erf(x * _INV_SQRT2) + 1.0) + x * jnp.exp(-0.5 * x * x) * _INV_SQRT2PI


def _sgu_norm(pv, lg, lb):
    zv = _gelu(pv)
    mu = jnp.mean(zv, axis=-1, keepdims=True)
    cen = zv - mu
    rs = lax.rsqrt(jnp.mean(cen * cen, axis=-1, keepdims=True) + EPS)
    xhat = cen * rs
    return xhat, rs, (xhat * lg + lb).astype(BF16)


def _causal(w, upper=False):
    t = lax.broadcasted_iota(jnp.int32, (CHUNK, CHUNK), 0)
    u = lax.broadcasted_iota(jnp.int32, (CHUNK, CHUNK), 1)
    return jnp.where((u >= t) if upper else (t >= u), w, 0.0).astype(BF16)


def _sgu_fwd(pz, lg, lb, w, bt, name):
    s, sw = pz.shape[0], pz.shape[1] // 2
    groups = sw // GROUP_DIM

    def body(i, ins, consts, outs, accs):
        lgv, lbv, w_ref, btv = consts[0][...], consts[1][...], consts[2], consts[3][...]
        zu = _gelu(ins[0][:, :sw].astype(F32))
        _, _, vn = _sgu_norm(ins[0][:, sw:].astype(F32), lgv, lbv)
        for g in range(groups):
            gs = slice(g * GROUP_DIM, (g + 1) * GROUP_DIM)
            sv = jnp.dot(_causal(w_ref[g]), vn[:, gs], preferred_element_type=F32) + btv[:, g:g + 1]
            outs[0][:, gs] = (zu[:, gs] * sv).astype(BF16)

    return _rowwise(body, name, s, CHUNK, [pz], [lg, lb, w, bt], [_sds((s, sw), BF16)])[0]


def _sgu_bwd(pz, dy, lg, lb, w, wt, bt, name, after=()):
    s, sw = pz.shape[0], pz.shape[1] // 2
    groups = sw // GROUP_DIM

    def body(i, ins, consts, outs, accs):
        lgv, lbv, w_ref, wt_ref, btv = consts[0][...], consts[1][...], consts[2], consts[3], consts[4][...]

        @pl.when(i == 0)
        def _():
            for a in accs:
                a[...] = jnp.zeros_like(a)

        pu, pv = ins[0][:, :sw].astype(F32), ins[0][:, sw:].astype(F32)
        dyv = ins[1][...].astype(F32)
        zu = _gelu(pu)
        xhat, rs, vn = _sgu_norm(pv, lgv, lbv)
        dvn_parts, db_parts = [], []
        lower = lax.broadcasted_iota(jnp.int32, (CHUNK, CHUNK), 0) >= lax.broadcasted_iota(jnp.int32, (CHUNK, CHUNK), 1)
        for g in range(groups):
            gs = slice(g * GROUP_DIM, (g + 1) * GROUP_DIM)
            sv = jnp.dot(_causal(w_ref[g]), vn[:, gs], preferred_element_type=F32) + btv[:, g:g + 1]
            dpu = dyv[:, gs] * sv * _gelu_grad(pu[:, gs])
            outs[0][:, gs] = dpu.astype(BF16)
            accs[4][:, gs] += jnp.sum(dpu, axis=0, keepdims=True)
            dsv = dyv[:, gs] * zu[:, gs]
            dsvb = dsv.astype(BF16)
            db_parts.append(jnp.sum(dsv, axis=1, keepdims=True))
            accs[2][g] += jnp.where(lower, lax.dot_general(dsvb, vn[:, gs], _DIMS["nt"], preferred_element_type=F32), 0.0)
            dvn_parts.append(jnp.dot(_causal(wt_ref[g], upper=True), dsvb, preferred_element_type=F32))
        dvn = jnp.concatenate(dvn_parts, axis=1)
        accs[3][...] += jnp.concatenate(db_parts, axis=1)
        accs[0][...] += jnp.sum(dvn * xhat, axis=0, keepdims=True)
        accs[1][...] += jnp.sum(dvn, axis=0, keepdims=True)
        dxh = dvn * lgv
        dz = rs * (dxh - jnp.mean(dxh, axis=-1, keepdims=True) - xhat * jnp.mean(dxh * xhat, axis=-1, keepdims=True))
        dpv = dz * _gelu_grad(pv)
        outs[0][:, sw:] = dpv.astype(BF16)
        accs[4][:, sw:] += jnp.sum(dpv, axis=0, keepdims=True)

    return _rowwise(body, name, s, CHUNK, [pz, dy], [lg, lb, w, wt, bt], [_sds((s, 2 * sw), BF16)],
                    [_sds((1, sw), F32), _sds((1, sw), F32), _sds((groups, CHUNK, CHUNK), F32), _sds((CHUNK, groups), F32),
                     _sds((1, 2 * sw), F32)],
                    after=after)


def _mesh_place():
    x, y, c = lax.axis_index("x"), lax.axis_index("y"), lax.axis_index("c")
    return x, y, c, 4 * x + 2 * y + c


def _peer(x, y, c, k):
    px, py, pc = x ^ ((k >> 2) & 1), y ^ ((k >> 1) & 1), c ^ (k & 1)
    return (px, py, pc), 4 * px + 2 * py + pc


BY_SLOTS, BY_COLS, BY_PAIRED_COLS = 0, 1, 2


def _col_block(ref, idx, width, cols):
    if cols == BY_PAIRED_COLS:
        idx = (idx % (N_DEV // 2)) * 2 + idx // (N_DEV // 2)
    return ref.at[:, pl.ds(pl.multiple_of(idx * width, LANES), width)]


def _exchange_copy(src_ref, land_ref, send_sems, recv_sems, k, place, scatter, arriving, cols):
    x, y, c, me = place
    peer, pidx = _peer(x, y, c, k)
    slot = pidx if arriving else me
    if scatter:
        src = _col_block(src_ref, pidx, land_ref.shape[-1], cols) if cols else src_ref.at[pidx]
        dst = land_ref.at[slot]
    else:
        src = src_ref
        dst = _col_block(land_ref, slot, src_ref.shape[-1], cols) if cols else land_ref.at[slot]
    return pltpu.make_async_remote_copy(
        src_ref=src, dst_ref=dst, send_sem=send_sems[k - 1], recv_sem=recv_sems[k - 1], device_id=peer,
        device_id_type=MESH_TYPE)


def _own_copy(src_ref, land_ref, sem, place, scatter, cols):
    me = place[3]
    if scatter:
        src = _col_block(src_ref, me, land_ref.shape[-1], cols) if cols else src_ref.at[me]
        dst = land_ref.at[me]
    else:
        src = src_ref
        dst = _col_block(land_ref, me, src_ref.shape[-1], cols) if cols else land_ref.at[me]
    return pltpu.make_async_copy(src, dst, sem)


N_PEERS = N_DEV - 1
N_EXCHANGE_SEMS = 2 * N_PEERS + 1


def _land_shape(a, scatter, cols):
    if scatter:
        return (N_DEV, a.shape[0], a.shape[1] // N_DEV) if cols else a.shape
    return (a.shape[0], N_DEV * a.shape[1]) if cols else (N_DEV,) + a.shape


def _exchange_start(srcs, scatter, cols, name, after):
    n = len(srcs)
    land_shapes = [_land_shape(a, scatter, cl) for a, cl in zip(srcs, cols)]

    def body(*refs):
        src, land = refs[:n], refs[n:2 * n]
        send_sems = refs[2 * n + 1:2 * n + 1 + N_PEERS]
        recv_sems = refs[2 * n + 1 + N_PEERS:2 * n + 1 + 2 * N_PEERS]
        own_sem = refs[2 * n + 1 + 2 * N_PEERS]
        token = refs[-1]
        place = _mesh_place()
        for t in range(n):
            for k in range(1, N_DEV):
                _exchange_copy(src[t], land[t], send_sems, recv_sems, k, place, scatter, False, cols[t]).start()
            _own_copy(src[t], land[t], own_sem, place, scatter, cols[t]).start()
        token[...] = jnp.zeros_like(token)

    return pl.pallas_call(
        body,
        name=name,
        out_shape=(*[pltpu.SemaphoreType.DMA(())] * N_EXCHANGE_SEMS, *[pltpu.HBM(a.shape, a.dtype) for a in srcs],
                   *[pltpu.HBM(shp, a.dtype) for shp, a in zip(land_shapes, srcs)], _sds((SUBLANES, LANES), F32)),
        in_specs=[HBM] * (2 * n) + [ANY],
        out_specs=(*[SEM] * N_EXCHANGE_SEMS, *[HBM] * (2 * n), pl.BlockSpec(memory_space=pltpu.VMEM)),
        input_output_aliases={i: N_EXCHANGE_SEMS + i for i in range(2 * n)},
        compiler_params=pltpu.CompilerParams(has_side_effects=DATAFLOW_EFFECT),
    )(*[pltpu.with_memory_space_constraint(a, pltpu.HBM) for a in srcs],
      *[pltpu.with_memory_space_constraint(lax.empty(shp, a.dtype), pltpu.HBM) for shp, a in zip(land_shapes, srcs)],
      after)


def _exchange_wait(started, after, scatter, cols, name):
    sems = started[:N_EXCHANGE_SEMS]
    thru = started[N_EXCHANGE_SEMS:-1]
    n = len(thru) // 2

    def body(*refs):
        src, land = refs[:n], refs[n:2 * n]
        send_sems = refs[2 * n:2 * n + N_PEERS]
        recv_sems = refs[2 * n + N_PEERS:2 * n + 2 * N_PEERS]
        own_sem = refs[2 * n + 2 * N_PEERS]
        place = _mesh_place()
        for t in range(n):
            for k in range(1, N_DEV):
                cp = _exchange_copy(src[t], land[t], send_sems, recv_sems, k, place, scatter, True, cols[t])
                cp.wait_send()
                cp.wait_recv()
            _own_copy(src[t], land[t], own_sem, place, scatter, cols[t]).wait()

    out = pl.pallas_call(
        body,
        name=name,
        out_shape=tuple(pltpu.HBM(a.shape, a.dtype) for a in thru),
        in_specs=[HBM] * (2 * n) + [SEM] * N_EXCHANGE_SEMS + [ANY],
        out_specs=tuple([HBM] * (2 * n)),
        input_output_aliases={i: i for i in range(2 * n)},
        compiler_params=pltpu.CompilerParams(has_side_effects=DATAFLOW_EFFECT),
    )(*thru, *sems, after)
    return out[:n], out[n:]


def _adamw(w, g, m, v):
    m = ADAM_B1 * m + (1.0 - ADAM_B1) * g
    v = ADAM_B2 * v + (1.0 - ADAM_B2) * (g * g)
    m_hat = m / (1.0 - ADAM_B1 ** ADAM_STEP)
    v_hat = v / (1.0 - ADAM_B2 ** ADAM_STEP)
    delta = -ADAM_LR * (m_hat / (jnp.sqrt(v_hat) + ADAM_EPS) + ADAM_WD * w)
    return delta, m, v


def _adam_rows(r, c):
    fits = [t for t in range(BF16_SUBLANES, r + 1, BF16_SUBLANES) if r % t == 0 and t * c <= ADAM_BLOCK_ELEMS]
    return max(fits) if fits else r


def _adam_body(p_ref, w_ref, m_ref, v_ref, g_out, d_out, m_out, v_out):
    g = p_ref[0].astype(F32)
    for d in range(1, N_DEV):
        g = g + p_ref[d].astype(F32)
    delta, mn, vn = _adamw(w_ref[...], g, m_ref[...], v_ref[...])
    g_out[...] = g
    d_out[...] = delta
    m_out[...] = mn
    v_out[...] = vn


def _reduce_adam_layer(parts, w, m, v, prev, layer, name):
    nl, r, c = w.shape
    tr = _adam_rows(r, c)
    if prev is None:
        prev = [lax.empty((nl, r, c), F32) for _ in range(4)]

    def body(p_ref, w_ref, m_ref, v_ref, *rest):
        _adam_body(p_ref, w_ref, m_ref, v_ref, *rest[4:])

    blk = pl.BlockSpec((None, tr, c), lambda i: (layer, i, 0))
    out = _sds((nl, r, c), F32)
    return pl.pallas_call(
        body,
        name=name,
        grid=(r // tr,),
        in_specs=[pl.BlockSpec((N_DEV, tr, c), lambda i: (0, i, 0)), blk, blk, blk, ANY, ANY, ANY, ANY],
        out_specs=[blk, blk, blk, blk],
        out_shape=[out, out, out, out],
        input_output_aliases={4: 0, 5: 1, 6: 2, 7: 3},
        compiler_params=_params("parallel"),
    )(parts, w, m, v, *prev)


def _reduce_adam(parts, w, m, v, name):
    nl, _, r, c = parts.shape
    tr = _adam_rows(r, c)

    def body(*refs):
        _adam_body(*refs)

    blk = pl.BlockSpec((None, tr, c), lambda l, i: (l, i, 0))
    out = _sds((nl, r, c), F32)
    return pl.pallas_call(
        body,
        name=name,
        grid=(nl, r // tr),
        in_specs=[pl.BlockSpec((None, N_DEV, tr, c), lambda l, i: (l, 0, i, 0)), blk, blk, blk],
        out_specs=[blk, blk, blk, blk],
        out_shape=[out, out, out, out],
        compiler_params=_params("parallel", "parallel"),
    )(parts, w, m, v)


def _pack(arrays):
    flat = []
    for a in arrays:
        a = a.reshape(-1).astype(F32)
        flat.append(jnp.pad(a, (0, (-a.shape[0]) % PACK_UNIT)))
    out = jnp.concatenate(flat)
    rows = out.shape[0] // LANES
    pad_rows = (-rows) % 512
    return jnp.pad(out, (0, pad_rows * LANES)).reshape(rows + pad_rows, LANES)


def _unpack(packed, shapes):
    flat = packed.reshape(-1)
    out, off = [], 0
    for shp in shapes:
        size = math.prod(shp)
        out.append(flat[off:off + size].reshape(shp))
        off += size + (-size) % PACK_UNIT
    return out


def _in_runs(d, qkv_w, sw, gb):
    g0 = qkv_w + 2 * sw
    runs = [(0, qkv_w, "qkv", 0), (qkv_w, 2 * sw, "z", 0)]
    for j in range(d // gb):
        runs.append((g0 + j * gb, gb, "g", 2 * j * gb))
        runs.append((g0 + d + j * gb, gb, "g", (2 * j + 1) * gb))
    return runs


def _pieces_from_global(take, runs):
    out = {}
    for piece in ("qkv", "z", "g"):
        own = sorted((r for r in runs if r[2] == piece), key=lambda r: r[3])
        out[piece] = jnp.concatenate([take(g, g + w) for g, w, _, _ in own], axis=-1)
    return out


def _global_from_pieces(pieces, runs, a, b):
    segs = []
    for g, w, piece, start in sorted(runs):
        lo, hi = max(a, g), min(b, g + w)
        if lo < hi:
            segs.append(pieces[piece][..., start + lo - g:start + hi - g])
    return jnp.concatenate(segs, axis=-1)


def _take_from_shards(land):
    c = land.shape[2]

    def take(a, b):
        parts = []
        while a < b:
            dev = a // c
            lo, hi = a - dev * c, min(b - dev * c, c)
            parts.append(land[dev][:, lo:hi])
            a = dev * c + hi
        return jnp.concatenate(parts, axis=-1)

    return take


def _to_full_cols(g):
    d, k, n = g.shape
    return jnp.transpose(g, (1, 0, 2)).reshape(k, d * n)


def _to_col_shards(a):
    k, n = a.shape
    return jnp.transpose(a.reshape(k, N_DEV, n // N_DEV), (1, 0, 2))


def kernel(x, positions, norm1_g, w_in, b_in, sinks, sgu_ln_g, sgu_ln_b, sgu_w, sgu_b, w_attn_branch, w_sgu_branch, w_out, norm2_g, w_gate_up, w_down, final_g, loss_target, m_norm1_g, m_w_in, m_b_in, m_sinks, m_sgu_ln_g, m_sgu_ln_b, m_sgu_w, m_sgu_b, m_w_attn_branch, m_w_sgu_branch, m_w_out, m_norm2_g, m_w_gate_up, m_w_down, m_final_g, v_norm1_g, v_w_in, v_b_in, v_sinks, v_sgu_ln_g, v_sgu_ln_b, v_sgu_w, v_sgu_b, v_w_attn_branch, v_w_sgu_branch, v_w_out, v_norm2_g, v_w_gate_up, v_w_down, v_final_g):
    nl = w_in.shape[0]
    s, d = x.shape[1], x.shape[2]
    aw = w_attn_branch.shape[1]
    sw = w_sgu_branch.shape[1]
    in_w = w_in.shape[2] * N_DEV
    kw = (in_w - aw - 2 * sw - 2 * d) // 2
    qkv_w = aw + 2 * kw
    groups = sw // GROUP_DIM
    ff = w_down.shape[1] * N_DEV

    h = x.reshape(s, d)
    target = loss_target.reshape(s, d)
    tabs = _rope_tables(positions.reshape(s, 1), "rope_tables")

    big = [w_in, w_attn_branch, w_sgu_branch, w_out, w_gate_up, w_down]
    big_m = [m_w_in, m_w_attn_branch, m_w_sgu_branch, m_w_out, m_w_gate_up, m_w_down]
    big_v = [v_w_in, v_w_attn_branch, v_w_sgu_branch, v_w_out, v_w_gate_up, v_w_down]
    big_names = ("w_in", "w_attn_branch", "w_sgu_branch", "w_out", "w_gate_up", "w_down")
    W_IN, W_AB, W_SB, W_OUT, W_GU, W_DOWN = range(6)
    weight_groups = ((W_IN,), (W_AB, W_SB, W_OUT), (W_GU, W_DOWN))
    grad_groups = ((W_DOWN, W_GU), (W_OUT, W_AB, W_SB), (W_IN,))

    col_sharded = (W_IN, W_AB, W_SB, W_GU)
    by_cols = [BY_COLS if t in col_sharded and big[t].shape[2] % LANES == 0 else BY_SLOTS for t in range(6)]
    assert by_cols[W_GU] == BY_COLS, "the fused swiglu kernels need gate/up column blocks of whole lane tiles"
    by_cols[W_GU] = BY_PAIRED_COLS

    def start_gather(l, group, after):
        return _exchange_start([big[t][l].astype(BF16) for t in group], False, tuple(by_cols[t] for t in group),
                               f"gather_start_l{l}_{big_names[group[0]]}", after)

    def full_weight(t, land):
        if by_cols[t]:
            return land
        if t == W_IN:
            return _pieces_from_global(_take_from_shards(land), in_runs)
        return _to_full_cols(land) if t in col_sharded else land.reshape(N_DEV * land.shape[1], land.shape[2])

    gate_block = _pick(d, GATE_BLOCK_PREFS)
    in_runs = _in_runs(d, qkv_w, sw, gate_block)

    saved = []
    started = {}
    token = h
    for l in range(nl):
        for ll in ((0, 1) if l == 0 else (l + 1,)):
            if ll < nl:
                for group in weight_groups:
                    started[(ll, group)] = start_gather(ll, group, token)
                    token = started[(ll, group)][-1]
        gathered = {}

        def weight(t, after, l=l, gathered=gathered):
            if t not in gathered:
                group = next(g for (ll, g) in started if ll == l and t in g)
                srcs, lands = _exchange_wait(started.pop((l, group)), after, False, tuple(by_cols[tt] for tt in group),
                                             f"gather_wait_l{l}_{big_names[group[0]]}")
                for tt, ld in zip(group, lands):
                    gathered[tt] = full_weight(tt, ld)
            return gathered[t]

        bias = b_in[l].reshape(1, in_w)
        g1, g2 = norm1_g[l].reshape(1, d), norm2_g[l].reshape(1, d)
        lg, lb = sgu_ln_g[l].reshape(1, sw), sgu_ln_b[l].reshape(1, sw)
        bt = sgu_b[l].T

        xn = _rms_fwd(h, g1, "rms1_fwd", after=(token,))
        wts = dict(weight(W_IN, xn))
        biases = _pieces_from_global(lambda a, b: bias[:, a:b], in_runs)
        qkv = _matmul(xn, wts["qkv"], "nn", BF16, "proj_qkv", bias=biases["qkv"])
        pz = _matmul(xn, wts["z"], "nn", BF16, "proj_z", bias=biases["z"])
        pg = _matmul(xn, wts["g"], "nn", BF16, "proj_g", bias=biases["g"])
        y_attn = _attn_fwd(qkv, tabs, sinks[l], aw, kw, "attn_fwd")
        y_sgu = _sgu_fwd(pz, lg, lb, sgu_w[l], bt, "sgu_fwd")
        wts.update(ab=weight(W_AB, y_sgu), sb=weight(W_SB, y_sgu), out=weight(W_OUT, y_sgu))
        a_br, s_br, merged = _branches_merge(y_attn, y_sgu, wts["ab"], wts["sb"], pg, gate_block, "branches_merge")
        h_mid, hn = _out_proj_rms(merged, wts["out"], h, g2, "out_proj")
        wts.update(gu=weight(W_GU, hn), down=weight(W_DOWN, hn))
        gu, act = _gate_up_swiglu(hn, wts["gu"], "gate_up")
        h_out = _matmul(act, wts["down"], "nn", F32, "down_proj", res=h_mid)
        saved.append(dict(wts=wts, h=h, xn=xn, qkv=qkv, pz=pz, pg=pg, y_attn=y_attn, y_sgu=y_sgu, a_br=a_br,
                          s_br=s_br, merged=merged, h_mid=h_mid, hn=hn, gu=gu, act=act,
                          g1=g1, g2=g2, lg=lg, lb=lb, bt=bt))
        h = h_out

    dh, dhb, d_final_g, loss_blk = _loss_head(h, final_g.reshape(1, d), target, "loss_head")

    small = {n: [None] * nl for n in ("norm1_g", "b_in", "sinks", "sgu_ln_g", "sgu_ln_b", "sgu_w", "sgu_b", "norm2_g")}
    scattering = {}

    def start_scatter(l, group, grads, after):
        sends = []
        for t, dw in zip(group, grads):
            if by_cols[t]:
                sends.append(dw)
            elif t == W_IN:
                c = big[t].shape[2]
                sends.append(jnp.stack([_global_from_pieces(dw, in_runs, dev * c, (dev + 1) * c) for dev in range(N_DEV)]))
            elif t in col_sharded:
                sends.append(_to_col_shards(dw))
            else:
                sends.append(dw.reshape(N_DEV, dw.shape[0] // N_DEV, dw.shape[1]))
        scattering[(l, group)] = _exchange_start(sends, True, tuple(by_cols[t] for t in group),
                                                 f"scatter_start_l{l}_{big_names[group[0]]}", after)
        return scattering[(l, group)][-1]

    for l in reversed(range(nl)):
        sv = saved[l]
        wts = sv["wts"]
        d_gu = _d_act_swiglu(dhb, wts["down"], sv["gu"], "d_act")
        dw_down = _matmul(sv["act"], dhb, "tn", BF16, "dw_down")
        dw_gu = _matmul(sv["hn"], d_gu, "tn", BF16, "dw_gate_up")
        token = start_scatter(l, grad_groups[0], [dw_down, dw_gu], token)
        d_hn = _matmul(d_gu, wts["gu"], "nt", BF16, "d_hn", after=token)
        dh_mid, dmb, dg2 = _rms_bwd(sv["h_mid"], sv["g2"], d_hn, dh, "rms2_bwd")
        d_a, d_s, d_pg, db_g = _d_merged_merge_bwd(dmb, wts["out"], sv["pg"], sv["a_br"], sv["s_br"], gate_block, "d_merged")
        dw_out = _matmul(sv["merged"], dmb, "tn", BF16, "dw_out")
        d_y_attn = _matmul(d_a, wts["ab"], "nt", BF16, "d_y_attn")
        dw_ab = _matmul(sv["y_attn"], d_a, "tn", BF16, "dw_attn_branch")
        d_y_sgu = _matmul(d_s, wts["sb"], "nt", BF16, "d_y_sgu")
        dw_sb = _matmul(sv["y_sgu"], d_s, "tn", BF16, "dw_sgu_branch")
        token = start_scatter(l, grad_groups[1], [dw_out, dw_ab, dw_sb], token)
        d_pz, d_lg, d_lb, d_sw, d_sbt, db_z = _sgu_bwd(sv["pz"], d_y_sgu, sv["lg"], sv["lb"], sgu_w[l],
                                                 jnp.transpose(sgu_w[l], (0, 2, 1)), sv["bt"], "sgu_bwd", after=(token,))
        d_q, d_kv, d_sinks = _attn_bwd(sv["qkv"], tabs, sinks[l], sv["y_attn"], d_y_attn, aw, kw, "attn_bwd")
        d_qkv = jnp.concatenate([d_q, d_kv], axis=1)
        dw_qkv = _matmul(sv["xn"], d_qkv, "tn", BF16, "dw_qkv")
        dw_z = _matmul(sv["xn"], d_pz, "tn", BF16, "dw_z")
        dw_g = _matmul(sv["xn"], d_pg, "tn", BF16, "dw_g")
        token = start_scatter(l, grad_groups[2], [dict(qkv=dw_qkv, z=dw_z, g=dw_g)], token)
        d_xn = _matmul_nt_sum([(d_qkv, wts["qkv"]), (d_pz, wts["z"]), (d_pg, wts["g"])], BF16, "d_xn", token)
        dh, dhb, dg1 = _rms_bwd(sv["h"], sv["g1"], d_xn, dh_mid, "rms1_bwd")

        small["norm1_g"][l], small["norm2_g"][l] = dg1, dg2
        small["b_in"][l] = _global_from_pieces(dict(qkv=_colsum(d_qkv, "db_qkv"), z=db_z, g=db_g), in_runs, 0, in_w)
        small["sinks"][l] = d_sinks[0, :aw // HEAD_DIM]
        small["sgu_ln_g"][l], small["sgu_ln_b"][l] = d_lg, d_lb
        small["sgu_w"][l] = d_sw
        small["sgu_b"][l] = d_sbt.T

    grad_x = dh.reshape(x.shape)

    names = ["norm1_g", "b_in", "sinks", "sgu_ln_g", "sgu_ln_b", "sgu_w", "sgu_b", "norm2_g"]
    small_w = [norm1_g, b_in, sinks, sgu_ln_g, sgu_ln_b, sgu_w, sgu_b, norm2_g, final_g]
    small_m = [m_norm1_g, m_b_in, m_sinks, m_sgu_ln_g, m_sgu_ln_b, m_sgu_w, m_sgu_b, m_norm2_g, m_final_g]
    small_v = [v_norm1_g, v_b_in, v_sinks, v_sgu_ln_g, v_sgu_ln_b, v_sgu_w, v_sgu_b, v_norm2_g, v_final_g]
    shapes = [w.shape for w in small_w] + [(1,)]
    partial = [jnp.stack([p.reshape(w.shape[1:]) for p in small[n]]) for n, w in zip(names, small_w)]
    partial += [d_final_g.reshape(final_g.shape), loss_blk[0, :1]]
    zero = jnp.zeros((1,), F32)
    small_started = _exchange_start([_pack(partial)], False, (False,), "gather_start_small_grads", dhb)

    big_out = [None] * len(big)
    after = small_started[-1]
    for l in reversed(range(nl)):
        for group in grad_groups:
            srcs, lands = _exchange_wait(scattering.pop((l, group)), after, True, tuple(by_cols[t] for t in group),
                                         f"scatter_wait_l{l}_{big_names[group[0]]}")
            for t, parts in zip(group, lands):
                big_out[t] = _reduce_adam_layer(parts, big[t], big_m[t], big_v[t], big_out[t], l, f"adam_{big_names[t]}")
                after = big_out[t][0]

    srcs, lands = _exchange_wait(small_started, after, False, (False,), "gather_wait_small_grads")
    sm = _reduce_adam(lands[0][None], _pack(small_w + [zero])[None], _pack(small_m + [zero])[None],
                      _pack(small_v + [zero])[None], "adam_small")
    sm_g, sm_d, sm_m, sm_v = [_unpack(a[0], shapes) for a in sm]
    loss = sm_g[-1].reshape(())

    def ordered(kind_small, kind_big):
        by_name = dict(zip(["norm1_g", "b_in", "sinks", "sgu_ln_g", "sgu_ln_b", "sgu_w", "sgu_b", "norm2_g", "final_g"], kind_small))
        by_name.update(zip(["w_in", "w_attn_branch", "w_sgu_branch", "w_out", "w_gate_up", "w_down"], kind_big))
        order = ["norm1_g", "w_in", "b_in", "sinks", "sgu_ln_g", "sgu_ln_b", "sgu_w", "sgu_b", "w_attn_branch",
                 "w_sgu_branch", "w_out", "norm2_g", "w_gate_up", "w_down", "final_g"]
        return [by_name[n] for n in order]

    outs = [loss, grad_x]
    for idx, sm_kind in enumerate((sm_g, sm_d, sm_m, sm_v)):
        outs += ordered(sm_kind[:-1], [o[idx] for o in big_out])
    return tuple(outs)
```

```python
import math

import jax
import jax.numpy as jnp
from jax import lax
from jax.experimental import pallas as pl
from jax.experimental.pallas import tpu as pltpu

F32 = jnp.float32
BF16 = jnp.bfloat16

N_DEV = 8
HEAD_DIM = 64
WINDOW = 128
CHUNK = 128
GROUP_DIM = 128
ROPE_DIM = HEAD_DIM // 4
ROPE_HALF = ROPE_DIM // 2
ROPE_THETA = 500000.0
EPS = 1e-5
NEG = -1e30
ATTN_SCALE = HEAD_DIM ** -0.5
ADAM_LR = 0.001
ADAM_B1 = 0.9
ADAM_B2 = 0.999
ADAM_EPS = 1e-08
ADAM_WD = 0.01
ADAM_STEP = 10
LANES = 128
SUBLANES = 8
BF16_SUBLANES = 16
PACK_UNIT = SUBLANES * LANES
ADAM_BLOCK_ELEMS = 256 * 1024
V7X_VMEM_LIMIT_BYTES = 56 * 1024 * 1024
MATMUL_TILE_PREFS = (1024, 1408, 768, 512, 384, 256, 128)
MATMUL_WHOLE_K = 2048
MATMUL_TN_K = 4096
MATMUL_VMEM_BUDGET_BYTES = 52 * 1024 * 1024
MATMUL_K_PREFS = (2816, 2048, 1536, 1408, 1024, 768, 512, 384, 256, 128)
ROW_TILE_PREFS = (512, 256, 128)
SWIGLU_ROW_PREFS = (512, 256, 128)
D_ACT_ROW_PREFS = (1024, 512, 256, 128)
OUT_PROJ_ROW_PREFS = (512, 256, 128)
MERGE_ROW_PREFS = (1024, 512, 256, 128)
GATE_BLOCK_PREFS = (1024, 512, 256, 128)
FUSED_ROW_CHUNK = 256
MESH_TYPE = pl.DeviceIdType.MESH
ANY = pl.BlockSpec(memory_space=pl.ANY)
HBM = pl.BlockSpec(memory_space=pltpu.HBM)
SEM = pl.BlockSpec(memory_space=pltpu.SEMAPHORE)
DATAFLOW_EFFECT = pltpu.SideEffectType.DATAFLOW_SIDE_EFFECTING


def _pick(n, prefs):
    for p in prefs:
        if n % p == 0:
            return p
    return n


def _params(*sem):
    return pltpu.CompilerParams(dimension_semantics=sem, vmem_limit_bytes=V7X_VMEM_LIMIT_BYTES)


_DIMS = {"nn": (((1,), (0,)), ((), ())), "nt": (((1,), (1,)), ((), ())), "tn": (((0,), (0,)), ((), ()))}


def _matmul(a, b, mode, out_dtype, name, bias=None, res=None, after=None):
    if mode == "nn":
        (m, k), n = a.shape, b.shape[1]
    elif mode == "nt":
        (m, k), n = a.shape, b.shape[0]
    else:
        (k, m), n = a.shape, b.shape[1]
    tm, tn = _pick(m, MATMUL_TILE_PREFS), _pick(n, MATMUL_TILE_PREFS)
    if k <= MATMUL_WHOLE_K:
        tk = k
    else:
        fits = [t for t in ((MATMUL_TN_K,) if mode == "tn" else ()) + MATMUL_K_PREFS
                if k % t == 0 and 4 * t * (tm + tn) + 16 * tm * tn <= MATMUL_VMEM_BUDGET_BYTES]
        tk = fits[0]
    nk = k // tk
    dims = _DIMS[mode]
    a_spec = pl.BlockSpec((tk, tm), lambda i, j, kk: (kk, i)) if mode == "tn" else pl.BlockSpec((tm, tk), lambda i, j, kk: (i, kk))
    b_spec = pl.BlockSpec((tn, tk), lambda i, j, kk: (j, kk)) if mode == "nt" else pl.BlockSpec((tk, tn), lambda i, j, kk: (kk, j))
    in_specs, args = [a_spec, b_spec], [a, b]
    if bias is not None:
        in_specs.append(pl.BlockSpec((1, tn), lambda i, j, kk: (0, j)))
        args.append(bias)
    if res is not None:
        in_specs.append(pl.BlockSpec((tm, tn), lambda i, j, kk: (i, j)))
        args.append(res)
    if after is not None:
        in_specs.append(ANY)
        args.append(after)

    def body(*refs):
        a_ref, b_ref = refs[0], refs[1]
        pos = 2
        bias_ref = res_ref = None
        if bias is not None:
            bias_ref = refs[pos]
            pos += 1
        if res is not None:
            res_ref = refs[pos]
            pos += 1
        if after is not None:
            pos += 1
        o_ref = refs[pos]

        def finish(r):
            if bias_ref is not None:
                r = r + bias_ref[...]
            if res_ref is not None:
                r = r + res_ref[...]
            o_ref[...] = r.astype(out_dtype)

        part = lax.dot_general(a_ref[...], b_ref[...], dims, preferred_element_type=F32)
        if nk == 1:
            finish(part)
        else:
            acc_ref = refs[pos + 1]
            kk = pl.program_id(2)

            @pl.when(kk == 0)
            def _():
                acc_ref[...] = part

            @pl.when((kk > 0) & (kk < nk - 1))
            def _():
                acc_ref[...] += part

            @pl.when(kk == nk - 1)
            def _():
                finish(acc_ref[...] + part)

    return pl.pallas_call(
        body,
        name=name,
        grid=(m // tm, n // tn, nk),
        in_specs=in_specs,
        out_specs=pl.BlockSpec((tm, tn), lambda i, j, kk: (i, j)),
        out_shape=jax.ShapeDtypeStruct((m, n), out_dtype),
        scratch_shapes=[] if nk == 1 else [pltpu.VMEM((tm, tn), F32)],
        compiler_params=_params("parallel", "parallel", "arbitrary"),
    )(*args)


def _out_proj_rms(merged, w_out, h, g, name):
    s, d = h.shape
    tm = _pick(s, OUT_PROJ_ROW_PREFS)

    def body(a_ref, b_ref, h_ref, g_ref, o_ref, n_ref):
        for rows in _row_chunks(tm):
            r = h_ref[rows, :] + jnp.dot(a_ref[rows, :], b_ref[...], preferred_element_type=F32)
            o_ref[rows, :] = r
            rs = lax.rsqrt(jnp.mean(r * r, axis=-1, keepdims=True) + EPS)
            n_ref[rows, :] = (r * rs * g_ref[...]).astype(BF16)

    row = pl.BlockSpec((tm, d), lambda i: (i, 0))
    return pl.pallas_call(
        body,
        name=name,
        grid=(s // tm,),
        in_specs=[row, pl.BlockSpec((d, d), lambda i: (0, 0)), row, pl.BlockSpec((1, d), lambda i: (0, 0))],
        out_specs=[row, row],
        out_shape=[_sds((s, d), F32), _sds((s, d), BF16)],
        compiler_params=_params("parallel"),
    )(merged, w_out, h, g)


def _rowwise(body, name, rows, tr, ins, consts, outs, accs=(), after=()):
    n_in, n_c, n_o, n_a = len(ins), len(consts), len(outs), len(after)

    def wrapped(*refs):
        body(pl.program_id(0), refs[:n_in], refs[n_in:n_in + n_c], refs[n_in + n_c + n_a:n_in + n_c + n_a + n_o],
             refs[n_in + n_c + n_a + n_o:])

    def whole(shape):
        zeros = (0,) * len(shape)
        return pl.BlockSpec(tuple(shape), lambda i: zeros)

    in_specs = ([pl.BlockSpec((tr, a.shape[1]), lambda i: (i, 0)) for a in ins] + [whole(c.shape) for c in consts]
                + [ANY] * n_a)
    out_specs = [pl.BlockSpec((tr, o.shape[1]), lambda i: (i, 0)) for o in outs] + [whole(a.shape) for a in accs]
    return pl.pallas_call(
        wrapped,
        name=name,
        grid=(rows // tr,),
        in_specs=in_specs,
        out_specs=out_specs,
        out_shape=list(outs) + list(accs),
        compiler_params=_params("arbitrary" if accs else "parallel"),
    )(*ins, *consts, *after)


def _sds(shape, dtype):
    return jax.ShapeDtypeStruct(tuple(shape), dtype)


def _rms_fwd(h, g, name, after=()):
    s, d = h.shape
    tr = _pick(s, ROW_TILE_PREFS)

    def body(i, ins, consts, outs, accs):
        x = ins[0][...]
        r = lax.rsqrt(jnp.mean(x * x, axis=-1, keepdims=True) + EPS)
        outs[0][...] = (x * r * consts[0][...]).astype(BF16)

    return _rowwise(body, name, s, tr, [h], [g], [_sds((s, d), BF16)], after=after)[0]


def _rms_bwd(h, g, dy, dh_up, name):
    s, d = h.shape
    tr = _pick(s, ROW_TILE_PREFS)

    def body(i, ins, consts, outs, accs):
        x, dyv, up = ins[0][...], ins[1][...].astype(F32), ins[2][...]
        r = lax.rsqrt(jnp.mean(x * x, axis=-1, keepdims=True) + EPS)
        xr = x * r
        gy = dyv * consts[0][...]
        dx = r * (gy - xr * jnp.mean(gy * xr, axis=-1, keepdims=True))
        outs[0][...] = up + dx
        outs[1][...] = (up + dx).astype(BF16)

        @pl.when(i == 0)
        def _():
            accs[0][...] = jnp.zeros_like(accs[0])

        accs[0][...] += jnp.sum(dyv * xr, axis=0, keepdims=True)

    return _rowwise(body, name, s, tr, [h, dy, dh_up], [g], [_sds((s, d), F32), _sds((s, d), BF16)], [_sds((1, d), F32)])


def _loss_head(h, g, target, name):
    s, d = h.shape
    tr = _pick(s, ROW_TILE_PREFS)

    def body(i, ins, consts, outs, accs):
        x, t = ins[0][...], ins[1][...]
        gv = consts[0][...]
        r = lax.rsqrt(jnp.mean(x * x, axis=-1, keepdims=True) + EPS)
        xr = x * r
        diff = xr * gv - t
        dyv = diff * (1.0 / d)
        gy = dyv * gv
        dx = r * (gy - xr * jnp.mean(gy * xr, axis=-1, keepdims=True))
        outs[0][...] = dx
        outs[1][...] = dx.astype(BF16)

        @pl.when(i == 0)
        def _():
            accs[0][...] = jnp.zeros_like(accs[0])
            accs[1][...] = jnp.zeros_like(accs[1])

        accs[0][...] += jnp.sum(dyv * xr, axis=0, keepdims=True)
        part = 0.5 * jnp.sum(jnp.mean(diff * diff, axis=-1, keepdims=True), axis=0, keepdims=True)
        accs[1][...] += jnp.broadcast_to(part, accs[1].shape)

    return _rowwise(body, name, s, tr, [h, target], [g], [_sds((s, d), F32), _sds((s, d), BF16)],
                    [_sds((1, d), F32), _sds((SUBLANES, LANES), F32)])


def _colsum(a, name):
    s, w = a.shape
    tr = _pick(s, ROW_TILE_PREFS)

    def body(i, ins, consts, outs, accs):
        @pl.when(i == 0)
        def _():
            accs[0][...] = jnp.zeros_like(accs[0])

        accs[0][...] += jnp.sum(ins[0][...].astype(F32), axis=0, keepdims=True)

    return _rowwise(body, name, s, tr, [a], [], [], [_sds((1, w), F32)])[0]


def _sigmoid(x):
    return 1.0 / (1.0 + jnp.exp(-x))


def _branches_merge(y_attn, y_sgu, w_ab, w_sb, pg, gb, name):
    s, aw = y_attn.shape
    sw, d = w_sb.shape
    tm = _pick(s, MERGE_ROW_PREFS)

    def body(ya_ref, ys_ref, wa_ref, ws_ref, pg_ref, a_out, s_out, m_out):
        for rows in _row_chunks(tm):
            a = jnp.dot(ya_ref[rows, :], wa_ref[...], preferred_element_type=F32)
            b = jnp.dot(ys_ref[rows, :], ws_ref[...], preferred_element_type=F32)
            ga, gs = _sigmoid(pg_ref[rows, :gb].astype(F32)), _sigmoid(pg_ref[rows, gb:].astype(F32))
            a_out[rows, :] = a.astype(BF16)
            s_out[rows, :] = b.astype(BF16)
            m_out[rows, :] = (ga * a + gs * b).astype(BF16)

    blk = pl.BlockSpec((tm, gb), lambda i, j: (i, j))
    out = _sds((s, d), BF16)
    return pl.pallas_call(
        body,
        name=name,
        grid=(s // tm, d // gb),
        in_specs=[pl.BlockSpec((tm, aw), lambda i, j: (i, 0)), pl.BlockSpec((tm, sw), lambda i, j: (i, 0)),
                  pl.BlockSpec((aw, gb), lambda i, j: (0, j)), pl.BlockSpec((sw, gb), lambda i, j: (0, j)),
                  pl.BlockSpec((tm, 2 * gb), lambda i, j: (i, j))],
        out_specs=[blk, blk, blk],
        out_shape=[out, out, out],
        compiler_params=_params("parallel", "parallel"),
    )(y_attn, y_sgu, w_ab, w_sb, pg)


def _d_merged_merge_bwd(dmb, w_out, pg, a_br, s_br, gb, name):
    s, d = dmb.shape
    tm = _pick(s, MERGE_ROW_PREFS)

    def body(a_ref, b_ref, pg_ref, ab_ref, sb_ref, da_out, ds_out, dpg_out, db_out):
        @pl.when(pl.program_id(1) == 0)
        def _():
            db_out[...] = jnp.zeros_like(db_out)

        for rows in _row_chunks(tm):
            dm = lax.dot_general(a_ref[rows, :], b_ref[...], _DIMS["nt"], preferred_element_type=F32)
            ga, gs = _sigmoid(pg_ref[rows, :gb].astype(F32)), _sigmoid(pg_ref[rows, gb:].astype(F32))
            da_out[rows, :] = (dm * ga).astype(BF16)
            ds_out[rows, :] = (dm * gs).astype(BF16)
            dpa = dm * ab_ref[rows, :].astype(F32) * ga * (1.0 - ga)
            dps = dm * sb_ref[rows, :].astype(F32) * gs * (1.0 - gs)
            dpg_out[rows, :gb] = dpa.astype(BF16)
            dpg_out[rows, gb:] = dps.astype(BF16)
            db_out[:, :gb] += jnp.sum(dpa, axis=0, keepdims=True)
            db_out[:, gb:] += jnp.sum(dps, axis=0, keepdims=True)

    blk = pl.BlockSpec((tm, gb), lambda j, i: (i, j))
    pair = pl.BlockSpec((tm, 2 * gb), lambda j, i: (i, j))
    return pl.pallas_call(
        body,
        name=name,
        grid=(d // gb, s // tm),
        in_specs=[pl.BlockSpec((tm, d), lambda j, i: (i, 0)), pl.BlockSpec((gb, d), lambda j, i: (j, 0)), pair, blk, blk],
        out_specs=[blk, blk, pair, pl.BlockSpec((1, 2 * gb), lambda j, i: (0, j))],
        out_shape=[_sds((s, d), BF16), _sds((s, d), BF16), _sds((s, 2 * d), BF16), _sds((1, 2 * d), F32)],
        compiler_params=_params("parallel", "arbitrary"),
    )(dmb, w_out, pg, a_br, s_br)


def _row_chunks(tm):
    rc = _pick(tm, (FUSED_ROW_CHUNK,))
    return [slice(r, r + rc) for r in range(0, tm, rc)]


def _gate_up_swiglu(hn, w_gu, name):
    s, d = hn.shape
    n2 = w_gu.shape[1]
    fb = n2 // N_DEV
    tm = _pick(s, SWIGLU_ROW_PREFS)

    def body(a_ref, b_ref, gu_ref, act_ref):
        r = jnp.dot(a_ref[...], b_ref[...], preferred_element_type=F32)
        gu_ref[...] = r.astype(BF16)
        gate, up = r[:, :fb], r[:, fb:]
        act_ref[...] = (gate * _sigmoid(gate) * up).astype(BF16)

    return pl.pallas_call(
        body,
        name=name,
        grid=(s // tm, N_DEV // 2),
        in_specs=[pl.BlockSpec((tm, d), lambda i, j: (i, 0)), pl.BlockSpec((d, 2 * fb), lambda i, j: (0, j))],
        out_specs=[pl.BlockSpec((tm, 2 * fb), lambda i, j: (i, j)), pl.BlockSpec((tm, fb), lambda i, j: (i, j))],
        out_shape=[_sds((s, n2), BF16), _sds((s, n2 // 2), BF16)],
        compiler_params=_params("parallel", "parallel"),
    )(hn, w_gu)


def _d_act_swiglu(dhb, w_down, gu, name):
    s, d = dhb.shape
    n2 = gu.shape[1]
    fb = n2 // N_DEV
    tm = _pick(s, D_ACT_ROW_PREFS)

    def body(a_ref, b_ref, gu_ref, o_ref):
        for rows in _row_chunks(tm):
            da = lax.dot_general(a_ref[rows, :], b_ref[...], _DIMS["nt"], preferred_element_type=F32)
            gate, up = gu_ref[rows, :fb].astype(F32), gu_ref[rows, fb:].astype(F32)
            sg = _sigmoid(gate)
            o_ref[rows, :fb] = (da * up * sg * (1.0 + gate * (1.0 - sg))).astype(BF16)
            o_ref[rows, fb:] = (da * gate * sg).astype(BF16)

    pair = pl.BlockSpec((tm, 2 * fb), lambda i, j: (i, j))
    return pl.pallas_call(
        body,
        name=name,
        grid=(s // tm, N_DEV // 2),
        in_specs=[pl.BlockSpec((tm, d), lambda i, j: (i, 0)), pl.BlockSpec((fb, d), lambda i, j: (j, 0)), pair],
        out_specs=pair,
        out_shape=_sds((s, n2), BF16),
        compiler_params=_params("parallel", "parallel"),
    )(dhb, w_down, gu)


def _rope_tables(pos_col, name):
    s = pos_col.shape[0]
    tr = _pick(s, (1024, 512, 256, 128))
    inv = ROPE_THETA ** (-jnp.arange(0, ROPE_DIM, 2, dtype=F32) / ROPE_DIM)
    lane = jnp.arange(LANES)
    inv_lanes = inv[lane % ROPE_HALF].reshape(1, LANES)

    def body(i, ins, consts, outs, accs):
        ang = ins[0][...].astype(F32) * consts[0][...]
        c, sn = jnp.cos(ang), jnp.sin(ang)
        in_head = lax.broadcasted_iota(jnp.int32, ang.shape, 1) % HEAD_DIM
        outs[0][:, 0:LANES] = jnp.where(in_head < ROPE_DIM, c, 1.0)
        outs[0][:, LANES:2 * LANES] = jnp.where(in_head < ROPE_HALF, -sn, 0.0)
        outs[0][:, 2 * LANES:] = jnp.where((in_head >= ROPE_HALF) & (in_head < ROPE_DIM), sn, 0.0)

    return _rowwise(body, name, s, tr, [pos_col], [inv_lanes], [_sds((s, 3 * LANES), F32)])[0]


def _rope(x, tab, inverse=False):
    width = x.shape[1]
    reps = width // LANES
    c = jnp.tile(tab[:, 0:LANES], (1, reps))
    lo = jnp.tile(tab[:, LANES:2 * LANES], (1, reps))
    hi = jnp.tile(tab[:, 2 * LANES:], (1, reps))
    if inverse:
        lo, hi = -lo, -hi
    return x * c + pltpu.roll(x, width - ROPE_HALF, 1) * lo + pltpu.roll(x, ROPE_HALF, 1) * hi


def _attn_specs(aw, kw):
    kb = aw // kw
    prev = lambda i: jnp.maximum(i - 1, 0)
    return [
        pl.BlockSpec(memory_space=pltpu.SMEM),
        pl.BlockSpec((WINDOW, aw), lambda i: (i, 0)),
        pl.BlockSpec((WINDOW, kw), lambda i: (i, kb)),
        pl.BlockSpec((WINDOW, kw), lambda i: (prev(i), kb)),
        pl.BlockSpec((WINDOW, kw), lambda i: (i, kb + 1)),
        pl.BlockSpec((WINDOW, kw), lambda i: (prev(i), kb + 1)),
        pl.BlockSpec((WINDOW, 3 * LANES), lambda i: (i, 0)),
        pl.BlockSpec((WINDOW, 3 * LANES), lambda i: (prev(i), 0)),
    ]


def _attn_common(i, q_ref, kc_ref, kp_ref, vc_ref, vp_ref, tq_ref, tp_ref):
    tq, tp = tq_ref[...], tp_ref[...]
    qt = (_rope(q_ref[...].astype(F32), tq) * ATTN_SCALE).astype(BF16).T
    kc = _rope(kc_ref[...].astype(F32), tq)
    kp = _rope(kp_ref[...].astype(F32), tp)
    k2 = jnp.concatenate([kp, kc], axis=0).astype(BF16)
    v2 = jnp.concatenate([vp_ref[...], vc_ref[...]], axis=0)
    kj = lax.broadcasted_iota(jnp.int32, (2 * WINDOW, WINDOW), 0)
    qi = lax.broadcasted_iota(jnp.int32, (2 * WINDOW, WINDOW), 1)
    rel = qi + WINDOW - kj
    ok = (rel >= 0) & (rel < WINDOW) & ((kj >= WINDOW) | (i > 0))
    return qt, k2, v2, ok, tq, tp


def _head_probs(qt_h, kg, ok, sink):
    s = jnp.dot(kg, qt_h, preferred_element_type=F32)
    s = jnp.where(ok, s, NEG)
    m = jnp.maximum(jnp.max(s, axis=0, keepdims=True), sink)
    p = jnp.exp(s - m)
    es = jnp.exp(sink - m)
    inv = 1.0 / (jnp.sum(p, axis=0, keepdims=True) + es)
    return p * inv, es * inv


def _attn_fwd(qkv, tabs, sinks, aw, kw, name):
    s = qkv.shape[0]
    nq, nkv = aw // HEAD_DIM, kw // HEAD_DIM
    qpk = nq // nkv

    def body(s_ref, q_ref, kc_ref, kp_ref, vc_ref, vp_ref, tq_ref, tp_ref, o_ref):
        i = pl.program_id(0)
        tq, tp = tq_ref[...], tp_ref[...]
        q = (_rope(q_ref[...].astype(F32), tq) * ATTN_SCALE).astype(BF16)
        k2 = jnp.concatenate([_rope(kp_ref[...].astype(F32), tp), _rope(kc_ref[...].astype(F32), tq)], axis=0).astype(BF16)
        v2 = jnp.concatenate([vp_ref[...], vc_ref[...]], axis=0)
        qi = lax.broadcasted_iota(jnp.int32, (WINDOW, 2 * WINDOW), 0)
        kj = lax.broadcasted_iota(jnp.int32, (WINDOW, 2 * WINDOW), 1)
        rel = qi + WINDOW - kj
        ok = (rel >= 0) & (rel < WINDOW) & ((kj >= WINDOW) | (i > 0))
        for h in range(nq):
            g = h // qpk
            hs, gs = slice(h * HEAD_DIM, (h + 1) * HEAD_DIM), slice(g * HEAD_DIM, (g + 1) * HEAD_DIM)
            sc = lax.dot_general(q[:, hs], k2[:, gs], _DIMS["nt"], preferred_element_type=F32)
            sc = jnp.where(ok, sc, NEG)
            m = jnp.maximum(jnp.max(sc, axis=1, keepdims=True), s_ref[h])
            p = jnp.exp(sc - m)
            inv = 1.0 / (jnp.sum(p, axis=1, keepdims=True) + jnp.exp(s_ref[h] - m))
            o = jnp.dot((p * inv).astype(BF16), v2[:, gs], preferred_element_type=F32)
            o_ref[:, hs] = o.astype(BF16)

    return pl.pallas_call(
        body,
        name=name,
        grid=(s // WINDOW,),
        in_specs=_attn_specs(aw, kw),
        out_specs=pl.BlockSpec((WINDOW, aw), lambda i: (i, 0)),
        out_shape=_sds((s, aw), BF16),
        compiler_params=_params("parallel"),
    )(sinks, qkv, qkv, qkv, qkv, qkv, tabs, tabs)


def _attn_bwd(qkv, tabs, sinks, o, do, aw, kw, name):
    s = qkv.shape[0]
    nb = s // WINDOW
    nq, nkv = aw // HEAD_DIM, kw // HEAD_DIM
    qpk = nq // nkv

    def body(s_ref, q_ref, kc_ref, kp_ref, vc_ref, vp_ref, tq_ref, tp_ref, o_ref, do_ref,
             dq_ref, dkv_ref, ds_ref, ck_ref, cv_ref):
        i = pl.program_id(0)

        @pl.when(i == 0)
        def _():
            ck_ref[...] = jnp.zeros_like(ck_ref)
            cv_ref[...] = jnp.zeros_like(cv_ref)
            ds_ref[...] = jnp.zeros_like(ds_ref)

        qt, k2, v2, ok, tq, tp = _attn_common(i, q_ref, kc_ref, kp_ref, vc_ref, vp_ref, tq_ref, tp_ref)
        dot_t, ot = do_ref[...].T, o_ref[...].T
        k2t = k2.T
        row0 = lax.broadcasted_iota(jnp.int32, (SUBLANES, LANES), 0) == 0
        lane = lax.broadcasted_iota(jnp.int32, (SUBLANES, LANES), 1)
        dsink = jnp.zeros((SUBLANES, LANES), F32)
        dqt_parts, dk_parts, dv_parts = [], [], []
        for g in range(nkv):
            gs = slice(g * HEAD_DIM, (g + 1) * HEAD_DIM)
            kg, vg = k2[:, gs], v2[:, gs]
            dk_g = jnp.zeros((2 * WINDOW, HEAD_DIM), F32)
            dv_g = jnp.zeros((2 * WINDOW, HEAD_DIM), F32)
            for j in range(qpk):
                h = g * qpk + j
                hs = slice(h * HEAD_DIM, (h + 1) * HEAD_DIM)
                pn, psink = _head_probs(qt[hs], kg, ok, s_ref[h])
                delta = jnp.sum(dot_t[hs].astype(F32) * ot[hs].astype(F32), axis=0, keepdims=True)
                dp = jnp.dot(vg, dot_t[hs], preferred_element_type=F32)
                dsb = (pn * (dp - delta)).astype(BF16)
                dsink = dsink + jnp.where(row0 & (lane == h), -jnp.sum(psink * delta, axis=1, keepdims=True), 0.0)
                dqt_parts.append(jnp.dot(k2t[gs], dsb, preferred_element_type=F32))
                dk_g = dk_g + lax.dot_general(dsb, qt[hs], _DIMS["nt"], preferred_element_type=F32)
                dv_g = dv_g + lax.dot_general(pn.astype(BF16), dot_t[hs], _DIMS["nt"], preferred_element_type=F32)
            dk_parts.append(dk_g)
            dv_parts.append(dv_g)
        ds_ref[...] += dsink
        dq_ref[...] = _rope(jnp.concatenate(dqt_parts, axis=0).T * ATTN_SCALE, tq, inverse=True).astype(BF16)
        dk2 = jnp.concatenate(dk_parts, axis=1)
        dv2 = jnp.concatenate(dv_parts, axis=1)
        dk_prev = _rope(ck_ref[...] + dk2[:WINDOW], tp, inverse=True)
        dv_prev = cv_ref[...] + dv2[:WINDOW]

        @pl.when(i > 0)
        def _():
            dkv_ref[pl.ds(pl.multiple_of((i - 1) * WINDOW, WINDOW), WINDOW), :] = jnp.concatenate(
                [dk_prev, dv_prev], axis=1).astype(BF16)

        ck_ref[...] = dk2[WINDOW:]
        cv_ref[...] = dv2[WINDOW:]

        @pl.when(i == nb - 1)
        def _():
            dkv_ref[pl.ds(pl.multiple_of(i * WINDOW, WINDOW), WINDOW), :] = jnp.concatenate(
                [_rope(dk2[WINDOW:], tq, inverse=True), dv2[WINDOW:]], axis=1).astype(BF16)

    blk = pl.BlockSpec((WINDOW, aw), lambda i: (i, 0))
    return pl.pallas_call(
        body,
        name=name,
        grid=(nb,),
        in_specs=_attn_specs(aw, kw) + [blk, blk],
        out_specs=[blk, pl.BlockSpec((s, 2 * kw), lambda i: (0, 0)), pl.BlockSpec((SUBLANES, LANES), lambda i: (0, 0))],
        out_shape=[_sds((s, aw), BF16), _sds((s, 2 * kw), BF16), _sds((SUBLANES, LANES), F32)],
        scratch_shapes=[pltpu.VMEM((WINDOW, kw), F32), pltpu.VMEM((WINDOW, kw), F32)],
        compiler_params=_params("arbitrary"),
    )(sinks, qkv, qkv, qkv, qkv, qkv, tabs, tabs, o, do)


_INV_SQRT2 = 1.0 / math.sqrt(2.0)
_INV_SQRT2PI = 1.0 / math.sqrt(2.0 * math.pi)


def _gelu(x):
    return x * (lax.erf(x * _INV_SQRT2) + 1.0) * 0.5


def _gelu_grad(x):
    return 0.5 * (lax.erf(x * _INV_SQRT2) + 1.0) + x * jnp.exp(-0.5 * x * x) * _INV_SQRT2PI


def _sgu_norm(pv, lg, lb):
    zv = _gelu(pv)
    mu = jnp.mean(zv, axis=-1, keepdims=True)
    cen = zv - mu
    rs = lax.rsqrt(jnp.mean(cen * cen, axis=-1, keepdims=True) + EPS)
    xhat = cen * rs
    return xhat, rs, (xhat * lg + lb).astype(BF16)


def _causal(w, upper=False):
    t = lax.broadcasted_iota(jnp.int32, (CHUNK, CHUNK), 0)
    u = lax.broadcasted_iota(jnp.int32, (CHUNK, CHUNK), 1)
    return jnp.where((u >= t) if upper else (t >= u), w, 0.0).astype(BF16)


def _sgu_fwd(pz, lg, lb, w, bt, name):
    s, sw = pz.shape[0], pz.shape[1] // 2
    groups = sw // GROUP_DIM

    def body(i, ins, consts, outs, accs):
        lgv, lbv, w_ref, btv = consts[0][...], consts[1][...], consts[2], consts[3][...]
        zu = _gelu(ins[0][:, :sw].astype(F32))
        _, _, vn = _sgu_norm(ins[0][:, sw:].astype(F32), lgv, lbv)
        for g in range(groups):
            gs = slice(g * GROUP_DIM, (g + 1) * GROUP_DIM)
            sv = jnp.dot(_causal(w_ref[g]), vn[:, gs], preferred_element_type=F32) + btv[:, g:g + 1]
            outs[0][:, gs] = (zu[:, gs] * sv).astype(BF16)

    return _rowwise(body, name, s, CHUNK, [pz], [lg, lb, w, bt], [_sds((s, sw), BF16)])[0]


def _sgu_bwd(pz, dy, lg, lb, w, wt, bt, name, after=()):
    s, sw = pz.shape[0], pz.shape[1] // 2
    groups = sw // GROUP_DIM

    def body(i, ins, consts, outs, accs):
        lgv, lbv, w_ref, wt_ref, btv = consts[0][...], consts[1][...], consts[2], consts[3], consts[4][...]

        @pl.when(i == 0)
        def _():
            for a in accs:
                a[...] = jnp.zeros_like(a)

        pu, pv = ins[0][:, :sw].astype(F32), ins[0][:, sw:].astype(F32)
        dyv = ins[1][...].astype(F32)
        zu = _gelu(pu)
        xhat, rs, vn = _sgu_norm(pv, lgv, lbv)
        dvn_parts, db_parts = [], []
        lower = lax.broadcasted_iota(jnp.int32, (CHUNK, CHUNK), 0) >= lax.broadcasted_iota(jnp.int32, (CHUNK, CHUNK), 1)
        for g in range(groups):
            gs = slice(g * GROUP_DIM, (g + 1) * GROUP_DIM)
            sv = jnp.dot(_causal(w_ref[g]), vn[:, gs], preferred_element_type=F32) + btv[:, g:g + 1]
            dpu = dyv[:, gs] * sv * _gelu_grad(pu[:, gs])
            outs[0][:, gs] = dpu.astype(BF16)
            accs[4][:, gs] += jnp.sum(dpu, axis=0, keepdims=True)
            dsv = dyv[:, gs] * zu[:, gs]
            dsvb = dsv.astype(BF16)
            db_parts.append(jnp.sum(dsv, axis=1, keepdims=True))
            accs[2][g] += jnp.where(lower, lax.dot_general(dsvb, vn[:, gs], _DIMS["nt"], preferred_element_type=F32), 0.0)
            dvn_parts.append(jnp.dot(_causal(wt_ref[g], upper=True), dsvb, preferred_element_type=F32))
        dvn = jnp.concatenate(dvn_parts, axis=1)
        accs[3][...] += jnp.concatenate(db_parts, axis=1)
        accs[0][...] += jnp.sum(dvn * xhat, axis=0, keepdims=True)
        accs[1][...] += jnp.sum(dvn, axis=0, keepdims=True)
        dxh = dvn * lgv
        dz = rs * (dxh - jnp.mean(dxh, axis=-1, keepdims=True) - xhat * jnp.mean(dxh * xhat, axis=-1, keepdims=True))
        dpv = dz * _gelu_grad(pv)
        outs[0][:, sw:] = dpv.astype(BF16)
        accs[4][:, sw:] += jnp.sum(dpv, axis=0, keepdims=True)

    return _rowwise(body, name, s, CHUNK, [pz, dy], [lg, lb, w, wt, bt], [_sds((s, 2 * sw), BF16)],
                    [_sds((1, sw), F32), _sds((1, sw), F32), _sds((groups, CHUNK, CHUNK), F32), _sds((CHUNK, groups), F32),
                     _sds((1, 2 * sw), F32)],
                    after=after)


def _mesh_place():
    x, y, c = lax.axis_index("x"), lax.axis_index("y"), lax.axis_index("c")
    return x, y, c, 4 * x + 2 * y + c


def _peer(x, y, c, k):
    px, py, pc = x ^ ((k >> 2) & 1), y ^ ((k >> 1) & 1), c ^ (k & 1)
    return (px, py, pc), 4 * px + 2 * py + pc


BY_SLOTS, BY_COLS, BY_PAIRED_COLS = 0, 1, 2


def _col_block(ref, idx, width, cols):
    if cols == BY_PAIRED_COLS:
        idx = (idx % (N_DEV // 2)) * 2 + idx // (N_DEV // 2)
    return ref.at[:, pl.ds(pl.multiple_of(idx * width, LANES), width)]


def _exchange_copy(src_ref, land_ref, send_sems, recv_sems, k, place, scatter, arriving, cols):
    x, y, c, me = place
    peer, pidx = _peer(x, y, c, k)
    slot = pidx if arriving else me
    if scatter:
        src = _col_block(src_ref, pidx, land_ref.shape[-1], cols) if cols else src_ref.at[pidx]
        dst = land_ref.at[slot]
    else:
        src = src_ref
        dst = _col_block(land_ref, slot, src_ref.shape[-1], cols) if cols else land_ref.at[slot]
    return pltpu.make_async_remote_copy(
        src_ref=src, dst_ref=dst, send_sem=send_sems[k - 1], recv_sem=recv_sems[k - 1], device_id=peer,
        device_id_type=MESH_TYPE)


def _own_copy(src_ref, land_ref, sem, place, scatter, cols):
    me = place[3]
    if scatter:
        src = _col_block(src_ref, me, land_ref.shape[-1], cols) if cols else src_ref.at[me]
        dst = land_ref.at[me]
    else:
        src = src_ref
        dst = _col_block(land_ref, me, src_ref.shape[-1], cols) if cols else land_ref.at[me]
    return pltpu.make_async_copy(src, dst, sem)


N_PEERS = N_DEV - 1
N_EXCHANGE_SEMS = 2 * N_PEERS + 1


def _land_shape(a, scatter, cols):
    if scatter:
        return (N_DEV, a.shape[0], a.shape[1] // N_DEV) if cols else a.shape
    return (a.shape[0], N_DEV * a.shape[1]) if cols else (N_DEV,) + a.shape


def _exchange_start(srcs, scatter, cols, name, after):
    n = len(srcs)
    land_shapes = [_land_shape(a, scatter, cl) for a, cl in zip(srcs, cols)]

    def body(*refs):
        src, land = refs[:n], refs[n:2 * n]
        send_sems = refs[2 * n + 1:2 * n + 1 + N_PEERS]
        recv_sems = refs[2 * n + 1 + N_PEERS:2 * n + 1 + 2 * N_PEERS]
        own_sem = refs[2 * n + 1 + 2 * N_PEERS]
        token = refs[-1]
        place = _mesh_place()
        for t in range(n):
            for k in range(1, N_DEV):
                _exchange_copy(src[t], land[t], send_sems, recv_sems, k, place, scatter, False, cols[t]).start()
            _own_copy(src[t], land[t], own_sem, place, scatter, cols[t]).start()
        token[...] = jnp.zeros_like(token)

    return pl.pallas_call(
        body,
        name=name,
        out_shape=(*[pltpu.SemaphoreType.DMA(())] * N_EXCHANGE_SEMS, *[pltpu.HBM(a.shape, a.dtype) for a in srcs],
                   *[pltpu.HBM(shp, a.dtype) for shp, a in zip(land_shapes, srcs)], _sds((SUBLANES, LANES), F32)),
        in_specs=[HBM] * (2 * n) + [ANY],
        out_specs=(*[SEM] * N_EXCHANGE_SEMS, *[HBM] * (2 * n), pl.BlockSpec(memory_space=pltpu.VMEM)),
        input_output_aliases={i: N_EXCHANGE_SEMS + i for i in range(2 * n)},
        compiler_params=pltpu.CompilerParams(has_side_effects=DATAFLOW_EFFECT),
    )(*[pltpu.with_memory_space_constraint(a, pltpu.HBM) for a in srcs],
      *[pltpu.with_memory_space_constraint(lax.empty(shp, a.dtype), pltpu.HBM) for shp, a in zip(land_shapes, srcs)],
      after)


def _exchange_wait(started, after, scatter, cols, name):
    sems = started[:N_EXCHANGE_SEMS]
    thru = started[N_EXCHANGE_SEMS:-1]
    n = len(thru) // 2

    def body(*refs):
        src, land = refs[:n], refs[n:2 * n]
        send_sems = refs[2 * n:2 * n + N_PEERS]
        recv_sems = refs[2 * n + N_PEERS:2 * n + 2 * N_PEERS]
        own_sem = refs[2 * n + 2 * N_PEERS]
        place = _mesh_place()
        for t in range(n):
            for k in range(1, N_DEV):
                cp = _exchange_copy(src[t], land[t], send_sems, recv_sems, k, place, scatter, True, cols[t])
                cp.wait_send()
                cp.wait_recv()
            _own_copy(src[t], land[t], own_sem, place, scatter, cols[t]).wait()

    out = pl.pallas_call(
        body,
        name=name,
        out_shape=tuple(pltpu.HBM(a.shape, a.dtype) for a in thru),
        in_specs=[HBM] * (2 * n) + [SEM] * N_EXCHANGE_SEMS + [ANY],
        out_specs=tuple([HBM] * (2 * n)),
        input_output_aliases={i: i for i in range(2 * n)},
        compiler_params=pltpu.CompilerParams(has_side_effects=DATAFLOW_EFFECT),
    )(*thru, *sems, after)
    return out[:n], out[n:]


def _adamw(w, g, m, v):
    m = ADAM_B1 * m + (1.0 - ADAM_B1) * g
    v = ADAM_B2 * v + (1.0 - ADAM_B2) * (g * g)
    m_hat = m / (1.0 - ADAM_B1 ** ADAM_STEP)
    v_hat = v / (1.0 - ADAM_B2 ** ADAM_STEP)
    delta = -ADAM_LR * (m_hat / (jnp.sqrt(v_hat) + ADAM_EPS) + ADAM_WD * w)
    return delta, m, v


def _adam_rows(r, c):
    fits = [t for t in range(BF16_SUBLANES, r + 1, BF16_SUBLANES) if r % t == 0 and t * c <= ADAM_BLOCK_ELEMS]
    return max(fits) if fits else r


def _adam_body(p_ref, w_ref, m_ref, v_ref, g_out, d_out, m_out, v_out):
    g = p_ref[0].astype(F32)
    for d in range(1, N_DEV):
        g = g + p_ref[d].astype(F32)
    delta, mn, vn = _adamw(w_ref[...], g, m_ref[...], v_ref[...])
    g_out[...] = g
    d_out[...] = delta
    m_out[...] = mn
    v_out[...] = vn


def _reduce_adam_layer(parts, w, m, v, prev, layer, name):
    nl, r, c = w.shape
    tr = _adam_rows(r, c)
    if prev is None:
        prev = [lax.empty((nl, r, c), F32) for _ in range(4)]

    def body(p_ref, w_ref, m_ref, v_ref, *rest):
        _adam_body(p_ref, w_ref, m_ref, v_ref, *rest[4:])

    blk = pl.BlockSpec((None, tr, c), lambda i: (layer, i, 0))
    out = _sds((nl, r, c), F32)
    return pl.pallas_call(
        body,
        name=name,
        grid=(r // tr,),
        in_specs=[pl.BlockSpec((N_DEV, tr, c), lambda i: (0, i, 0)), blk, blk, blk, ANY, ANY, ANY, ANY],
        out_specs=[blk, blk, blk, blk],
        out_shape=[out, out, out, out],
        input_output_aliases={4: 0, 5: 1, 6: 2, 7: 3},
        compiler_params=_params("parallel"),
    )(parts, w, m, v, *prev)


def _reduce_adam(parts, w, m, v, name):
    nl, _, r, c = parts.shape
    tr = _adam_rows(r, c)

    def body(*refs):
        _adam_body(*refs)

    blk = pl.BlockSpec((None, tr, c), lambda l, i: (l, i, 0))
    out = _sds((nl, r, c), F32)
    return pl.pallas_call(
        body,
        name=name,
        grid=(nl, r // tr),
        in_specs=[pl.BlockSpec((None, N_DEV, tr, c), lambda l, i: (l, 0, i, 0)), blk, blk, blk],
        out_specs=[blk, blk, blk, blk],
        out_shape=[out, out, out, out],
        compiler_params=_params("parallel", "parallel"),
    )(parts, w, m, v)


def _pack(arrays):
    flat = []
    for a in arrays:
        a = a.reshape(-1).astype(F32)
        flat.append(jnp.pad(a, (0, (-a.shape[0]) % PACK_UNIT)))
    out = jnp.concatenate(flat)
    rows = out.shape[0] // LANES
    pad_rows = (-rows) % 512
    return jnp.pad(out, (0, pad_rows * LANES)).reshape(rows + pad_rows, LANES)


def _unpack(packed, shapes):
    flat = packed.reshape(-1)
    out, off = [], 0
    for shp in shapes:
        size = math.prod(shp)
        out.append(flat[off:off + size].reshape(shp))
        off += size + (-size) % PACK_UNIT
    return out


def _in_runs(d, qkv_w, sw, gb):
    g0 = qkv_w + 2 * sw
    runs = [(0, qkv_w, "qkv", 0), (qkv_w, 2 * sw, "z", 0)]
    for j in range(d // gb):
        runs.append((g0 + j * gb, gb, "g", 2 * j * gb))
        runs.append((g0 + d + j * gb, gb, "g", (2 * j + 1) * gb))
    return runs


def _pieces_from_global(take, runs):
    out = {}
    for piece in ("qkv", "z", "g"):
        own = sorted((r for r in runs if r[2] == piece), key=lambda r: r[3])
        out[piece] = jnp.concatenate([take(g, g + w) for g, w, _, _ in own], axis=-1)
    return out


def _global_from_pieces(pieces, runs, a, b):
    segs = []
    for g, w, piece, start in sorted(runs):
        lo, hi = max(a, g), min(b, g + w)
        if lo < hi:
            segs.append(pieces[piece][..., start + lo - g:start + hi - g])
    return jnp.concatenate(segs, axis=-1)


def _take_from_shards(land):
    c = land.shape[2]

    def take(a, b):
        parts = []
        while a < b:
            dev = a // c
            lo, hi = a - dev * c, min(b - dev * c, c)
            parts.append(land[dev][:, lo:hi])
            a = dev * c + hi
        return jnp.concatenate(parts, axis=-1)

    return take


def _to_full_cols(g):
    d, k, n = g.shape
    return jnp.transpose(g, (1, 0, 2)).reshape(k, d * n)


def _to_col_shards(a):
    k, n = a.shape
    return jnp.transpose(a.reshape(k, N_DEV, n // N_DEV), (1, 0, 2))


def kernel(x, positions, norm1_g, w_in, b_in, sinks, sgu_ln_g, sgu_ln_b, sgu_w, sgu_b, w_attn_branch, w_sgu_branch, w_out, norm2_g, w_gate_up, w_down, final_g, loss_target, m_norm1_g, m_w_in, m_b_in, m_sinks, m_sgu_ln_g, m_sgu_ln_b, m_sgu_w, m_sgu_b, m_w_attn_branch, m_w_sgu_branch, m_w_out, m_norm2_g, m_w_gate_up, m_w_down, m_final_g, v_norm1_g, v_w_in, v_b_in, v_sinks, v_sgu_ln_g, v_sgu_ln_b, v_sgu_w, v_sgu_b, v_w_attn_branch, v_w_sgu_branch, v_w_out, v_norm2_g, v_w_gate_up, v_w_down, v_final_g):
    nl = w_in.shape[0]
    s, d = x.shape[1], x.shape[2]
    aw = w_attn_branch.shape[1]
    sw = w_sgu_branch.shape[1]
    in_w = w_in.shape[2] * N_DEV
    kw = (in_w - aw - 2 * sw - 2 * d) // 2
    qkv_w = aw + 2 * kw
    groups = sw // GROUP_DIM
    ff = w_down.shape[1] * N_DEV

    h = x.reshape(s, d)
    target = loss_target.reshape(s, d)
    tabs = _rope_tables(positions.reshape(s, 1), "rope_tables")

    big = [w_in, w_attn_branch, w_sgu_branch, w_out, w_gate_up, w_down]
    big_m = [m_w_in, m_w_attn_branch, m_w_sgu_branch, m_w_out, m_w_gate_up, m_w_down]
    big_v = [v_w_in, v_w_attn_branch, v_w_sgu_branch, v_w_out, v_w_gate_up, v_w_down]
    big_names = ("w_in", "w_attn_branch", "w_sgu_branch", "w_out", "w_gate_up", "w_down")
    W_IN, W_AB, W_SB, W_OUT, W_GU, W_DOWN = range(6)
    weight_groups = ((W_IN,), (W_AB, W_SB, W_OUT), (W_GU, W_DOWN))
    grad_groups = ((W_DOWN, W_GU), (W_OUT, W_AB, W_SB), (W_IN,))

    col_sharded = (W_IN, W_AB, W_SB, W_GU)
    by_cols = [BY_COLS if t in col_sharded and big[t].shape[2] % LANES == 0 else BY_SLOTS for t in range(6)]
    assert by_cols[W_GU] == BY_COLS, "the fused swiglu kernels need gate/up column blocks of whole lane tiles"
    by_cols[W_GU] = BY_PAIRED_COLS

    def start_gather(l, group, after):
        return _exchange_start([big[t][l].astype(BF16) for t in group], False, tuple(by_cols[t] for t in group),
                               f"gather_start_l{l}_{big_names[group[0]]}", after)

    def full_weight(t, land):
        if by_cols[t]:
            return land
        if t == W_IN:
            return _pieces_from_global(_take_from_shards(land), in_runs)
        return _to_full_cols(land) if t in col_sharded else land.reshape(N_DEV * land.shape[1], land.shape[2])

    gate_block = _pick(d, GATE_BLOCK_PREFS)
    in_runs = _in_runs(d, qkv_w, sw, gate_block)

    saved = []
    started = {}
    token = h
    for l in range(nl):
        for ll in ((0, 1) if l == 0 else (l + 1,)):
            if ll < nl:
                for group in weight_groups:
                    started[(ll, group)] = start_gather(ll, group, token)
                    token = started[(ll, group)][-1]
        gathered = {}

        def weight(t, after, l=l, gathered=gathered):
            if t not in gathered:
                group = next(g for (ll, g) in started if ll == l and t in g)
                srcs, lands = _exchange_wait(started.pop((l, group)), after, False, tuple(by_cols[tt] for tt in group),
                                             f"gather_wait_l{l}_{big_names[group[0]]}")
                for tt, ld in zip(group, lands):
                    gathered[tt] = full_weight(tt, ld)
            return gathered[t]

        bias = b_in[l].reshape(1, in_w)
        g1, g2 = norm1_g[l].reshape(1, d), norm2_g[l].reshape(1, d)
        lg, lb = sgu_ln_g[l].reshape(1, sw), sgu_ln_b[l].reshape(1, sw)
        bt = sgu_b[l].T

        xn = _rms_fwd(h, g1, "rms1_fwd", after=(token,))
        wts = dict(weight(W_IN, xn))
        biases = _pieces_from_global(lambda a, b: bias[:, a:b], in_runs)
        qkv = _matmul(xn, wts["qkv"], "nn", BF16, "proj_qkv", bias=biases["qkv"])
        pz = _matmul(xn, wts["z"], "nn", BF16, "proj_z", bias=biases["z"])
        pg = _matmul(xn, wts["g"], "nn", BF16, "proj_g", bias=biases["g"])
        y_attn = _attn_fwd(qkv, tabs, sinks[l], aw, kw, "attn_fwd")
        y_sgu = _sgu_fwd(pz, lg, lb, sgu_w[l], bt, "sgu_fwd")
        wts.update(ab=weight(W_AB, y_sgu), sb=weight(W_SB, y_sgu), out=weight(W_OUT, y_sgu))
        a_br, s_br, merged = _branches_merge(y_attn, y_sgu, wts["ab"], wts["sb"], pg, gate_block, "branches_merge")
        h_mid, hn = _out_proj_rms(merged, wts["out"], h, g2, "out_proj")
        wts.update(gu=weight(W_GU, hn), down=weight(W_DOWN, hn))
        gu, act = _gate_up_swiglu(hn, wts["gu"], "gate_up")
        h_out = _matmul(act, wts["down"], "nn", F32, "down_proj", res=h_mid)
        saved.append(dict(wts=wts, h=h, xn=xn, qkv=qkv, pz=pz, pg=pg, y_attn=y_attn, y_sgu=y_sgu, a_br=a_br,
                          s_br=s_br, merged=merged, h_mid=h_mid, hn=hn, gu=gu, act=act,
                          g1=g1, g2=g2, lg=lg, lb=lb, bt=bt))
        h = h_out

    dh, dhb, d_final_g, loss_blk = _loss_head(h, final_g.reshape(1, d), target, "loss_head")

    small = {n: [None] * nl for n in ("norm1_g", "b_in", "sinks", "sgu_ln_g", "sgu_ln_b", "sgu_w", "sgu_b", "norm2_g")}
    scattering = {}

    def start_scatter(l, group, grads, after):
        sends = []
        for t, dw in zip(group, grads):
            if by_cols[t]:
                sends.append(dw)
            elif t == W_IN:
                c = big[t].shape[2]
                sends.append(jnp.stack([_global_from_pieces(dw, in_runs, dev * c, (dev + 1) * c) for dev in range(N_DEV)]))
            elif t in col_sharded:
                sends.append(_to_col_shards(dw))
            else:
                sends.append(dw.reshape(N_DEV, dw.shape[0] // N_DEV, dw.shape[1]))
        scattering[(l, group)] = _exchange_start(sends, True, tuple(by_cols[t] for t in group),
                                                 f"scatter_start_l{l}_{big_names[group[0]]}", after)
        return scattering[(l, group)][-1]

    for l in reversed(range(nl)):
        sv = saved[l]
        wts = sv["wts"]
        d_gu = _d_act_swiglu(dhb, wts["down"], sv["gu"], "d_act")
        dw_down = _matmul(sv["act"], dhb, "tn", BF16, "dw_down")
        dw_gu = _matmul(sv["hn"], d_gu, "tn", BF16, "dw_gate_up")
        token = start_scatter(l, grad_groups[0], [dw_down, dw_gu], token)
        d_hn = _matmul(d_gu, wts["gu"], "nt", BF16, "d_hn", after=token)
        dh_mid, dmb, dg2 = _rms_bwd(sv["h_mid"], sv["g2"], d_hn, dh, "rms2_bwd")
        d_a, d_s, d_pg, db_g = _d_merged_merge_bwd(dmb, wts["out"], sv["pg"], sv["a_br"], sv["s_br"], gate_block, "d_merged")
        dw_out = _matmul(sv["merged"], dmb, "tn", BF16, "dw_out")
        d_y_attn = _matmul(d_a, wts["ab"], "nt", BF16, "d_y_attn")
        dw_ab = _matmul(sv["y_attn"], d_a, "tn", BF16, "dw_attn_branch")
        d_y_sgu = _matmul(d_s, wts["sb"], "nt", BF16, "d_y_sgu")
        dw_sb = _matmul(sv["y_sgu"], d_s, "tn", BF16, "dw_sgu_branch")
        token = start_scatter(l, grad_groups[1], [dw_out, dw_ab, dw_sb], token)
        d_pz, d_lg, d_lb, d_sw, d_sbt, db_z = _sgu_bwd(sv["pz"], d_y_sgu, sv["lg"], sv["lb"], sgu_w[l],
                                                 jnp.transpose(sgu_w[l], (0, 2, 1)), sv["bt"], "sgu_bwd", after=(token,))
        d_q, d_kv, d_sinks = _attn_bwd(sv["qkv"], tabs, sinks[l], sv["y_attn"], d_y_attn, aw, kw, "attn_bwd")
        d_qkv = jnp.concatenate([d_q, d_kv], axis=1)
        dw_qkv = _matmul(sv["xn"], d_qkv, "tn", BF16, "dw_qkv")
        dw_z = _matmul(sv["xn"], d_pz, "tn", BF16, "dw_z")
        dw_g = _matmul(sv["xn"], d_pg, "tn", BF16, "dw_g")
        token = start_scatter(l, grad_groups[2], [dict(qkv=dw_qkv, z=dw_z, g=dw_g)], token)
        d_xn = _matmul(d_qkv, wts["qkv"], "nt", F32, "d_xn_qkv", after=token)
        d_xn = _matmul(d_pz, wts["z"], "nt", F32, "d_xn_z", res=d_xn)
        d_xn = _matmul(d_pg, wts["g"], "nt", F32, "d_xn_g", res=d_xn)
        dh, dhb, dg1 = _rms_bwd(sv["h"], sv["g1"], d_xn, dh_mid, "rms1_bwd")

        small["norm1_g"][l], small["norm2_g"][l] = dg1, dg2
        small["b_in"][l] = _global_from_pieces(dict(qkv=_colsum(d_qkv, "db_qkv"), z=db_z, g=db_g), in_runs, 0, in_w)
        small["sinks"][l] = d_sinks[0, :aw // HEAD_DIM]
        small["sgu_ln_g"][l], small["sgu_ln_b"][l] = d_lg, d_lb
        small["sgu_w"][l] = d_sw
        small["sgu_b"][l] = d_sbt.T

    grad_x = dh.reshape(x.shape)

    names = ["norm1_g", "b_in", "sinks", "sgu_ln_g", "sgu_ln_b", "sgu_w", "sgu_b", "norm2_g"]
    small_w = [norm1_g, b_in, sinks, sgu_ln_g, sgu_ln_b, sgu_w, sgu_b, norm2_g, final_g]
    small_m = [m_norm1_g, m_b_in, m_sinks, m_sgu_ln_g, m_sgu_ln_b, m_sgu_w, m_sgu_b, m_norm2_g, m_final_g]
    small_v = [v_norm1_g, v_b_in, v_sinks, v_sgu_ln_g, v_sgu_ln_b, v_sgu_w, v_sgu_b, v_norm2_g, v_final_g]
    shapes = [w.shape for w in small_w] + [(1,)]
    partial = [jnp.stack([p.reshape(w.shape[1:]) for p in small[n]]) for n, w in zip(names, small_w)]
    partial += [d_final_g.reshape(final_g.shape), loss_blk[0, :1]]
    zero = jnp.zeros((1,), F32)
    small_started = _exchange_start([_pack(partial)], False, (False,), "gather_start_small_grads", dhb)

    big_out = [None] * len(big)
    after = small_started[-1]
    for l in reversed(range(nl)):
        for group in grad_groups:
            srcs, lands = _exchange_wait(scattering.pop((l, group)), after, True, tuple(by_cols[t] for t in group),
                                         f"scatter_wait_l{l}_{big_names[group[0]]}")
            for t, parts in zip(group, lands):
                big_out[t] = _reduce_adam_layer(parts, big[t], big_m[t], big_v[t], big_out[t], l, f"adam_{big_names[t]}")
                after = big_out[t][0]

    srcs, lands = _exchange_wait(small_started, after, False, (False,), "gather_wait_small_grads")
    sm = _reduce_adam(lands[0][None], _pack(small_w + [zero])[None], _pack(small_m + [zero])[None],
                      _pack(small_v + [zero])[None], "adam_small")
    sm_g, sm_d, sm_m, sm_v = [_unpack(a[0], shapes) for a in sm]
    loss = sm_g[-1].reshape(())

    def ordered(kind_small, kind_big):
        by_name = dict(zip(["norm1_g", "b_in", "sinks", "sgu_ln_g", "sgu_ln_b", "sgu_w", "sgu_b", "norm2_g", "final_g"], kind_small))
        by_name.update(zip(["w_in", "w_attn_branch", "w_sgu_branch", "w_out", "w_gate_up", "w_down"], kind_big))
        order = ["norm1_g", "w_in", "b_in", "sinks", "sgu_ln_g", "sgu_ln_b", "sgu_w", "sgu_b", "w_attn_branch",
                 "w_sgu_branch", "w_out", "norm2_g", "w_gate_up", "w_down", "final_g"]
        return [by_name[n] for n in order]

    outs = [loss, grad_x]
    for idx, sm_kind in enumerate((sm_g, sm_d, sm_m, sm_v)):
        outs += ordered(sm_kind[:-1], [o[idx] for o in big_out])
    return tuple(outs)
```

```python
import math

import jax
import jax.numpy as jnp
from jax import lax
from jax.experimental import pallas as pl
from jax.experimental.pallas import tpu as pltpu

F32 = jnp.float32
BF16 = jnp.bfloat16

N_DEV = 8
HEAD_DIM = 64
WINDOW = 128
CHUNK = 128
GROUP_DIM = 128
ROPE_DIM = HEAD_DIM // 4
ROPE_HALF = ROPE_DIM // 2
ROPE_THETA = 500000.0
EPS = 1e-5
NEG = -1e30
ATTN_SCALE = HEAD_DIM ** -0.5
ADAM_LR = 0.001
ADAM_B1 = 0.9
ADAM_B2 = 0.999
ADAM_EPS = 1e-08
ADAM_WD = 0.01
ADAM_STEP = 10
LANES = 128
SUBLANES = 8
BF16_SUBLANES = 16
PACK_UNIT = SUBLANES * LANES
ADAM_BLOCK_ELEMS = 256 * 1024
V7X_VMEM_LIMIT_BYTES = 56 * 1024 * 1024
MATMUL_TILE_PREFS = (1024, 1408, 768, 512, 384, 256, 128)
MATMUL_WHOLE_K = 2048
MATMUL_TN_K = 4096
MATMUL_VMEM_BUDGET_BYTES = 52 * 1024 * 1024
MATMUL_K_PREFS = (2816, 2048, 1536, 1408, 1024, 768, 512, 384, 256, 128)
ROW_TILE_PREFS = (512, 256, 128)
SWIGLU_ROW_PREFS = (512, 256, 128)
D_ACT_ROW_PREFS = (1024, 512, 256, 128)
OUT_PROJ_ROW_PREFS = (512, 256, 128)
MERGE_ROW_PREFS = (1024, 512, 256, 128)
GATE_BLOCK_PREFS = (1024, 512, 256, 128)
FUSED_ROW_CHUNK = 256
MESH_TYPE = pl.DeviceIdType.MESH
ANY = pl.BlockSpec(memory_space=pl.ANY)
HBM = pl.BlockSpec(memory_space=pltpu.HBM)
SEM = pl.BlockSpec(memory_space=pltpu.SEMAPHORE)
DATAFLOW_EFFECT = pltpu.SideEffectType.DATAFLOW_SIDE_EFFECTING


def _pick(n, prefs):
    for p in prefs:
        if n % p == 0:
            return p
    return n


def _params(*sem):
    return pltpu.CompilerParams(dimension_semantics=sem, vmem_limit_bytes=V7X_VMEM_LIMIT_BYTES)


_DIMS = {"nn": (((1,), (0,)), ((), ())), "nt": (((1,), (1,)), ((), ())), "tn": (((0,), (0,)), ((), ()))}


def _matmul(a, b, mode, out_dtype, name, bias=None, res=None, after=None):
    if mode == "nn":
        (m, k), n = a.shape, b.shape[1]
    elif mode == "nt":
        (m, k), n = a.shape, b.shape[0]
    else:
        (k, m), n = a.shape, b.shape[1]
    tm, tn = _pick(m, MATMUL_TILE_PREFS), _pick(n, MATMUL_TILE_PREFS)
    if k <= MATMUL_WHOLE_K:
        tk = k
    else:
        fits = [t for t in ((MATMUL_TN_K,) if mode == "tn" else ()) + MATMUL_K_PREFS
                if k % t == 0 and 4 * t * (tm + tn) + 16 * tm * tn <= MATMUL_VMEM_BUDGET_BYTES]
        tk = fits[0]
    nk = k // tk
    dims = _DIMS[mode]
    a_spec = pl.BlockSpec((tk, tm), lambda i, j, kk: (kk, i)) if mode == "tn" else pl.BlockSpec((tm, tk), lambda i, j, kk: (i, kk))
    b_spec = pl.BlockSpec((tn, tk), lambda i, j, kk: (j, kk)) if mode == "nt" else pl.BlockSpec((tk, tn), lambda i, j, kk: (kk, j))
    in_specs, args = [a_spec, b_spec], [a, b]
    if bias is not None:
        in_specs.append(pl.BlockSpec((1, tn), lambda i, j, kk: (0, j)))
        args.append(bias)
    if res is not None:
        in_specs.append(pl.BlockSpec((tm, tn), lambda i, j, kk: (i, j)))
        args.append(res)
    if after is not None:
        in_specs.append(ANY)
        args.append(after)

    def body(*refs):
        a_ref, b_ref = refs[0], refs[1]
        pos = 2
        bias_ref = res_ref = None
        if bias is not None:
            bias_ref = refs[pos]
            pos += 1
        if res is not None:
            res_ref = refs[pos]
            pos += 1
        if after is not None:
            pos += 1
        o_ref = refs[pos]

        def finish(r):
            if bias_ref is not None:
                r = r + bias_ref[...]
            if res_ref is not None:
                r = r + res_ref[...]
            o_ref[...] = r.astype(out_dtype)

        part = lax.dot_general(a_ref[...], b_ref[...], dims, preferred_element_type=F32)
        if nk == 1:
            finish(part)
        else:
            acc_ref = refs[pos + 1]
            kk = pl.program_id(2)

            @pl.when(kk == 0)
            def _():
                acc_ref[...] = part

            @pl.when((kk > 0) & (kk < nk - 1))
            def _():
                acc_ref[...] += part

            @pl.when(kk == nk - 1)
            def _():
                finish(acc_ref[...] + part)

    return pl.pallas_call(
        body,
        name=name,
        grid=(m // tm, n // tn, nk),
        in_specs=in_specs,
        out_specs=pl.BlockSpec((tm, tn), lambda i, j, kk: (i, j)),
        out_shape=jax.ShapeDtypeStruct((m, n), out_dtype),
        scratch_shapes=[] if nk == 1 else [pltpu.VMEM((tm, tn), F32)],
        compiler_params=_params("parallel", "parallel", "arbitrary"),
    )(*args)


def _out_proj_rms(merged, w_out, h, g, name):
    s, d = h.shape
    tm = _pick(s, OUT_PROJ_ROW_PREFS)

    def body(a_ref, b_ref, h_ref, g_ref, o_ref, n_ref):
        for rows in _row_chunks(tm):
            r = h_ref[rows, :] + jnp.dot(a_ref[rows, :], b_ref[...], preferred_element_type=F32)
            o_ref[rows, :] = r
            rs = lax.rsqrt(jnp.mean(r * r, axis=-1, keepdims=True) + EPS)
            n_ref[rows, :] = (r * rs * g_ref[...]).astype(BF16)

    row = pl.BlockSpec((tm, d), lambda i: (i, 0))
    return pl.pallas_call(
        body,
        name=name,
        grid=(s // tm,),
        in_specs=[row, pl.BlockSpec((d, d), lambda i: (0, 0)), row, pl.BlockSpec((1, d), lambda i: (0, 0))],
        out_specs=[row, row],
        out_shape=[_sds((s, d), F32), _sds((s, d), BF16)],
        compiler_params=_params("parallel"),
    )(merged, w_out, h, g)


def _rowwise(body, name, rows, tr, ins, consts, outs, accs=(), after=()):
    n_in, n_c, n_o, n_a = len(ins), len(consts), len(outs), len(after)

    def wrapped(*refs):
        body(pl.program_id(0), refs[:n_in], refs[n_in:n_in + n_c], refs[n_in + n_c + n_a:n_in + n_c + n_a + n_o],
             refs[n_in + n_c + n_a + n_o:])

    def whole(shape):
        zeros = (0,) * len(shape)
        return pl.BlockSpec(tuple(shape), lambda i: zeros)

    in_specs = ([pl.BlockSpec((tr, a.shape[1]), lambda i: (i, 0)) for a in ins] + [whole(c.shape) for c in consts]
                + [ANY] * n_a)
    out_specs = [pl.BlockSpec((tr, o.shape[1]), lambda i: (i, 0)) for o in outs] + [whole(a.shape) for a in accs]
    return pl.pallas_call(
        wrapped,
        name=name,
        grid=(rows // tr,),
        in_specs=in_specs,
        out_specs=out_specs,
        out_shape=list(outs) + list(accs),
        compiler_params=_params("arbitrary" if accs else "parallel"),
    )(*ins, *consts, *after)


def _sds(shape, dtype):
    return jax.ShapeDtypeStruct(tuple(shape), dtype)


def _rms_fwd(h, g, name, after=()):
    s, d = h.shape
    tr = _pick(s, ROW_TILE_PREFS)

    def body(i, ins, consts, outs, accs):
        x = ins[0][...]
        r = lax.rsqrt(jnp.mean(x * x, axis=-1, keepdims=True) + EPS)
        outs[0][...] = (x * r * consts[0][...]).astype(BF16)

    return _rowwise(body, name, s, tr, [h], [g], [_sds((s, d), BF16)], after=after)[0]


def _rms_bwd(h, g, dy, dh_up, name):
    s, d = h.shape
    tr = _pick(s, ROW_TILE_PREFS)

    def body(i, ins, consts, outs, accs):
        x, dyv, up = ins[0][...], ins[1][...].astype(F32), ins[2][...]
        r = lax.rsqrt(jnp.mean(x * x, axis=-1, keepdims=True) + EPS)
        xr = x * r
        gy = dyv * consts[0][...]
        dx = r * (gy - xr * jnp.mean(gy * xr, axis=-1, keepdims=True))
        outs[0][...] = up + dx
        outs[1][...] = (up + dx).astype(BF16)

        @pl.when(i == 0)
        def _():
            accs[0][...] = jnp.zeros_like(accs[0])

        accs[0][...] += jnp.sum(dyv * xr, axis=0, keepdims=True)

    return _rowwise(body, name, s, tr, [h, dy, dh_up], [g], [_sds((s, d), F32), _sds((s, d), BF16)], [_sds((1, d), F32)])


def _loss_head(h, g, target, name):
    s, d = h.shape
    tr = _pick(s, ROW_TILE_PREFS)

    def body(i, ins, consts, outs, accs):
        x, t = ins[0][...], ins[1][...]
        gv = consts[0][...]
        r = lax.rsqrt(jnp.mean(x * x, axis=-1, keepdims=True) + EPS)
        xr = x * r
        diff = xr * gv - t
        dyv = diff * (1.0 / d)
        gy = dyv * gv
        dx = r * (gy - xr * jnp.mean(gy * xr, axis=-1, keepdims=True))
        outs[0][...] = dx
        outs[1][...] = dx.astype(BF16)

        @pl.when(i == 0)
        def _():
            accs[0][...] = jnp.zeros_like(accs[0])
            accs[1][...] = jnp.zeros_like(accs[1])

        accs[0][...] += jnp.sum(dyv * xr, axis=0, keepdims=True)
        part = 0.5 * jnp.sum(jnp.mean(diff * diff, axis=-1, keepdims=True), axis=0, keepdims=True)
        accs[1][...] += jnp.broadcast_to(part, accs[1].shape)

    return _rowwise(body, name, s, tr, [h, target], [g], [_sds((s, d), F32), _sds((s, d), BF16)],
                    [_sds((1, d), F32), _sds((SUBLANES, LANES), F32)])


def _colsum(a, name):
    s, w = a.shape
    tr = _pick(s, ROW_TILE_PREFS)

    def body(i, ins, consts, outs, accs):
        @pl.when(i == 0)
        def _():
            accs[0][...] = jnp.zeros_like(accs[0])

        accs[0][...] += jnp.sum(ins[0][...].astype(F32), axis=0, keepdims=True)

    return _rowwise(body, name, s, tr, [a], [], [], [_sds((1, w), F32)])[0]


def _sigmoid(x):
    return 1.0 / (1.0 + jnp.exp(-x))


def _branches_merge(y_attn, y_sgu, w_ab, w_sb, pg, gb, name):
    s, aw = y_attn.shape
    sw, d = w_sb.shape
    tm = _pick(s, MERGE_ROW_PREFS)

    def body(ya_ref, ys_ref, wa_ref, ws_ref, pg_ref, a_out, s_out, m_out):
        for rows in _row_chunks(tm):
            a = jnp.dot(ya_ref[rows, :], wa_ref[...], preferred_element_type=F32)
            b = jnp.dot(ys_ref[rows, :], ws_ref[...], preferred_element_type=F32)
            ga, gs = _sigmoid(pg_ref[rows, :gb].astype(F32)), _sigmoid(pg_ref[rows, gb:].astype(F32))
            a_out[rows, :] = a.astype(BF16)
            s_out[rows, :] = b.astype(BF16)
            m_out[rows, :] = (ga * a + gs * b).astype(BF16)

    blk = pl.BlockSpec((tm, gb), lambda i, j: (i, j))
    out = _sds((s, d), BF16)
    return pl.pallas_call(
        body,
        name=name,
        grid=(s // tm, d // gb),
        in_specs=[pl.BlockSpec((tm, aw), lambda i, j: (i, 0)), pl.BlockSpec((tm, sw), lambda i, j: (i, 0)),
                  pl.BlockSpec((aw, gb), lambda i, j: (0, j)), pl.BlockSpec((sw, gb), lambda i, j: (0, j)),
                  pl.BlockSpec((tm, 2 * gb), lambda i, j: (i, j))],
        out_specs=[blk, blk, blk],
        out_shape=[out, out, out],
        compiler_params=_params("parallel", "parallel"),
    )(y_attn, y_sgu, w_ab, w_sb, pg)


def _d_merged_merge_bwd(dmb, w_out, pg, a_br, s_br, gb, name):
    s, d = dmb.shape
    tm = _pick(s, MERGE_ROW_PREFS)

    def body(a_ref, b_ref, pg_ref, ab_ref, sb_ref, da_out, ds_out, dpg_out, db_out):
        @pl.when(pl.program_id(1) == 0)
        def _():
            db_out[...] = jnp.zeros_like(db_out)

        for rows in _row_chunks(tm):
            dm = lax.dot_general(a_ref[rows, :], b_ref[...], _DIMS["nt"], preferred_element_type=F32)
            ga, gs = _sigmoid(pg_ref[rows, :gb].astype(F32)), _sigmoid(pg_ref[rows, gb:].astype(F32))
            da_out[rows, :] = (dm * ga).astype(BF16)
            ds_out[rows, :] = (dm * gs).astype(BF16)
            dpa = dm * ab_ref[rows, :].astype(F32) * ga * (1.0 - ga)
            dps = dm * sb_ref[rows, :].astype(F32) * gs * (1.0 - gs)
            dpg_out[rows, :gb] = dpa.astype(BF16)
            dpg_out[rows, gb:] = dps.astype(BF16)
            db_out[:, :gb] += jnp.sum(dpa, axis=0, keepdims=True)
            db_out[:, gb:] += jnp.sum(dps, axis=0, keepdims=True)

    blk = pl.BlockSpec((tm, gb), lambda j, i: (i, j))
    pair = pl.BlockSpec((tm, 2 * gb), lambda j, i: (i, j))
    return pl.pallas_call(
        body,
        name=name,
        grid=(d // gb, s // tm),
        in_specs=[pl.BlockSpec((tm, d), lambda j, i: (i, 0)), pl.BlockSpec((gb, d), lambda j, i: (j, 0)), pair, blk, blk],
        out_specs=[blk, blk, pair, pl.BlockSpec((1, 2 * gb), lambda j, i: (0, j))],
        out_shape=[_sds((s, d), BF16), _sds((s, d), BF16), _sds((s, 2 * d), BF16), _sds((1, 2 * d), F32)],
        compiler_params=_params("parallel", "arbitrary"),
    )(dmb, w_out, pg, a_br, s_br)


def _row_chunks(tm):
    rc = _pick(tm, (FUSED_ROW_CHUNK,))
    return [slice(r, r + rc) for r in range(0, tm, rc)]


def _gate_up_swiglu(hn, w_gu, name):
    s, d = hn.shape
    n2 = w_gu.shape[1]
    fb = n2 // N_DEV
    tm = _pick(s, SWIGLU_ROW_PREFS)

    def body(a_ref, b_ref, gu_ref, act_ref):
        r = jnp.dot(a_ref[...], b_ref[...], preferred_element_type=F32)
        gu_ref[...] = r.astype(BF16)
        gate, up = r[:, :fb], r[:, fb:]
        act_ref[...] = (gate * _sigmoid(gate) * up).astype(BF16)

    return pl.pallas_call(
        body,
        name=name,
        grid=(s // tm, N_DEV // 2),
        in_specs=[pl.BlockSpec((tm, d), lambda i, j: (i, 0)), pl.BlockSpec((d, 2 * fb), lambda i, j: (0, j))],
        out_specs=[pl.BlockSpec((tm, 2 * fb), lambda i, j: (i, j)), pl.BlockSpec((tm, fb), lambda i, j: (i, j))],
        out_shape=[_sds((s, n2), BF16), _sds((s, n2 // 2), BF16)],
        compiler_params=_params("parallel", "parallel"),
    )(hn, w_gu)


def _d_act_swiglu(dhb, w_down, gu, name):
    s, d = dhb.shape
    n2 = gu.shape[1]
    fb = n2 // N_DEV
    tm = _pick(s, D_ACT_ROW_PREFS)

    def body(a_ref, b_ref, gu_ref, o_ref):
        for rows in _row_chunks(tm):
            da = lax.dot_general(a_ref[rows, :], b_ref[...], _DIMS["nt"], preferred_element_type=F32)
            gate, up = gu_ref[rows, :fb].astype(F32), gu_ref[rows, fb:].astype(F32)
            sg = _sigmoid(gate)
            o_ref[rows, :fb] = (da * up * sg * (1.0 + gate * (1.0 - sg))).astype(BF16)
            o_ref[rows, fb:] = (da * gate * sg).astype(BF16)

    pair = pl.BlockSpec((tm, 2 * fb), lambda i, j: (i, j))
    return pl.pallas_call(
        body,
        name=name,
        grid=(s // tm, N_DEV // 2),
        in_specs=[pl.BlockSpec((tm, d), lambda i, j: (i, 0)), pl.BlockSpec((fb, d), lambda i, j: (j, 0)), pair],
        out_specs=pair,
        out_shape=_sds((s, n2), BF16),
        compiler_params=_params("parallel", "parallel"),
    )(dhb, w_down, gu)


def _rope_tables(pos_col, name):
    s = pos_col.shape[0]
    tr = _pick(s, (1024, 512, 256, 128))
    inv = ROPE_THETA ** (-jnp.arange(0, ROPE_DIM, 2, dtype=F32) / ROPE_DIM)
    lane = jnp.arange(LANES)
    inv_lanes = inv[lane % ROPE_HALF].reshape(1, LANES)

    def body(i, ins, consts, outs, accs):
        ang = ins[0][...].astype(F32) * consts[0][...]
        c, sn = jnp.cos(ang), jnp.sin(ang)
        in_head = lax.broadcasted_iota(jnp.int32, ang.shape, 1) % HEAD_DIM
        outs[0][:, 0:LANES] = jnp.where(in_head < ROPE_DIM, c, 1.0)
        outs[0][:, LANES:2 * LANES] = jnp.where(in_head < ROPE_HALF, -sn, 0.0)
        outs[0][:, 2 * LANES:] = jnp.where((in_head >= ROPE_HALF) & (in_head < ROPE_DIM), sn, 0.0)

    return _rowwise(body, name, s, tr, [pos_col], [inv_lanes], [_sds((s, 3 * LANES), F32)])[0]


def _rope(x, tab, inverse=False):
    width = x.shape[1]
    reps = width // LANES
    c = jnp.tile(tab[:, 0:LANES], (1, reps))
    lo = jnp.tile(tab[:, LANES:2 * LANES], (1, reps))
    hi = jnp.tile(tab[:, 2 * LANES:], (1, reps))
    if inverse:
        lo, hi = -lo, -hi
    return x * c + pltpu.roll(x, width - ROPE_HALF, 1) * lo + pltpu.roll(x, ROPE_HALF, 1) * hi


def _attn_specs(aw, kw):
    kb = aw // kw
    prev = lambda i: jnp.maximum(i - 1, 0)
    return [
        pl.BlockSpec(memory_space=pltpu.SMEM),
        pl.BlockSpec((WINDOW, aw), lambda i: (i, 0)),
        pl.BlockSpec((WINDOW, kw), lambda i: (i, kb)),
        pl.BlockSpec((WINDOW, kw), lambda i: (prev(i), kb)),
        pl.BlockSpec((WINDOW, kw), lambda i: (i, kb + 1)),
        pl.BlockSpec((WINDOW, kw), lambda i: (prev(i), kb + 1)),
        pl.BlockSpec((WINDOW, 3 * LANES), lambda i: (i, 0)),
        pl.BlockSpec((WINDOW, 3 * LANES), lambda i: (prev(i), 0)),
    ]


def _attn_common(i, q_ref, kc_ref, kp_ref, vc_ref, vp_ref, tq_ref, tp_ref):
    tq, tp = tq_ref[...], tp_ref[...]
    qt = (_rope(q_ref[...].astype(F32), tq) * ATTN_SCALE).astype(BF16).T
    kc = _rope(kc_ref[...].astype(F32), tq)
    kp = _rope(kp_ref[...].astype(F32), tp)
    k2 = jnp.concatenate([kp, kc], axis=0).astype(BF16)
    v2 = jnp.concatenate([vp_ref[...], vc_ref[...]], axis=0)
    kj = lax.broadcasted_iota(jnp.int32, (2 * WINDOW, WINDOW), 0)
    qi = lax.broadcasted_iota(jnp.int32, (2 * WINDOW, WINDOW), 1)
    rel = qi + WINDOW - kj
    ok = (rel >= 0) & (rel < WINDOW) & ((kj >= WINDOW) | (i > 0))
    return qt, k2, v2, ok, tq, tp


def _head_probs(qt_h, kg, ok, sink):
    s = jnp.dot(kg, qt_h, preferred_element_type=F32)
    s = jnp.where(ok, s, NEG)
    m = jnp.maximum(jnp.max(s, axis=0, keepdims=True), sink)
    p = jnp.exp(s - m)
    es = jnp.exp(sink - m)
    inv = 1.0 / (jnp.sum(p, axis=0, keepdims=True) + es)
    return p * inv, es * inv


def _attn_fwd(qkv, tabs, sinks, aw, kw, name):
    s = qkv.shape[0]
    nq, nkv = aw // HEAD_DIM, kw // HEAD_DIM
    qpk = nq // nkv

    def body(s_ref, q_ref, kc_ref, kp_ref, vc_ref, vp_ref, tq_ref, tp_ref, o_ref):
        i = pl.program_id(0)
        tq, tp = tq_ref[...], tp_ref[...]
        q = (_rope(q_ref[...].astype(F32), tq) * ATTN_SCALE).astype(BF16)
        k2 = jnp.concatenate([_rope(kp_ref[...].astype(F32), tp), _rope(kc_ref[...].astype(F32), tq)], axis=0).astype(BF16)
        v2 = jnp.concatenate([vp_ref[...], vc_ref[...]], axis=0)
        qi = lax.broadcasted_iota(jnp.int32, (WINDOW, 2 * WINDOW), 0)
        kj = lax.broadcasted_iota(jnp.int32, (WINDOW, 2 * WINDOW), 1)
        rel = qi + WINDOW - kj
        ok = (rel >= 0) & (rel < WINDOW) & ((kj >= WINDOW) | (i > 0))
        for h in range(nq):
            g = h // qpk
            hs, gs = slice(h * HEAD_DIM, (h + 1) * HEAD_DIM), slice(g * HEAD_DIM, (g + 1) * HEAD_DIM)
            sc = lax.dot_general(q[:, hs], k2[:, gs], _DIMS["nt"], preferred_element_type=F32)
            sc = jnp.where(ok, sc, NEG)
            m = jnp.maximum(jnp.max(sc, axis=1, keepdims=True), s_ref[h])
            p = jnp.exp(sc - m)
            inv = 1.0 / (jnp.sum(p, axis=1, keepdims=True) + jnp.exp(s_ref[h] - m))
            o = jnp.dot((p * inv).astype(BF16), v2[:, gs], preferred_element_type=F32)
            o_ref[:, hs] = o.astype(BF16)

    return pl.pallas_call(
        body,
        name=name,
        grid=(s // WINDOW,),
        in_specs=_attn_specs(aw, kw),
        out_specs=pl.BlockSpec((WINDOW, aw), lambda i: (i, 0)),
        out_shape=_sds((s, aw), BF16),
        compiler_params=_params("parallel"),
    )(sinks, qkv, qkv, qkv, qkv, qkv, tabs, tabs)


def _attn_bwd(qkv, tabs, sinks, o, do, aw, kw, name):
    s = qkv.shape[0]
    nb = s // WINDOW
    nq, nkv = aw // HEAD_DIM, kw // HEAD_DIM
    qpk = nq // nkv

    def body(s_ref, q_ref, kc_ref, kp_ref, vc_ref, vp_ref, tq_ref, tp_ref, o_ref, do_ref,
             dq_ref, dkv_ref, ds_ref, ck_ref, cv_ref):
        i = pl.program_id(0)

        @pl.when(i == 0)
        def _():
            ck_ref[...] = jnp.zeros_like(ck_ref)
            cv_ref[...] = jnp.zeros_like(cv_ref)
            ds_ref[...] = jnp.zeros_like(ds_ref)

        qt, k2, v2, ok, tq, tp = _attn_common(i, q_ref, kc_ref, kp_ref, vc_ref, vp_ref, tq_ref, tp_ref)
        dot_t, ot = do_ref[...].T, o_ref[...].T
        k2t = k2.T
        row0 = lax.broadcasted_iota(jnp.int32, (SUBLANES, LANES), 0) == 0
        lane = lax.broadcasted_iota(jnp.int32, (SUBLANES, LANES), 1)
        dsink = jnp.zeros((SUBLANES, LANES), F32)
        dqt_parts, dk_parts, dv_parts = [], [], []
        for g in range(nkv):
            gs = slice(g * HEAD_DIM, (g + 1) * HEAD_DIM)
            kg, vg = k2[:, gs], v2[:, gs]
            dk_g = jnp.zeros((2 * WINDOW, HEAD_DIM), F32)
            dv_g = jnp.zeros((2 * WINDOW, HEAD_DIM), F32)
            for j in range(qpk):
                h = g * qpk + j
                hs = slice(h * HEAD_DIM, (h + 1) * HEAD_DIM)
                pn, psink = _head_probs(qt[hs], kg, ok, s_ref[h])
                delta = jnp.sum(dot_t[hs].astype(F32) * ot[hs].astype(F32), axis=0, keepdims=True)
                dp = jnp.dot(vg, dot_t[hs], preferred_element_type=F32)
                dsb = (pn * (dp - delta)).astype(BF16)
                dsink = dsink + jnp.where(row0 & (lane == h), -jnp.sum(psink * delta, axis=1, keepdims=True), 0.0)
                dqt_parts.append(jnp.dot(k2t[gs], dsb, preferred_element_type=F32))
                dk_g = dk_g + lax.dot_general(dsb, qt[hs], _DIMS["nt"], preferred_element_type=F32)
                dv_g = dv_g + lax.dot_general(pn.astype(BF16), dot_t[hs], _DIMS["nt"], preferred_element_type=F32)
            dk_parts.append(dk_g)
            dv_parts.append(dv_g)
        ds_ref[...] += dsink
        dq_ref[...] = _rope(jnp.concatenate(dqt_parts, axis=0).T * ATTN_SCALE, tq, inverse=True).astype(BF16)
        dk2 = jnp.concatenate(dk_parts, axis=1)
        dv2 = jnp.concatenate(dv_parts, axis=1)
        dk_prev = _rope(ck_ref[...] + dk2[:WINDOW], tp, inverse=True)
        dv_prev = cv_ref[...] + dv2[:WINDOW]

        @pl.when(i > 0)
        def _():
            dkv_ref[pl.ds(pl.multiple_of((i - 1) * WINDOW, WINDOW), WINDOW), :] = jnp.concatenate(
                [dk_prev, dv_prev], axis=1).astype(BF16)

        ck_ref[...] = dk2[WINDOW:]
        cv_ref[...] = dv2[WINDOW:]

        @pl.when(i == nb - 1)
        def _():
            dkv_ref[pl.ds(pl.multiple_of(i * WINDOW, WINDOW), WINDOW), :] = jnp.concatenate(
                [_rope(dk2[WINDOW:], tq, inverse=True), dv2[WINDOW:]], axis=1).astype(BF16)

    blk = pl.BlockSpec((WINDOW, aw), lambda i: (i, 0))
    return pl.pallas_call(
        body,
        name=name,
        grid=(nb,),
        in_specs=_attn_specs(aw, kw) + [blk, blk],
        out_specs=[blk, pl.BlockSpec((s, 2 * kw), lambda i: (0, 0)), pl.BlockSpec((SUBLANES, LANES), lambda i: (0, 0))],
        out_shape=[_sds((s, aw), BF16), _sds((s, 2 * kw), BF16), _sds((SUBLANES, LANES), F32)],
        scratch_shapes=[pltpu.VMEM((WINDOW, kw), F32), pltpu.VMEM((WINDOW, kw), F32)],
        compiler_params=_params("arbitrary"),
    )(sinks, qkv, qkv, qkv, qkv, qkv, tabs, tabs, o, do)


_INV_SQRT2 = 1.0 / math.sqrt(2.0)
_INV_SQRT2PI = 1.0 / math.sqrt(2.0 * math.pi)


def _gelu(x):
    return x * (lax.erf(x * _INV_SQRT2) + 1.0) * 0.5


def _gelu_grad(x):
    return 0.5 * (lax.erf(x * _INV_SQRT2) + 1.0) + x * jnp.exp(-0.5 * x * x) * _INV_SQRT2PI


def _sgu_norm(pv, lg, lb):
    zv = _gelu(pv)
    mu = jnp.mean(zv, axis=-1, keepdims=True)
    cen = zv - mu
    rs = lax.rsqrt(jnp.mean(cen * cen, axis=-1, keepdims=True) + EPS)
    xhat = cen * rs
    return xhat, rs, (xhat * lg + lb).astype(BF16)


def _causal(w, upper=False):
    t = lax.broadcasted_iota(jnp.int32, (CHUNK, CHUNK), 0)
    u = lax.broadcasted_iota(jnp.int32, (CHUNK, CHUNK), 1)
    return jnp.where((u >= t) if upper else (t >= u), w, 0.0).astype(BF16)


def _sgu_fwd(pz, lg, lb, w, bt, name):
    s, sw = pz.shape[0], pz.shape[1] // 2
    groups = sw // GROUP_DIM

    def body(i, ins, consts, outs, accs):
        lgv, lbv, w_ref, btv = consts[0][...], consts[1][...], consts[2], consts[3][...]
        zu = _gelu(ins[0][:, :sw].astype(F32))
        _, _, vn = _sgu_norm(ins[0][:, sw:].astype(F32), lgv, lbv)
        for g in range(groups):
            gs = slice(g * GROUP_DIM, (g + 1) * GROUP_DIM)
            sv = jnp.dot(_causal(w_ref[g]), vn[:, gs], preferred_element_type=F32) + btv[:, g:g + 1]
            outs[0][:, gs] = (zu[:, gs] * sv).astype(BF16)

    return _rowwise(body, name, s, CHUNK, [pz], [lg, lb, w, bt], [_sds((s, sw), BF16)])[0]


def _sgu_bwd(pz, dy, lg, lb, w, wt, bt, name, after=()):
    s, sw = pz.shape[0], pz.shape[1] // 2
    groups = sw // GROUP_DIM

    def body(i, ins, consts, outs, accs):
        lgv, lbv, w_ref, wt_ref, btv = consts[0][...], consts[1][...], consts[2], consts[3], consts[4][...]

        @pl.when(i == 0)
        def _():
            for a in accs:
                a[...] = jnp.zeros_like(a)

        pu, pv = ins[0][:, :sw].astype(F32), ins[0][:, sw:].astype(F32)
        dyv = ins[1][...].astype(F32)
        zu = _gelu(pu)
        xhat, rs, vn = _sgu_norm(pv, lgv, lbv)
        dvn_parts, db_parts = [], []
        lower = lax.broadcasted_iota(jnp.int32, (CHUNK, CHUNK), 0) >= lax.broadcasted_iota(jnp.int32, (CHUNK, CHUNK), 1)
        for g in range(groups):
            gs = slice(g * GROUP_DIM, (g + 1) * GROUP_DIM)
            sv = jnp.dot(_causal(w_ref[g]), vn[:, gs], preferred_element_type=F32) + btv[:, g:g + 1]
            dpu = dyv[:, gs] * sv * _gelu_grad(pu[:, gs])
            outs[0][:, gs] = dpu.astype(BF16)
            accs[4][:, gs] += jnp.sum(dpu, axis=0, keepdims=True)
            dsv = dyv[:, gs] * zu[:, gs]
            dsvb = dsv.astype(BF16)
            db_parts.append(jnp.sum(dsv, axis=1, keepdims=True))
            accs[2][g] += jnp.where(lower, lax.dot_general(dsvb, vn[:, gs], _DIMS["nt"], preferred_element_type=F32), 0.0)
            dvn_parts.append(jnp.dot(_causal(wt_ref[g], upper=True), dsvb, preferred_element_type=F32))
        dvn = jnp.concatenate(dvn_parts, axis=1)
        accs[3][...] += jnp.concatenate(db_parts, axis=1)
        accs[0][...] += jnp.sum(dvn * xhat, axis=0, keepdims=True)
        accs[1][...] += jnp.sum(dvn, axis=0, keepdims=True)
        dxh = dvn * lgv
        dz = rs * (dxh - jnp.mean(dxh, axis=-1, keepdims=True) - xhat * jnp.mean(dxh * xhat, axis=-1, keepdims=True))
        dpv = dz * _gelu_grad(pv)
        outs[0][:, sw:] = dpv.astype(BF16)
        accs[4][:, sw:] += jnp.sum(dpv, axis=0, keepdims=True)

    return _rowwise(body, name, s, CHUNK, [pz, dy], [lg, lb, w, wt, bt], [_sds((s, 2 * sw), BF16)],
                    [_sds((1, sw), F32), _sds((1, sw), F32), _sds((groups, CHUNK, CHUNK), F32), _sds((CHUNK, groups), F32),
                     _sds((1, 2 * sw), F32)],
                    after=after)


def _mesh_place():
    x, y, c = lax.axis_index("x"), lax.axis_index("y"), lax.axis_index("c")
    return x, y, c, 4 * x + 2 * y + c


def _peer(x, y, c, k):
    px, py, pc = x ^ ((k >> 2) & 1), y ^ ((k >> 1) & 1), c ^ (k & 1)
    return (px, py, pc), 4 * px + 2 * py + pc


BY_SLOTS, BY_COLS, BY_PAIRED_COLS = 0, 1, 2


def _col_block(ref, idx, width, cols):
    if cols == BY_PAIRED_COLS:
        idx = (idx % (N_DEV // 2)) * 2 + idx // (N_DEV // 2)
    return ref.at[:, pl.ds(pl.multiple_of(idx * width, LANES), width)]


def _exchange_copy(src_ref, land_ref, send_sems, recv_sems, k, place, scatter, arriving, cols):
    x, y, c, me = place
    peer, pidx = _peer(x, y, c, k)
    slot = pidx if arriving else me
    if scatter:
        src = _col_block(src_ref, pidx, land_ref.shape[-1], cols) if cols else src_ref.at[pidx]
        dst = land_ref.at[slot]
    else:
        src = src_ref
        dst = _col_block(land_ref, slot, src_ref.shape[-1], cols) if cols else land_ref.at[slot]
    return pltpu.make_async_remote_copy(
        src_ref=src, dst_ref=dst, send_sem=send_sems[k - 1], recv_sem=recv_sems[k - 1], device_id=peer,
        device_id_type=MESH_TYPE)


def _own_copy(src_ref, land_ref, sem, place, scatter, cols):
    me = place[3]
    if scatter:
        src = _col_block(src_ref, me, land_ref.shape[-1], cols) if cols else src_ref.at[me]
        dst = land_ref.at[me]
    else:
        src = src_ref
        dst = _col_block(land_ref, me, src_ref.shape[-1], cols) if cols else land_ref.at[me]
    return pltpu.make_async_copy(src, dst, sem)


N_PEERS = N_DEV - 1
N_EXCHANGE_SEMS = 2 * N_PEERS + 1


def _land_shape(a, scatter, cols):
    if scatter:
        return (N_DEV, a.shape[0], a.shape[1] // N_DEV) if cols else a.shape
    return (a.shape[0], N_DEV * a.shape[1]) if cols else (N_DEV,) + a.shape


def _exchange_start(srcs, scatter, cols, name, after):
    n = len(srcs)
    land_shapes = [_land_shape(a, scatter, cl) for a, cl in zip(srcs, cols)]

    def body(*refs):
        src, land = refs[:n], refs[n:2 * n]
        send_sems = refs[2 * n + 1:2 * n + 1 + N_PEERS]
        recv_sems = refs[2 * n + 1 + N_PEERS:2 * n + 1 + 2 * N_PEERS]
        own_sem = refs[2 * n + 1 + 2 * N_PEERS]
        token = refs[-1]
        place = _mesh_place()
        for t in range(n):
            for k in range(1, N_DEV):
                _exchange_copy(src[t], land[t], send_sems, recv_sems, k, place, scatter, False, cols[t]).start()
            _own_copy(src[t], land[t], own_sem, place, scatter, cols[t]).start()
        token[...] = jnp.zeros_like(token)

    return pl.pallas_call(
        body,
        name=name,
        out_shape=(*[pltpu.SemaphoreType.DMA(())] * N_EXCHANGE_SEMS, *[pltpu.HBM(a.shape, a.dtype) for a in srcs],
                   *[pltpu.HBM(shp, a.dtype) for shp, a in zip(land_shapes, srcs)], _sds((SUBLANES, LANES), F32)),
        in_specs=[HBM] * (2 * n) + [ANY],
        out_specs=(*[SEM] * N_EXCHANGE_SEMS, *[HBM] * (2 * n), pl.BlockSpec(memory_space=pltpu.VMEM)),
        input_output_aliases={i: N_EXCHANGE_SEMS + i for i in range(2 * n)},
        compiler_params=pltpu.CompilerParams(has_side_effects=DATAFLOW_EFFECT),
    )(*[pltpu.with_memory_space_constraint(a, pltpu.HBM) for a in srcs],
      *[pltpu.with_memory_space_constraint(lax.empty(shp, a.dtype), pltpu.HBM) for shp, a in zip(land_shapes, srcs)],
      after)


def _exchange_wait(started, after, scatter, cols, name):
    sems = started[:N_EXCHANGE_SEMS]
    thru = started[N_EXCHANGE_SEMS:-1]
    n = len(thru) // 2

    def body(*refs):
        src, land = refs[:n], refs[n:2 * n]
        send_sems = refs[2 * n:2 * n + N_PEERS]
        recv_sems = refs[2 * n + N_PEERS:2 * n + 2 * N_PEERS]
        own_sem = refs[2 * n + 2 * N_PEERS]
        place = _mesh_place()
        for t in range(n):
            for k in range(1, N_DEV):
                cp = _exchange_copy(src[t], land[t], send_sems, recv_sems, k, place, scatter, True, cols[t])
                cp.wait_send()
                cp.wait_recv()
            _own_copy(src[t], land[t], own_sem, place, scatter, cols[t]).wait()

    out = pl.pallas_call(
        body,
        name=name,
        out_shape=tuple(pltpu.HBM(a.shape, a.dtype) for a in thru),
        in_specs=[HBM] * (2 * n) + [SEM] * N_EXCHANGE_SEMS + [ANY],
        out_specs=tuple([HBM] * (2 * n)),
        input_output_aliases={i: i for i in range(2 * n)},
        compiler_params=pltpu.CompilerParams(has_side_effects=DATAFLOW_EFFECT),
    )(*thru, *sems, after)
    return out[:n], out[n:]


def _adamw(w, g, m, v):
    m = ADAM_B1 * m + (1.0 - ADAM_B1) * g
    v = ADAM_B2 * v + (1.0 - ADAM_B2) * (g * g)
    m_hat = m / (1.0 - ADAM_B1 ** ADAM_STEP)
    v_hat = v / (1.0 - ADAM_B2 ** ADAM_STEP)
    delta = -ADAM_LR * (m_hat / (jnp.sqrt(v_hat) + ADAM_EPS) + ADAM_WD * w)
    return delta, m, v


def _adam_rows(r, c):
    fits = [t for t in range(BF16_SUBLANES, r + 1, BF16_SUBLANES) if r % t == 0 and t * c <= ADAM_BLOCK_ELEMS]
    return max(fits) if fits else r


def _adam_body(p_ref, w_ref, m_ref, v_ref, g_out, d_out, m_out, v_out):
    g = p_ref[0].astype(F32)
    for d in range(1, N_DEV):
        g = g + p_ref[d].astype(F32)
    delta, mn, vn = _adamw(w_ref[...], g, m_ref[...], v_ref[...])
    g_out[...] = g
    d_out[...] = delta
    m_out[...] = mn
    v_out[...] = vn


def _reduce_adam_layer(parts, w, m, v, prev, layer, name):
    nl, r, c = w.shape
    tr = _adam_rows(r, c)
    if prev is None:
        prev = [lax.empty((nl, r, c), F32) for _ in range(4)]

    def body(p_ref, w_ref, m_ref, v_ref, *rest):
        _adam_body(p_ref, w_ref, m_ref, v_ref, *rest[4:])

    blk = pl.BlockSpec((None, tr, c), lambda i: (layer, i, 0))
    out = _sds((nl, r, c), F32)
    return pl.pallas_call(
        body,
        name=name,
        grid=(r // tr,),
        in_specs=[pl.BlockSpec((N_DEV, tr, c), lambda i: (0, i, 0)), blk, blk, blk, ANY, ANY, ANY, ANY],
        out_specs=[blk, blk, blk, blk],
        out_shape=[out, out, out, out],
        input_output_aliases={4: 0, 5: 1, 6: 2, 7: 3},
        compiler_params=_params("parallel"),
    )(parts, w, m, v, *prev)


def _reduce_adam(parts, w, m, v, name):
    nl, _, r, c = parts.shape
    tr = _adam_rows(r, c)

    def body(*refs):
        _adam_body(*refs)

    blk = pl.BlockSpec((None, tr, c), lambda l, i: (l, i, 0))
    out = _sds((nl, r, c), F32)
    return pl.pallas_call(
        body,
        name=name,
        grid=(nl, r // tr),
        in_specs=[pl.BlockSpec((None, N_DEV, tr, c), lambda l, i: (l, 0, i, 0)), blk, blk, blk],
        out_specs=[blk, blk, blk, blk],
        out_shape=[out, out, out, out],
        compiler_params=_params("parallel", "parallel"),
    )(parts, w, m, v)


def _pack(arrays):
    flat = []
    for a in arrays:
        a = a.reshape(-1).astype(F32)
        flat.append(jnp.pad(a, (0, (-a.shape[0]) % PACK_UNIT)))
    out = jnp.concatenate(flat)
    rows = out.shape[0] // LANES
    pad_rows = (-rows) % 512
    return jnp.pad(out, (0, pad_rows * LANES)).reshape(rows + pad_rows, LANES)


def _unpack(packed, shapes):
    flat = packed.reshape(-1)
    out, off = [], 0
    for shp in shapes:
        size = math.prod(shp)
        out.append(flat[off:off + size].reshape(shp))
        off += size + (-size) % PACK_UNIT
    return out


def _in_runs(d, qkv_w, sw, gb):
    g0 = qkv_w + 2 * sw
    runs = [(0, qkv_w, "qkv", 0), (qkv_w, 2 * sw, "z", 0)]
    for j in range(d // gb):
        runs.append((g0 + j * gb, gb, "g", 2 * j * gb))
        runs.append((g0 + d + j * gb, gb, "g", (2 * j + 1) * gb))
    return runs


def _pieces_from_global(take, runs, axis=-1):
    out = {}
    for piece in ("qkv", "z", "g"):
        own = sorted((r for r in runs if r[2] == piece), key=lambda r: r[3])
        parts = [take(g, g + w) for g, w, _, _ in own]
        out[piece] = parts[0] if len(parts) == 1 else jnp.concatenate(parts, axis=axis)
    return out


def _global_from_pieces(pieces, runs, axis=-1):
    segs = [lax.slice_in_dim(pieces[piece], start, start + w, axis=axis % pieces[piece].ndim)
            for _, w, piece, start in sorted(runs)]
    return jnp.concatenate(segs, axis=axis)


def _to_full_cols(g):
    d, k, n = g.shape
    return jnp.transpose(g, (1, 0, 2)).reshape(k, d * n)


def _to_col_shards(a):
    k, n = a.shape
    return jnp.transpose(a.reshape(k, N_DEV, n // N_DEV), (1, 0, 2))


def kernel(x, positions, norm1_g, w_in, b_in, sinks, sgu_ln_g, sgu_ln_b, sgu_w, sgu_b, w_attn_branch, w_sgu_branch, w_out, norm2_g, w_gate_up, w_down, final_g, loss_target, m_norm1_g, m_w_in, m_b_in, m_sinks, m_sgu_ln_g, m_sgu_ln_b, m_sgu_w, m_sgu_b, m_w_attn_branch, m_w_sgu_branch, m_w_out, m_norm2_g, m_w_gate_up, m_w_down, m_final_g, v_norm1_g, v_w_in, v_b_in, v_sinks, v_sgu_ln_g, v_sgu_ln_b, v_sgu_w, v_sgu_b, v_w_attn_branch, v_w_sgu_branch, v_w_out, v_norm2_g, v_w_gate_up, v_w_down, v_final_g):
    nl = w_in.shape[0]
    s, d = x.shape[1], x.shape[2]
    aw = w_attn_branch.shape[1]
    sw = w_sgu_branch.shape[1]
    in_w = w_in.shape[2] * N_DEV
    kw = (in_w - aw - 2 * sw - 2 * d) // 2
    qkv_w = aw + 2 * kw
    groups = sw // GROUP_DIM
    ff = w_down.shape[1] * N_DEV

    h = x.reshape(s, d)
    target = loss_target.reshape(s, d)
    tabs = _rope_tables(positions.reshape(s, 1), "rope_tables")

    transposed = lambda a: jnp.swapaxes(a, 1, 2)
    big = [transposed(w_in), w_attn_branch, w_sgu_branch, w_out, w_gate_up, w_down]
    big_m = [transposed(m_w_in), m_w_attn_branch, m_w_sgu_branch, m_w_out, m_w_gate_up, m_w_down]
    big_v = [transposed(v_w_in), v_w_attn_branch, v_w_sgu_branch, v_w_out, v_w_gate_up, v_w_down]
    big_names = ("w_in", "w_attn_branch", "w_sgu_branch", "w_out", "w_gate_up", "w_down")
    W_IN, W_AB, W_SB, W_OUT, W_GU, W_DOWN = range(6)
    weight_groups = ((W_IN,), (W_AB, W_SB, W_OUT), (W_GU, W_DOWN))
    grad_groups = ((W_DOWN, W_GU), (W_OUT, W_AB, W_SB), (W_IN,))

    col_sharded = (W_AB, W_SB, W_GU)
    by_cols = [BY_COLS if t in col_sharded and big[t].shape[2] % LANES == 0 else BY_SLOTS for t in range(6)]
    assert by_cols[W_GU] == BY_COLS, "the fused swiglu kernels need gate/up column blocks of whole lane tiles"
    by_cols[W_GU] = BY_PAIRED_COLS

    def start_gather(l, group, after):
        return _exchange_start([big[t][l].astype(BF16) for t in group], False, tuple(by_cols[t] for t in group),
                               f"gather_start_l{l}_{big_names[group[0]]}", after)

    def full_weight(t, land):
        if by_cols[t]:
            return land
        if t in col_sharded:
            return _to_full_cols(land)
        whole = land.reshape(N_DEV * land.shape[1], land.shape[2])
        return _pieces_from_global(lambda a, b: whole[a:b], in_runs, axis=0) if t == W_IN else whole

    gate_block = _pick(d, GATE_BLOCK_PREFS)
    in_runs = _in_runs(d, qkv_w, sw, gate_block)

    saved = []
    started = {}
    token = h
    for l in range(nl):
        for ll in ((0, 1) if l == 0 else (l + 1,)):
            if ll < nl:
                for group in weight_groups:
                    started[(ll, group)] = start_gather(ll, group, token)
                    token = started[(ll, group)][-1]
        gathered = {}

        def weight(t, after, l=l, gathered=gathered):
            if t not in gathered:
                group = next(g for (ll, g) in started if ll == l and t in g)
                srcs, lands = _exchange_wait(started.pop((l, group)), after, False, tuple(by_cols[tt] for tt in group),
                                             f"gather_wait_l{l}_{big_names[group[0]]}")
                for tt, ld in zip(group, lands):
                    gathered[tt] = full_weight(tt, ld)
            return gathered[t]

        bias = b_in[l].reshape(1, in_w)
        g1, g2 = norm1_g[l].reshape(1, d), norm2_g[l].reshape(1, d)
        lg, lb = sgu_ln_g[l].reshape(1, sw), sgu_ln_b[l].reshape(1, sw)
        bt = sgu_b[l].T

        xn = _rms_fwd(h, g1, "rms1_fwd", after=(token,))
        wts = dict(weight(W_IN, xn))
        biases = _pieces_from_global(lambda a, b: bias[:, a:b], in_runs)
        qkv = _matmul(xn, wts["qkv"], "nt", BF16, "proj_qkv", bias=biases["qkv"])
        pz = _matmul(xn, wts["z"], "nt", BF16, "proj_z", bias=biases["z"])
        pg = _matmul(xn, wts["g"], "nt", BF16, "proj_g", bias=biases["g"])
        y_attn = _attn_fwd(qkv, tabs, sinks[l], aw, kw, "attn_fwd")
        y_sgu = _sgu_fwd(pz, lg, lb, sgu_w[l], bt, "sgu_fwd")
        wts.update(ab=weight(W_AB, y_sgu), sb=weight(W_SB, y_sgu), out=weight(W_OUT, y_sgu))
        a_br, s_br, merged = _branches_merge(y_attn, y_sgu, wts["ab"], wts["sb"], pg, gate_block, "branches_merge")
        h_mid, hn = _out_proj_rms(merged, wts["out"], h, g2, "out_proj")
        wts.update(gu=weight(W_GU, hn), down=weight(W_DOWN, hn))
        gu, act = _gate_up_swiglu(hn, wts["gu"], "gate_up")
        h_out = _matmul(act, wts["down"], "nn", F32, "down_proj", res=h_mid)
        saved.append(dict(wts=wts, h=h, xn=xn, qkv=qkv, pz=pz, pg=pg, y_attn=y_attn, y_sgu=y_sgu, a_br=a_br,
                          s_br=s_br, merged=merged, h_mid=h_mid, hn=hn, gu=gu, act=act,
                          g1=g1, g2=g2, lg=lg, lb=lb, bt=bt))
        h = h_out

    dh, dhb, d_final_g, loss_blk = _loss_head(h, final_g.reshape(1, d), target, "loss_head")

    small = {n: [None] * nl for n in ("norm1_g", "b_in", "sinks", "sgu_ln_g", "sgu_ln_b", "sgu_w", "sgu_b", "norm2_g")}
    scattering = {}

    def start_scatter(l, group, grads, after):
        sends = []
        for t, dw in zip(group, grads):
            if t == W_IN:
                dw = _global_from_pieces(dw, in_runs, axis=0)
            if by_cols[t]:
                sends.append(dw)
            elif t in col_sharded:
                sends.append(_to_col_shards(dw))
            else:
                sends.append(dw.reshape(N_DEV, dw.shape[0] // N_DEV, dw.shape[1]))
        scattering[(l, group)] = _exchange_start(sends, True, tuple(by_cols[t] for t in group),
                                                 f"scatter_start_l{l}_{big_names[group[0]]}", after)
        return scattering[(l, group)][-1]

    for l in reversed(range(nl)):
        sv = saved[l]
        wts = sv["wts"]
        d_gu = _d_act_swiglu(dhb, wts["down"], sv["gu"], "d_act")
        dw_down = _matmul(sv["act"], dhb, "tn", BF16, "dw_down")
        dw_gu = _matmul(sv["hn"], d_gu, "tn", BF16, "dw_gate_up")
        token = start_scatter(l, grad_groups[0], [dw_down, dw_gu], token)
        d_hn = _matmul(d_gu, wts["gu"], "nt", BF16, "d_hn", after=token)
        dh_mid, dmb, dg2 = _rms_bwd(sv["h_mid"], sv["g2"], d_hn, dh, "rms2_bwd")
        d_a, d_s, d_pg, db_g = _d_merged_merge_bwd(dmb, wts["out"], sv["pg"], sv["a_br"], sv["s_br"], gate_block, "d_merged")
        dw_out = _matmul(sv["merged"], dmb, "tn", BF16, "dw_out")
        d_y_attn = _matmul(d_a, wts["ab"], "nt", BF16, "d_y_attn")
        dw_ab = _matmul(sv["y_attn"], d_a, "tn", BF16, "dw_attn_branch")
        d_y_sgu = _matmul(d_s, wts["sb"], "nt", BF16, "d_y_sgu")
        dw_sb = _matmul(sv["y_sgu"], d_s, "tn", BF16, "dw_sgu_branch")
        token = start_scatter(l, grad_groups[1], [dw_out, dw_ab, dw_sb], token)
        d_pz, d_lg, d_lb, d_sw, d_sbt, db_z = _sgu_bwd(sv["pz"], d_y_sgu, sv["lg"], sv["lb"], sgu_w[l],
                                                 jnp.transpose(sgu_w[l], (0, 2, 1)), sv["bt"], "sgu_bwd", after=(token,))
        d_q, d_kv, d_sinks = _attn_bwd(sv["qkv"], tabs, sinks[l], sv["y_attn"], d_y_attn, aw, kw, "attn_bwd")
        d_qkv = jnp.concatenate([d_q, d_kv], axis=1)
        dw_qkv = _matmul(d_qkv, sv["xn"], "tn", BF16, "dw_qkv")
        dw_z = _matmul(d_pz, sv["xn"], "tn", BF16, "dw_z")
        dw_g = _matmul(d_pg, sv["xn"], "tn", BF16, "dw_g")
        token = start_scatter(l, grad_groups[2], [dict(qkv=dw_qkv, z=dw_z, g=dw_g)], token)
        d_xn = _matmul(d_qkv, wts["qkv"], "nn", F32, "d_xn_qkv", after=token)
        d_xn = _matmul(d_pz, wts["z"], "nn", F32, "d_xn_z", res=d_xn)
        d_xn = _matmul(d_pg, wts["g"], "nn", F32, "d_xn_g", res=d_xn)
        dh, dhb, dg1 = _rms_bwd(sv["h"], sv["g1"], d_xn, dh_mid, "rms1_bwd")

        small["norm1_g"][l], small["norm2_g"][l] = dg1, dg2
        small["b_in"][l] = _global_from_pieces(dict(qkv=_colsum(d_qkv, "db_qkv"), z=db_z, g=db_g), in_runs)
        small["sinks"][l] = d_sinks[0, :aw // HEAD_DIM]
        small["sgu_ln_g"][l], small["sgu_ln_b"][l] = d_lg, d_lb
        small["sgu_w"][l] = d_sw
        small["sgu_b"][l] = d_sbt.T

    grad_x = dh.reshape(x.shape)

    names = ["norm1_g", "b_in", "sinks", "sgu_ln_g", "sgu_ln_b", "sgu_w", "sgu_b", "norm2_g"]
    small_w = [norm1_g, b_in, sinks, sgu_ln_g, sgu_ln_b, sgu_w, sgu_b, norm2_g, final_g]
    small_m = [m_norm1_g, m_b_in, m_sinks, m_sgu_ln_g, m_sgu_ln_b, m_sgu_w, m_sgu_b, m_norm2_g, m_final_g]
    small_v = [v_norm1_g, v_b_in, v_sinks, v_sgu_ln_g, v_sgu_ln_b, v_sgu_w, v_sgu_b, v_norm2_g, v_final_g]
    shapes = [w.shape for w in small_w] + [(1,)]
    partial = [jnp.stack([p.reshape(w.shape[1:]) for p in small[n]]) for n, w in zip(names, small_w)]
    partial += [d_final_g.reshape(final_g.shape), loss_blk[0, :1]]
    zero = jnp.zeros((1,), F32)
    small_started = _exchange_start([_pack(partial)], False, (False,), "gather_start_small_grads", dhb)

    big_out = [None] * len(big)
    after = small_started[-1]
    for l in reversed(range(nl)):
        for group in grad_groups:
            srcs, lands = _exchange_wait(scattering.pop((l, group)), after, True, tuple(by_cols[t] for t in group),
                                         f"scatter_wait_l{l}_{big_names[group[0]]}")
            for t, parts in zip(group, lands):
                big_out[t] = _reduce_adam_layer(parts, big[t], big_m[t], big_v[t], big_out[t], l, f"adam_{big_names[t]}")
                after = big_out[t][0]

    srcs, lands = _exchange_wait(small_started, after, False, (False,), "gather_wait_small_grads")
    sm = _reduce_adam(lands[0][None], _pack(small_w + [zero])[None], _pack(small_m + [zero])[None],
                      _pack(small_v + [zero])[None], "adam_small")
    sm_g, sm_d, sm_m, sm_v = [_unpack(a[0], shapes) for a in sm]
    loss = sm_g[-1].reshape(())

    def ordered(kind_small, kind_big):
        by_name = dict(zip(["norm1_g", "b_in", "sinks", "sgu_ln_g", "sgu_ln_b", "sgu_w", "sgu_b", "norm2_g", "final_g"], kind_small))
        by_name.update(zip(["w_in", "w_attn_branch", "w_sgu_branch", "w_out", "w_gate_up", "w_down"], kind_big))
        order = ["norm1_g", "w_in", "b_in", "sinks", "sgu_ln_g", "sgu_ln_b", "sgu_w", "sgu_b", "w_attn_branch",
                 "w_sgu_branch", "w_out", "norm2_g", "w_gate_up", "w_down", "final_g"]
        return [by_name[n] for n in order]

    big_out[W_IN] = [transposed(o) for o in big_out[W_IN]]
    outs = [loss, grad_x]
    for idx, sm_kind in enumerate((sm_g, sm_d, sm_m, sm_v)):
        outs += ordered(sm_kind[:-1], [o[idx] for o in big_out])
    return tuple(outs)
```

```python
import math

import jax
import jax.numpy as jnp
from jax import lax
from jax.experimental import pallas as pl
from jax.experimental.pallas import tpu as pltpu

F32 = jnp.float32
BF16 = jnp.bfloat16

N_DEV = 8
HEAD_DIM = 64
WINDOW = 128
CHUNK = 128
GROUP_DIM = 128
ROPE_DIM = HEAD_DIM // 4
ROPE_HALF = ROPE_DIM // 2
ROPE_THETA = 500000.0
EPS = 1e-5
NEG = -1e30
ATTN_SCALE = HEAD_DIM ** -0.5
ADAM_LR = 0.001
ADAM_B1 = 0.9
ADAM_B2 = 0.999
ADAM_EPS = 1e-08
ADAM_WD = 0.01
ADAM_STEP = 10
LANES = 128
SUBLANES = 8
BF16_SUBLANES = 16
PACK_UNIT = SUBLANES * LANES
ADAM_BLOCK_ELEMS = 256 * 1024
V7X_VMEM_LIMIT_BYTES = 56 * 1024 * 1024
MATMUL_TILE_PREFS = (1024, 1408, 768, 512, 384, 256, 128)
MATMUL_WHOLE_K = 2048
MATMUL_TN_K = 4096
MATMUL_MIN_ROWS = 512
MATMUL_VMEM_BUDGET_BYTES = 52 * 1024 * 1024
MATMUL_K_PREFS = (2816, 2048, 1536, 1408, 1024, 768, 512, 384, 256, 128)
ROW_TILE_PREFS = (512, 256, 128)
SWIGLU_ROW_PREFS = (512, 256, 128)
D_ACT_ROW_PREFS = (1024, 512, 256, 128)
OUT_PROJ_ROW_PREFS = (512, 256, 128)
MERGE_ROW_PREFS = (1024, 512, 256, 128)
GATE_BLOCK_PREFS = (1024, 512, 256, 128)
FUSED_ROW_CHUNK = 256
MESH_TYPE = pl.DeviceIdType.MESH
ANY = pl.BlockSpec(memory_space=pl.ANY)
HBM = pl.BlockSpec(memory_space=pltpu.HBM)
SEM = pl.BlockSpec(memory_space=pltpu.SEMAPHORE)
DATAFLOW_EFFECT = pltpu.SideEffectType.DATAFLOW_SIDE_EFFECTING


def _pick(n, prefs):
    for p in prefs:
        if n % p == 0:
            return p
    return n


def _params(*sem):
    return pltpu.CompilerParams(dimension_semantics=sem, vmem_limit_bytes=V7X_VMEM_LIMIT_BYTES)


_DIMS = {"nn": (((1,), (0,)), ((), ())), "nt": (((1,), (1,)), ((), ())), "tn": (((0,), (0,)), ((), ()))}


def _matmul(a, b, mode, out_dtype, name, bias=None, res=None, after=None):
    if mode == "nn":
        (m, k), n = a.shape, b.shape[1]
    elif mode == "nt":
        (m, k), n = a.shape, b.shape[0]
    else:
        (k, m), n = a.shape, b.shape[1]
    tm, tn = _pick(m, MATMUL_TILE_PREFS), _pick(n, MATMUL_TILE_PREFS)

    def k_tile(rows):
        epilogue = (24 if res is not None else 16) * rows * tn
        fits = [t for t in (k, k // 2, MATMUL_TN_K) + MATMUL_K_PREFS
                if k % t == 0 and t % LANES == 0
                and (t <= MATMUL_WHOLE_K or 4 * t * (rows + tn) + epilogue <= MATMUL_VMEM_BUDGET_BYTES)]
        return fits[0]

    half = tm // 2
    if half % LANES == 0 and half >= MATMUL_MIN_ROWS and k // k_tile(half) < k // k_tile(tm):
        tm = half
    tk = k_tile(tm)
    nk = k // tk
    dims = _DIMS[mode]
    a_spec = pl.BlockSpec((tk, tm), lambda i, j, kk: (kk, i)) if mode == "tn" else pl.BlockSpec((tm, tk), lambda i, j, kk: (i, kk))
    b_spec = pl.BlockSpec((tn, tk), lambda i, j, kk: (j, kk)) if mode == "nt" else pl.BlockSpec((tk, tn), lambda i, j, kk: (kk, j))
    in_specs, args = [a_spec, b_spec], [a, b]
    if bias is not None:
        in_specs.append(pl.BlockSpec((1, tn), lambda i, j, kk: (0, j)))
        args.append(bias)
    if res is not None:
        in_specs.append(pl.BlockSpec((tm, tn), lambda i, j, kk: (i, j)))
        args.append(res)
    if after is not None:
        in_specs.append(ANY)
        args.append(after)

    def body(*refs):
        a_ref, b_ref = refs[0], refs[1]
        pos = 2
        bias_ref = res_ref = None
        if bias is not None:
            bias_ref = refs[pos]
            pos += 1
        if res is not None:
            res_ref = refs[pos]
            pos += 1
        if after is not None:
            pos += 1
        o_ref = refs[pos]

        def finish(r):
            if bias_ref is not None:
                r = r + bias_ref[...]
            if res_ref is not None:
                r = r + res_ref[...]
            o_ref[...] = r.astype(out_dtype)

        part = lax.dot_general(a_ref[...], b_ref[...], dims, preferred_element_type=F32)
        if nk == 1:
            finish(part)
        else:
            acc_ref = refs[pos + 1]
            kk = pl.program_id(2)

            @pl.when(kk == 0)
            def _():
                acc_ref[...] = part

            @pl.when((kk > 0) & (kk < nk - 1))
            def _():
                acc_ref[...] += part

            @pl.when(kk == nk - 1)
            def _():
                finish(acc_ref[...] + part)

    return pl.pallas_call(
        body,
        name=name,
        grid=(m // tm, n // tn, nk),
        in_specs=in_specs,
        out_specs=pl.BlockSpec((tm, tn), lambda i, j, kk: (i, j)),
        out_shape=jax.ShapeDtypeStruct((m, n), out_dtype),
        scratch_shapes=[] if nk == 1 else [pltpu.VMEM((tm, tn), F32)],
        compiler_params=_params("parallel", "parallel", "arbitrary"),
    )(*args)


def _out_proj_rms(merged, w_out, h, g, name):
    s, d = h.shape
    tm = _pick(s, OUT_PROJ_ROW_PREFS)

    def body(a_ref, b_ref, h_ref, g_ref, o_ref, n_ref):
        for rows in _row_chunks(tm):
            r = h_ref[rows, :] + jnp.dot(a_ref[rows, :], b_ref[...], preferred_element_type=F32)
            o_ref[rows, :] = r
            rs = lax.rsqrt(jnp.mean(r * r, axis=-1, keepdims=True) + EPS)
            n_ref[rows, :] = (r * rs * g_ref[...]).astype(BF16)

    row = pl.BlockSpec((tm, d), lambda i: (i, 0))
    return pl.pallas_call(
        body,
        name=name,
        grid=(s // tm,),
        in_specs=[row, pl.BlockSpec((d, d), lambda i: (0, 0)), row, pl.BlockSpec((1, d), lambda i: (0, 0))],
        out_specs=[row, row],
        out_shape=[_sds((s, d), F32), _sds((s, d), BF16)],
        compiler_params=_params("parallel"),
    )(merged, w_out, h, g)


def _rowwise(body, name, rows, tr, ins, consts, outs, accs=(), after=()):
    n_in, n_c, n_o, n_a = len(ins), len(consts), len(outs), len(after)

    def wrapped(*refs):
        body(pl.program_id(0), refs[:n_in], refs[n_in:n_in + n_c], refs[n_in + n_c + n_a:n_in + n_c + n_a + n_o],
             refs[n_in + n_c + n_a + n_o:])

    def whole(shape):
        zeros = (0,) * len(shape)
        return pl.BlockSpec(tuple(shape), lambda i: zeros)

    in_specs = ([pl.BlockSpec((tr, a.shape[1]), lambda i: (i, 0)) for a in ins] + [whole(c.shape) for c in consts]
                + [ANY] * n_a)
    out_specs = [pl.BlockSpec((tr, o.shape[1]), lambda i: (i, 0)) for o in outs] + [whole(a.shape) for a in accs]
    return pl.pallas_call(
        wrapped,
        name=name,
        grid=(rows // tr,),
        in_specs=in_specs,
        out_specs=out_specs,
        out_shape=list(outs) + list(accs),
        compiler_params=_params("arbitrary" if accs else "parallel"),
    )(*ins, *consts, *after)


def _sds(shape, dtype):
    return jax.ShapeDtypeStruct(tuple(shape), dtype)


def _rms_fwd(h, g, name, after=()):
    s, d = h.shape
    tr = _pick(s, ROW_TILE_PREFS)

    def body(i, ins, consts, outs, accs):
        x = ins[0][...]
        r = lax.rsqrt(jnp.mean(x * x, axis=-1, keepdims=True) + EPS)
        outs[0][...] = (x * r * consts[0][...]).astype(BF16)

    return _rowwise(body, name, s, tr, [h], [g], [_sds((s, d), BF16)], after=after)[0]


def _rms_bwd(h, g, dy, dh_up, name):
    s, d = h.shape
    tr = _pick(s, ROW_TILE_PREFS)

    def body(i, ins, consts, outs, accs):
        x, dyv, up = ins[0][...], ins[1][...].astype(F32), ins[2][...]
        r = lax.rsqrt(jnp.mean(x * x, axis=-1, keepdims=True) + EPS)
        xr = x * r
        gy = dyv * consts[0][...]
        dx = r * (gy - xr * jnp.mean(gy * xr, axis=-1, keepdims=True))
        outs[0][...] = up + dx
        outs[1][...] = (up + dx).astype(BF16)

        @pl.when(i == 0)
        def _():
            accs[0][...] = jnp.zeros_like(accs[0])

        accs[0][...] += jnp.sum(dyv * xr, axis=0, keepdims=True)

    return _rowwise(body, name, s, tr, [h, dy, dh_up], [g], [_sds((s, d), F32), _sds((s, d), BF16)], [_sds((1, d), F32)])


def _loss_head(h, g, target, name):
    s, d = h.shape
    tr = _pick(s, ROW_TILE_PREFS)

    def body(i, ins, consts, outs, accs):
        x, t = ins[0][...], ins[1][...]
        gv = consts[0][...]
        r = lax.rsqrt(jnp.mean(x * x, axis=-1, keepdims=True) + EPS)
        xr = x * r
        diff = xr * gv - t
        dyv = diff * (1.0 / d)
        gy = dyv * gv
        dx = r * (gy - xr * jnp.mean(gy * xr, axis=-1, keepdims=True))
        outs[0][...] = dx
        outs[1][...] = dx.astype(BF16)

        @pl.when(i == 0)
        def _():
            accs[0][...] = jnp.zeros_like(accs[0])
            accs[1][...] = jnp.zeros_like(accs[1])

        accs[0][...] += jnp.sum(dyv * xr, axis=0, keepdims=True)
        part = 0.5 * jnp.sum(jnp.mean(diff * diff, axis=-1, keepdims=True), axis=0, keepdims=True)
        accs[1][...] += jnp.broadcast_to(part, accs[1].shape)

    return _rowwise(body, name, s, tr, [h, target], [g], [_sds((s, d), F32), _sds((s, d), BF16)],
                    [_sds((1, d), F32), _sds((SUBLANES, LANES), F32)])


def _colsum(a, name):
    s, w = a.shape
    tr = _pick(s, ROW_TILE_PREFS)

    def body(i, ins, consts, outs, accs):
        @pl.when(i == 0)
        def _():
            accs[0][...] = jnp.zeros_like(accs[0])

        accs[0][...] += jnp.sum(ins[0][...].astype(F32), axis=0, keepdims=True)

    return _rowwise(body, name, s, tr, [a], [], [], [_sds((1, w), F32)])[0]


def _sigmoid(x):
    return 1.0 / (1.0 + jnp.exp(-x))


def _branches_merge(y_attn, y_sgu, w_ab, w_sb, pg, gb, name):
    s, aw = y_attn.shape
    sw, d = w_sb.shape
    tm = _pick(s, MERGE_ROW_PREFS)

    def body(ya_ref, ys_ref, wa_ref, ws_ref, pg_ref, a_out, s_out, m_out):
        for rows in _row_chunks(tm):
            a = jnp.dot(ya_ref[rows, :], wa_ref[...], preferred_element_type=F32)
            b = jnp.dot(ys_ref[rows, :], ws_ref[...], preferred_element_type=F32)
            ga, gs = _sigmoid(pg_ref[rows, :gb].astype(F32)), _sigmoid(pg_ref[rows, gb:].astype(F32))
            a_out[rows, :] = a.astype(BF16)
            s_out[rows, :] = b.astype(BF16)
            m_out[rows, :] = (ga * a + gs * b).astype(BF16)

    blk = pl.BlockSpec((tm, gb), lambda i, j: (i, j))
    out = _sds((s, d), BF16)
    return pl.pallas_call(
        body,
        name=name,
        grid=(s // tm, d // gb),
        in_specs=[pl.BlockSpec((tm, aw), lambda i, j: (i, 0)), pl.BlockSpec((tm, sw), lambda i, j: (i, 0)),
                  pl.BlockSpec((aw, gb), lambda i, j: (0, j)), pl.BlockSpec((sw, gb), lambda i, j: (0, j)),
                  pl.BlockSpec((tm, 2 * gb), lambda i, j: (i, j))],
        out_specs=[blk, blk, blk],
        out_shape=[out, out, out],
        compiler_params=_params("parallel", "parallel"),
    )(y_attn, y_sgu, w_ab, w_sb, pg)


def _d_merged_merge_bwd(dmb, w_out, pg, a_br, s_br, gb, name):
    s, d = dmb.shape
    tm = _pick(s, MERGE_ROW_PREFS)

    def body(a_ref, b_ref, pg_ref, ab_ref, sb_ref, da_out, ds_out, dpg_out, db_out):
        @pl.when(pl.program_id(1) == 0)
        def _():
            db_out[...] = jnp.zeros_like(db_out)

        for rows in _row_chunks(tm):
            dm = lax.dot_general(a_ref[rows, :], b_ref[...], _DIMS["nt"], preferred_element_type=F32)
            ga, gs = _sigmoid(pg_ref[rows, :gb].astype(F32)), _sigmoid(pg_ref[rows, gb:].astype(F32))
            da_out[rows, :] = (dm * ga).astype(BF16)
            ds_out[rows, :] = (dm * gs).astype(BF16)
            dpa = dm * ab_ref[rows, :].astype(F32) * ga * (1.0 - ga)
            dps = dm * sb_ref[rows, :].astype(F32) * gs * (1.0 - gs)
            dpg_out[rows, :gb] = dpa.astype(BF16)
            dpg_out[rows, gb:] = dps.astype(BF16)
            db_out[:, :gb] += jnp.sum(dpa, axis=0, keepdims=True)
            db_out[:, gb:] += jnp.sum(dps, axis=0, keepdims=True)

    blk = pl.BlockSpec((tm, gb), lambda j, i: (i, j))
    pair = pl.BlockSpec((tm, 2 * gb), lambda j, i: (i, j))
    return pl.pallas_call(
        body,
        name=name,
        grid=(d // gb, s // tm),
        in_specs=[pl.BlockSpec((tm, d), lambda j, i: (i, 0)), pl.BlockSpec((gb, d), lambda j, i: (j, 0)), pair, blk, blk],
        out_specs=[blk, blk, pair, pl.BlockSpec((1, 2 * gb), lambda j, i: (0, j))],
        out_shape=[_sds((s, d), BF16), _sds((s, d), BF16), _sds((s, 2 * d), BF16), _sds((1, 2 * d), F32)],
        compiler_params=_params("parallel", "arbitrary"),
    )(dmb, w_out, pg, a_br, s_br)


def _row_chunks(tm):
    rc = _pick(tm, (FUSED_ROW_CHUNK,))
    return [slice(r, r + rc) for r in range(0, tm, rc)]


def _gate_up_swiglu(hn, w_gu, name):
    s, d = hn.shape
    n2 = w_gu.shape[1]
    fb = n2 // N_DEV
    tm = _pick(s, SWIGLU_ROW_PREFS)

    def body(a_ref, b_ref, gu_ref, act_ref):
        r = jnp.dot(a_ref[...], b_ref[...], preferred_element_type=F32)
        gu_ref[...] = r.astype(BF16)
        gate, up = r[:, :fb], r[:, fb:]
        act_ref[...] = (gate * _sigmoid(gate) * up).astype(BF16)

    return pl.pallas_call(
        body,
        name=name,
        grid=(s // tm, N_DEV // 2),
        in_specs=[pl.BlockSpec((tm, d), lambda i, j: (i, 0)), pl.BlockSpec((d, 2 * fb), lambda i, j: (0, j))],
        out_specs=[pl.BlockSpec((tm, 2 * fb), lambda i, j: (i, j)), pl.BlockSpec((tm, fb), lambda i, j: (i, j))],
        out_shape=[_sds((s, n2), BF16), _sds((s, n2 // 2), BF16)],
        compiler_params=_params("parallel", "parallel"),
    )(hn, w_gu)


def _d_act_swiglu(dhb, w_down, gu, name):
    s, d = dhb.shape
    n2 = gu.shape[1]
    fb = n2 // N_DEV
    tm = _pick(s, D_ACT_ROW_PREFS)

    def body(a_ref, b_ref, gu_ref, o_ref):
        for rows in _row_chunks(tm):
            da = lax.dot_general(a_ref[rows, :], b_ref[...], _DIMS["nt"], preferred_element_type=F32)
            gate, up = gu_ref[rows, :fb].astype(F32), gu_ref[rows, fb:].astype(F32)
            sg = _sigmoid(gate)
            o_ref[rows, :fb] = (da * up * sg * (1.0 + gate * (1.0 - sg))).astype(BF16)
            o_ref[rows, fb:] = (da * gate * sg).astype(BF16)

    pair = pl.BlockSpec((tm, 2 * fb), lambda i, j: (i, j))
    return pl.pallas_call(
        body,
        name=name,
        grid=(s // tm, N_DEV // 2),
        in_specs=[pl.BlockSpec((tm, d), lambda i, j: (i, 0)), pl.BlockSpec((fb, d), lambda i, j: (j, 0)), pair],
        out_specs=pair,
        out_shape=_sds((s, n2), BF16),
        compiler_params=_params("parallel", "parallel"),
    )(dhb, w_down, gu)


def _rope_tables(pos_col, name):
    s = pos_col.shape[0]
    tr = _pick(s, (1024, 512, 256, 128))
    inv = ROPE_THETA ** (-jnp.arange(0, ROPE_DIM, 2, dtype=F32) / ROPE_DIM)
    lane = jnp.arange(LANES)
    inv_lanes = inv[lane % ROPE_HALF].reshape(1, LANES)

    def body(i, ins, consts, outs, accs):
        ang = ins[0][...].astype(F32) * consts[0][...]
        c, sn = jnp.cos(ang), jnp.sin(ang)
        in_head = lax.broadcasted_iota(jnp.int32, ang.shape, 1) % HEAD_DIM
        outs[0][:, 0:LANES] = jnp.where(in_head < ROPE_DIM, c, 1.0)
        outs[0][:, LANES:2 * LANES] = jnp.where(in_head < ROPE_HALF, -sn, 0.0)
        outs[0][:, 2 * LANES:] = jnp.where((in_head >= ROPE_HALF) & (in_head < ROPE_DIM), sn, 0.0)

    return _rowwise(body, name, s, tr, [pos_col], [inv_lanes], [_sds((s, 3 * LANES), F32)])[0]


def _rope(x, tab, inverse=False):
    width = x.shape[1]
    reps = width // LANES
    c = jnp.tile(tab[:, 0:LANES], (1, reps))
    lo = jnp.tile(tab[:, LANES:2 * LANES], (1, reps))
    hi = jnp.tile(tab[:, 2 * LANES:], (1, reps))
    if inverse:
        lo, hi = -lo, -hi
    return x * c + pltpu.roll(x, width - ROPE_HALF, 1) * lo + pltpu.roll(x, ROPE_HALF, 1) * hi


def _attn_specs(aw, kw):
    kb = aw // kw
    prev = lambda i: jnp.maximum(i - 1, 0)
    return [
        pl.BlockSpec(memory_space=pltpu.SMEM),
        pl.BlockSpec((WINDOW, aw), lambda i: (i, 0)),
        pl.BlockSpec((WINDOW, kw), lambda i: (i, kb)),
        pl.BlockSpec((WINDOW, kw), lambda i: (prev(i), kb)),
        pl.BlockSpec((WINDOW, kw), lambda i: (i, kb + 1)),
        pl.BlockSpec((WINDOW, kw), lambda i: (prev(i), kb + 1)),
        pl.BlockSpec((WINDOW, 3 * LANES), lambda i: (i, 0)),
        pl.BlockSpec((WINDOW, 3 * LANES), lambda i: (prev(i), 0)),
    ]


def _attn_common(i, q_ref, kc_ref, kp_ref, vc_ref, vp_ref, tq_ref, tp_ref):
    tq, tp = tq_ref[...], tp_ref[...]
    qt = (_rope(q_ref[...].astype(F32), tq) * ATTN_SCALE).astype(BF16).T
    kc = _rope(kc_ref[...].astype(F32), tq)
    kp = _rope(kp_ref[...].astype(F32), tp)
    k2 = jnp.concatenate([kp, kc], axis=0).astype(BF16)
    v2 = jnp.concatenate([vp_ref[...], vc_ref[...]], axis=0)
    kj = lax.broadcasted_iota(jnp.int32, (2 * WINDOW, WINDOW), 0)
    qi = lax.broadcasted_iota(jnp.int32, (2 * WINDOW, WINDOW), 1)
    rel = qi + WINDOW - kj
    ok = (rel >= 0) & (rel < WINDOW) & ((kj >= WINDOW) | (i > 0))
    return qt, k2, v2, ok, tq, tp


def _head_probs(qt_h, kg, ok, sink):
    s = jnp.dot(kg, qt_h, preferred_element_type=F32)
    s = jnp.where(ok, s, NEG)
    m = jnp.maximum(jnp.max(s, axis=0, keepdims=True), sink)
    p = jnp.exp(s - m)
    es = jnp.exp(sink - m)
    inv = 1.0 / (jnp.sum(p, axis=0, keepdims=True) + es)
    return p * inv, es * inv


def _attn_fwd(qkv, tabs, sinks, aw, kw, name):
    s = qkv.shape[0]
    nq, nkv = aw // HEAD_DIM, kw // HEAD_DIM
    qpk = nq // nkv

    def body(s_ref, q_ref, kc_ref, kp_ref, vc_ref, vp_ref, tq_ref, tp_ref, o_ref):
        i = pl.program_id(0)
        tq, tp = tq_ref[...], tp_ref[...]
        q = (_rope(q_ref[...].astype(F32), tq) * ATTN_SCALE).astype(BF16)
        k2 = jnp.concatenate([_rope(kp_ref[...].astype(F32), tp), _rope(kc_ref[...].astype(F32), tq)], axis=0).astype(BF16)
        v2 = jnp.concatenate([vp_ref[...], vc_ref[...]], axis=0)
        qi = lax.broadcasted_iota(jnp.int32, (WINDOW, 2 * WINDOW), 0)
        kj = lax.broadcasted_iota(jnp.int32, (WINDOW, 2 * WINDOW), 1)
        rel = qi + WINDOW - kj
        ok = (rel >= 0) & (rel < WINDOW) & ((kj >= WINDOW) | (i > 0))
        for h in range(nq):
            g = h // qpk
            hs, gs = slice(h * HEAD_DIM, (h + 1) * HEAD_DIM), slice(g * HEAD_DIM, (g + 1) * HEAD_DIM)
            sc = lax.dot_general(q[:, hs], k2[:, gs], _DIMS["nt"], preferred_element_type=F32)
            sc = jnp.where(ok, sc, NEG)
            m = jnp.maximum(jnp.max(sc, axis=1, keepdims=True), s_ref[h])
            p = jnp.exp(sc - m)
            inv = 1.0 / (jnp.sum(p, axis=1, keepdims=True) + jnp.exp(s_ref[h] - m))
            o = jnp.dot((p * inv).astype(BF16), v2[:, gs], preferred_element_type=F32)
            o_ref[:, hs] = o.astype(BF16)

    return pl.pallas_call(
        body,
        name=name,
        grid=(s // WINDOW,),
        in_specs=_attn_specs(aw, kw),
        out_specs=pl.BlockSpec((WINDOW, aw), lambda i: (i, 0)),
        out_shape=_sds((s, aw), BF16),
        compiler_params=_params("parallel"),
    )(sinks, qkv, qkv, qkv, qkv, qkv, tabs, tabs)


def _attn_bwd(qkv, tabs, sinks, o, do, aw, kw, name):
    s = qkv.shape[0]
    nb = s // WINDOW
    nq, nkv = aw // HEAD_DIM, kw // HEAD_DIM
    qpk = nq // nkv

    def body(s_ref, q_ref, kc_ref, kp_ref, vc_ref, vp_ref, tq_ref, tp_ref, o_ref, do_ref,
             dq_ref, dkv_ref, ds_ref, ck_ref, cv_ref):
        i = pl.program_id(0)

        @pl.when(i == 0)
        def _():
            ck_ref[...] = jnp.zeros_like(ck_ref)
            cv_ref[...] = jnp.zeros_like(cv_ref)
            ds_ref[...] = jnp.zeros_like(ds_ref)

        qt, k2, v2, ok, tq, tp = _attn_common(i, q_ref, kc_ref, kp_ref, vc_ref, vp_ref, tq_ref, tp_ref)
        dot_t, ot = do_ref[...].T, o_ref[...].T
        k2t = k2.T
        row0 = lax.broadcasted_iota(jnp.int32, (SUBLANES, LANES), 0) == 0
        lane = lax.broadcasted_iota(jnp.int32, (SUBLANES, LANES), 1)
        dsink = jnp.zeros((SUBLANES, LANES), F32)
        dqt_parts, dk_parts, dv_parts = [], [], []
        for g in range(nkv):
            gs = slice(g * HEAD_DIM, (g + 1) * HEAD_DIM)
            kg, vg = k2[:, gs], v2[:, gs]
            dk_g = jnp.zeros((2 * WINDOW, HEAD_DIM), F32)
            dv_g = jnp.zeros((2 * WINDOW, HEAD_DIM), F32)
            for j in range(qpk):
                h = g * qpk + j
                hs = slice(h * HEAD_DIM, (h + 1) * HEAD_DIM)
                pn, psink = _head_probs(qt[hs], kg, ok, s_ref[h])
                delta = jnp.sum(dot_t[hs].astype(F32) * ot[hs].astype(F32), axis=0, keepdims=True)
                dp = jnp.dot(vg, dot_t[hs], preferred_element_type=F32)
                dsb = (pn * (dp - delta)).astype(BF16)
                dsink = dsink + jnp.where(row0 & (lane == h), -jnp.sum(psink * delta, axis=1, keepdims=True), 0.0)
                dqt_parts.append(jnp.dot(k2t[gs], dsb, preferred_element_type=F32))
                dk_g = dk_g + lax.dot_general(dsb, qt[hs], _DIMS["nt"], preferred_element_type=F32)
                dv_g = dv_g + lax.dot_general(pn.astype(BF16), dot_t[hs], _DIMS["nt"], preferred_element_type=F32)
            dk_parts.append(dk_g)
            dv_parts.append(dv_g)
        ds_ref[...] += dsink
        dq_ref[...] = _rope(jnp.concatenate(dqt_parts, axis=0).T * ATTN_SCALE, tq, inverse=True).astype(BF16)
        dk2 = jnp.concatenate(dk_parts, axis=1)
        dv2 = jnp.concatenate(dv_parts, axis=1)
        dk_prev = _rope(ck_ref[...] + dk2[:WINDOW], tp, inverse=True)
        dv_prev = cv_ref[...] + dv2[:WINDOW]

        @pl.when(i > 0)
        def _():
            dkv_ref[pl.ds(pl.multiple_of((i - 1) * WINDOW, WINDOW), WINDOW), :] = jnp.concatenate(
                [dk_prev, dv_prev], axis=1).astype(BF16)

        ck_ref[...] = dk2[WINDOW:]
        cv_ref[...] = dv2[WINDOW:]

        @pl.when(i == nb - 1)
        def _():
            dkv_ref[pl.ds(pl.multiple_of(i * WINDOW, WINDOW), WINDOW), :] = jnp.concatenate(
                [_rope(dk2[WINDOW:], tq, inverse=True), dv2[WINDOW:]], axis=1).astype(BF16)

    blk = pl.BlockSpec((WINDOW, aw), lambda i: (i, 0))
    return pl.pallas_call(
        body,
        name=name,
        grid=(nb,),
        in_specs=_attn_specs(aw, kw) + [blk, blk],
        out_specs=[blk, pl.BlockSpec((s, 2 * kw), lambda i: (0, 0)), pl.BlockSpec((SUBLANES, LANES), lambda i: (0, 0))],
        out_shape=[_sds((s, aw), BF16), _sds((s, 2 * kw), BF16), _sds((SUBLANES, LANES), F32)],
        scratch_shapes=[pltpu.VMEM((WINDOW, kw), F32), pltpu.VMEM((WINDOW, kw), F32)],
        compiler_params=_params("arbitrary"),
    )(sinks, qkv, qkv, qkv, qkv, qkv, tabs, tabs, o, do)


_INV_SQRT2 = 1.0 / math.sqrt(2.0)
_INV_SQRT2PI = 1.0 / math.sqrt(2.0 * math.pi)


def _gelu(x):
    return x * (lax.erf(x * _INV_SQRT2) + 1.0) * 0.5


def _gelu_grad(x):
    return 0.5 * (lax.erf(x * _INV_SQRT2) + 1.0) + x * jnp.exp(-0.5 * x * x) * _INV_SQRT2PI


def _sgu_norm(pv, lg, lb):
    zv = _gelu(pv)
    mu = jnp.mean(zv, axis=-1, keepdims=True)
    cen = zv - mu
    rs = lax.rsqrt(jnp.mean(cen * cen, axis=-1, keepdims=True) + EPS)
    xhat = cen * rs
    return xhat, rs, (xhat * lg + lb).astype(BF16)


def _causal(w, upper=False):
    t = lax.broadcasted_iota(jnp.int32, (CHUNK, CHUNK), 0)
    u = lax.broadcasted_iota(jnp.int32, (CHUNK, CHUNK), 1)
    return jnp.where((u >= t) if upper else (t >= u), w, 0.0).astype(BF16)


def _sgu_fwd(pz, lg, lb, w, bt, name):
    s, sw = pz.shape[0], pz.shape[1] // 2
    groups = sw // GROUP_DIM

    def body(i, ins, consts, outs, accs):
        lgv, lbv, w_ref, btv = consts[0][...], consts[1][...], consts[2], consts[3][...]
        zu = _gelu(ins[0][:, :sw].astype(F32))
        _, _, vn = _sgu_norm(ins[0][:, sw:].astype(F32), lgv, lbv)
        for g in range(groups):
            gs = slice(g * GROUP_DIM, (g + 1) * GROUP_DIM)
            sv = jnp.dot(_causal(w_ref[g]), vn[:, gs], preferred_element_type=F32) + btv[:, g:g + 1]
            outs[0][:, gs] = (zu[:, gs] * sv).astype(BF16)

    return _rowwise(body, name, s, CHUNK, [pz], [lg, lb, w, bt], [_sds((s, sw), BF16)])[0]


def _sgu_bwd(pz, dy, lg, lb, w, wt, bt, name, after=()):
    s, sw = pz.shape[0], pz.shape[1] // 2
    groups = sw // GROUP_DIM

    def body(i, ins, consts, outs, accs):
        lgv, lbv, w_ref, wt_ref, btv = consts[0][...], consts[1][...], consts[2], consts[3], consts[4][...]

        @pl.when(i == 0)
        def _():
            for a in accs:
                a[...] = jnp.zeros_like(a)

        pu, pv = ins[0][:, :sw].astype(F32), ins[0][:, sw:].astype(F32)
        dyv = ins[1][...].astype(F32)
        zu = _gelu(pu)
        xhat, rs, vn = _sgu_norm(pv, lgv, lbv)
        dvn_parts, db_parts = [], []
        lower = lax.broadcasted_iota(jnp.int32, (CHUNK, CHUNK), 0) >= lax.broadcasted_iota(jnp.int32, (CHUNK, CHUNK), 1)
        for g in range(groups):
            gs = slice(g * GROUP_DIM, (g + 1) * GROUP_DIM)
            sv = jnp.dot(_causal(w_ref[g]), vn[:, gs], preferred_element_type=F32) + btv[:, g:g + 1]
            dpu = dyv[:, gs] * sv * _gelu_grad(pu[:, gs])
            outs[0][:, gs] = dpu.astype(BF16)
            accs[4][:, gs] += jnp.sum(dpu, axis=0, keepdims=True)
            dsv = dyv[:, gs] * zu[:, gs]
            dsvb = dsv.astype(BF16)
            db_parts.append(jnp.sum(dsv, axis=1, keepdims=True))
            accs[2][g] += jnp.where(lower, lax.dot_general(dsvb, vn[:, gs], _DIMS["nt"], preferred_element_type=F32), 0.0)
            dvn_parts.append(jnp.dot(_causal(wt_ref[g], upper=True), dsvb, preferred_element_type=F32))
        dvn = jnp.concatenate(dvn_parts, axis=1)
        accs[3][...] += jnp.concatenate(db_parts, axis=1)
        accs[0][...] += jnp.sum(dvn * xhat, axis=0, keepdims=True)
        accs[1][...] += jnp.sum(dvn, axis=0, keepdims=True)
        dxh = dvn * lgv
        dz = rs * (dxh - jnp.mean(dxh, axis=-1, keepdims=True) - xhat * jnp.mean(dxh * xhat, axis=-1, keepdims=True))
        dpv = dz * _gelu_grad(pv)
        outs[0][:, sw:] = dpv.astype(BF16)
        accs[4][:, sw:] += jnp.sum(dpv, axis=0, keepdims=True)

    return _rowwise(body, name, s, CHUNK, [pz, dy], [lg, lb, w, wt, bt], [_sds((s, 2 * sw), BF16)],
                    [_sds((1, sw), F32), _sds((1, sw), F32), _sds((groups, CHUNK, CHUNK), F32), _sds((CHUNK, groups), F32),
                     _sds((1, 2 * sw), F32)],
                    after=after)


def _mesh_place():
    x, y, c = lax.axis_index("x"), lax.axis_index("y"), lax.axis_index("c")
    return x, y, c, 4 * x + 2 * y + c


def _peer(x, y, c, k):
    px, py, pc = x ^ ((k >> 2) & 1), y ^ ((k >> 1) & 1), c ^ (k & 1)
    return (px, py, pc), 4 * px + 2 * py + pc


BY_SLOTS, BY_COLS, BY_PAIRED_COLS = 0, 1, 2


def _col_block(ref, idx, width, cols):
    if cols == BY_PAIRED_COLS:
        idx = (idx % (N_DEV // 2)) * 2 + idx // (N_DEV // 2)
    return ref.at[:, pl.ds(pl.multiple_of(idx * width, LANES), width)]


def _exchange_copy(src_ref, land_ref, send_sems, recv_sems, k, place, scatter, arriving, cols):
    x, y, c, me = place
    peer, pidx = _peer(x, y, c, k)
    slot = pidx if arriving else me
    if scatter:
        src = _col_block(src_ref, pidx, land_ref.shape[-1], cols) if cols else src_ref.at[pidx]
        dst = land_ref.at[slot]
    else:
        src = src_ref
        dst = _col_block(land_ref, slot, src_ref.shape[-1], cols) if cols else land_ref.at[slot]
    return pltpu.make_async_remote_copy(
        src_ref=src, dst_ref=dst, send_sem=send_sems[k - 1], recv_sem=recv_sems[k - 1], device_id=peer,
        device_id_type=MESH_TYPE)


def _own_copy(src_ref, land_ref, sem, place, scatter, cols):
    me = place[3]
    if scatter:
        src = _col_block(src_ref, me, land_ref.shape[-1], cols) if cols else src_ref.at[me]
        dst = land_ref.at[me]
    else:
        src = src_ref
        dst = _col_block(land_ref, me, src_ref.shape[-1], cols) if cols else land_ref.at[me]
    return pltpu.make_async_copy(src, dst, sem)


N_PEERS = N_DEV - 1
N_EXCHANGE_SEMS = 2 * N_PEERS + 1


def _land_shape(a, scatter, cols):
    if scatter:
        return (N_DEV, a.shape[0], a.shape[1] // N_DEV) if cols else a.shape
    return (a.shape[0], N_DEV * a.shape[1]) if cols else (N_DEV,) + a.shape


def _exchange_start(srcs, scatter, cols, name, after):
    n = len(srcs)
    land_shapes = [_land_shape(a, scatter, cl) for a, cl in zip(srcs, cols)]

    def body(*refs):
        src, land = refs[:n], refs[n:2 * n]
        send_sems = refs[2 * n + 1:2 * n + 1 + N_PEERS]
        recv_sems = refs[2 * n + 1 + N_PEERS:2 * n + 1 + 2 * N_PEERS]
        own_sem = refs[2 * n + 1 + 2 * N_PEERS]
        token = refs[-1]
        place = _mesh_place()
        for t in range(n):
            for k in range(1, N_DEV):
                _exchange_copy(src[t], land[t], send_sems, recv_sems, k, place, scatter, False, cols[t]).start()
            _own_copy(src[t], land[t], own_sem, place, scatter, cols[t]).start()
        token[...] = jnp.zeros_like(token)

    return pl.pallas_call(
        body,
        name=name,
        out_shape=(*[pltpu.SemaphoreType.DMA(())] * N_EXCHANGE_SEMS, *[pltpu.HBM(a.shape, a.dtype) for a in srcs],
                   *[pltpu.HBM(shp, a.dtype) for shp, a in zip(land_shapes, srcs)], _sds((SUBLANES, LANES), F32)),
        in_specs=[HBM] * (2 * n) + [ANY],
        out_specs=(*[SEM] * N_EXCHANGE_SEMS, *[HBM] * (2 * n), pl.BlockSpec(memory_space=pltpu.VMEM)),
        input_output_aliases={i: N_EXCHANGE_SEMS + i for i in range(2 * n)},
        compiler_params=pltpu.CompilerParams(has_side_effects=DATAFLOW_EFFECT),
    )(*[pltpu.with_memory_space_constraint(a, pltpu.HBM) for a in srcs],
      *[pltpu.with_memory_space_constraint(lax.empty(shp, a.dtype), pltpu.HBM) for shp, a in zip(land_shapes, srcs)],
      after)


def _exchange_wait(started, after, scatter, cols, name):
    sems = started[:N_EXCHANGE_SEMS]
    thru = started[N_EXCHANGE_SEMS:-1]
    n = len(thru) // 2

    def body(*refs):
        src, land = refs[:n], refs[n:2 * n]
        send_sems = refs[2 * n:2 * n + N_PEERS]
        recv_sems = refs[2 * n + N_PEERS:2 * n + 2 * N_PEERS]
        own_sem = refs[2 * n + 2 * N_PEERS]
        place = _mesh_place()
        for t in range(n):
            for k in range(1, N_DEV):
                cp = _exchange_copy(src[t], land[t], send_sems, recv_sems, k, place, scatter, True, cols[t])
                cp.wait_send()
                cp.wait_recv()
            _own_copy(src[t], land[t], own_sem, place, scatter, cols[t]).wait()

    out = pl.pallas_call(
        body,
        name=name,
        out_shape=tuple(pltpu.HBM(a.shape, a.dtype) for a in thru),
        in_specs=[HBM] * (2 * n) + [SEM] * N_EXCHANGE_SEMS + [ANY],
        out_specs=tuple([HBM] * (2 * n)),
        input_output_aliases={i: i for i in range(2 * n)},
        compiler_params=pltpu.CompilerParams(has_side_effects=DATAFLOW_EFFECT),
    )(*thru, *sems, after)
    return out[:n], out[n:]


def _adamw(w, g, m, v):
    m = ADAM_B1 * m + (1.0 - ADAM_B1) * g
    v = ADAM_B2 * v + (1.0 - ADAM_B2) * (g * g)
    m_hat = m / (1.0 - ADAM_B1 ** ADAM_STEP)
    v_hat = v / (1.0 - ADAM_B2 ** ADAM_STEP)
    delta = -ADAM_LR * (m_hat / (jnp.sqrt(v_hat) + ADAM_EPS) + ADAM_WD * w)
    return delta, m, v


def _adam_rows(r, c):
    fits = [t for t in range(BF16_SUBLANES, r + 1, BF16_SUBLANES) if r % t == 0 and t * c <= ADAM_BLOCK_ELEMS]
    return max(fits) if fits else r


def _adam_body(p_ref, w_ref, m_ref, v_ref, g_out, d_out, m_out, v_out):
    g = p_ref[0].astype(F32)
    for d in range(1, N_DEV):
        g = g + p_ref[d].astype(F32)
    delta, mn, vn = _adamw(w_ref[...], g, m_ref[...], v_ref[...])
    g_out[...] = g
    d_out[...] = delta
    m_out[...] = mn
    v_out[...] = vn


def _reduce_adam_layer(parts, w, m, v, prev, layer, name):
    nl, r, c = w.shape
    tr = _adam_rows(r, c)
    if prev is None:
        prev = [lax.empty((nl, r, c), F32) for _ in range(4)]

    def body(p_ref, w_ref, m_ref, v_ref, *rest):
        _adam_body(p_ref, w_ref, m_ref, v_ref, *rest[4:])

    blk = pl.BlockSpec((None, tr, c), lambda i: (layer, i, 0))
    out = _sds((nl, r, c), F32)
    return pl.pallas_call(
        body,
        name=name,
        grid=(r // tr,),
        in_specs=[pl.BlockSpec((N_DEV, tr, c), lambda i: (0, i, 0)), blk, blk, blk, ANY, ANY, ANY, ANY],
        out_specs=[blk, blk, blk, blk],
        out_shape=[out, out, out, out],
        input_output_aliases={4: 0, 5: 1, 6: 2, 7: 3},
        compiler_params=_params("parallel"),
    )(parts, w, m, v, *prev)


def _reduce_adam(parts, w, m, v, name):
    nl, _, r, c = parts.shape
    tr = _adam_rows(r, c)

    def body(*refs):
        _adam_body(*refs)

    blk = pl.BlockSpec((None, tr, c), lambda l, i: (l, i, 0))
    out = _sds((nl, r, c), F32)
    return pl.pallas_call(
        body,
        name=name,
        grid=(nl, r // tr),
        in_specs=[pl.BlockSpec((None, N_DEV, tr, c), lambda l, i: (l, 0, i, 0)), blk, blk, blk],
        out_specs=[blk, blk, blk, blk],
        out_shape=[out, out, out, out],
        compiler_params=_params("parallel", "parallel"),
    )(parts, w, m, v)


def _pack(arrays):
    flat = []
    for a in arrays:
        a = a.reshape(-1).astype(F32)
        flat.append(jnp.pad(a, (0, (-a.shape[0]) % PACK_UNIT)))
    out = jnp.concatenate(flat)
    rows = out.shape[0] // LANES
    pad_rows = (-rows) % 512
    return jnp.pad(out, (0, pad_rows * LANES)).reshape(rows + pad_rows, LANES)


def _unpack(packed, shapes):
    flat = packed.reshape(-1)
    out, off = [], 0
    for shp in shapes:
        size = math.prod(shp)
        out.append(flat[off:off + size].reshape(shp))
        off += size + (-size) % PACK_UNIT
    return out


def _in_runs(d, qkv_w, sw, gb):
    g0 = qkv_w + 2 * sw
    runs = [(0, qkv_w, "qkv", 0), (qkv_w, 2 * sw, "z", 0)]
    for j in range(d // gb):
        runs.append((g0 + j * gb, gb, "g", 2 * j * gb))
        runs.append((g0 + d + j * gb, gb, "g", (2 * j + 1) * gb))
    return runs


def _pieces_from_global(take, runs, axis=-1):
    out = {}
    for piece in ("qkv", "z", "g"):
        own = sorted((r for r in runs if r[2] == piece), key=lambda r: r[3])
        parts = [take(g, g + w) for g, w, _, _ in own]
        out[piece] = parts[0] if len(parts) == 1 else jnp.concatenate(parts, axis=axis)
    return out


def _global_from_pieces(pieces, runs, axis=-1):
    segs = [lax.slice_in_dim(pieces[piece], start, start + w, axis=axis % pieces[piece].ndim)
            for _, w, piece, start in sorted(runs)]
    return jnp.concatenate(segs, axis=axis)


def _to_full_cols(g):
    d, k, n = g.shape
    return jnp.transpose(g, (1, 0, 2)).reshape(k, d * n)


def _to_col_shards(a):
    k, n = a.shape
    return jnp.transpose(a.reshape(k, N_DEV, n // N_DEV), (1, 0, 2))


def kernel(x, positions, norm1_g, w_in, b_in, sinks, sgu_ln_g, sgu_ln_b, sgu_w, sgu_b, w_attn_branch, w_sgu_branch, w_out, norm2_g, w_gate_up, w_down, final_g, loss_target, m_norm1_g, m_w_in, m_b_in, m_sinks, m_sgu_ln_g, m_sgu_ln_b, m_sgu_w, m_sgu_b, m_w_attn_branch, m_w_sgu_branch, m_w_out, m_norm2_g, m_w_gate_up, m_w_down, m_final_g, v_norm1_g, v_w_in, v_b_in, v_sinks, v_sgu_ln_g, v_sgu_ln_b, v_sgu_w, v_sgu_b, v_w_attn_branch, v_w_sgu_branch, v_w_out, v_norm2_g, v_w_gate_up, v_w_down, v_final_g):
    nl = w_in.shape[0]
    s, d = x.shape[1], x.shape[2]
    aw = w_attn_branch.shape[1]
    sw = w_sgu_branch.shape[1]
    in_w = w_in.shape[2] * N_DEV
    kw = (in_w - aw - 2 * sw - 2 * d) // 2
    qkv_w = aw + 2 * kw
    groups = sw // GROUP_DIM
    ff = w_down.shape[1] * N_DEV

    h = x.reshape(s, d)
    target = loss_target.reshape(s, d)
    tabs = _rope_tables(positions.reshape(s, 1), "rope_tables")

    transposed = lambda a: jnp.swapaxes(a, 1, 2)
    big = [transposed(w_in), w_attn_branch, w_sgu_branch, w_out, w_gate_up, w_down]
    big_m = [transposed(m_w_in), m_w_attn_branch, m_w_sgu_branch, m_w_out, m_w_gate_up, m_w_down]
    big_v = [transposed(v_w_in), v_w_attn_branch, v_w_sgu_branch, v_w_out, v_w_gate_up, v_w_down]
    big_names = ("w_in", "w_attn_branch", "w_sgu_branch", "w_out", "w_gate_up", "w_down")
    W_IN, W_AB, W_SB, W_OUT, W_GU, W_DOWN = range(6)
    weight_groups = ((W_IN,), (W_AB, W_SB, W_OUT), (W_GU, W_DOWN))
    grad_groups = ((W_DOWN, W_GU), (W_OUT, W_AB, W_SB), (W_IN,))

    col_sharded = (W_AB, W_SB, W_GU)
    by_cols = [BY_COLS if t in col_sharded and big[t].shape[2] % LANES == 0 else BY_SLOTS for t in range(6)]
    assert by_cols[W_GU] == BY_COLS, "the fused swiglu kernels need gate/up column blocks of whole lane tiles"
    by_cols[W_GU] = BY_PAIRED_COLS

    def start_gather(l, group, after):
        return _exchange_start([big[t][l].astype(BF16) for t in group], False, tuple(by_cols[t] for t in group),
                               f"gather_start_l{l}_{big_names[group[0]]}", after)

    def full_weight(t, land):
        if by_cols[t]:
            return land
        if t in col_sharded:
            return _to_full_cols(land)
        whole = land.reshape(N_DEV * land.shape[1], land.shape[2])
        return _pieces_from_global(lambda a, b: whole[a:b], in_runs, axis=0) if t == W_IN else whole

    gate_block = _pick(d, GATE_BLOCK_PREFS)
    in_runs = _in_runs(d, qkv_w, sw, gate_block)

    saved = []
    started = {}
    token = h
    for l in range(nl):
        for ll in ((0, 1) if l == 0 else (l + 1,)):
            if ll < nl:
                for group in weight_groups:
                    started[(ll, group)] = start_gather(ll, group, token)
                    token = started[(ll, group)][-1]
        gathered = {}

        def weight(t, after, l=l, gathered=gathered):
            if t not in gathered:
                group = next(g for (ll, g) in started if ll == l and t in g)
                srcs, lands = _exchange_wait(started.pop((l, group)), after, False, tuple(by_cols[tt] for tt in group),
                                             f"gather_wait_l{l}_{big_names[group[0]]}")
                for tt, ld in zip(group, lands):
                    gathered[tt] = full_weight(tt, ld)
            return gathered[t]

        bias = b_in[l].reshape(1, in_w)
        g1, g2 = norm1_g[l].reshape(1, d), norm2_g[l].reshape(1, d)
        lg, lb = sgu_ln_g[l].reshape(1, sw), sgu_ln_b[l].reshape(1, sw)
        bt = sgu_b[l].T

        xn = _rms_fwd(h, g1, "rms1_fwd", after=(token,))
        wts = dict(weight(W_IN, xn))
        biases = _pieces_from_global(lambda a, b: bias[:, a:b], in_runs)
        qkv = _matmul(xn, wts["qkv"], "nt", BF16, "proj_qkv", bias=biases["qkv"])
        pz = _matmul(xn, wts["z"], "nt", BF16, "proj_z", bias=biases["z"])
        pg = _matmul(xn, wts["g"], "nt", BF16, "proj_g", bias=biases["g"])
        y_attn = _attn_fwd(qkv, tabs, sinks[l], aw, kw, "attn_fwd")
        y_sgu = _sgu_fwd(pz, lg, lb, sgu_w[l], bt, "sgu_fwd")
        wts.update(ab=weight(W_AB, y_sgu), sb=weight(W_SB, y_sgu), out=weight(W_OUT, y_sgu))
        a_br, s_br, merged = _branches_merge(y_attn, y_sgu, wts["ab"], wts["sb"], pg, gate_block, "branches_merge")
        h_mid, hn = _out_proj_rms(merged, wts["out"], h, g2, "out_proj")
        wts.update(gu=weight(W_GU, hn), down=weight(W_DOWN, hn))
        gu, act = _gate_up_swiglu(hn, wts["gu"], "gate_up")
        h_out = _matmul(act, wts["down"], "nn", F32, "down_proj", res=h_mid)
        saved.append(dict(wts=wts, h=h, xn=xn, qkv=qkv, pz=pz, pg=pg, y_attn=y_attn, y_sgu=y_sgu, a_br=a_br,
                          s_br=s_br, merged=merged, h_mid=h_mid, hn=hn, gu=gu, act=act,
                          g1=g1, g2=g2, lg=lg, lb=lb, bt=bt))
        h = h_out

    dh, dhb, d_final_g, loss_blk = _loss_head(h, final_g.reshape(1, d), target, "loss_head")

    small = {n: [None] * nl for n in ("norm1_g", "b_in", "sinks", "sgu_ln_g", "sgu_ln_b", "sgu_w", "sgu_b", "norm2_g")}
    scattering = {}

    def start_scatter(l, group, grads, after):
        sends = []
        for t, dw in zip(group, grads):
            if t == W_IN:
                dw = _global_from_pieces(dw, in_runs, axis=0)
            if by_cols[t]:
                sends.append(dw)
            elif t in col_sharded:
                sends.append(_to_col_shards(dw))
            else:
                sends.append(dw.reshape(N_DEV, dw.shape[0] // N_DEV, dw.shape[1]))
        scattering[(l, group)] = _exchange_start(sends, True, tuple(by_cols[t] for t in group),
                                                 f"scatter_start_l{l}_{big_names[group[0]]}", after)
        return scattering[(l, group)][-1]

    for l in reversed(range(nl)):
        sv = saved[l]
        wts = sv["wts"]
        d_gu = _d_act_swiglu(dhb, wts["down"], sv["gu"], "d_act")
        dw_down = _matmul(sv["act"], dhb, "tn", BF16, "dw_down")
        dw_gu = _matmul(sv["hn"], d_gu, "tn", BF16, "dw_gate_up")
        token = start_scatter(l, grad_groups[0], [dw_down, dw_gu], token)
        d_hn = _matmul(d_gu, wts["gu"], "nt", BF16, "d_hn", after=token)
        dh_mid, dmb, dg2 = _rms_bwd(sv["h_mid"], sv["g2"], d_hn, dh, "rms2_bwd")
        d_a, d_s, d_pg, db_g = _d_merged_merge_bwd(dmb, wts["out"], sv["pg"], sv["a_br"], sv["s_br"], gate_block, "d_merged")
        dw_out = _matmul(sv["merged"], dmb, "tn", BF16, "dw_out")
        d_y_attn = _matmul(d_a, wts["ab"], "nt", BF16, "d_y_attn")
        dw_ab = _matmul(sv["y_attn"], d_a, "tn", BF16, "dw_attn_branch")
        d_y_sgu = _matmul(d_s, wts["sb"], "nt", BF16, "d_y_sgu")
        dw_sb = _matmul(sv["y_sgu"], d_s, "tn", BF16, "dw_sgu_branch")
        token = start_scatter(l, grad_groups[1], [dw_out, dw_ab, dw_sb], token)
        d_pz, d_lg, d_lb, d_sw, d_sbt, db_z = _sgu_bwd(sv["pz"], d_y_sgu, sv["lg"], sv["lb"], sgu_w[l],
                                                 jnp.transpose(sgu_w[l], (0, 2, 1)), sv["bt"], "sgu_bwd", after=(token,))
        d_q, d_kv, d_sinks = _attn_bwd(sv["qkv"], tabs, sinks[l], sv["y_attn"], d_y_attn, aw, kw, "attn_bwd")
        d_qkv = jnp.concatenate([d_q, d_kv], axis=1)
        dw_qkv = _matmul(d_qkv, sv["xn"], "tn", BF16, "dw_qkv")
        dw_z = _matmul(d_pz, sv["xn"], "tn", BF16, "dw_z")
        dw_g = _matmul(d_pg, sv["xn"], "tn", BF16, "dw_g")
        token = start_scatter(l, grad_groups[2], [dict(qkv=dw_qkv, z=dw_z, g=dw_g)], token)
        d_xn = _matmul(d_qkv, wts["qkv"], "nn", F32, "d_xn_qkv", after=token)
        d_xn = _matmul(d_pz, wts["z"], "nn", F32, "d_xn_z", res=d_xn)
        d_xn = _matmul(d_pg, wts["g"], "nn", F32, "d_xn_g", res=d_xn)
        dh, dhb, dg1 = _rms_bwd(sv["h"], sv["g1"], d_xn, dh_mid, "rms1_bwd")

        small["norm1_g"][l], small["norm2_g"][l] = dg1, dg2
        small["b_in"][l] = _global_from_pieces(dict(qkv=_colsum(d_qkv, "db_qkv"), z=db_z, g=db_g), in_runs)
        small["sinks"][l] = d_sinks[0, :aw // HEAD_DIM]
        small["sgu_ln_g"][l], small["sgu_ln_b"][l] = d_lg, d_lb
        small["sgu_w"][l] = d_sw
        small["sgu_b"][l] = d_sbt.T

    grad_x = dh.reshape(x.shape)

    names = ["norm1_g", "b_in", "sinks", "sgu_ln_g", "sgu_ln_b", "sgu_w", "sgu_b", "norm2_g"]
    small_w = [norm1_g, b_in, sinks, sgu_ln_g, sgu_ln_b, sgu_w, sgu_b, norm2_g, final_g]
    small_m = [m_norm1_g, m_b_in, m_sinks, m_sgu_ln_g, m_sgu_ln_b, m_sgu_w, m_sgu_b, m_norm2_g, m_final_g]
    small_v = [v_norm1_g, v_b_in, v_sinks, v_sgu_ln_g, v_sgu_ln_b, v_sgu_w, v_sgu_b, v_norm2_g, v_final_g]
    shapes = [w.shape for w in small_w] + [(1,)]
    partial = [jnp.stack([p.reshape(w.shape[1:]) for p in small[n]]) for n, w in zip(names, small_w)]
    partial += [d_final_g.reshape(final_g.shape), loss_blk[0, :1]]
    zero = jnp.zeros((1,), F32)
    small_started = _exchange_start([_pack(partial)], False, (False,), "gather_start_small_grads", dhb)

    big_out = [None] * len(big)
    after = small_started[-1]
    for l in reversed(range(nl)):
        for group in grad_groups:
            srcs, lands = _exchange_wait(scattering.pop((l, group)), after, True, tuple(by_cols[t] for t in group),
                                         f"scatter_wait_l{l}_{big_names[group[0]]}")
            for t, parts in zip(group, lands):
                big_out[t] = _reduce_adam_layer(parts, big[t], big_m[t], big_v[t], big_out[t], l, f"adam_{big_names[t]}")
                after = big_out[t][0]

    srcs, lands = _exchange_wait(small_started, after, False, (False,), "gather_wait_small_grads")
    sm = _reduce_adam(lands[0][None], _pack(small_w + [zero])[None], _pack(small_m + [zero])[None],
                      _pack(small_v + [zero])[None], "adam_small")
    sm_g, sm_d, sm_m, sm_v = [_unpack(a[0], shapes) for a in sm]
    loss = sm_g[-1].reshape(())

    def ordered(kind_small, kind_big):
        by_name = dict(zip(["norm1_g", "b_in", "sinks", "sgu_ln_g", "sgu_ln_b", "sgu_w", "sgu_b", "norm2_g", "final_g"], kind_small))
        by_name.update(zip(["w_in", "w_attn_branch", "w_sgu_branch", "w_out", "w_gate_up", "w_down"], kind_big))
        order = ["norm1_g", "w_in", "b_in", "sinks", "sgu_ln_g", "sgu_ln_b", "sgu_w", "sgu_b", "w_attn_branch",
                 "w_sgu_branch", "w_out", "norm2_g", "w_gate_up", "w_down", "final_g"]
        return [by_name[n] for n in order]

    big_out[W_IN] = [transposed(o) for o in big_out[W_IN]]
    outs = [loss, grad_x]
    for idx, sm_kind in enumerate((sm_g, sm_d, sm_m, sm_v)):
        outs += ordered(sm_kind[:-1], [o[idx] for o in big_out])
    return tuple(outs)
```

```python
import math

import jax
import jax.numpy as jnp
from jax import lax
from jax.experimental import pallas as pl
from jax.experimental.pallas import tpu as pltpu

F32 = jnp.float32
BF16 = jnp.bfloat16

N_DEV = 8
HEAD_DIM = 64
WINDOW = 128
CHUNK = 128
GROUP_DIM = 128
ROPE_DIM = HEAD_DIM // 4
ROPE_HALF = ROPE_DIM // 2
ROPE_THETA = 500000.0
EPS = 1e-5
NEG = -1e30
ATTN_SCALE = HEAD_DIM ** -0.5
ADAM_LR = 0.001
ADAM_B1 = 0.9
ADAM_B2 = 0.999
ADAM_EPS = 1e-08
ADAM_WD = 0.01
ADAM_STEP = 10
LANES = 128
SUBLANES = 8
BF16_SUBLANES = 16
PACK_UNIT = SUBLANES * LANES
ADAM_BLOCK_ELEMS = 256 * 1024
V7X_VMEM_LIMIT_BYTES = 56 * 1024 * 1024
MATMUL_TILE_PREFS = (1024, 1408, 768, 512, 384, 256, 128)
MATMUL_WHOLE_K = 2048
MATMUL_TN_K = 4096
MATMUL_VMEM_BUDGET_BYTES = 52 * 1024 * 1024
MATMUL_K_PREFS = (2816, 2048, 1536, 1408, 1024, 768, 512, 384, 256, 128)
ROW_TILE_PREFS = (512, 256, 128)
SWIGLU_ROW_PREFS = (512, 256, 128)
D_ACT_ROW_PREFS = (1024, 512, 256, 128)
OUT_PROJ_ROW_PREFS = (512, 256, 128)
MERGE_ROW_PREFS = (1024, 512, 256, 128)
GATE_BLOCK_PREFS = (1024, 512, 256, 128)
FUSED_ROW_CHUNK = 256
MESH_TYPE = pl.DeviceIdType.MESH
ANY = pl.BlockSpec(memory_space=pl.ANY)
HBM = pl.BlockSpec(memory_space=pltpu.HBM)
SEM = pl.BlockSpec(memory_space=pltpu.SEMAPHORE)
DATAFLOW_EFFECT = pltpu.SideEffectType.DATAFLOW_SIDE_EFFECTING


def _pick(n, prefs):
    for p in prefs:
        if n % p == 0:
            return p
    return n


def _params(*sem):
    return pltpu.CompilerParams(dimension_semantics=sem, vmem_limit_bytes=V7X_VMEM_LIMIT_BYTES)


_DIMS = {"nn": (((1,), (0,)), ((), ())), "nt": (((1,), (1,)), ((), ())), "tn": (((0,), (0,)), ((), ()))}


def _matmul(a, b, mode, out_dtype, name, bias=None, res=None, after=None):
    if mode == "nn":
        (m, k), n = a.shape, b.shape[1]
    elif mode == "nt":
        (m, k), n = a.shape, b.shape[0]
    else:
        (k, m), n = a.shape, b.shape[1]
    tm, tn = _pick(m, MATMUL_TILE_PREFS), _pick(n, MATMUL_TILE_PREFS)
    if k <= MATMUL_WHOLE_K:
        tk = k
    else:
        fits = [t for t in ((MATMUL_TN_K,) if mode == "tn" else ()) + MATMUL_K_PREFS
                if k % t == 0 and 4 * t * (tm + tn) + 16 * tm * tn <= MATMUL_VMEM_BUDGET_BYTES]
        tk = fits[0]
    nk = k // tk
    dims = _DIMS[mode]
    a_spec = pl.BlockSpec((tk, tm), lambda i, j, kk: (kk, i)) if mode == "tn" else pl.BlockSpec((tm, tk), lambda i, j, kk: (i, kk))
    b_spec = pl.BlockSpec((tn, tk), lambda i, j, kk: (j, kk)) if mode == "nt" else pl.BlockSpec((tk, tn), lambda i, j, kk: (kk, j))
    in_specs, args = [a_spec, b_spec], [a, b]
    if bias is not None:
        in_specs.append(pl.BlockSpec((1, tn), lambda i, j, kk: (0, j)))
        args.append(bias)
    if res is not None:
        in_specs.append(pl.BlockSpec((tm, tn), lambda i, j, kk: (i, j)))
        args.append(res)
    if after is not None:
        in_specs.append(ANY)
        args.append(after)

    def body(*refs):
        a_ref, b_ref = refs[0], refs[1]
        pos = 2
        bias_ref = res_ref = None
        if bias is not None:
            bias_ref = refs[pos]
            pos += 1
        if res is not None:
            res_ref = refs[pos]
            pos += 1
        if after is not None:
            pos += 1
        o_ref = refs[pos]

        def finish(r):
            if bias_ref is not None:
                r = r + bias_ref[...]
            if res_ref is not None:
                r = r + res_ref[...]
            o_ref[...] = r.astype(out_dtype)

        part = lax.dot_general(a_ref[...], b_ref[...], dims, preferred_element_type=F32)
        if nk == 1:
            finish(part)
        else:
            acc_ref = refs[pos + 1]
            kk = pl.program_id(2)

            @pl.when(kk == 0)
            def _():
                acc_ref[...] = part

            @pl.when((kk > 0) & (kk < nk - 1))
            def _():
                acc_ref[...] += part

            @pl.when(kk == nk - 1)
            def _():
                finish(acc_ref[...] + part)

    return pl.pallas_call(
        body,
        name=name,
        grid=(m // tm, n // tn, nk),
        in_specs=in_specs,
        out_specs=pl.BlockSpec((tm, tn), lambda i, j, kk: (i, j)),
        out_shape=jax.ShapeDtypeStruct((m, n), out_dtype),
        scratch_shapes=[] if nk == 1 else [pltpu.VMEM((tm, tn), F32)],
        compiler_params=_params("parallel", "parallel", "arbitrary"),
    )(*args)


def _out_proj_rms(merged, w_out, h, g, name):
    s, d = h.shape
    tm = _pick(s, OUT_PROJ_ROW_PREFS)

    def body(a_ref, b_ref, h_ref, g_ref, o_ref, n_ref):
        for rows in _row_chunks(tm):
            r = h_ref[rows, :] + jnp.dot(a_ref[rows, :], b_ref[...], preferred_element_type=F32)
            o_ref[rows, :] = r
            rs = lax.rsqrt(jnp.mean(r * r, axis=-1, keepdims=True) + EPS)
            n_ref[rows, :] = (r * rs * g_ref[...]).astype(BF16)

    row = pl.BlockSpec((tm, d), lambda i: (i, 0))
    return pl.pallas_call(
        body,
        name=name,
        grid=(s // tm,),
        in_specs=[row, pl.BlockSpec((d, d), lambda i: (0, 0)), row, pl.BlockSpec((1, d), lambda i: (0, 0))],
        out_specs=[row, row],
        out_shape=[_sds((s, d), F32), _sds((s, d), BF16)],
        compiler_params=_params("parallel"),
    )(merged, w_out, h, g)


def _rowwise(body, name, rows, tr, ins, consts, outs, accs=(), after=()):
    n_in, n_c, n_o, n_a = len(ins), len(consts), len(outs), len(after)

    def wrapped(*refs):
        body(pl.program_id(0), refs[:n_in], refs[n_in:n_in + n_c], refs[n_in + n_c + n_a:n_in + n_c + n_a + n_o],
             refs[n_in + n_c + n_a + n_o:])

    def whole(shape):
        zeros = (0,) * len(shape)
        return pl.BlockSpec(tuple(shape), lambda i: zeros)

    in_specs = ([pl.BlockSpec((tr, a.shape[1]), lambda i: (i, 0)) for a in ins] + [whole(c.shape) for c in consts]
                + [ANY] * n_a)
    out_specs = [pl.BlockSpec((tr, o.shape[1]), lambda i: (i, 0)) for o in outs] + [whole(a.shape) for a in accs]
    return pl.pallas_call(
        wrapped,
        name=name,
        grid=(rows // tr,),
        in_specs=in_specs,
        out_specs=out_specs,
        out_shape=list(outs) + list(accs),
        compiler_params=_params("arbitrary" if accs else "parallel"),
    )(*ins, *consts, *after)


def _sds(shape, dtype):
    return jax.ShapeDtypeStruct(tuple(shape), dtype)


def _rms_fwd(h, g, name, after=()):
    s, d = h.shape
    tr = _pick(s, ROW_TILE_PREFS)

    def body(i, ins, consts, outs, accs):
        x = ins[0][...]
        r = lax.rsqrt(jnp.mean(x * x, axis=-1, keepdims=True) + EPS)
        outs[0][...] = (x * r * consts[0][...]).astype(BF16)

    return _rowwise(body, name, s, tr, [h], [g], [_sds((s, d), BF16)], after=after)[0]


def _rms_bwd(h, g, dy, dh_up, name):
    s, d = h.shape
    tr = _pick(s, ROW_TILE_PREFS)

    def body(i, ins, consts, outs, accs):
        x, dyv, up = ins[0][...], ins[1][...].astype(F32), ins[2][...]
        r = lax.rsqrt(jnp.mean(x * x, axis=-1, keepdims=True) + EPS)
        xr = x * r
        gy = dyv * consts[0][...]
        dx = r * (gy - xr * jnp.mean(gy * xr, axis=-1, keepdims=True))
        outs[0][...] = up + dx
        outs[1][...] = (up + dx).astype(BF16)

        @pl.when(i == 0)
        def _():
            accs[0][...] = jnp.zeros_like(accs[0])

        accs[0][...] += jnp.sum(dyv * xr, axis=0, keepdims=True)

    return _rowwise(body, name, s, tr, [h, dy, dh_up], [g], [_sds((s, d), F32), _sds((s, d), BF16)], [_sds((1, d), F32)])


def _loss_head(h, g, target, name):
    s, d = h.shape
    tr = _pick(s, ROW_TILE_PREFS)

    def body(i, ins, consts, outs, accs):
        x, t = ins[0][...], ins[1][...]
        gv = consts[0][...]
        r = lax.rsqrt(jnp.mean(x * x, axis=-1, keepdims=True) + EPS)
        xr = x * r
        diff = xr * gv - t
        dyv = diff * (1.0 / d)
        gy = dyv * gv
        dx = r * (gy - xr * jnp.mean(gy * xr, axis=-1, keepdims=True))
        outs[0][...] = dx
        outs[1][...] = dx.astype(BF16)

        @pl.when(i == 0)
        def _():
            accs[0][...] = jnp.zeros_like(accs[0])
            accs[1][...] = jnp.zeros_like(accs[1])

        accs[0][...] += jnp.sum(dyv * xr, axis=0, keepdims=True)
        part = 0.5 * jnp.sum(jnp.mean(diff * diff, axis=-1, keepdims=True), axis=0, keepdims=True)
        accs[1][...] += jnp.broadcast_to(part, accs[1].shape)

    return _rowwise(body, name, s, tr, [h, target], [g], [_sds((s, d), F32), _sds((s, d), BF16)],
                    [_sds((1, d), F32), _sds((SUBLANES, LANES), F32)])


def _colsum(a, name):
    s, w = a.shape
    tr = _pick(s, ROW_TILE_PREFS)

    def body(i, ins, consts, outs, accs):
        @pl.when(i == 0)
        def _():
            accs[0][...] = jnp.zeros_like(accs[0])

        accs[0][...] += jnp.sum(ins[0][...].astype(F32), axis=0, keepdims=True)

    return _rowwise(body, name, s, tr, [a], [], [], [_sds((1, w), F32)])[0]


def _sigmoid(x):
    return 1.0 / (1.0 + jnp.exp(-x))


def _branches_merge(y_attn, y_sgu, w_ab, w_sb, pg, gb, name):
    s, aw = y_attn.shape
    sw, d = w_sb.shape
    tm = _pick(s, MERGE_ROW_PREFS)

    def body(ya_ref, ys_ref, wa_ref, ws_ref, pg_ref, a_out, s_out, m_out):
        for rows in _row_chunks(tm):
            a = jnp.dot(ya_ref[rows, :], wa_ref[...], preferred_element_type=F32)
            b = jnp.dot(ys_ref[rows, :], ws_ref[...], preferred_element_type=F32)
            ga, gs = _sigmoid(pg_ref[rows, :gb].astype(F32)), _sigmoid(pg_ref[rows, gb:].astype(F32))
            a_out[rows, :] = a.astype(BF16)
            s_out[rows, :] = b.astype(BF16)
            m_out[rows, :] = (ga * a + gs * b).astype(BF16)

    blk = pl.BlockSpec((tm, gb), lambda i, j: (i, j))
    out = _sds((s, d), BF16)
    return pl.pallas_call(
        body,
        name=name,
        grid=(s // tm, d // gb),
        in_specs=[pl.BlockSpec((tm, aw), lambda i, j: (i, 0)), pl.BlockSpec((tm, sw), lambda i, j: (i, 0)),
                  pl.BlockSpec((aw, gb), lambda i, j: (0, j)), pl.BlockSpec((sw, gb), lambda i, j: (0, j)),
                  pl.BlockSpec((tm, 2 * gb), lambda i, j: (i, j))],
        out_specs=[blk, blk, blk],
        out_shape=[out, out, out],
        compiler_params=_params("parallel", "parallel"),
    )(y_attn, y_sgu, w_ab, w_sb, pg)


def _d_merged_merge_bwd(dmb, w_out, pg, a_br, s_br, gb, name):
    s, d = dmb.shape
    tm = _pick(s, MERGE_ROW_PREFS)

    def body(a_ref, b_ref, pg_ref, ab_ref, sb_ref, da_out, ds_out, dpg_out, db_out):
        @pl.when(pl.program_id(1) == 0)
        def _():
            db_out[...] = jnp.zeros_like(db_out)

        for rows in _row_chunks(tm):
            dm = lax.dot_general(a_ref[rows, :], b_ref[...], _DIMS["nt"], preferred_element_type=F32)
            ga, gs = _sigmoid(pg_ref[rows, :gb].astype(F32)), _sigmoid(pg_ref[rows, gb:].astype(F32))
            da_out[rows, :] = (dm * ga).astype(BF16)
            ds_out[rows, :] = (dm * gs).astype(BF16)
            dpa = dm * ab_ref[rows, :].astype(F32) * ga * (1.0 - ga)
            dps = dm * sb_ref[rows, :].astype(F32) * gs * (1.0 - gs)
            dpg_out[rows, :gb] = dpa.astype(BF16)
            dpg_out[rows, gb:] = dps.astype(BF16)
            db_out[:, :gb] += jnp.sum(dpa, axis=0, keepdims=True)
            db_out[:, gb:] += jnp.sum(dps, axis=0, keepdims=True)

    blk = pl.BlockSpec((tm, gb), lambda j, i: (i, j))
    pair = pl.BlockSpec((tm, 2 * gb), lambda j, i: (i, j))
    return pl.pallas_call(
        body,
        name=name,
        grid=(d // gb, s // tm),
        in_specs=[pl.BlockSpec((tm, d), lambda j, i: (i, 0)), pl.BlockSpec((gb, d), lambda j, i: (j, 0)), pair, blk, blk],
        out_specs=[blk, blk, pair, pl.BlockSpec((1, 2 * gb), lambda j, i: (0, j))],
        out_shape=[_sds((s, d), BF16), _sds((s, d), BF16), _sds((s, 2 * d), BF16), _sds((1, 2 * d), F32)],
        compiler_params=_params("parallel", "arbitrary"),
    )(dmb, w_out, pg, a_br, s_br)


def _row_chunks(tm):
    rc = _pick(tm, (FUSED_ROW_CHUNK,))
    return [slice(r, r + rc) for r in range(0, tm, rc)]


def _gate_up_swiglu(hn, w_gu, name):
    s, d = hn.shape
    n2 = w_gu.shape[1]
    fb = n2 // N_DEV
    tm = _pick(s, SWIGLU_ROW_PREFS)

    def body(a_ref, b_ref, gu_ref, act_ref):
        r = jnp.dot(a_ref[...], b_ref[...], preferred_element_type=F32)
        gu_ref[...] = r.astype(BF16)
        gate, up = r[:, :fb], r[:, fb:]
        act_ref[...] = (gate * _sigmoid(gate) * up).astype(BF16)

    return pl.pallas_call(
        body,
        name=name,
        grid=(s // tm, N_DEV // 2),
        in_specs=[pl.BlockSpec((tm, d), lambda i, j: (i, 0)), pl.BlockSpec((d, 2 * fb), lambda i, j: (0, j))],
        out_specs=[pl.BlockSpec((tm, 2 * fb), lambda i, j: (i, j)), pl.BlockSpec((tm, fb), lambda i, j: (i, j))],
        out_shape=[_sds((s, n2), BF16), _sds((s, n2 // 2), BF16)],
        compiler_params=_params("parallel", "parallel"),
    )(hn, w_gu)


def _d_act_swiglu(dhb, w_down, gu, name):
    s, d = dhb.shape
    n2 = gu.shape[1]
    fb = n2 // N_DEV
    tm = _pick(s, D_ACT_ROW_PREFS)

    def body(a_ref, b_ref, gu_ref, o_ref):
        for rows in _row_chunks(tm):
            da = lax.dot_general(a_ref[rows, :], b_ref[...], _DIMS["nt"], preferred_element_type=F32)
            gate, up = gu_ref[rows, :fb].astype(F32), gu_ref[rows, fb:].astype(F32)
            sg = _sigmoid(gate)
            o_ref[rows, :fb] = (da * up * sg * (1.0 + gate * (1.0 - sg))).astype(BF16)
            o_ref[rows, fb:] = (da * gate * sg).astype(BF16)

    pair = pl.BlockSpec((tm, 2 * fb), lambda i, j: (i, j))
    return pl.pallas_call(
        body,
        name=name,
        grid=(s // tm, N_DEV // 2),
        in_specs=[pl.BlockSpec((tm, d), lambda i, j: (i, 0)), pl.BlockSpec((fb, d), lambda i, j: (j, 0)), pair],
        out_specs=pair,
        out_shape=_sds((s, n2), BF16),
        compiler_params=_params("parallel", "parallel"),
    )(dhb, w_down, gu)


def _rope_tables(pos_col, name):
    s = pos_col.shape[0]
    tr = _pick(s, (1024, 512, 256, 128))
    inv = ROPE_THETA ** (-jnp.arange(0, ROPE_DIM, 2, dtype=F32) / ROPE_DIM)
    lane = jnp.arange(LANES)
    inv_lanes = inv[lane % ROPE_HALF].reshape(1, LANES)

    def body(i, ins, consts, outs, accs):
        ang = ins[0][...].astype(F32) * consts[0][...]
        c, sn = jnp.cos(ang), jnp.sin(ang)
        in_head = lax.broadcasted_iota(jnp.int32, ang.shape, 1) % HEAD_DIM
        outs[0][:, 0:LANES] = jnp.where(in_head < ROPE_DIM, c, 1.0)
        outs[0][:, LANES:2 * LANES] = jnp.where(in_head < ROPE_HALF, -sn, 0.0)
        outs[0][:, 2 * LANES:] = jnp.where((in_head >= ROPE_HALF) & (in_head < ROPE_DIM), sn, 0.0)

    return _rowwise(body, name, s, tr, [pos_col], [inv_lanes], [_sds((s, 3 * LANES), F32)])[0]


def _rope(x, tab, inverse=False):
    width = x.shape[1]
    reps = width // LANES
    c = jnp.tile(tab[:, 0:LANES], (1, reps))
    lo = jnp.tile(tab[:, LANES:2 * LANES], (1, reps))
    hi = jnp.tile(tab[:, 2 * LANES:], (1, reps))
    if inverse:
        lo, hi = -lo, -hi
    return x * c + pltpu.roll(x, width - ROPE_HALF, 1) * lo + pltpu.roll(x, ROPE_HALF, 1) * hi


def _attn_specs(aw, kw):
    kb = aw // kw
    prev = lambda i: jnp.maximum(i - 1, 0)
    return [
        pl.BlockSpec(memory_space=pltpu.SMEM),
        pl.BlockSpec((WINDOW, aw), lambda i: (i, 0)),
        pl.BlockSpec((WINDOW, kw), lambda i: (i, kb)),
        pl.BlockSpec((WINDOW, kw), lambda i: (prev(i), kb)),
        pl.BlockSpec((WINDOW, kw), lambda i: (i, kb + 1)),
        pl.BlockSpec((WINDOW, kw), lambda i: (prev(i), kb + 1)),
        pl.BlockSpec((WINDOW, 3 * LANES), lambda i: (i, 0)),
        pl.BlockSpec((WINDOW, 3 * LANES), lambda i: (prev(i), 0)),
    ]


def _attn_common(i, q_ref, kc_ref, kp_ref, vc_ref, vp_ref, tq_ref, tp_ref):
    tq, tp = tq_ref[...], tp_ref[...]
    qt = (_rope(q_ref[...].astype(F32), tq) * ATTN_SCALE).astype(BF16).T
    kc = _rope(kc_ref[...].astype(F32), tq)
    kp = _rope(kp_ref[...].astype(F32), tp)
    k2 = jnp.concatenate([kp, kc], axis=0).astype(BF16)
    v2 = jnp.concatenate([vp_ref[...], vc_ref[...]], axis=0)
    kj = lax.broadcasted_iota(jnp.int32, (2 * WINDOW, WINDOW), 0)
    qi = lax.broadcasted_iota(jnp.int32, (2 * WINDOW, WINDOW), 1)
    rel = qi + WINDOW - kj
    ok = (rel >= 0) & (rel < WINDOW) & ((kj >= WINDOW) | (i > 0))
    return qt, k2, v2, ok, tq, tp


def _head_probs(qt_h, kg, ok, sink):
    s = jnp.dot(kg, qt_h, preferred_element_type=F32)
    s = jnp.where(ok, s, NEG)
    m = jnp.maximum(jnp.max(s, axis=0, keepdims=True), sink)
    p = jnp.exp(s - m)
    es = jnp.exp(sink - m)
    inv = 1.0 / (jnp.sum(p, axis=0, keepdims=True) + es)
    return p * inv, es * inv


def _attn_fwd(qkv, tabs, sinks, aw, kw, name):
    s = qkv.shape[0]
    nq, nkv = aw // HEAD_DIM, kw // HEAD_DIM
    qpk = nq // nkv

    def body(s_ref, q_ref, kc_ref, kp_ref, vc_ref, vp_ref, tq_ref, tp_ref, o_ref):
        i = pl.program_id(0)
        tq, tp = tq_ref[...], tp_ref[...]
        q = (_rope(q_ref[...].astype(F32), tq) * ATTN_SCALE).astype(BF16)
        k2 = jnp.concatenate([_rope(kp_ref[...].astype(F32), tp), _rope(kc_ref[...].astype(F32), tq)], axis=0).astype(BF16)
        v2 = jnp.concatenate([vp_ref[...], vc_ref[...]], axis=0)
        qi = lax.broadcasted_iota(jnp.int32, (WINDOW, 2 * WINDOW), 0)
        kj = lax.broadcasted_iota(jnp.int32, (WINDOW, 2 * WINDOW), 1)
        rel = qi + WINDOW - kj
        ok = (rel >= 0) & (rel < WINDOW) & ((kj >= WINDOW) | (i > 0))
        for h in range(nq):
            g = h // qpk
            hs, gs = slice(h * HEAD_DIM, (h + 1) * HEAD_DIM), slice(g * HEAD_DIM, (g + 1) * HEAD_DIM)
            sc = lax.dot_general(q[:, hs], k2[:, gs], _DIMS["nt"], preferred_element_type=F32)
            sc = jnp.where(ok, sc, NEG)
            m = jnp.maximum(jnp.max(sc, axis=1, keepdims=True), s_ref[h])
            p = jnp.exp(sc - m)
            inv = 1.0 / (jnp.sum(p, axis=1, keepdims=True) + jnp.exp(s_ref[h] - m))
            o = jnp.dot((p * inv).astype(BF16), v2[:, gs], preferred_element_type=F32)
            o_ref[:, hs] = o.astype(BF16)

    return pl.pallas_call(
        body,
        name=name,
        grid=(s // WINDOW,),
        in_specs=_attn_specs(aw, kw),
        out_specs=pl.BlockSpec((WINDOW, aw), lambda i: (i, 0)),
        out_shape=_sds((s, aw), BF16),
        compiler_params=_params("parallel"),
    )(sinks, qkv, qkv, qkv, qkv, qkv, tabs, tabs)


def _attn_bwd(qkv, tabs, sinks, o, do, aw, kw, name):
    s = qkv.shape[0]
    nb = s // WINDOW
    nq, nkv = aw // HEAD_DIM, kw // HEAD_DIM
    qpk = nq // nkv

    def body(s_ref, q_ref, kc_ref, kp_ref, vc_ref, vp_ref, tq_ref, tp_ref, o_ref, do_ref,
             dq_ref, dkv_ref, ds_ref, ck_ref, cv_ref):
        i = pl.program_id(0)

        @pl.when(i == 0)
        def _():
            ck_ref[...] = jnp.zeros_like(ck_ref)
            cv_ref[...] = jnp.zeros_like(cv_ref)
            ds_ref[...] = jnp.zeros_like(ds_ref)

        qt, k2, v2, ok, tq, tp = _attn_common(i, q_ref, kc_ref, kp_ref, vc_ref, vp_ref, tq_ref, tp_ref)
        dot_t, ot = do_ref[...].T, o_ref[...].T
        k2t = k2.T
        row0 = lax.broadcasted_iota(jnp.int32, (SUBLANES, LANES), 0) == 0
        lane = lax.broadcasted_iota(jnp.int32, (SUBLANES, LANES), 1)
        dsink = jnp.zeros((SUBLANES, LANES), F32)
        dqt_parts, dk_parts, dv_parts = [], [], []
        for g in range(nkv):
            gs = slice(g * HEAD_DIM, (g + 1) * HEAD_DIM)
            kg, vg = k2[:, gs], v2[:, gs]
            dk_g = jnp.zeros((2 * WINDOW, HEAD_DIM), F32)
            dv_g = jnp.zeros((2 * WINDOW, HEAD_DIM), F32)
            for j in range(qpk):
                h = g * qpk + j
                hs = slice(h * HEAD_DIM, (h + 1) * HEAD_DIM)
                pn, psink = _head_probs(qt[hs], kg, ok, s_ref[h])
                delta = jnp.sum(dot_t[hs].astype(F32) * ot[hs].astype(F32), axis=0, keepdims=True)
                dp = jnp.dot(vg, dot_t[hs], preferred_element_type=F32)
                dsb = (pn * (dp - delta)).astype(BF16)
                dsink = dsink + jnp.where(row0 & (lane == h), -jnp.sum(psink * delta, axis=1, keepdims=True), 0.0)
                dqt_parts.append(jnp.dot(k2t[gs], dsb, preferred_element_type=F32))
                dk_g = dk_g + lax.dot_general(dsb, qt[hs], _DIMS["nt"], preferred_element_type=F32)
                dv_g = dv_g + lax.dot_general(pn.astype(BF16), dot_t[hs], _DIMS["nt"], preferred_element_type=F32)
            dk_parts.append(dk_g)
            dv_parts.append(dv_g)
        ds_ref[...] += dsink
        dq_ref[...] = _rope(jnp.concatenate(dqt_parts, axis=0).T * ATTN_SCALE, tq, inverse=True).astype(BF16)
        dk2 = jnp.concatenate(dk_parts, axis=1)
        dv2 = jnp.concatenate(dv_parts, axis=1)
        dk_prev = _rope(ck_ref[...] + dk2[:WINDOW], tp, inverse=True)
        dv_prev = cv_ref[...] + dv2[:WINDOW]

        @pl.when(i > 0)
        def _():
            dkv_ref[pl.ds(pl.multiple_of((i - 1) * WINDOW, WINDOW), WINDOW), :] = jnp.concatenate(
                [dk_prev, dv_prev], axis=1).astype(BF16)

        ck_ref[...] = dk2[WINDOW:]
        cv_ref[...] = dv2[WINDOW:]

        @pl.when(i == nb - 1)
        def _():
            dkv_ref[pl.ds(pl.multiple_of(i * WINDOW, WINDOW), WINDOW), :] = jnp.concatenate(
                [_rope(dk2[WINDOW:], tq, inverse=True), dv2[WINDOW:]], axis=1).astype(BF16)

    blk = pl.BlockSpec((WINDOW, aw), lambda i: (i, 0))
    return pl.pallas_call(
        body,
        name=name,
        grid=(nb,),
        in_specs=_attn_specs(aw, kw) + [blk, blk],
        out_specs=[blk, pl.BlockSpec((s, 2 * kw), lambda i: (0, 0)), pl.BlockSpec((SUBLANES, LANES), lambda i: (0, 0))],
        out_shape=[_sds((s, aw), BF16), _sds((s, 2 * kw), BF16), _sds((SUBLANES, LANES), F32)],
        scratch_shapes=[pltpu.VMEM((WINDOW, kw), F32), pltpu.VMEM((WINDOW, kw), F32)],
        compiler_params=_params("arbitrary"),
    )(sinks, qkv, qkv, qkv, qkv, qkv, tabs, tabs, o, do)


_INV_SQRT2 = 1.0 / math.sqrt(2.0)
_INV_SQRT2PI = 1.0 / math.sqrt(2.0 * math.pi)


def _gelu(x):
    return x * (lax.erf(x * _INV_SQRT2) + 1.0) * 0.5


def _gelu_grad(x):
    return 0.5 * (lax.erf(x * _INV_SQRT2) + 1.0) + x * jnp.exp(-0.5 * x * x) * _INV_SQRT2PI


def _sgu_norm(pv, lg, lb):
    zv = _gelu(pv)
    mu = jnp.mean(zv, axis=-1, keepdims=True)
    cen = zv - mu
    rs = lax.rsqrt(jnp.mean(cen * cen, axis=-1, keepdims=True) + EPS)
    xhat = cen * rs
    return xhat, rs, (xhat * lg + lb).astype(BF16)


def _causal(w, upper=False):
    t = lax.broadcasted_iota(jnp.int32, (CHUNK, CHUNK), 0)
    u = lax.broadcasted_iota(jnp.int32, (CHUNK, CHUNK), 1)
    return jnp.where((u >= t) if upper else (t >= u), w, 0.0).astype(BF16)


def _sgu_fwd(pz, lg, lb, w, bt, name):
    s, sw = pz.shape[0], pz.shape[1] // 2
    groups = sw // GROUP_DIM

    def body(i, ins, consts, outs, accs):
        lgv, lbv, w_ref, btv = consts[0][...], consts[1][...], consts[2], consts[3][...]
        zu = _gelu(ins[0][:, :sw].astype(F32))
        _, _, vn = _sgu_norm(ins[0][:, sw:].astype(F32), lgv, lbv)
        for g in range(groups):
            gs = slice(g * GROUP_DIM, (g + 1) * GROUP_DIM)
            sv = jnp.dot(_causal(w_ref[g]), vn[:, gs], preferred_element_type=F32) + btv[:, g:g + 1]
            outs[0][:, gs] = (zu[:, gs] * sv).astype(BF16)

    return _rowwise(body, name, s, CHUNK, [pz], [lg, lb, w, bt], [_sds((s, sw), BF16)])[0]


def _sgu_bwd(pz, dy, lg, lb, w, wt, bt, name, after=()):
    s, sw = pz.shape[0], pz.shape[1] // 2
    groups = sw // GROUP_DIM

    def body(i, ins, consts, outs, accs):
        lgv, lbv, w_ref, wt_ref, btv = consts[0][...], consts[1][...], consts[2], consts[3], consts[4][...]

        @pl.when(i == 0)
        def _():
            for a in accs:
                a[...] = jnp.zeros_like(a)

        pu, pv = ins[0][:, :sw].astype(F32), ins[0][:, sw:].astype(F32)
        dyv = ins[1][...].astype(F32)
        zu = _gelu(pu)
        xhat, rs, vn = _sgu_norm(pv, lgv, lbv)
        dvn_parts, db_parts = [], []
        lower = lax.broadcasted_iota(jnp.int32, (CHUNK, CHUNK), 0) >= lax.broadcasted_iota(jnp.int32, (CHUNK, CHUNK), 1)
        for g in range(groups):
            gs = slice(g * GROUP_DIM, (g + 1) * GROUP_DIM)
            sv = jnp.dot(_causal(w_ref[g]), vn[:, gs], preferred_element_type=F32) + btv[:, g:g + 1]
            dpu = dyv[:, gs] * sv * _gelu_grad(pu[:, gs])
            outs[0][:, gs] = dpu.astype(BF16)
            accs[4][:, gs] += jnp.sum(dpu, axis=0, keepdims=True)
            dsv = dyv[:, gs] * zu[:, gs]
            dsvb = dsv.astype(BF16)
            db_parts.append(jnp.sum(dsv, axis=1, keepdims=True))
            accs[2][g] += jnp.where(lower, lax.dot_general(dsvb, vn[:, gs], _DIMS["nt"], preferred_element_type=F32), 0.0)
            dvn_parts.append(jnp.dot(_causal(wt_ref[g], upper=True), dsvb, preferred_element_type=F32))
        dvn = jnp.concatenate(dvn_parts, axis=1)
        accs[3][...] += jnp.concatenate(db_parts, axis=1)
        accs[0][...] += jnp.sum(dvn * xhat, axis=0, keepdims=True)
        accs[1][...] += jnp.sum(dvn, axis=0, keepdims=True)
        dxh = dvn * lgv
        dz = rs * (dxh - jnp.mean(dxh, axis=-1, keepdims=True) - xhat * jnp.mean(dxh * xhat, axis=-1, keepdims=True))
        dpv = dz * _gelu_grad(pv)
        outs[0][:, sw:] = dpv.astype(BF16)
        accs[4][:, sw:] += jnp.sum(dpv, axis=0, keepdims=True)

    return _rowwise(body, name, s, CHUNK, [pz, dy], [lg, lb, w, wt, bt], [_sds((s, 2 * sw), BF16)],
                    [_sds((1, sw), F32), _sds((1, sw), F32), _sds((groups, CHUNK, CHUNK), F32), _sds((CHUNK, groups), F32),
                     _sds((1, 2 * sw), F32)],
                    after=after)


def _mesh_place():
    x, y, c = lax.axis_index("x"), lax.axis_index("y"), lax.axis_index("c")
    return x, y, c, 4 * x + 2 * y + c


def _peer(x, y, c, k):
    px, py, pc = x ^ ((k >> 2) & 1), y ^ ((k >> 1) & 1), c ^ (k & 1)
    return (px, py, pc), 4 * px + 2 * py + pc


BY_SLOTS, BY_COLS, BY_PAIRED_COLS = 0, 1, 2


def _col_block(ref, idx, width, cols):
    if cols == BY_PAIRED_COLS:
        idx = (idx % (N_DEV // 2)) * 2 + idx // (N_DEV // 2)
    return ref.at[:, pl.ds(pl.multiple_of(idx * width, LANES), width)]


def _exchange_copy(src_ref, land_ref, send_sems, recv_sems, k, place, scatter, arriving, cols):
    x, y, c, me = place
    peer, pidx = _peer(x, y, c, k)
    slot = pidx if arriving else me
    if scatter:
        src = _col_block(src_ref, pidx, land_ref.shape[-1], cols) if cols else src_ref.at[pidx]
        dst = land_ref.at[slot]
    else:
        src = src_ref
        dst = _col_block(land_ref, slot, src_ref.shape[-1], cols) if cols else land_ref.at[slot]
    return pltpu.make_async_remote_copy(
        src_ref=src, dst_ref=dst, send_sem=send_sems[k - 1], recv_sem=recv_sems[k - 1], device_id=peer,
        device_id_type=MESH_TYPE)


def _own_copy(src_ref, land_ref, sem, place, scatter, cols):
    me = place[3]
    if scatter:
        src = _col_block(src_ref, me, land_ref.shape[-1], cols) if cols else src_ref.at[me]
        dst = land_ref.at[me]
    else:
        src = src_ref
        dst = _col_block(land_ref, me, src_ref.shape[-1], cols) if cols else land_ref.at[me]
    return pltpu.make_async_copy(src, dst, sem)


N_PEERS = N_DEV - 1
N_EXCHANGE_SEMS = 2 * N_PEERS + 1


def _land_shape(a, scatter, cols):
    if scatter:
        return (N_DEV, a.shape[0], a.shape[1] // N_DEV) if cols else a.shape
    return (a.shape[0], N_DEV * a.shape[1]) if cols else (N_DEV,) + a.shape


def _exchange_start(srcs, scatter, cols, name, after):
    n = len(srcs)
    land_shapes = [_land_shape(a, scatter, cl) for a, cl in zip(srcs, cols)]

    def body(*refs):
        src, land = refs[:n], refs[n:2 * n]
        send_sems = refs[2 * n + 1:2 * n + 1 + N_PEERS]
        recv_sems = refs[2 * n + 1 + N_PEERS:2 * n + 1 + 2 * N_PEERS]
        own_sem = refs[2 * n + 1 + 2 * N_PEERS]
        token = refs[-1]
        place = _mesh_place()
        for t in range(n):
            for k in range(1, N_DEV):
                _exchange_copy(src[t], land[t], send_sems, recv_sems, k, place, scatter, False, cols[t]).start()
            _own_copy(src[t], land[t], own_sem, place, scatter, cols[t]).start()
        token[...] = jnp.zeros_like(token)

    return pl.pallas_call(
        body,
        name=name,
        out_shape=(*[pltpu.SemaphoreType.DMA(())] * N_EXCHANGE_SEMS, *[pltpu.HBM(a.shape, a.dtype) for a in srcs],
                   *[pltpu.HBM(shp, a.dtype) for shp, a in zip(land_shapes, srcs)], _sds((SUBLANES, LANES), F32)),
        in_specs=[HBM] * (2 * n) + [ANY],
        out_specs=(*[SEM] * N_EXCHANGE_SEMS, *[HBM] * (2 * n), pl.BlockSpec(memory_space=pltpu.VMEM)),
        input_output_aliases={i: N_EXCHANGE_SEMS + i for i in range(2 * n)},
        compiler_params=pltpu.CompilerParams(has_side_effects=DATAFLOW_EFFECT),
    )(*[pltpu.with_memory_space_constraint(a, pltpu.HBM) for a in srcs],
      *[pltpu.with_memory_space_constraint(lax.empty(shp, a.dtype), pltpu.HBM) for shp, a in zip(land_shapes, srcs)],
      after)


def _exchange_wait(started, after, scatter, cols, name):
    sems = started[:N_EXCHANGE_SEMS]
    thru = started[N_EXCHANGE_SEMS:-1]
    n = len(thru) // 2

    def body(*refs):
        src, land = refs[:n], refs[n:2 * n]
        send_sems = refs[2 * n:2 * n + N_PEERS]
        recv_sems = refs[2 * n + N_PEERS:2 * n + 2 * N_PEERS]
        own_sem = refs[2 * n + 2 * N_PEERS]
        place = _mesh_place()
        for t in range(n):
            for k in range(1, N_DEV):
                cp = _exchange_copy(src[t], land[t], send_sems, recv_sems, k, place, scatter, True, cols[t])
                cp.wait_send()
                cp.wait_recv()
            _own_copy(src[t], land[t], own_sem, place, scatter, cols[t]).wait()

    out = pl.pallas_call(
        body,
        name=name,
        out_shape=tuple(pltpu.HBM(a.shape, a.dtype) for a in thru),
        in_specs=[HBM] * (2 * n) + [SEM] * N_EXCHANGE_SEMS + [ANY],
        out_specs=tuple([HBM] * (2 * n)),
        input_output_aliases={i: i for i in range(2 * n)},
        compiler_params=pltpu.CompilerParams(has_side_effects=DATAFLOW_EFFECT),
    )(*thru, *sems, after)
    return out[:n], out[n:]


def _adamw(w, g, m, v):
    m = ADAM_B1 * m + (1.0 - ADAM_B1) * g
    v = ADAM_B2 * v + (1.0 - ADAM_B2) * (g * g)
    m_hat = m / (1.0 - ADAM_B1 ** ADAM_STEP)
    v_hat = v / (1.0 - ADAM_B2 ** ADAM_STEP)
    delta = -ADAM_LR * (m_hat / (jnp.sqrt(v_hat) + ADAM_EPS) + ADAM_WD * w)
    return delta, m, v


def _adam_rows(r, c):
    fits = [t for t in range(BF16_SUBLANES, r + 1, BF16_SUBLANES) if r % t == 0 and t * c <= ADAM_BLOCK_ELEMS]
    return max(fits) if fits else r


def _adam_body(p_ref, w_ref, m_ref, v_ref, g_out, d_out, m_out, v_out):
    g = p_ref[0].astype(F32)
    for d in range(1, N_DEV):
        g = g + p_ref[d].astype(F32)
    delta, mn, vn = _adamw(w_ref[...], g, m_ref[...], v_ref[...])
    g_out[...] = g
    d_out[...] = delta
    m_out[...] = mn
    v_out[...] = vn


def _reduce_adam_layer(parts, w, m, v, prev, layer, name):
    nl, r, c = w.shape
    tr = _adam_rows(r, c)
    if prev is None:
        prev = [lax.empty((nl, r, c), F32) for _ in range(4)]

    def body(p_ref, w_ref, m_ref, v_ref, *rest):
        _adam_body(p_ref, w_ref, m_ref, v_ref, *rest[4:])

    blk = pl.BlockSpec((None, tr, c), lambda i: (layer, i, 0))
    out = _sds((nl, r, c), F32)
    return pl.pallas_call(
        body,
        name=name,
        grid=(r // tr,),
        in_specs=[pl.BlockSpec((N_DEV, tr, c), lambda i: (0, i, 0)), blk, blk, blk, ANY, ANY, ANY, ANY],
        out_specs=[blk, blk, blk, blk],
        out_shape=[out, out, out, out],
        input_output_aliases={4: 0, 5: 1, 6: 2, 7: 3},
        compiler_params=_params("parallel"),
    )(parts, w, m, v, *prev)


def _reduce_adam(parts, w, m, v, name):
    nl, _, r, c = parts.shape
    tr = _adam_rows(r, c)

    def body(*refs):
        _adam_body(*refs)

    blk = pl.BlockSpec((None, tr, c), lambda l, i: (l, i, 0))
    out = _sds((nl, r, c), F32)
    return pl.pallas_call(
        body,
        name=name,
        grid=(nl, r // tr),
        in_specs=[pl.BlockSpec((None, N_DEV, tr, c), lambda l, i: (l, 0, i, 0)), blk, blk, blk],
        out_specs=[blk, blk, blk, blk],
        out_shape=[out, out, out, out],
        compiler_params=_params("parallel", "parallel"),
    )(parts, w, m, v)


def _pack(arrays):
    flat = []
    for a in arrays:
        a = a.reshape(-1).astype(F32)
        flat.append(jnp.pad(a, (0, (-a.shape[0]) % PACK_UNIT)))
    out = jnp.concatenate(flat)
    rows = out.shape[0] // LANES
    pad_rows = (-rows) % 512
    return jnp.pad(out, (0, pad_rows * LANES)).reshape(rows + pad_rows, LANES)


def _unpack(packed, shapes):
    flat = packed.reshape(-1)
    out, off = [], 0
    for shp in shapes:
        size = math.prod(shp)
        out.append(flat[off:off + size].reshape(shp))
        off += size + (-size) % PACK_UNIT
    return out


def _in_runs(d, qkv_w, sw, gb):
    g0 = qkv_w + 2 * sw
    runs = [(0, qkv_w, "qkv", 0), (qkv_w, 2 * sw, "z", 0)]
    for j in range(d // gb):
        runs.append((g0 + j * gb, gb, "g", 2 * j * gb))
        runs.append((g0 + d + j * gb, gb, "g", (2 * j + 1) * gb))
    return runs


def _pieces_from_global(take, runs, axis=-1):
    out = {}
    for piece in ("qkv", "z", "g"):
        own = sorted((r for r in runs if r[2] == piece), key=lambda r: r[3])
        parts = [take(g, g + w) for g, w, _, _ in own]
        out[piece] = parts[0] if len(parts) == 1 else jnp.concatenate(parts, axis=axis)
    return out


def _global_from_pieces(pieces, runs, axis=-1):
    segs = [lax.slice_in_dim(pieces[piece], start, start + w, axis=axis % pieces[piece].ndim)
            for _, w, piece, start in sorted(runs)]
    return jnp.concatenate(segs, axis=axis)


def _to_full_cols(g):
    d, k, n = g.shape
    return jnp.transpose(g, (1, 0, 2)).reshape(k, d * n)


def _to_col_shards(a):
    k, n = a.shape
    return jnp.transpose(a.reshape(k, N_DEV, n // N_DEV), (1, 0, 2))


def kernel(x, positions, norm1_g, w_in, b_in, sinks, sgu_ln_g, sgu_ln_b, sgu_w, sgu_b, w_attn_branch, w_sgu_branch, w_out, norm2_g, w_gate_up, w_down, final_g, loss_target, m_norm1_g, m_w_in, m_b_in, m_sinks, m_sgu_ln_g, m_sgu_ln_b, m_sgu_w, m_sgu_b, m_w_attn_branch, m_w_sgu_branch, m_w_out, m_norm2_g, m_w_gate_up, m_w_down, m_final_g, v_norm1_g, v_w_in, v_b_in, v_sinks, v_sgu_ln_g, v_sgu_ln_b, v_sgu_w, v_sgu_b, v_w_attn_branch, v_w_sgu_branch, v_w_out, v_norm2_g, v_w_gate_up, v_w_down, v_final_g):
    nl = w_in.shape[0]
    s, d = x.shape[1], x.shape[2]
    aw = w_attn_branch.shape[1]
    sw = w_sgu_branch.shape[1]
    in_w = w_in.shape[2] * N_DEV
    kw = (in_w - aw - 2 * sw - 2 * d) // 2
    qkv_w = aw + 2 * kw
    ff = w_down.shape[1] * N_DEV

    h = x.reshape(s, d)
    target = loss_target.reshape(s, d)
    tabs = _rope_tables(positions.reshape(s, 1), "rope_tables")

    transposed = lambda a: jnp.swapaxes(a, 1, 2)
    big = [transposed(w_in), w_attn_branch, w_sgu_branch, w_out, w_gate_up, w_down]
    big_m = [transposed(m_w_in), m_w_attn_branch, m_w_sgu_branch, m_w_out, m_w_gate_up, m_w_down]
    big_v = [transposed(v_w_in), v_w_attn_branch, v_w_sgu_branch, v_w_out, v_w_gate_up, v_w_down]
    big_names = ("w_in", "w_attn_branch", "w_sgu_branch", "w_out", "w_gate_up", "w_down")
    W_IN, W_AB, W_SB, W_OUT, W_GU, W_DOWN = range(6)
    weight_groups = ((W_IN,), (W_AB, W_SB, W_OUT), (W_GU,), (W_DOWN,))
    grad_groups = ((W_DOWN, W_GU), (W_OUT, W_AB, W_SB), (W_IN,))

    col_sharded = (W_AB, W_SB, W_GU)
    by_cols = [BY_COLS if t in col_sharded and big[t].shape[2] % LANES == 0 else BY_SLOTS for t in range(6)]
    assert by_cols[W_GU] == BY_COLS, "the fused swiglu kernels need gate/up column blocks of whole lane tiles"
    by_cols[W_GU] = BY_PAIRED_COLS

    def start_gather(l, group, after):
        return _exchange_start([big[t][l].astype(BF16) for t in group], False, tuple(by_cols[t] for t in group),
                               f"gather_start_l{l}_{big_names[group[0]]}", after)

    def full_weight(t, land):
        if by_cols[t]:
            return land
        if t in col_sharded:
            return _to_full_cols(land)
        whole = land.reshape(N_DEV * land.shape[1], land.shape[2])
        return _pieces_from_global(lambda a, b: whole[a:b], in_runs, axis=0) if t == W_IN else whole

    gate_block = _pick(d, GATE_BLOCK_PREFS)
    in_runs = _in_runs(d, qkv_w, sw, gate_block)

    saved = []
    started = {}
    token = h
    for l in range(nl):
        for ll in ((0, 1) if l == 0 else (l + 1,)):
            if ll < nl:
                for group in weight_groups:
                    started[(ll, group)] = start_gather(ll, group, token)
                    token = started[(ll, group)][-1]
        gathered = {}

        def weight(t, after, l=l, gathered=gathered):
            if t not in gathered:
                group = next(g for (ll, g) in started if ll == l and t in g)
                srcs, lands = _exchange_wait(started.pop((l, group)), after, False, tuple(by_cols[tt] for tt in group),
                                             f"gather_wait_l{l}_{big_names[group[0]]}")
                for tt, ld in zip(group, lands):
                    gathered[tt] = full_weight(tt, ld)
            return gathered[t]

        bias = b_in[l].reshape(1, in_w)
        g1, g2 = norm1_g[l].reshape(1, d), norm2_g[l].reshape(1, d)
        lg, lb = sgu_ln_g[l].reshape(1, sw), sgu_ln_b[l].reshape(1, sw)
        bt = sgu_b[l].T

        xn = _rms_fwd(h, g1, "rms1_fwd", after=(token,))
        wts = dict(weight(W_IN, xn))
        biases = _pieces_from_global(lambda a, b: bias[:, a:b], in_runs)
        qkv = _matmul(xn, wts["qkv"], "nt", BF16, "proj_qkv", bias=biases["qkv"])
        pz = _matmul(xn, wts["z"], "nt", BF16, "proj_z", bias=biases["z"])
        pg = _matmul(xn, wts["g"], "nt", BF16, "proj_g", bias=biases["g"])
        y_attn = _attn_fwd(qkv, tabs, sinks[l], aw, kw, "attn_fwd")
        y_sgu = _sgu_fwd(pz, lg, lb, sgu_w[l], bt, "sgu_fwd")
        wts.update(ab=weight(W_AB, y_sgu), sb=weight(W_SB, y_sgu), out=weight(W_OUT, y_sgu))
        a_br, s_br, merged = _branches_merge(y_attn, y_sgu, wts["ab"], wts["sb"], pg, gate_block, "branches_merge")
        h_mid, hn = _out_proj_rms(merged, wts["out"], h, g2, "out_proj")
        wts.update(gu=weight(W_GU, hn))
        gu, act = _gate_up_swiglu(hn, wts["gu"], "gate_up")
        wts.update(down=weight(W_DOWN, act))
        h_out = _matmul(act, wts["down"], "nn", F32, "down_proj", res=h_mid)
        saved.append(dict(wts=wts, h=h, xn=xn, qkv=qkv, pz=pz, pg=pg, y_attn=y_attn, y_sgu=y_sgu, a_br=a_br,
                          s_br=s_br, merged=merged, h_mid=h_mid, hn=hn, gu=gu, act=act,
                          g1=g1, g2=g2, lg=lg, lb=lb, bt=bt))
        h = h_out

    dh, dhb, d_final_g, loss_blk = _loss_head(h, final_g.reshape(1, d), target, "loss_head")

    small = {n: [None] * nl for n in ("norm1_g", "b_in", "sinks", "sgu_ln_g", "sgu_ln_b", "sgu_w", "sgu_b", "norm2_g")}
    scattering = {}

    def start_scatter(l, group, grads, after):
        sends = []
        for t, dw in zip(group, grads):
            if t == W_IN:
                dw = _global_from_pieces(dw, in_runs, axis=0)
            if by_cols[t]:
                sends.append(dw)
            elif t in col_sharded:
                sends.append(_to_col_shards(dw))
            else:
                sends.append(dw.reshape(N_DEV, dw.shape[0] // N_DEV, dw.shape[1]))
        scattering[(l, group)] = _exchange_start(sends, True, tuple(by_cols[t] for t in group),
                                                 f"scatter_start_l{l}_{big_names[group[0]]}", after)
        return scattering[(l, group)][-1]

    for l in reversed(range(nl)):
        sv = saved[l]
        wts = sv["wts"]
        d_gu = _d_act_swiglu(dhb, wts["down"], sv["gu"], "d_act")
        dw_down = _matmul(sv["act"], dhb, "tn", BF16, "dw_down")
        dw_gu = _matmul(sv["hn"], d_gu, "tn", BF16, "dw_gate_up")
        token = start_scatter(l, grad_groups[0], [dw_down, dw_gu], token)
        d_hn = _matmul(d_gu, wts["gu"], "nt", BF16, "d_hn", after=token)
        dh_mid, dmb, dg2 = _rms_bwd(sv["h_mid"], sv["g2"], d_hn, dh, "rms2_bwd")
        d_a, d_s, d_pg, db_g = _d_merged_merge_bwd(dmb, wts["out"], sv["pg"], sv["a_br"], sv["s_br"], gate_block, "d_merged")
        dw_out = _matmul(sv["merged"], dmb, "tn", BF16, "dw_out")
        d_y_attn = _matmul(d_a, wts["ab"], "nt", BF16, "d_y_attn")
        dw_ab = _matmul(sv["y_attn"], d_a, "tn", BF16, "dw_attn_branch")
        d_y_sgu = _matmul(d_s, wts["sb"], "nt", BF16, "d_y_sgu")
        dw_sb = _matmul(sv["y_sgu"], d_s, "tn", BF16, "dw_sgu_branch")
        token = start_scatter(l, grad_groups[1], [dw_out, dw_ab, dw_sb], token)
        d_pz, d_lg, d_lb, d_sw, d_sbt, db_z = _sgu_bwd(sv["pz"], d_y_sgu, sv["lg"], sv["lb"], sgu_w[l],
                                                 jnp.transpose(sgu_w[l], (0, 2, 1)), sv["bt"], "sgu_bwd", after=(token,))
        d_q, d_kv, d_sinks = _attn_bwd(sv["qkv"], tabs, sinks[l], sv["y_attn"], d_y_attn, aw, kw, "attn_bwd")
        d_qkv = jnp.concatenate([d_q, d_kv], axis=1)
        dw_qkv = _matmul(d_qkv, sv["xn"], "tn", BF16, "dw_qkv")
        dw_z = _matmul(d_pz, sv["xn"], "tn", BF16, "dw_z")
        dw_g = _matmul(d_pg, sv["xn"], "tn", BF16, "dw_g")
        token = start_scatter(l, grad_groups[2], [dict(qkv=dw_qkv, z=dw_z, g=dw_g)], token)
        d_xn = _matmul(d_qkv, wts["qkv"], "nn", F32, "d_xn_qkv", after=token)
        d_xn = _matmul(d_pz, wts["z"], "nn", F32, "d_xn_z", res=d_xn)
        d_xn = _matmul(d_pg, wts["g"], "nn", F32, "d_xn_g", res=d_xn)
        dh, dhb, dg1 = _rms_bwd(sv["h"], sv["g1"], d_xn, dh_mid, "rms1_bwd")

        small["norm1_g"][l], small["norm2_g"][l] = dg1, dg2
        small["b_in"][l] = _global_from_pieces(dict(qkv=_colsum(d_qkv, "db_qkv"), z=db_z, g=db_g), in_runs)
        small["sinks"][l] = d_sinks[0, :aw // HEAD_DIM]
        small["sgu_ln_g"][l], small["sgu_ln_b"][l] = d_lg, d_lb
        small["sgu_w"][l] = d_sw
        small["sgu_b"][l] = d_sbt.T

    grad_x = dh.reshape(x.shape)

    names = ["norm1_g", "b_in", "sinks", "sgu_ln_g", "sgu_ln_b", "sgu_w", "sgu_b", "norm2_g"]
    small_w = [norm1_g, b_in, sinks, sgu_ln_g, sgu_ln_b, sgu_w, sgu_b, norm2_g, final_g]
    small_m = [m_norm1_g, m_b_in, m_sinks, m_sgu_ln_g, m_sgu_ln_b, m_sgu_w, m_sgu_b, m_norm2_g, m_final_g]
    small_v = [v_norm1_g, v_b_in, v_sinks, v_sgu_ln_g, v_sgu_ln_b, v_sgu_w, v_sgu_b, v_norm2_g, v_final_g]
    shapes = [w.shape for w in small_w] + [(1,)]
    partial = [jnp.stack([p.reshape(w.shape[1:]) for p in small[n]]) for n, w in zip(names, small_w)]
    partial += [d_final_g.reshape(final_g.shape), loss_blk[0, :1]]
    zero = jnp.zeros((1,), F32)
    small_started = _exchange_start([_pack(partial)], False, (False,), "gather_start_small_grads", dhb)

    big_out = [None] * len(big)
    after = small_started[-1]
    for l in reversed(range(nl)):
        for group in grad_groups:
            srcs, lands = _exchange_wait(scattering.pop((l, group)), after, True, tuple(by_cols[t] for t in group),
                                         f"scatter_wait_l{l}_{big_names[group[0]]}")
            for t, parts in zip(group, lands):
                big_out[t] = _reduce_adam_layer(parts, big[t], big_m[t], big_v[t], big_out[t], l, f"adam_{big_names[t]}")
                after = big_out[t][0]

    srcs, lands = _exchange_wait(small_started, after, False, (False,), "gather_wait_small_grads")
    sm = _reduce_adam(lands[0][None], _pack(small_w + [zero])[None], _pack(small_m + [zero])[None],
                      _pack(small_v + [zero])[None], "adam_small")
    sm_g, sm_d, sm_m, sm_v = [_unpack(a[0], shapes) for a in sm]
    loss = sm_g[-1].reshape(())

    def ordered(kind_small, kind_big):
        by_name = dict(zip(["norm1_g", "b_in", "sinks", "sgu_ln_g", "sgu_ln_b", "sgu_w", "sgu_b", "norm2_g", "final_g"], kind_small))
        by_name.update(zip(["w_in", "w_attn_branch", "w_sgu_branch", "w_out", "w_gate_up", "w_down"], kind_big))
        order = ["norm1_g", "w_in", "b_in", "sinks", "sgu_ln_g", "sgu_ln_b", "sgu_w", "sgu_b", "w_attn_branch",
                 "w_sgu_branch", "w_out", "norm2_g", "w_gate_up", "w_down", "final_g"]
        return [by_name[n] for n in order]

    big_out[W_IN] = [transposed(o) for o in big_out[W_IN]]
    outs = [loss, grad_x]
    for idx, sm_kind in enumerate((sm_g, sm_d, sm_m, sm_v)):
        outs += ordered(sm_kind[:-1], [o[idx] for o in big_out])
    return tuple(outs)
```

```python
import math

import jax
import jax.numpy as jnp
from jax import lax
from jax.experimental import pallas as pl
from jax.experimental.pallas import tpu as pltpu

F32 = jnp.float32
BF16 = jnp.bfloat16

N_DEV = 8
HEAD_DIM = 64
WINDOW = 128
CHUNK = 128
GROUP_DIM = 128
ROPE_DIM = HEAD_DIM // 4
ROPE_HALF = ROPE_DIM // 2
ROPE_THETA = 500000.0
EPS = 1e-5
NEG = -1e30
ATTN_SCALE = HEAD_DIM ** -0.5
ADAM_LR = 0.001
ADAM_B1 = 0.9
ADAM_B2 = 0.999
ADAM_EPS = 1e-08
ADAM_WD = 0.01
ADAM_STEP = 10
LANES = 128
SUBLANES = 8
BF16_SUBLANES = 16
PACK_UNIT = SUBLANES * LANES
ADAM_BLOCK_ELEMS = 256 * 1024
V7X_VMEM_LIMIT_BYTES = 56 * 1024 * 1024
MATMUL_TILE_PREFS = (1024, 1408, 768, 512, 384, 256, 128)
MATMUL_WHOLE_K = 2048
MATMUL_TN_K = 4096
MATMUL_VMEM_BUDGET_BYTES = 52 * 1024 * 1024
MATMUL_K_PREFS = (2816, 2048, 1536, 1408, 1024, 768, 512, 384, 256, 128)
ROW_TILE_PREFS = (512, 256, 128)
SWIGLU_ROW_PREFS = (512, 256, 128)
D_ACT_ROW_PREFS = (1024, 512, 256, 128)
OUT_PROJ_ROW_PREFS = (512, 256, 128)
MERGE_ROW_PREFS = (1024, 512, 256, 128)
GATE_BLOCK_PREFS = (1024, 512, 256, 128)
FUSED_ROW_CHUNK = 256
MESH_TYPE = pl.DeviceIdType.MESH
ANY = pl.BlockSpec(memory_space=pl.ANY)
HBM = pl.BlockSpec(memory_space=pltpu.HBM)
SEM = pl.BlockSpec(memory_space=pltpu.SEMAPHORE)
DATAFLOW_EFFECT = pltpu.SideEffectType.DATAFLOW_SIDE_EFFECTING


def _pick(n, prefs):
    for p in prefs:
        if n % p == 0:
            return p
    return n


def _params(*sem):
    return pltpu.CompilerParams(dimension_semantics=sem, vmem_limit_bytes=V7X_VMEM_LIMIT_BYTES)


_DIMS = {"nn": (((1,), (0,)), ((), ())), "nt": (((1,), (1,)), ((), ())), "tn": (((0,), (0,)), ((), ()))}


def _matmul(a, b, mode, out_dtype, name, bias=None, res=None, after=None):
    if mode == "nn":
        (m, k), n = a.shape, b.shape[1]
    elif mode == "nt":
        (m, k), n = a.shape, b.shape[0]
    else:
        (k, m), n = a.shape, b.shape[1]
    tm, tn = _pick(m, MATMUL_TILE_PREFS), _pick(n, MATMUL_TILE_PREFS)
    if k <= MATMUL_WHOLE_K:
        tk = k
    else:
        fits = [t for t in ((MATMUL_TN_K,) if mode == "tn" else ()) + MATMUL_K_PREFS
                if k % t == 0 and 4 * t * (tm + tn) + 16 * tm * tn <= MATMUL_VMEM_BUDGET_BYTES]
        tk = fits[0]
    nk = k // tk
    dims = _DIMS[mode]
    a_spec = pl.BlockSpec((tk, tm), lambda i, j, kk: (kk, i)) if mode == "tn" else pl.BlockSpec((tm, tk), lambda i, j, kk: (i, kk))
    b_spec = pl.BlockSpec((tn, tk), lambda i, j, kk: (j, kk)) if mode == "nt" else pl.BlockSpec((tk, tn), lambda i, j, kk: (kk, j))
    in_specs, args = [a_spec, b_spec], [a, b]
    if bias is not None:
        in_specs.append(pl.BlockSpec((1, tn), lambda i, j, kk: (0, j)))
        args.append(bias)
    if res is not None:
        in_specs.append(pl.BlockSpec((tm, tn), lambda i, j, kk: (i, j)))
        args.append(res)
    if after is not None:
        in_specs.append(ANY)
        args.append(after)

    def body(*refs):
        a_ref, b_ref = refs[0], refs[1]
        pos = 2
        bias_ref = res_ref = None
        if bias is not None:
            bias_ref = refs[pos]
            pos += 1
        if res is not None:
            res_ref = refs[pos]
            pos += 1
        if after is not None:
            pos += 1
        o_ref = refs[pos]

        def finish(r):
            if bias_ref is not None:
                r = r + bias_ref[...]
            if res_ref is not None:
                r = r + res_ref[...]
            o_ref[...] = r.astype(out_dtype)

        part = lax.dot_general(a_ref[...], b_ref[...], dims, preferred_element_type=F32)
        if nk == 1:
            finish(part)
        else:
            acc_ref = refs[pos + 1]
            kk = pl.program_id(2)

            @pl.when(kk == 0)
            def _():
                acc_ref[...] = part

            @pl.when((kk > 0) & (kk < nk - 1))
            def _():
                acc_ref[...] += part

            @pl.when(kk == nk - 1)
            def _():
                finish(acc_ref[...] + part)

    return pl.pallas_call(
        body,
        name=name,
        grid=(m // tm, n // tn, nk),
        in_specs=in_specs,
        out_specs=pl.BlockSpec((tm, tn), lambda i, j, kk: (i, j)),
        out_shape=jax.ShapeDtypeStruct((m, n), out_dtype),
        scratch_shapes=[] if nk == 1 else [pltpu.VMEM((tm, tn), F32)],
        compiler_params=_params("parallel", "parallel", "arbitrary"),
    )(*args)


def _out_proj_rms(merged, w_out, h, g, name):
    s, d = h.shape
    tm = _pick(s, OUT_PROJ_ROW_PREFS)

    def body(a_ref, b_ref, h_ref, g_ref, o_ref, n_ref):
        for rows in _row_chunks(tm):
            r = h_ref[rows, :] + jnp.dot(a_ref[rows, :], b_ref[...], preferred_element_type=F32)
            o_ref[rows, :] = r
            rs = lax.rsqrt(jnp.mean(r * r, axis=-1, keepdims=True) + EPS)
            n_ref[rows, :] = (r * rs * g_ref[...]).astype(BF16)

    row = pl.BlockSpec((tm, d), lambda i: (i, 0))
    return pl.pallas_call(
        body,
        name=name,
        grid=(s // tm,),
        in_specs=[row, pl.BlockSpec((d, d), lambda i: (0, 0)), row, pl.BlockSpec((1, d), lambda i: (0, 0))],
        out_specs=[row, row],
        out_shape=[_sds((s, d), F32), _sds((s, d), BF16)],
        compiler_params=_params("parallel"),
    )(merged, w_out, h, g)


def _rowwise(body, name, rows, tr, ins, consts, outs, accs=(), after=()):
    n_in, n_c, n_o, n_a = len(ins), len(consts), len(outs), len(after)

    def wrapped(*refs):
        body(pl.program_id(0), refs[:n_in], refs[n_in:n_in + n_c], refs[n_in + n_c + n_a:n_in + n_c + n_a + n_o],
             refs[n_in + n_c + n_a + n_o:])

    def whole(shape):
        zeros = (0,) * len(shape)
        return pl.BlockSpec(tuple(shape), lambda i: zeros)

    in_specs = ([pl.BlockSpec((tr, a.shape[1]), lambda i: (i, 0)) for a in ins] + [whole(c.shape) for c in consts]
                + [ANY] * n_a)
    out_specs = [pl.BlockSpec((tr, o.shape[1]), lambda i: (i, 0)) for o in outs] + [whole(a.shape) for a in accs]
    return pl.pallas_call(
        wrapped,
        name=name,
        grid=(rows // tr,),
        in_specs=in_specs,
        out_specs=out_specs,
        out_shape=list(outs) + list(accs),
        compiler_params=_params("arbitrary" if accs else "parallel"),
    )(*ins, *consts, *after)


def _sds(shape, dtype):
    return jax.ShapeDtypeStruct(tuple(shape), dtype)


def _rms_fwd(h, g, name, after=()):
    s, d = h.shape
    tr = _pick(s, ROW_TILE_PREFS)

    def body(i, ins, consts, outs, accs):
        x = ins[0][...]
        r = lax.rsqrt(jnp.mean(x * x, axis=-1, keepdims=True) + EPS)
        outs[0][...] = (x * r * consts[0][...]).astype(BF16)

    return _rowwise(body, name, s, tr, [h], [g], [_sds((s, d), BF16)], after=after)[0]


def _rms_bwd(h, g, dy, dh_up, name):
    s, d = h.shape
    tr = _pick(s, ROW_TILE_PREFS)

    def body(i, ins, consts, outs, accs):
        x, dyv, up = ins[0][...], ins[1][...].astype(F32), ins[2][...]
        r = lax.rsqrt(jnp.mean(x * x, axis=-1, keepdims=True) + EPS)
        xr = x * r
        gy = dyv * consts[0][...]
        dx = r * (gy - xr * jnp.mean(gy * xr, axis=-1, keepdims=True))
        outs[0][...] = up + dx
        outs[1][...] = (up + dx).astype(BF16)

        @pl.when(i == 0)
        def _():
            accs[0][...] = jnp.zeros_like(accs[0])

        accs[0][...] += jnp.sum(dyv * xr, axis=0, keepdims=True)

    return _rowwise(body, name, s, tr, [h, dy, dh_up], [g], [_sds((s, d), F32), _sds((s, d), BF16)], [_sds((1, d), F32)])


def _loss_head(h, g, target, name):
    s, d = h.shape
    tr = _pick(s, ROW_TILE_PREFS)

    def body(i, ins, consts, outs, accs):
        x, t = ins[0][...], ins[1][...]
        gv = consts[0][...]
        r = lax.rsqrt(jnp.mean(x * x, axis=-1, keepdims=True) + EPS)
        xr = x * r
        diff = xr * gv - t
        dyv = diff * (1.0 / d)
        gy = dyv * gv
        dx = r * (gy - xr * jnp.mean(gy * xr, axis=-1, keepdims=True))
        outs[0][...] = dx
        outs[1][...] = dx.astype(BF16)

        @pl.when(i == 0)
        def _():
            accs[0][...] = jnp.zeros_like(accs[0])
            accs[1][...] = jnp.zeros_like(accs[1])

        accs[0][...] += jnp.sum(dyv * xr, axis=0, keepdims=True)
        part = 0.5 * jnp.sum(jnp.mean(diff * diff, axis=-1, keepdims=True), axis=0, keepdims=True)
        accs[1][...] += jnp.broadcast_to(part, accs[1].shape)

    return _rowwise(body, name, s, tr, [h, target], [g], [_sds((s, d), F32), _sds((s, d), BF16)],
                    [_sds((1, d), F32), _sds((SUBLANES, LANES), F32)])


def _colsum(a, name):
    s, w = a.shape
    tr = _pick(s, ROW_TILE_PREFS)

    def body(i, ins, consts, outs, accs):
        @pl.when(i == 0)
        def _():
            accs[0][...] = jnp.zeros_like(accs[0])

        accs[0][...] += jnp.sum(ins[0][...].astype(F32), axis=0, keepdims=True)

    return _rowwise(body, name, s, tr, [a], [], [], [_sds((1, w), F32)])[0]


def _sigmoid(x):
    return 1.0 / (1.0 + jnp.exp(-x))


def _branches_merge(y_attn, y_sgu, w_ab, w_sb, pg, gb, name):
    s, aw = y_attn.shape
    sw, d = w_sb.shape
    tm = _pick(s, MERGE_ROW_PREFS)

    def body(ya_ref, ys_ref, wa_ref, ws_ref, pg_ref, a_out, s_out, m_out):
        for rows in _row_chunks(tm):
            a = jnp.dot(ya_ref[rows, :], wa_ref[...], preferred_element_type=F32)
            b = jnp.dot(ys_ref[rows, :], ws_ref[...], preferred_element_type=F32)
            ga, gs = _sigmoid(pg_ref[rows, :gb].astype(F32)), _sigmoid(pg_ref[rows, gb:].astype(F32))
            a_out[rows, :] = a.astype(BF16)
            s_out[rows, :] = b.astype(BF16)
            m_out[rows, :] = (ga * a + gs * b).astype(BF16)

    blk = pl.BlockSpec((tm, gb), lambda i, j: (i, j))
    out = _sds((s, d), BF16)
    return pl.pallas_call(
        body,
        name=name,
        grid=(s // tm, d // gb),
        in_specs=[pl.BlockSpec((tm, aw), lambda i, j: (i, 0)), pl.BlockSpec((tm, sw), lambda i, j: (i, 0)),
                  pl.BlockSpec((aw, gb), lambda i, j: (0, j)), pl.BlockSpec((sw, gb), lambda i, j: (0, j)),
                  pl.BlockSpec((tm, 2 * gb), lambda i, j: (i, j))],
        out_specs=[blk, blk, blk],
        out_shape=[out, out, out],
        compiler_params=_params("parallel", "parallel"),
    )(y_attn, y_sgu, w_ab, w_sb, pg)


def _d_merged_merge_bwd(dmb, w_out, pg, a_br, s_br, gb, name):
    s, d = dmb.shape
    tm = _pick(s, MERGE_ROW_PREFS)

    def body(a_ref, b_ref, pg_ref, ab_ref, sb_ref, da_out, ds_out, dpg_out, db_out):
        @pl.when(pl.program_id(1) == 0)
        def _():
            db_out[...] = jnp.zeros_like(db_out)

        for rows in _row_chunks(tm):
            dm = lax.dot_general(a_ref[rows, :], b_ref[...], _DIMS["nt"], preferred_element_type=F32)
            ga, gs = _sigmoid(pg_ref[rows, :gb].astype(F32)), _sigmoid(pg_ref[rows, gb:].astype(F32))
            da_out[rows, :] = (dm * ga).astype(BF16)
            ds_out[rows, :] = (dm * gs).astype(BF16)
            dpa = dm * ab_ref[rows, :].astype(F32) * ga * (1.0 - ga)
            dps = dm * sb_ref[rows, :].astype(F32) * gs * (1.0 - gs)
            dpg_out[rows, :gb] = dpa.astype(BF16)
            dpg_out[rows, gb:] = dps.astype(BF16)
            db_out[:, :gb] += jnp.sum(dpa, axis=0, keepdims=True)
            db_out[:, gb:] += jnp.sum(dps, axis=0, keepdims=True)

    blk = pl.BlockSpec((tm, gb), lambda j, i: (i, j))
    pair = pl.BlockSpec((tm, 2 * gb), lambda j, i: (i, j))
    return pl.pallas_call(
        body,
        name=name,
        grid=(d // gb, s // tm),
        in_specs=[pl.BlockSpec((tm, d), lambda j, i: (i, 0)), pl.BlockSpec((gb, d), lambda j, i: (j, 0)), pair, blk, blk],
        out_specs=[blk, blk, pair, pl.BlockSpec((1, 2 * gb), lambda j, i: (0, j))],
        out_shape=[_sds((s, d), BF16), _sds((s, d), BF16), _sds((s, 2 * d), BF16), _sds((1, 2 * d), F32)],
        compiler_params=_params("parallel", "arbitrary"),
    )(dmb, w_out, pg, a_br, s_br)


def _row_chunks(tm):
    rc = _pick(tm, (FUSED_ROW_CHUNK,))
    return [slice(r, r + rc) for r in range(0, tm, rc)]


def _gate_up_swiglu(hn, w_gu, name):
    s, d = hn.shape
    n2 = w_gu.shape[1]
    fb = n2 // N_DEV
    tm = _pick(s, SWIGLU_ROW_PREFS)

    def body(a_ref, b_ref, gu_ref, act_ref):
        r = jnp.dot(a_ref[...], b_ref[...], preferred_element_type=F32)
        gu_ref[...] = r.astype(BF16)
        gate, up = r[:, :fb], r[:, fb:]
        act_ref[...] = (gate * _sigmoid(gate) * up).astype(BF16)

    return pl.pallas_call(
        body,
        name=name,
        grid=(s // tm, N_DEV // 2),
        in_specs=[pl.BlockSpec((tm, d), lambda i, j: (i, 0)), pl.BlockSpec((d, 2 * fb), lambda i, j: (0, j))],
        out_specs=[pl.BlockSpec((tm, 2 * fb), lambda i, j: (i, j)), pl.BlockSpec((tm, fb), lambda i, j: (i, j))],
        out_shape=[_sds((s, n2), BF16), _sds((s, n2 // 2), BF16)],
        compiler_params=_params("parallel", "parallel"),
    )(hn, w_gu)


def _d_act_swiglu(dhb, w_down, gu, name):
    s, d = dhb.shape
    n2 = gu.shape[1]
    fb = n2 // N_DEV
    tm = _pick(s, D_ACT_ROW_PREFS)

    def body(a_ref, b_ref, gu_ref, o_ref):
        for rows in _row_chunks(tm):
            da = lax.dot_general(a_ref[rows, :], b_ref[...], _DIMS["nt"], preferred_element_type=F32)
            gate, up = gu_ref[rows, :fb].astype(F32), gu_ref[rows, fb:].astype(F32)
            sg = _sigmoid(gate)
            o_ref[rows, :fb] = (da * up * sg * (1.0 + gate * (1.0 - sg))).astype(BF16)
            o_ref[rows, fb:] = (da * gate * sg).astype(BF16)

    pair = pl.BlockSpec((tm, 2 * fb), lambda i, j: (i, j))
    return pl.pallas_call(
        body,
        name=name,
        grid=(s // tm, N_DEV // 2),
        in_specs=[pl.BlockSpec((tm, d), lambda i, j: (i, 0)), pl.BlockSpec((fb, d), lambda i, j: (j, 0)), pair],
        out_specs=pair,
        out_shape=_sds((s, n2), BF16),
        compiler_params=_params("parallel", "parallel"),
    )(dhb, w_down, gu)


def _rope_tables(pos_col, name):
    s = pos_col.shape[0]
    tr = _pick(s, (1024, 512, 256, 128))
    inv = ROPE_THETA ** (-jnp.arange(0, ROPE_DIM, 2, dtype=F32) / ROPE_DIM)
    lane = jnp.arange(LANES)
    inv_lanes = inv[lane % ROPE_HALF].reshape(1, LANES)

    def body(i, ins, consts, outs, accs):
        ang = ins[0][...].astype(F32) * consts[0][...]
        c, sn = jnp.cos(ang), jnp.sin(ang)
        in_head = lax.broadcasted_iota(jnp.int32, ang.shape, 1) % HEAD_DIM
        outs[0][:, 0:LANES] = jnp.where(in_head < ROPE_DIM, c, 1.0)
        outs[0][:, LANES:2 * LANES] = jnp.where(in_head < ROPE_HALF, -sn, 0.0)
        outs[0][:, 2 * LANES:] = jnp.where((in_head >= ROPE_HALF) & (in_head < ROPE_DIM), sn, 0.0)

    return _rowwise(body, name, s, tr, [pos_col], [inv_lanes], [_sds((s, 3 * LANES), F32)])[0]


def _rope(x, tab, inverse=False):
    width = x.shape[1]
    reps = width // LANES
    c = jnp.tile(tab[:, 0:LANES], (1, reps))
    lo = jnp.tile(tab[:, LANES:2 * LANES], (1, reps))
    hi = jnp.tile(tab[:, 2 * LANES:], (1, reps))
    if inverse:
        lo, hi = -lo, -hi
    return x * c + pltpu.roll(x, width - ROPE_HALF, 1) * lo + pltpu.roll(x, ROPE_HALF, 1) * hi


def _attn_specs(aw, kw):
    kb = aw // kw
    prev = lambda i: jnp.maximum(i - 1, 0)
    return [
        pl.BlockSpec(memory_space=pltpu.SMEM),
        pl.BlockSpec((WINDOW, aw), lambda i: (i, 0)),
        pl.BlockSpec((WINDOW, kw), lambda i: (i, kb)),
        pl.BlockSpec((WINDOW, kw), lambda i: (prev(i), kb)),
        pl.BlockSpec((WINDOW, kw), lambda i: (i, kb + 1)),
        pl.BlockSpec((WINDOW, kw), lambda i: (prev(i), kb + 1)),
        pl.BlockSpec((WINDOW, 3 * LANES), lambda i: (i, 0)),
        pl.BlockSpec((WINDOW, 3 * LANES), lambda i: (prev(i), 0)),
    ]


def _attn_common(i, q_ref, kc_ref, kp_ref, vc_ref, vp_ref, tq_ref, tp_ref):
    tq, tp = tq_ref[...], tp_ref[...]
    qt = (_rope(q_ref[...].astype(F32), tq) * ATTN_SCALE).astype(BF16).T
    kc = _rope(kc_ref[...].astype(F32), tq)
    kp = _rope(kp_ref[...].astype(F32), tp)
    k2 = jnp.concatenate([kp, kc], axis=0).astype(BF16)
    v2 = jnp.concatenate([vp_ref[...], vc_ref[...]], axis=0)
    kj = lax.broadcasted_iota(jnp.int32, (2 * WINDOW, WINDOW), 0)
    qi = lax.broadcasted_iota(jnp.int32, (2 * WINDOW, WINDOW), 1)
    rel = qi + WINDOW - kj
    ok = (rel >= 0) & (rel < WINDOW) & ((kj >= WINDOW) | (i > 0))
    return qt, k2, v2, ok, tq, tp


def _head_probs(qt_h, kg, ok, sink):
    s = jnp.dot(kg, qt_h, preferred_element_type=F32)
    s = jnp.where(ok, s, NEG)
    m = jnp.maximum(jnp.max(s, axis=0, keepdims=True), sink)
    p = jnp.exp(s - m)
    es = jnp.exp(sink - m)
    inv = 1.0 / (jnp.sum(p, axis=0, keepdims=True) + es)
    return p * inv, es * inv


def _attn_fwd(qkv, tabs, sinks, aw, kw, name):
    s = qkv.shape[0]
    nq, nkv = aw // HEAD_DIM, kw // HEAD_DIM
    qpk = nq // nkv

    def body(s_ref, q_ref, kc_ref, kp_ref, vc_ref, vp_ref, tq_ref, tp_ref, o_ref):
        i = pl.program_id(0)
        tq, tp = tq_ref[...], tp_ref[...]
        q = (_rope(q_ref[...].astype(F32), tq) * ATTN_SCALE).astype(BF16)
        k2 = jnp.concatenate([_rope(kp_ref[...].astype(F32), tp), _rope(kc_ref[...].astype(F32), tq)], axis=0).astype(BF16)
        v2 = jnp.concatenate([vp_ref[...], vc_ref[...]], axis=0)
        qi = lax.broadcasted_iota(jnp.int32, (WINDOW, 2 * WINDOW), 0)
        kj = lax.broadcasted_iota(jnp.int32, (WINDOW, 2 * WINDOW), 1)
        rel = qi + WINDOW - kj
        ok = (rel >= 0) & (rel < WINDOW) & ((kj >= WINDOW) | (i > 0))
        for h in range(nq):
            g = h // qpk
            hs, gs = slice(h * HEAD_DIM, (h + 1) * HEAD_DIM), slice(g * HEAD_DIM, (g + 1) * HEAD_DIM)
            sc = lax.dot_general(q[:, hs], k2[:, gs], _DIMS["nt"], preferred_element_type=F32)
            sc = jnp.where(ok, sc, NEG)
            m = jnp.maximum(jnp.max(sc, axis=1, keepdims=True), s_ref[h])
            p = jnp.exp(sc - m)
            inv = 1.0 / (jnp.sum(p, axis=1, keepdims=True) + jnp.exp(s_ref[h] - m))
            o = jnp.dot((p * inv).astype(BF16), v2[:, gs], preferred_element_type=F32)
            o_ref[:, hs] = o.astype(BF16)

    return pl.pallas_call(
        body,
        name=name,
        grid=(s // WINDOW,),
        in_specs=_attn_specs(aw, kw),
        out_specs=pl.BlockSpec((WINDOW, aw), lambda i: (i, 0)),
        out_shape=_sds((s, aw), BF16),
        compiler_params=_params("parallel"),
    )(sinks, qkv, qkv, qkv, qkv, qkv, tabs, tabs)


def _attn_bwd(qkv, tabs, sinks, o, do, aw, kw, name):
    s = qkv.shape[0]
    nb = s // WINDOW
    nq, nkv = aw // HEAD_DIM, kw // HEAD_DIM
    qpk = nq // nkv

    def body(s_ref, q_ref, kc_ref, kp_ref, vc_ref, vp_ref, tq_ref, tp_ref, o_ref, do_ref,
             dq_ref, dkv_ref, ds_ref, ck_ref, cv_ref, qt_ref, dot_ref, ot_ref, dqt_ref):
        i = pl.program_id(0)

        @pl.when(i == 0)
        def _():
            ck_ref[...] = jnp.zeros_like(ck_ref)
            cv_ref[...] = jnp.zeros_like(cv_ref)
            ds_ref[...] = jnp.zeros_like(ds_ref)

        qt, k2, v2, ok, tq, tp = _attn_common(i, q_ref, kc_ref, kp_ref, vc_ref, vp_ref, tq_ref, tp_ref)
        qt_ref[...] = qt
        dot_ref[...] = do_ref[...].T
        ot_ref[...] = o_ref[...].T
        k2t = k2.T
        row0 = lax.broadcasted_iota(jnp.int32, (SUBLANES, LANES), 0) == 0
        lane = lax.broadcasted_iota(jnp.int32, (SUBLANES, LANES), 1)
        dsink = jnp.zeros((SUBLANES, LANES), F32)
        dk_parts, dv_parts = [], []
        for g in range(nkv):
            gs = slice(g * HEAD_DIM, (g + 1) * HEAD_DIM)
            kg, vg = k2[:, gs], v2[:, gs]
            dk_g = jnp.zeros((2 * WINDOW, HEAD_DIM), F32)
            dv_g = jnp.zeros((2 * WINDOW, HEAD_DIM), F32)
            for j in range(qpk):
                h = g * qpk + j
                hs = slice(h * HEAD_DIM, (h + 1) * HEAD_DIM)
                qt_h, dot_h = qt_ref[hs, :], dot_ref[hs, :]
                pn, psink = _head_probs(qt_h, kg, ok, s_ref[h])
                delta = jnp.sum(dot_h.astype(F32) * ot_ref[hs, :].astype(F32), axis=0, keepdims=True)
                dp = jnp.dot(vg, dot_h, preferred_element_type=F32)
                dsb = (pn * (dp - delta)).astype(BF16)
                dsink = dsink + jnp.where(row0 & (lane == h), -jnp.sum(psink * delta, axis=1, keepdims=True), 0.0)
                dqt_ref[hs, :] = jnp.dot(k2t[gs], dsb, preferred_element_type=F32)
                dk_g = dk_g + lax.dot_general(dsb, qt_h, _DIMS["nt"], preferred_element_type=F32)
                dv_g = dv_g + lax.dot_general(pn.astype(BF16), dot_h, _DIMS["nt"], preferred_element_type=F32)
            dk_parts.append(dk_g)
            dv_parts.append(dv_g)
        ds_ref[...] += dsink
        dq_ref[...] = _rope(dqt_ref[...].T * ATTN_SCALE, tq, inverse=True).astype(BF16)
        dk2 = jnp.concatenate(dk_parts, axis=1)
        dv2 = jnp.concatenate(dv_parts, axis=1)
        dk_prev = _rope(ck_ref[...] + dk2[:WINDOW], tp, inverse=True)
        dv_prev = cv_ref[...] + dv2[:WINDOW]

        @pl.when(i > 0)
        def _():
            dkv_ref[pl.ds(pl.multiple_of((i - 1) * WINDOW, WINDOW), WINDOW), :] = jnp.concatenate(
                [dk_prev, dv_prev], axis=1).astype(BF16)

        ck_ref[...] = dk2[WINDOW:]
        cv_ref[...] = dv2[WINDOW:]

        @pl.when(i == nb - 1)
        def _():
            dkv_ref[pl.ds(pl.multiple_of(i * WINDOW, WINDOW), WINDOW), :] = jnp.concatenate(
                [_rope(dk2[WINDOW:], tq, inverse=True), dv2[WINDOW:]], axis=1).astype(BF16)

    blk = pl.BlockSpec((WINDOW, aw), lambda i: (i, 0))
    return pl.pallas_call(
        body,
        name=name,
        grid=(nb,),
        in_specs=_attn_specs(aw, kw) + [blk, blk],
        out_specs=[blk, pl.BlockSpec((s, 2 * kw), lambda i: (0, 0)), pl.BlockSpec((SUBLANES, LANES), lambda i: (0, 0))],
        out_shape=[_sds((s, aw), BF16), _sds((s, 2 * kw), BF16), _sds((SUBLANES, LANES), F32)],
        scratch_shapes=[pltpu.VMEM((WINDOW, kw), F32), pltpu.VMEM((WINDOW, kw), F32), pltpu.VMEM((aw, WINDOW), BF16),
                        pltpu.VMEM((aw, WINDOW), BF16), pltpu.VMEM((aw, WINDOW), BF16), pltpu.VMEM((aw, WINDOW), F32)],
        compiler_params=_params("arbitrary"),
    )(sinks, qkv, qkv, qkv, qkv, qkv, tabs, tabs, o, do)


_INV_SQRT2 = 1.0 / math.sqrt(2.0)
_INV_SQRT2PI = 1.0 / math.sqrt(2.0 * math.pi)


def _gelu(x):
    return x * (lax.erf(x * _INV_SQRT2) + 1.0) * 0.5


def _gelu_grad(x):
    return 0.5 * (lax.erf(x * _INV_SQRT2) + 1.0) + x * jnp.exp(-0.5 * x * x) * _INV_SQRT2PI


def _sgu_norm(pv, lg, lb):
    zv = _gelu(pv)
    mu = jnp.mean(zv, axis=-1, keepdims=True)
    cen = zv - mu
    rs = lax.rsqrt(jnp.mean(cen * cen, axis=-1, keepdims=True) + EPS)
    xhat = cen * rs
    return xhat, rs, (xhat * lg + lb).astype(BF16)


def _causal(w, upper=False):
    t = lax.broadcasted_iota(jnp.int32, (CHUNK, CHUNK), 0)
    u = lax.broadcasted_iota(jnp.int32, (CHUNK, CHUNK), 1)
    return jnp.where((u >= t) if upper else (t >= u), w, 0.0).astype(BF16)


def _sgu_fwd(pz, lg, lb, w, bt, name):
    s, sw = pz.shape[0], pz.shape[1] // 2
    groups = sw // GROUP_DIM

    def body(i, ins, consts, outs, accs):
        lgv, lbv, w_ref, btv = consts[0][...], consts[1][...], consts[2], consts[3][...]
        zu = _gelu(ins[0][:, :sw].astype(F32))
        _, _, vn = _sgu_norm(ins[0][:, sw:].astype(F32), lgv, lbv)
        for g in range(groups):
            gs = slice(g * GROUP_DIM, (g + 1) * GROUP_DIM)
            sv = jnp.dot(_causal(w_ref[g]), vn[:, gs], preferred_element_type=F32) + btv[:, g:g + 1]
            outs[0][:, gs] = (zu[:, gs] * sv).astype(BF16)

    return _rowwise(body, name, s, CHUNK, [pz], [lg, lb, w, bt], [_sds((s, sw), BF16)])[0]


def _sgu_bwd(pz, dy, lg, lb, w, wt, bt, name, after=()):
    s, sw = pz.shape[0], pz.shape[1] // 2
    groups = sw // GROUP_DIM

    def body(i, ins, consts, outs, accs):
        lgv, lbv, w_ref, wt_ref, btv = consts[0][...], consts[1][...], consts[2], consts[3], consts[4][...]

        @pl.when(i == 0)
        def _():
            for a in accs:
                a[...] = jnp.zeros_like(a)

        pu, pv = ins[0][:, :sw].astype(F32), ins[0][:, sw:].astype(F32)
        dyv = ins[1][...].astype(F32)
        zu = _gelu(pu)
        xhat, rs, vn = _sgu_norm(pv, lgv, lbv)
        dvn_parts, db_parts = [], []
        lower = lax.broadcasted_iota(jnp.int32, (CHUNK, CHUNK), 0) >= lax.broadcasted_iota(jnp.int32, (CHUNK, CHUNK), 1)
        for g in range(groups):
            gs = slice(g * GROUP_DIM, (g + 1) * GROUP_DIM)
            sv = jnp.dot(_causal(w_ref[g]), vn[:, gs], preferred_element_type=F32) + btv[:, g:g + 1]
            dpu = dyv[:, gs] * sv * _gelu_grad(pu[:, gs])
            outs[0][:, gs] = dpu.astype(BF16)
            accs[4][:, gs] += jnp.sum(dpu, axis=0, keepdims=True)
            dsv = dyv[:, gs] * zu[:, gs]
            dsvb = dsv.astype(BF16)
            db_parts.append(jnp.sum(dsv, axis=1, keepdims=True))
            accs[2][g] += jnp.where(lower, lax.dot_general(dsvb, vn[:, gs], _DIMS["nt"], preferred_element_type=F32), 0.0)
            dvn_parts.append(jnp.dot(_causal(wt_ref[g], upper=True), dsvb, preferred_element_type=F32))
        dvn = jnp.concatenate(dvn_parts, axis=1)
        accs[3][...] += jnp.concatenate(db_parts, axis=1)
        accs[0][...] += jnp.sum(dvn * xhat, axis=0, keepdims=True)
        accs[1][...] += jnp.sum(dvn, axis=0, keepdims=True)
        dxh = dvn * lgv
        dz = rs * (dxh - jnp.mean(dxh, axis=-1, keepdims=True) - xhat * jnp.mean(dxh * xhat, axis=-1, keepdims=True))
        dpv = dz * _gelu_grad(pv)
        outs[0][:, sw:] = dpv.astype(BF16)
        accs[4][:, sw:] += jnp.sum(dpv, axis=0, keepdims=True)

    return _rowwise(body, name, s, CHUNK, [pz, dy], [lg, lb, w, wt, bt], [_sds((s, 2 * sw), BF16)],
                    [_sds((1, sw), F32), _sds((1, sw), F32), _sds((groups, CHUNK, CHUNK), F32), _sds((CHUNK, groups), F32),
                     _sds((1, 2 * sw), F32)],
                    after=after)


def _mesh_place():
    x, y, c = lax.axis_index("x"), lax.axis_index("y"), lax.axis_index("c")
    return x, y, c, 4 * x + 2 * y + c


def _peer(x, y, c, k):
    px, py, pc = x ^ ((k >> 2) & 1), y ^ ((k >> 1) & 1), c ^ (k & 1)
    return (px, py, pc), 4 * px + 2 * py + pc


BY_SLOTS, BY_COLS, BY_PAIRED_COLS = 0, 1, 2


def _col_block(ref, idx, width, cols):
    if cols == BY_PAIRED_COLS:
        idx = (idx % (N_DEV // 2)) * 2 + idx // (N_DEV // 2)
    return ref.at[:, pl.ds(pl.multiple_of(idx * width, LANES), width)]


def _exchange_copy(src_ref, land_ref, send_sems, recv_sems, k, place, scatter, arriving, cols):
    x, y, c, me = place
    peer, pidx = _peer(x, y, c, k)
    slot = pidx if arriving else me
    if scatter:
        src = _col_block(src_ref, pidx, land_ref.shape[-1], cols) if cols else src_ref.at[pidx]
        dst = land_ref.at[slot]
    else:
        src = src_ref
        dst = _col_block(land_ref, slot, src_ref.shape[-1], cols) if cols else land_ref.at[slot]
    return pltpu.make_async_remote_copy(
        src_ref=src, dst_ref=dst, send_sem=send_sems[k - 1], recv_sem=recv_sems[k - 1], device_id=peer,
        device_id_type=MESH_TYPE)


def _own_copy(src_ref, land_ref, sem, place, scatter, cols):
    me = place[3]
    if scatter:
        src = _col_block(src_ref, me, land_ref.shape[-1], cols) if cols else src_ref.at[me]
        dst = land_ref.at[me]
    else:
        src = src_ref
        dst = _col_block(land_ref, me, src_ref.shape[-1], cols) if cols else land_ref.at[me]
    return pltpu.make_async_copy(src, dst, sem)


N_PEERS = N_DEV - 1
N_EXCHANGE_SEMS = 2 * N_PEERS + 1


def _land_shape(a, scatter, cols):
    if scatter:
        return (N_DEV, a.shape[0], a.shape[1] // N_DEV) if cols else a.shape
    return (a.shape[0], N_DEV * a.shape[1]) if cols else (N_DEV,) + a.shape


def _exchange_start(srcs, scatter, cols, name, after):
    n = len(srcs)
    land_shapes = [_land_shape(a, scatter, cl) for a, cl in zip(srcs, cols)]

    def body(*refs):
        src, land = refs[:n], refs[n:2 * n]
        send_sems = refs[2 * n + 1:2 * n + 1 + N_PEERS]
        recv_sems = refs[2 * n + 1 + N_PEERS:2 * n + 1 + 2 * N_PEERS]
        own_sem = refs[2 * n + 1 + 2 * N_PEERS]
        token = refs[-1]
        place = _mesh_place()
        for t in range(n):
            for k in range(1, N_DEV):
                _exchange_copy(src[t], land[t], send_sems, recv_sems, k, place, scatter, False, cols[t]).start()
            _own_copy(src[t], land[t], own_sem, place, scatter, cols[t]).start()
        token[...] = jnp.zeros_like(token)

    return pl.pallas_call(
        body,
        name=name,
        out_shape=(*[pltpu.SemaphoreType.DMA(())] * N_EXCHANGE_SEMS, *[pltpu.HBM(a.shape, a.dtype) for a in srcs],
                   *[pltpu.HBM(shp, a.dtype) for shp, a in zip(land_shapes, srcs)], _sds((SUBLANES, LANES), F32)),
        in_specs=[HBM] * (2 * n) + [ANY],
        out_specs=(*[SEM] * N_EXCHANGE_SEMS, *[HBM] * (2 * n), pl.BlockSpec(memory_space=pltpu.VMEM)),
        input_output_aliases={i: N_EXCHANGE_SEMS + i for i in range(2 * n)},
        compiler_params=pltpu.CompilerParams(has_side_effects=DATAFLOW_EFFECT),
    )(*[pltpu.with_memory_space_constraint(a, pltpu.HBM) for a in srcs],
      *[pltpu.with_memory_space_constraint(lax.empty(shp, a.dtype), pltpu.HBM) for shp, a in zip(land_shapes, srcs)],
      after)


def _exchange_wait(started, after, scatter, cols, name):
    sems = started[:N_EXCHANGE_SEMS]
    thru = started[N_EXCHANGE_SEMS:-1]
    n = len(thru) // 2

    def body(*refs):
        src, land = refs[:n], refs[n:2 * n]
        send_sems = refs[2 * n:2 * n + N_PEERS]
        recv_sems = refs[2 * n + N_PEERS:2 * n + 2 * N_PEERS]
        own_sem = refs[2 * n + 2 * N_PEERS]
        place = _mesh_place()
        for t in range(n):
            for k in range(1, N_DEV):
                cp = _exchange_copy(src[t], land[t], send_sems, recv_sems, k, place, scatter, True, cols[t])
                cp.wait_send()
                cp.wait_recv()
            _own_copy(src[t], land[t], own_sem, place, scatter, cols[t]).wait()

    out = pl.pallas_call(
        body,
        name=name,
        out_shape=tuple(pltpu.HBM(a.shape, a.dtype) for a in thru),
        in_specs=[HBM] * (2 * n) + [SEM] * N_EXCHANGE_SEMS + [ANY],
        out_specs=tuple([HBM] * (2 * n)),
        input_output_aliases={i: i for i in range(2 * n)},
        compiler_params=pltpu.CompilerParams(has_side_effects=DATAFLOW_EFFECT),
    )(*thru, *sems, after)
    return out[:n], out[n:]


def _adamw(w, g, m, v):
    m = ADAM_B1 * m + (1.0 - ADAM_B1) * g
    v = ADAM_B2 * v + (1.0 - ADAM_B2) * (g * g)
    m_hat = m / (1.0 - ADAM_B1 ** ADAM_STEP)
    v_hat = v / (1.0 - ADAM_B2 ** ADAM_STEP)
    delta = -ADAM_LR * (m_hat / (jnp.sqrt(v_hat) + ADAM_EPS) + ADAM_WD * w)
    return delta, m, v


def _adam_rows(r, c):
    fits = [t for t in range(BF16_SUBLANES, r + 1, BF16_SUBLANES) if r % t == 0 and t * c <= ADAM_BLOCK_ELEMS]
    return max(fits) if fits else r


def _adam_body(p_ref, w_ref, m_ref, v_ref, g_out, d_out, m_out, v_out):
    g = p_ref[0].astype(F32)
    for d in range(1, N_DEV):
        g = g + p_ref[d].astype(F32)
    delta, mn, vn = _adamw(w_ref[...], g, m_ref[...], v_ref[...])
    g_out[...] = g
    d_out[...] = delta
    m_out[...] = mn
    v_out[...] = vn


def _reduce_adam_layer(parts, w, m, v, prev, layer, name):
    nl, r, c = w.shape
    tr = _adam_rows(r, c)
    if prev is None:
        prev = [lax.empty((nl, r, c), F32) for _ in range(4)]

    def body(p_ref, w_ref, m_ref, v_ref, *rest):
        _adam_body(p_ref, w_ref, m_ref, v_ref, *rest[4:])

    blk = pl.BlockSpec((None, tr, c), lambda i: (layer, i, 0))
    out = _sds((nl, r, c), F32)
    return pl.pallas_call(
        body,
        name=name,
        grid=(r // tr,),
        in_specs=[pl.BlockSpec((N_DEV, tr, c), lambda i: (0, i, 0)), blk, blk, blk, ANY, ANY, ANY, ANY],
        out_specs=[blk, blk, blk, blk],
        out_shape=[out, out, out, out],
        input_output_aliases={4: 0, 5: 1, 6: 2, 7: 3},
        compiler_params=_params("parallel"),
    )(parts, w, m, v, *prev)


def _reduce_adam(parts, w, m, v, name):
    nl, _, r, c = parts.shape
    tr = _adam_rows(r, c)

    def body(*refs):
        _adam_body(*refs)

    blk = pl.BlockSpec((None, tr, c), lambda l, i: (l, i, 0))
    out = _sds((nl, r, c), F32)
    return pl.pallas_call(
        body,
        name=name,
        grid=(nl, r // tr),
        in_specs=[pl.BlockSpec((None, N_DEV, tr, c), lambda l, i: (l, 0, i, 0)), blk, blk, blk],
        out_specs=[blk, blk, blk, blk],
        out_shape=[out, out, out, out],
        compiler_params=_params("parallel", "parallel"),
    )(parts, w, m, v)


def _pack(arrays):
    flat = []
    for a in arrays:
        a = a.reshape(-1).astype(F32)
        flat.append(jnp.pad(a, (0, (-a.shape[0]) % PACK_UNIT)))
    out = jnp.concatenate(flat)
    rows = out.shape[0] // LANES
    pad_rows = (-rows) % 512
    return jnp.pad(out, (0, pad_rows * LANES)).reshape(rows + pad_rows, LANES)


def _unpack(packed, shapes):
    flat = packed.reshape(-1)
    out, off = [], 0
    for shp in shapes:
        size = math.prod(shp)
        out.append(flat[off:off + size].reshape(shp))
        off += size + (-size) % PACK_UNIT
    return out


def _in_runs(d, qkv_w, sw, gb):
    g0 = qkv_w + 2 * sw
    runs = [(0, qkv_w, "qkv", 0), (qkv_w, 2 * sw, "z", 0)]
    for j in range(d // gb):
        runs.append((g0 + j * gb, gb, "g", 2 * j * gb))
        runs.append((g0 + d + j * gb, gb, "g", (2 * j + 1) * gb))
    return runs


def _pieces_from_global(take, runs, axis=-1):
    out = {}
    for piece in ("qkv", "z", "g"):
        own = sorted((r for r in runs if r[2] == piece), key=lambda r: r[3])
        parts = [take(g, g + w) for g, w, _, _ in own]
        out[piece] = parts[0] if len(parts) == 1 else jnp.concatenate(parts, axis=axis)
    return out


def _global_from_pieces(pieces, runs, axis=-1):
    segs = [lax.slice_in_dim(pieces[piece], start, start + w, axis=axis % pieces[piece].ndim)
            for _, w, piece, start in sorted(runs)]
    return jnp.concatenate(segs, axis=axis)


def _to_full_cols(g):
    d, k, n = g.shape
    return jnp.transpose(g, (1, 0, 2)).reshape(k, d * n)


def _to_col_shards(a):
    k, n = a.shape
    return jnp.transpose(a.reshape(k, N_DEV, n // N_DEV), (1, 0, 2))


def kernel(x, positions, norm1_g, w_in, b_in, sinks, sgu_ln_g, sgu_ln_b, sgu_w, sgu_b, w_attn_branch, w_sgu_branch, w_out, norm2_g, w_gate_up, w_down, final_g, loss_target, m_norm1_g, m_w_in, m_b_in, m_sinks, m_sgu_ln_g, m_sgu_ln_b, m_sgu_w, m_sgu_b, m_w_attn_branch, m_w_sgu_branch, m_w_out, m_norm2_g, m_w_gate_up, m_w_down, m_final_g, v_norm1_g, v_w_in, v_b_in, v_sinks, v_sgu_ln_g, v_sgu_ln_b, v_sgu_w, v_sgu_b, v_w_attn_branch, v_w_sgu_branch, v_w_out, v_norm2_g, v_w_gate_up, v_w_down, v_final_g):
    nl = w_in.shape[0]
    s, d = x.shape[1], x.shape[2]
    aw = w_attn_branch.shape[1]
    sw = w_sgu_branch.shape[1]
    in_w = w_in.shape[2] * N_DEV
    kw = (in_w - aw - 2 * sw - 2 * d) // 2
    qkv_w = aw + 2 * kw
    ff = w_down.shape[1] * N_DEV

    h = x.reshape(s, d)
    target = loss_target.reshape(s, d)
    tabs = _rope_tables(positions.reshape(s, 1), "rope_tables")

    transposed = lambda a: jnp.swapaxes(a, 1, 2)
    big = [transposed(w_in), w_attn_branch, w_sgu_branch, w_out, w_gate_up, w_down]
    big_m = [transposed(m_w_in), m_w_attn_branch, m_w_sgu_branch, m_w_out, m_w_gate_up, m_w_down]
    big_v = [transposed(v_w_in), v_w_attn_branch, v_w_sgu_branch, v_w_out, v_w_gate_up, v_w_down]
    big_names = ("w_in", "w_attn_branch", "w_sgu_branch", "w_out", "w_gate_up", "w_down")
    W_IN, W_AB, W_SB, W_OUT, W_GU, W_DOWN = range(6)
    weight_groups = ((W_IN,), (W_AB, W_SB, W_OUT), (W_GU,), (W_DOWN,))
    grad_groups = ((W_DOWN, W_GU), (W_OUT, W_AB, W_SB), (W_IN,))

    col_sharded = (W_AB, W_SB, W_GU)
    by_cols = [BY_COLS if t in col_sharded and big[t].shape[2] % LANES == 0 else BY_SLOTS for t in range(6)]
    assert by_cols[W_GU] == BY_COLS, "the fused swiglu kernels need gate/up column blocks of whole lane tiles"
    by_cols[W_GU] = BY_PAIRED_COLS

    def start_gather(l, group, after):
        return _exchange_start([big[t][l].astype(BF16) for t in group], False, tuple(by_cols[t] for t in group),
                               f"gather_start_l{l}_{big_names[group[0]]}", after)

    def full_weight(t, land):
        if by_cols[t]:
            return land
        if t in col_sharded:
            return _to_full_cols(land)
        whole = land.reshape(N_DEV * land.shape[1], land.shape[2])
        return _pieces_from_global(lambda a, b: whole[a:b], in_runs, axis=0) if t == W_IN else whole

    gate_block = _pick(d, GATE_BLOCK_PREFS)
    in_runs = _in_runs(d, qkv_w, sw, gate_block)

    saved = []
    started = {}
    token = h
    for l in range(nl):
        for ll in ((0, 1) if l == 0 else (l + 1,)):
            if ll < nl:
                for group in weight_groups:
                    started[(ll, group)] = start_gather(ll, group, token)
                    token = started[(ll, group)][-1]
        gathered = {}

        def weight(t, after, l=l, gathered=gathered):
            if t not in gathered:
                group = next(g for (ll, g) in started if ll == l and t in g)
                srcs, lands = _exchange_wait(started.pop((l, group)), after, False, tuple(by_cols[tt] for tt in group),
                                             f"gather_wait_l{l}_{big_names[group[0]]}")
                for tt, ld in zip(group, lands):
                    gathered[tt] = full_weight(tt, ld)
            return gathered[t]

        bias = b_in[l].reshape(1, in_w)
        g1, g2 = norm1_g[l].reshape(1, d), norm2_g[l].reshape(1, d)
        lg, lb = sgu_ln_g[l].reshape(1, sw), sgu_ln_b[l].reshape(1, sw)
        bt = sgu_b[l].T

        xn = _rms_fwd(h, g1, "rms1_fwd", after=(token,))
        wts = dict(weight(W_IN, xn))
        biases = _pieces_from_global(lambda a, b: bias[:, a:b], in_runs)
        qkv = _matmul(xn, wts["qkv"], "nt", BF16, "proj_qkv", bias=biases["qkv"])
        pz = _matmul(xn, wts["z"], "nt", BF16, "proj_z", bias=biases["z"])
        pg = _matmul(xn, wts["g"], "nt", BF16, "proj_g", bias=biases["g"])
        y_attn = _attn_fwd(qkv, tabs, sinks[l], aw, kw, "attn_fwd")
        y_sgu = _sgu_fwd(pz, lg, lb, sgu_w[l], bt, "sgu_fwd")
        wts.update(ab=weight(W_AB, y_sgu), sb=weight(W_SB, y_sgu), out=weight(W_OUT, y_sgu))
        a_br, s_br, merged = _branches_merge(y_attn, y_sgu, wts["ab"], wts["sb"], pg, gate_block, "branches_merge")
        h_mid, hn = _out_proj_rms(merged, wts["out"], h, g2, "out_proj")
        wts.update(gu=weight(W_GU, hn))
        gu, act = _gate_up_swiglu(hn, wts["gu"], "gate_up")
        wts.update(down=weight(W_DOWN, act))
        h_out = _matmul(act, wts["down"], "nn", F32, "down_proj", res=h_mid)
        saved.append(dict(wts=wts, h=h, xn=xn, qkv=qkv, pz=pz, pg=pg, y_attn=y_attn, y_sgu=y_sgu, a_br=a_br,
                          s_br=s_br, merged=merged, h_mid=h_mid, hn=hn, gu=gu, act=act,
                          g1=g1, g2=g2, lg=lg, lb=lb, bt=bt))
        h = h_out

    dh, dhb, d_final_g, loss_blk = _loss_head(h, final_g.reshape(1, d), target, "loss_head")

    small = {n: [None] * nl for n in ("norm1_g", "b_in", "sinks", "sgu_ln_g", "sgu_ln_b", "sgu_w", "sgu_b", "norm2_g")}
    scattering = {}

    def start_scatter(l, group, grads, after):
        sends = []
        for t, dw in zip(group, grads):
            if t == W_IN:
                dw = _global_from_pieces(dw, in_runs, axis=0)
            if by_cols[t]:
                sends.append(dw)
            elif t in col_sharded:
                sends.append(_to_col_shards(dw))
            else:
                sends.append(dw.reshape(N_DEV, dw.shape[0] // N_DEV, dw.shape[1]))
        scattering[(l, group)] = _exchange_start(sends, True, tuple(by_cols[t] for t in group),
                                                 f"scatter_start_l{l}_{big_names[group[0]]}", after)
        return scattering[(l, group)][-1]

    for l in reversed(range(nl)):
        sv = saved[l]
        wts = sv["wts"]
        d_gu = _d_act_swiglu(dhb, wts["down"], sv["gu"], "d_act")
        dw_down = _matmul(sv["act"], dhb, "tn", BF16, "dw_down")
        dw_gu = _matmul(sv["hn"], d_gu, "tn", BF16, "dw_gate_up")
        token = start_scatter(l, grad_groups[0], [dw_down, dw_gu], token)
        d_hn = _matmul(d_gu, wts["gu"], "nt", BF16, "d_hn", after=token)
        dh_mid, dmb, dg2 = _rms_bwd(sv["h_mid"], sv["g2"], d_hn, dh, "rms2_bwd")
        d_a, d_s, d_pg, db_g = _d_merged_merge_bwd(dmb, wts["out"], sv["pg"], sv["a_br"], sv["s_br"], gate_block, "d_merged")
        dw_out = _matmul(sv["merged"], dmb, "tn", BF16, "dw_out")
        d_y_attn = _matmul(d_a, wts["ab"], "nt", BF16, "d_y_attn")
        dw_ab = _matmul(sv["y_attn"], d_a, "tn", BF16, "dw_attn_branch")
        d_y_sgu = _matmul(d_s, wts["sb"], "nt", BF16, "d_y_sgu")
        dw_sb = _matmul(sv["y_sgu"], d_s, "tn", BF16, "dw_sgu_branch")
        token = start_scatter(l, grad_groups[1], [dw_out, dw_ab, dw_sb], token)
        d_pz, d_lg, d_lb, d_sw, d_sbt, db_z = _sgu_bwd(sv["pz"], d_y_sgu, sv["lg"], sv["lb"], sgu_w[l],
                                                 jnp.transpose(sgu_w[l], (0, 2, 1)), sv["bt"], "sgu_bwd", after=(token,))
        d_q, d_kv, d_sinks = _attn_bwd(sv["qkv"], tabs, sinks[l], sv["y_attn"], d_y_attn, aw, kw, "attn_bwd")
        d_qkv = jnp.concatenate([d_q, d_kv], axis=1)
        dw_qkv = _matmul(d_qkv, sv["xn"], "tn", BF16, "dw_qkv")
        dw_z = _matmul(d_pz, sv["xn"], "tn", BF16, "dw_z")
        dw_g = _matmul(d_pg, sv["xn"], "tn", BF16, "dw_g")
        token = start_scatter(l, grad_groups[2], [dict(qkv=dw_qkv, z=dw_z, g=dw_g)], token)
        d_xn = _matmul(d_qkv, wts["qkv"], "nn", F32, "d_xn_qkv", after=token)
        d_xn = _matmul(d_pz, wts["z"], "nn", F32, "d_xn_z", res=d_xn)
        d_xn = _matmul(d_pg, wts["g"], "nn", F32, "d_xn_g", res=d_xn)
        dh, dhb, dg1 = _rms_bwd(sv["h"], sv["g1"], d_xn, dh_mid, "rms1_bwd")

        small["norm1_g"][l], small["norm2_g"][l] = dg1, dg2
        small["b_in"][l] = _global_from_pieces(dict(qkv=_colsum(d_qkv, "db_qkv"), z=db_z, g=db_g), in_runs)
        small["sinks"][l] = d_sinks[0, :aw // HEAD_DIM]
        small["sgu_ln_g"][l], small["sgu_ln_b"][l] = d_lg, d_lb
        small["sgu_w"][l] = d_sw
        small["sgu_b"][l] = d_sbt.T

    grad_x = dh.reshape(x.shape)

    names = ["norm1_g", "b_in", "sinks", "sgu_ln_g", "sgu_ln_b", "sgu_w", "sgu_b", "norm2_g"]
    small_w = [norm1_g, b_in, sinks, sgu_ln_g, sgu_ln_b, sgu_w, sgu_b, norm2_g, final_g]
    small_m = [m_norm1_g, m_b_in, m_sinks, m_sgu_ln_g, m_sgu_ln_b, m_sgu_w, m_sgu_b, m_norm2_g, m_final_g]
    small_v = [v_norm1_g, v_b_in, v_sinks, v_sgu_ln_g, v_sgu_ln_b, v_sgu_w, v_sgu_b, v_norm2_g, v_final_g]
    shapes = [w.shape for w in small_w] + [(1,)]
    partial = [jnp.stack([p.reshape(w.shape[1:]) for p in small[n]]) for n, w in zip(names, small_w)]
    partial += [d_final_g.reshape(final_g.shape), loss_blk[0, :1]]
    zero = jnp.zeros((1,), F32)
    small_started = _exchange_start([_pack(partial)], False, (False,), "gather_start_small_grads", dhb)

    big_out = [None] * len(big)
    after = small_started[-1]
    for l in reversed(range(nl)):
        for group in grad_groups:
            srcs, lands = _exchange_wait(scattering.pop((l, group)), after, True, tuple(by_cols[t] for t in group),
                                         f"scatter_wait_l{l}_{big_names[group[0]]}")
            for t, parts in zip(group, lands):
                big_out[t] = _reduce_adam_layer(parts, big[t], big_m[t], big_v[t], big_out[t], l, f"adam_{big_names[t]}")
                after = big_out[t][0]

    srcs, lands = _exchange_wait(small_started, after, False, (False,), "gather_wait_small_grads")
    sm = _reduce_adam(lands[0][None], _pack(small_w + [zero])[None], _pack(small_m + [zero])[None],
                      _pack(small_v + [zero])[None], "adam_small")
    sm_g, sm_d, sm_m, sm_v = [_unpack(a[0], shapes) for a in sm]
    loss = sm_g[-1].reshape(())

    def ordered(kind_small, kind_big):
        by_name = dict(zip(["norm1_g", "b_in", "sinks", "sgu_ln_g", "sgu_ln_b", "sgu_w", "sgu_b", "norm2_g", "final_g"], kind_small))
        by_name.update(zip(["w_in", "w_attn_branch", "w_sgu_branch", "w_out", "w_gate_up", "w_down"], kind_big))
        order = ["norm1_g", "w_in", "b_in", "sinks", "sgu_ln_g", "sgu_ln_b", "sgu_w", "sgu_b", "w_attn_branch",
                 "w_sgu_branch", "w_out", "norm2_g", "w_gate_up", "w_down", "final_g"]
        return [by_name[n] for n in order]

    big_out[W_IN] = [transposed(o) for o in big_out[W_IN]]
    outs = [loss, grad_x]
    for idx, sm_kind in enumerate((sm_g, sm_d, sm_m, sm_v)):
        outs += ordered(sm_kind[:-1], [o[idx] for o in big_out])
    return tuple(outs)
```
